```python
import math
import jax, jax.numpy as jnp
from jax import lax
import numpy as np

D_MODEL = 2048
BATCH = 2
SEQ = 8192
DEPTH = 2
DEC_BATCH = 32
DEC_SEQ = 32
PAST_LEN = 1024

CHUNK = 64
N_MIXERS = 2
N_ATTN_LAYERS = (DEPTH + 1) // 2
N_SGU_LAYERS = DEPTH // 2
ATTN_HEAD_DIM = 128
ATTN_HEADS = D_MODEL // (2 * ATTN_HEAD_DIM)
ATTN_V_DIM = 2 * ATTN_HEAD_DIM
QK_WIDTH = ATTN_HEADS * 2 * ATTN_HEAD_DIM
V_WIDTH = ATTN_HEADS * ATTN_V_DIM
QKV_WIDTH = 2 * QK_WIDTH + V_WIDTH
Q_BLOCK = 128
SGU_WIDTH = 3 * D_MODEL
SGU_GROUPS = 8
SGU_GROUP_DIM = SGU_WIDTH // SGU_GROUPS
SGU_CHUNK = 128
N_EXPERTS = 64
TOP_K = 8
N_EXPERT_GROUPS = 8
TOPK_GROUPS = 4
EXPERT_DIM = 512
SHARED_DIM = 512
ROUTED_SCALE = 2.5
MOE_BLOCK = 128
EPS = 1e-6

kernel_name = 'hybrid_diffattn_sgu_moe_adaln_stream_step'


def rms_norm(x, g):
    xf = x.astype(jnp.float32)
    y = xf * lax.rsqrt(jnp.mean(xf * xf, axis=-1, keepdims=True) + EPS)
    return (y * g.astype(jnp.float32)).astype(x.dtype)


def layer_norm(x, g, b):
    xf = x.astype(jnp.float32)
    mu = jnp.mean(xf, axis=-1, keepdims=True)
    xc = xf - mu
    y = xc * lax.rsqrt(jnp.mean(xc * xc, axis=-1, keepdims=True) + EPS)
    return (y * g.astype(jnp.float32) + b.astype(jnp.float32)).astype(x.dtype)


def ada_modulation(c, w, b):
    mod = jax.nn.silu(c) @ w + b
    return [m[:, None, :] for m in jnp.split(mod, 6, axis=-1)]


def modulate(x, g, shift, scale):
    return rms_norm(x, g) * (1 + scale) + shift


def alibi_slopes():
    return 2.0 ** (-8.0 * jnp.arange(1, ATTN_HEADS + 1, dtype=jnp.float32) / ATTN_HEADS)


def diff_attend(q, k, v, q_pos, k_pos, lam):
    s = jnp.einsum('bqhmd,bkhmd->bhmqk', q, k, preferred_element_type=jnp.float32) * (ATTN_HEAD_DIM ** -0.5)
    dist = jnp.abs(q_pos[:, None] - k_pos[None, :]).astype(jnp.float32)
    bias = -alibi_slopes()[:, None, None, None] * dist
    visible = (k_pos[None, :] // CHUNK) <= (q_pos[:, None] // CHUNK)
    s = jnp.where(visible, s + bias, -jnp.inf)
    p = jax.nn.softmax(s, axis=-1)
    a = p[:, :, 0] - lam * p[:, :, 1]
    return jnp.einsum('bhqk,bkhd->bqhd', a.astype(v.dtype), v)


def diff_attention(h, q_pos, past_k, past_v, lam_init, w_qkv, w_o, q_norm, k_norm, lq1, lk1, lq2, lk2, subln_g):
    B, T, _ = h.shape
    qkv = h @ w_qkv
    q = rms_norm(qkv[..., :QK_WIDTH].reshape(B, T, ATTN_HEADS, 2, ATTN_HEAD_DIM), q_norm)
    k = rms_norm(qkv[..., QK_WIDTH:2 * QK_WIDTH].reshape(B, T, ATTN_HEADS, 2, ATTN_HEAD_DIM), k_norm)
    v = qkv[..., 2 * QK_WIDTH:].reshape(B, T, ATTN_HEADS, ATTN_V_DIM)
    f32 = jnp.float32
    lam = (jnp.exp(jnp.sum(lq1.astype(f32) * lk1.astype(f32)))
           - jnp.exp(jnp.sum(lq2.astype(f32) * lk2.astype(f32))) + lam_init)
    if past_k is None:
        nb = T // Q_BLOCK
        qb = jnp.moveaxis(q.reshape(B, nb, Q_BLOCK, ATTN_HEADS, 2, ATTN_HEAD_DIM), 1, 0)
        pb = q_pos.reshape(nb, Q_BLOCK)
        o = lax.map(lambda a: diff_attend(a[0], k, v, a[1], q_pos, lam), (qb, pb))
        o = jnp.moveaxis(o, 0, 1).reshape(B, T, ATTN_HEADS, ATTN_V_DIM)
    else:
        k_all = jnp.concatenate([past_k, k], axis=1)
        v_all = jnp.concatenate([past_v, v], axis=1)
        k_pos = jnp.arange(k_all.shape[1], dtype=jnp.int32)
        o = diff_attend(q, k_all, v_all, q_pos, k_pos, lam)
    o = rms_norm(o, subln_g) * (1 - lam_init)
    y = o.reshape(B, T, V_WIDTH) @ w_o
    return y, k, v


def sgu_mixer(h, chunk_len, w_in, ln_g, ln_b, w_s, b_s, w_o):
    B, T, _ = h.shape
    nc = T // chunk_len
    z = jax.nn.gelu(h @ w_in, approximate=False)
    u, v = z[..., :SGU_WIDTH], z[..., SGU_WIDTH:]
    v = layer_norm(v, ln_g, ln_b)
    tril = jnp.tril(jnp.ones((chunk_len, chunk_len), dtype=bool))
    w = jnp.where(tril, w_s[:, :chunk_len, :chunk_len], 0).astype(v.dtype)
    vg = v.reshape(B, nc, chunk_len, SGU_GROUPS, SGU_GROUP_DIM)
    mixed = jnp.einsum('gts,bnsgd->bntgd', w, vg) + b_s[:, :chunk_len].T[None, None, :, :, None]
    y = (u * mixed.reshape(B, T, SGU_WIDTH)) @ w_o
    return y, v


def swiglu(x, w_gate, w_up, w_down):
    return (jax.nn.silu(x @ w_gate) * (x @ w_up)) @ w_down


def moe_ffn(h, w_router, b_router, w_gate, w_up, w_down, sw_gate, sw_up, sw_down):
    T, D = h.shape
    per = N_EXPERTS // N_EXPERT_GROUPS
    s = jax.nn.sigmoid(h.astype(jnp.float32) @ w_router.astype(jnp.float32))
    sb = s + b_router.astype(jnp.float32)
    grp_score = lax.top_k(sb.reshape(T, N_EXPERT_GROUPS, per), 2)[0].sum(-1)
    _, gidx = lax.top_k(grp_score, TOPK_GROUPS)
    gmask = jax.nn.one_hot(gidx, N_EXPERT_GROUPS, dtype=jnp.float32).sum(1) > 0
    sel = jnp.where(jnp.repeat(gmask, per, axis=1), sb, -jnp.inf)
    _, idx = lax.top_k(sel, TOP_K)
    wts = jnp.take_along_axis(s, idx, axis=1)
    wts = wts / jnp.sum(wts, axis=-1, keepdims=True) * ROUTED_SCALE
    n_assign = T * TOP_K
    e_flat = idx.reshape(-1)
    tok_flat = jnp.arange(n_assign, dtype=jnp.int32) // TOP_K
    counts = jnp.zeros((N_EXPERTS,), jnp.int32).at[e_flat].add(1)
    padded = (counts + MOE_BLOCK - 1) // MOE_BLOCK * MOE_BLOCK
    pad_end = jnp.cumsum(padded)
    pad_start = pad_end - padded
    grp_start = jnp.cumsum(counts) - counts
    order = jnp.argsort(e_flat)
    e_sorted = e_flat[order]
    tok_sorted = tok_flat[order]
    w_sorted = wts.reshape(-1)[order]
    dest = pad_start[e_sorted] + jnp.arange(n_assign, dtype=jnp.int32) - grp_start[e_sorted]
    n_blocks = -(-(n_assign + N_EXPERTS * (MOE_BLOCK - 1)) // MOE_BLOCK)
    buf = jnp.zeros((n_blocks * MOE_BLOCK, D), h.dtype).at[dest].set(h[tok_sorted])
    blk_expert = jnp.minimum(
        jnp.searchsorted(pad_end, jnp.arange(n_blocks, dtype=jnp.int32) * MOE_BLOCK, side='right'),
        N_EXPERTS - 1)

    def run_block(args):
        xb, e = args
        return swiglu(xb, w_gate[e], w_up[e], w_down[e])

    out = lax.map(run_block, (buf.reshape(n_blocks, MOE_BLOCK, D), blk_expert)).reshape(-1, D)
    routed = jax.ops.segment_sum(out[dest] * w_sorted[:, None].astype(h.dtype), tok_sorted, num_segments=T)
    return swiglu(h, sw_gate, sw_up, sw_down) + routed


def setup_inputs(seed: int = 0) -> dict:
    key = jax.random.key(seed)
    keys = iter(jax.random.split(key, 40))

    def normal(shape, scale):
        return jax.random.normal(next(keys), shape, jnp.float32) * scale

    def gain(shape):
        return 1.0 + normal(shape, 0.1)

    D = D_MODEL
    return {
        'x_prompt': normal((BATCH, SEQ, D), 1.0),
        'x_sample': normal((DEC_BATCH, DEC_SEQ, D), 1.0),
        'cache_k_attn': normal((N_ATTN_LAYERS, DEC_BATCH, PAST_LEN, ATTN_HEADS, 2, ATTN_HEAD_DIM), 1.0),
        'cache_v_attn': normal((N_ATTN_LAYERS, DEC_BATCH, PAST_LEN, ATTN_HEADS, ATTN_V_DIM), 1.0),
        'c_prompt': normal((BATCH, D), 1.0),
        'c_sample': normal((DEC_BATCH, D), 1.0),
        'norm_mix_g': gain((DEPTH, D)),
        'norm_ffn_g': gain((DEPTH, D)),
        'ada_w': normal((DEPTH, D, 6 * D), 0.5 * D ** -0.5),
        'ada_b': normal((DEPTH, 6 * D), 0.02),
        'attn_w_qkv': normal((N_ATTN_LAYERS, D, QKV_WIDTH), D ** -0.5),
        'attn_w_o': normal((N_ATTN_LAYERS, V_WIDTH, D), V_WIDTH ** -0.5),
        'attn_q_norm': gain((N_ATTN_LAYERS, ATTN_HEAD_DIM)),
        'attn_k_norm': gain((N_ATTN_LAYERS, ATTN_HEAD_DIM)),
        'attn_lambda_q1': normal((N_ATTN_LAYERS, ATTN_HEAD_DIM), 0.1),
        'attn_lambda_k1': normal((N_ATTN_LAYERS, ATTN_HEAD_DIM), 0.1),
        'attn_lambda_q2': normal((N_ATTN_LAYERS, ATTN_HEAD_DIM), 0.1),
        'attn_lambda_k2': normal((N_ATTN_LAYERS, ATTN_HEAD_DIM), 0.1),
        'attn_subln_g': gain((N_ATTN_LAYERS, ATTN_V_DIM)),
        'sgu_w_in': normal((N_SGU_LAYERS, D, 2 * SGU_WIDTH), D ** -0.5),
        'sgu_ln_g': gain((N_SGU_LAYERS, SGU_WIDTH)),
        'sgu_ln_b': normal((N_SGU_LAYERS, SGU_WIDTH), 0.02),
        'sgu_w_s': normal((N_SGU_LAYERS, SGU_GROUPS, SGU_CHUNK, SGU_CHUNK), SGU_CHUNK ** -0.5),
        'sgu_b_s': gain((N_SGU_LAYERS, SGU_GROUPS, SGU_CHUNK)),
        'sgu_w_o': normal((N_SGU_LAYERS, SGU_WIDTH, D), SGU_WIDTH ** -0.5),
        'moe_w_router': normal((DEPTH, D, N_EXPERTS), D ** -0.5),
        'moe_b_router': normal((DEPTH, N_EXPERTS), 0.01),
        'moe_w_gate': normal((DEPTH, N_EXPERTS, D, EXPERT_DIM), D ** -0.5),
        'moe_w_up': normal((DEPTH, N_EXPERTS, D, EXPERT_DIM), D ** -0.5),
        'moe_w_down': normal((DEPTH, N_EXPERTS, EXPERT_DIM, D), EXPERT_DIM ** -0.5),
        'shared_w_gate': normal((DEPTH, D, SHARED_DIM), D ** -0.5),
        'shared_w_up': normal((DEPTH, D, SHARED_DIM), D ** -0.5),
        'shared_w_down': normal((DEPTH, SHARED_DIM, D), SHARED_DIM ** -0.5),
    }


def reference(x_prompt, x_sample, cache_k_attn, cache_v_attn, c_prompt, c_sample,
              norm_mix_g, norm_ffn_g, ada_w, ada_b,
              attn_w_qkv, attn_w_o, attn_q_norm, attn_k_norm,
              attn_lambda_q1, attn_lambda_k1, attn_lambda_q2, attn_lambda_k2, attn_subln_g,
              sgu_w_in, sgu_ln_g, sgu_ln_b, sgu_w_s, sgu_b_s, sgu_w_o,
              moe_w_router, moe_b_router, moe_w_gate, moe_w_up, moe_w_down,
              shared_w_gate, shared_w_up, shared_w_down):
    seq_p = x_prompt.shape[1]
    seq_s = x_sample.shape[1]
    past = cache_k_attn.shape[2]
    pos_p = jnp.arange(seq_p, dtype=jnp.int32)
    pos_s = past + jnp.arange(seq_s, dtype=jnp.int32)
    xp, xs = x_prompt, x_sample
    k_prompt, v_prompt, k_sample, v_sample, sgu_sample = [], [], [], [], []
    for i in range(DEPTH):
        j = i // N_MIXERS
        mp = ada_modulation(c_prompt, ada_w[i], ada_b[i])
        ms = ada_modulation(c_sample, ada_w[i], ada_b[i])
        hp = modulate(xp, norm_mix_g[i], mp[0], mp[1])
        hs = modulate(xs, norm_mix_g[i], ms[0], ms[1])
        if i % N_MIXERS == 0:
            attn_params = (attn_w_qkv[j], attn_w_o[j], attn_q_norm[j], attn_k_norm[j],
                           attn_lambda_q1[j], attn_lambda_k1[j], attn_lambda_q2[j], attn_lambda_k2[j],
                           attn_subln_g[j])
            lam_init = 0.8 - 0.6 * math.exp(-0.3 * i)
            yp, kp, vp = diff_attention(hp, pos_p, None, None, lam_init, *attn_params)
            ys, kn, vn = diff_attention(hs, pos_s, cache_k_attn[j], cache_v_attn[j], lam_init, *attn_params)
            k_prompt.append(kp)
            v_prompt.append(vp)
            k_sample.append(kn)
            v_sample.append(vn)
        else:
            sgu_params = (sgu_w_in[j], sgu_ln_g[j], sgu_ln_b[j], sgu_w_s[j], sgu_b_s[j], sgu_w_o[j])
            yp, _ = sgu_mixer(hp, SGU_CHUNK, *sgu_params)
            ys, vn = sgu_mixer(hs, seq_s, *sgu_params)
            sgu_sample.append(vn)
        xp = xp + mp[2] * yp
        xs = xs + ms[2] * ys
        moe_params = (moe_w_router[i], moe_b_router[i], moe_w_gate[i], moe_w_up[i], moe_w_down[i],
                      shared_w_gate[i], shared_w_up[i], shared_w_down[i])
        fp = moe_ffn(modulate(xp, norm_ffn_g[i], mp[3], mp[4]).reshape(-1, D_MODEL), *moe_params)
        fs = moe_ffn(modulate(xs, norm_ffn_g[i], ms[3], ms[4]).reshape(-1, D_MODEL), *moe_params)
        xp = xp + mp[5] * fp.reshape(xp.shape)
        xs = xs + ms[5] * fs.reshape(xs.shape)
    return (xp, xs, jnp.stack(k_prompt), jnp.stack(v_prompt), jnp.stack(k_sample), jnp.stack(v_sample),
            jnp.stack(sgu_sample))
```

```python
import functools
import math

import jax
import jax.numpy as jnp
from jax import lax
from jax.experimental import pallas as pl
from jax.experimental.pallas import tpu as pltpu

F32 = jnp.float32
BF16 = jnp.bfloat16

EPS = 1e-6
CHUNK = 64
N_MIXERS = 2
ATTN_HEAD_DIM = 128
SGU_GROUPS = 8
SGU_CHUNK = 128
TOP_K = 8
N_EXPERT_GROUPS = 8
TOPK_GROUPS = 4
ROUTED_SCALE = 2.5
LOG2E = 1.4426950408889634
NEG_BIG = -1e30

VMEM_LIMIT = 56 * 1024 * 1024
MOE_BLOCK = 256


def _params(n_axes):
    return pltpu.CompilerParams(dimension_semantics=("arbitrary",) * n_axes, vmem_limit_bytes=VMEM_LIMIT)


def _pick(n, pref):
    if n <= pref:
        return n
    b = pref
    while n % b:
        b //= 2
    return b


def _ada_kernel(c_ref, w_ref, b_ref, o_ref):
    c = c_ref[...]
    x = (c * jax.nn.sigmoid(c)).astype(BF16)
    o_ref[...] = jnp.dot(x, w_ref[...].astype(BF16), preferred_element_type=F32) + b_ref[...]


def ada_modulation(c, ada_w, ada_b):
    n_layers, d, n = ada_w.shape
    r = c.shape[0]
    bn = _pick(n, 1024)
    return pl.pallas_call(
        _ada_kernel,
        grid=(n_layers, n // bn),
        in_specs=[
            pl.BlockSpec((r, d), lambda l, j: (0, 0)),
            pl.BlockSpec((None, d, bn), lambda l, j: (l, 0, j)),
            pl.BlockSpec((None, 1, bn), lambda l, j: (l, 0, j)),
        ],
        out_specs=pl.BlockSpec((None, r, bn), lambda l, j: (l, 0, j)),
        out_shape=jax.ShapeDtypeStruct((n_layers, r, n), F32),
        compiler_params=_params(2),
        name="ada_modulation",
    )(c, ada_w, ada_b.reshape(n_layers, 1, n))


def _modulated(x, g, shift, scale):
    y = x * lax.rsqrt(jnp.mean(x * x, axis=-1, keepdims=True) + EPS)
    return y * g * (1.0 + scale) + shift


def _modulate_kernel(x_ref, g_ref, sh_ref, sc_ref, o_ref):
    o_ref[...] = _modulated(x_ref[...], g_ref[...], sh_ref[...], sc_ref[...]).astype(o_ref.dtype)


def _split3(x):
    hi = x.astype(BF16)
    lo = (x - hi.astype(F32)).astype(BF16)
    return hi, lo


def _modulate_router_kernel(x_ref, g_ref, sh_ref, sc_ref, wr_ref, o_ref, s_ref):
    h = _modulated(x_ref[...], g_ref[...], sh_ref[...], sc_ref[...])
    o_ref[...] = h.astype(o_ref.dtype)
    h_hi, h_lo = _split3(h)
    w_hi, w_lo = _split3(wr_ref[...])
    logits = (jnp.dot(h_hi, w_hi, preferred_element_type=F32)
              + jnp.dot(h_lo, w_hi, preferred_element_type=F32)
              + jnp.dot(h_hi, w_lo, preferred_element_type=F32))
    s_ref[...] = jax.nn.sigmoid(logits)


def _row_block(stream, m, pref):
    return _pick(m if stream["per_row"] else stream["rows_per_batch"], pref)


def _mod_spec(stream, bm, bn, ij):
    if stream["per_row"]:
        return pl.BlockSpec((None, bm, bn), lambda *g: (0,) + tuple(ij(*g)))
    bpg = stream["rows_per_batch"] // bm
    return pl.BlockSpec((None, 1, bn), lambda *g: (ij(*g)[0] // bpg, 0, ij(*g)[1]))


def modulate(stream, x, g, shift, scale, w_router=None):
    m, d = x.shape
    bm = _row_block(stream, m, 512)
    row_spec = pl.BlockSpec((bm, d), lambda i: (i, 0))
    mod_spec = _mod_spec(stream, bm, d, lambda i: (i, 0))
    g_spec = pl.BlockSpec((1, d), lambda i: (0, 0))
    if w_router is None:
        return pl.pallas_call(
            _modulate_kernel, grid=(m // bm,),
            in_specs=[row_spec, g_spec, mod_spec, mod_spec], out_specs=row_spec,
            out_shape=jax.ShapeDtypeStruct((m, d), BF16),
            compiler_params=_params(1), name="modulate",
        )(x, g.reshape(1, d), shift, scale)
    e = w_router.shape[1]
    return pl.pallas_call(
        _modulate_router_kernel, grid=(m // bm,),
        in_specs=[row_spec, g_spec, mod_spec, mod_spec, pl.BlockSpec((d, e), lambda i: (0, 0))],
        out_specs=[row_spec, pl.BlockSpec((bm, e), lambda i: (i, 0))],
        out_shape=[jax.ShapeDtypeStruct((m, d), BF16), jax.ShapeDtypeStruct((m, e), F32)],
        compiler_params=_params(1), name="modulate_router",
    )(x, g.reshape(1, d), shift, scale, w_router)


def _mm_kernel(x_ref, w_ref, *refs, n_extra, epilogue):
    extra, outs, w_bf = refs[:n_extra], refs[n_extra:-1], refs[-1]

    @pl.when(pl.program_id(1) == 0)
    def _():
        w_bf[...] = w_ref[...].astype(BF16)

    acc = jnp.dot(x_ref[...], w_bf[...], preferred_element_type=F32)
    epilogue(acc, extra, outs)


def matmul(x, w, *, bm, bn, col_block_off=0, n_cols=None, extra=(), extra_specs=(), out_dtypes, epilogue, name):
    m, k = x.shape
    n = w.shape[1] if n_cols is None else n_cols
    grid = (n // bn, m // bm)
    out_spec = pl.BlockSpec((bm, bn), lambda j, i: (i, j))
    return pl.pallas_call(
        functools.partial(_mm_kernel, n_extra=len(extra), epilogue=epilogue),
        grid=grid,
        in_specs=[pl.BlockSpec((bm, k), lambda j, i: (i, 0)),
                  pl.BlockSpec((k, bn), lambda j, i: (0, j + col_block_off))] + list(extra_specs),
        out_specs=[out_spec] * len(out_dtypes),
        out_shape=[jax.ShapeDtypeStruct((m, n), dt) for dt in out_dtypes],
        scratch_shapes=[pltpu.VMEM((k, bn), BF16)],
        compiler_params=_params(2), name=name,
    )(x, w, *extra)


def _head_rms(acc, gain, post_scale):
    pieces = []
    for c in range(acc.shape[1] // ATTN_HEAD_DIM):
        seg = acc[:, c * ATTN_HEAD_DIM:(c + 1) * ATTN_HEAD_DIM]
        y = seg * lax.rsqrt(jnp.mean(seg * seg, axis=-1, keepdims=True) + EPS)
        pieces.append(y * (gain * post_scale))
    return jnp.concatenate(pieces, axis=-1)


def _q_epilogue(acc, extra, outs):
    outs[0][...] = _head_rms(acc, extra[0][...], ATTN_HEAD_DIM ** -0.5 * LOG2E).astype(BF16)


def _k_epilogue(acc, extra, outs):
    kn = _head_rms(acc, extra[0][...], 1.0)
    outs[0][...] = kn
    outs[1][...] = kn.astype(BF16)


def _v_epilogue(acc, extra, outs):
    outs[0][...] = acc
    outs[1][...] = acc.astype(BF16)


def _gelu_epilogue(acc, extra, outs):
    outs[0][...] = (0.5 * acc * (1.0 + lax.erf(acc * (2.0 ** -0.5)))).astype(BF16)


def _residual_epilogue(acc, extra, outs):
    x_ref, gate_ref = extra
    outs[0][...] = x_ref[...] + gate_ref[...] * acc


def qkv_project(stream, h, w_qkv, q_norm, k_norm):
    m, d = h.shape
    bm = _row_block(stream, m, 1024)
    bn = _pick(d, 1024)
    nb = d // bn
    gain_spec = pl.BlockSpec((1, ATTN_HEAD_DIM), lambda j, i: (0, 0))
    common = dict(bm=bm, bn=bn, n_cols=d)
    (q_bf,) = matmul(h, w_qkv, col_block_off=0, extra=(q_norm.reshape(1, -1),), extra_specs=(gain_spec,),
                     out_dtypes=(BF16,), epilogue=_q_epilogue, name="q_proj", **common)
    k32, k_bf = matmul(h, w_qkv, col_block_off=nb, extra=(k_norm.reshape(1, -1),), extra_specs=(gain_spec,),
                       out_dtypes=(F32, BF16), epilogue=_k_epilogue, name="k_proj", **common)
    v32, v_bf = matmul(h, w_qkv, col_block_off=2 * nb, out_dtypes=(F32, BF16), epilogue=_v_epilogue,
                       name="v_proj", **common)
    return q_bf, k32, k_bf, v32, v_bf


def project_residual(stream, a, w, x, gate, name):
    k, n = w.shape
    deep = k > 4096
    bm = _row_block(stream, a.shape[0], 512 if deep else 1024)
    bn = _pick(n, 512 if deep else 1024)
    (out,) = matmul(a, w, bm=bm, bn=bn, extra=(x, gate),
                    extra_specs=(pl.BlockSpec((bm, bn), lambda j, i: (i, j)),
                                 _mod_spec(stream, bm, bn, lambda j, i: (i, j))),
                    out_dtypes=(F32,), epilogue=_residual_epilogue, name=name)
    return out


def _finish_heads(acc1, l1, acc2, l2, lam, g, post_scale):
    o = acc1 / l1 - lam * (acc2 / l2)
    o = o * lax.rsqrt(jnp.mean(o * o, axis=-1, keepdims=True) + EPS)
    return o * (g * post_scale)


def _attn_prompt_kernel(lam_ref, slope_ref, q_ref, k_ref, v_ref, g_ref, o_ref, *, bq, post_scale):
    hd = ATTN_HEAD_DIM
    qi = pl.program_id(2)
    sl2 = slope_ref[pl.program_id(1)]
    lam = lam_ref[0]
    q = q_ref[...]
    qs = (q[:, :hd], q[:, hd:])

    def scores(qm, k_blk):
        return lax.dot_general(qm, k_blk, (((1,), (1,)), ((), ())), preferred_element_type=F32)

    d0 = pl.multiple_of(qi * bq, bq)
    k_d = k_ref[pl.ds(d0, bq), :]
    v_d = v_ref[pl.ds(d0, bq), :]
    row = lax.broadcasted_iota(jnp.int32, (bq, bq), 0)
    col = lax.broadcasted_iota(jnp.int32, (bq, bq), 1)
    visible = (col // CHUNK) <= (row // CHUNK)
    bias_d = sl2 * (row - jnp.abs(row - col)).astype(F32)
    state = []
    for mi in range(2):
        s = jnp.where(visible, scores(qs[mi], k_d[:, mi * hd:(mi + 1) * hd]) + bias_d, NEG_BIG)
        m = jnp.max(s, axis=-1, keepdims=True)
        p = jnp.exp2(s - m)
        l = jnp.sum(p, axis=-1, keepdims=True)
        acc = jnp.dot(p.astype(BF16), v_d, preferred_element_type=F32)
        state += [m, l, acc]

    col_off = lax.broadcasted_iota(jnp.int32, (1, bq), 1)

    def body(kj, carry):
        k0 = pl.multiple_of(kj * bq, bq)
        k_b = k_ref[pl.ds(k0, bq), :]
        v_b = v_ref[pl.ds(k0, bq), :]
        bias = sl2 * (col_off + (kj - qi) * bq).astype(F32)
        new = []
        for mi in range(2):
            m_old, l_old, acc_old = carry[3 * mi:3 * mi + 3]
            s = scores(qs[mi], k_b[:, mi * hd:(mi + 1) * hd]) + bias
            m_new = jnp.maximum(m_old, jnp.max(s, axis=-1, keepdims=True))
            alpha = jnp.exp2(m_old - m_new)
            p = jnp.exp2(s - m_new)
            l_new = alpha * l_old + jnp.sum(p, axis=-1, keepdims=True)
            acc_new = alpha * acc_old + jnp.dot(p.astype(BF16), v_b, preferred_element_type=F32)
            new += [m_new, l_new, acc_new]
        return tuple(new)

    m1, l1, acc1, m2, l2, acc2 = lax.fori_loop(0, qi, body, tuple(state))
    o_ref[...] = _finish_heads(acc1, l1, acc2, l2, lam, g_ref[...], post_scale).astype(o_ref.dtype)


def attention_prompt(q_bf, k_bf, v_bf, batch, lam, slopes2, subln_g, post_scale):
    m, width = q_bf.shape
    s = m // batch
    dv = 2 * ATTN_HEAD_DIM
    heads = width // dv
    bq = _pick(s, 512)
    assert bq % CHUNK == 0
    nq = s // bq
    kv_spec = pl.BlockSpec((s, dv), lambda b, h, i: (b, h))
    q_spec = pl.BlockSpec((bq, dv), lambda b, h, i: (b * nq + i, h))
    smem = pl.BlockSpec(memory_space=pltpu.SMEM)
    return pl.pallas_call(
        functools.partial(_attn_prompt_kernel, bq=bq, post_scale=post_scale),
        grid=(batch, heads, nq),
        in_specs=[smem, smem, q_spec, kv_spec, kv_spec, pl.BlockSpec((1, dv), lambda b, h, i: (0, 0))],
        out_specs=q_spec,
        out_shape=jax.ShapeDtypeStruct((m, width), BF16),
        compiler_params=_params(3), name="attention_prompt",
    )(lam, slopes2, q_bf, k_bf, v_bf, subln_g.reshape(1, dv))


def _attn_sample_kernel(lam_ref, slope_ref, q_ref, kn_ref, vn_ref, ck_ref, cv_ref, g_ref, o_ref, *, post_scale):
    hd = ATTN_HEAD_DIM
    sl2 = slope_ref[pl.program_id(1)]
    lam = lam_ref[0]
    t = q_ref.shape[0]
    past = ck_ref.shape[0]
    q = q_ref[...]
    k_new = kn_ref[...]
    v_new = vn_ref[...]
    k_past = ck_ref[...].astype(BF16)
    v_past = cv_ref[...].astype(BF16)

    def bias_mask(n_keys, key0):
        q_pos = past + lax.broadcasted_iota(jnp.int32, (t, n_keys), 0)
        k_pos = key0 + lax.broadcasted_iota(jnp.int32, (t, n_keys), 1)
        return -sl2 * jnp.abs(q_pos - k_pos).astype(F32), (k_pos // CHUNK) <= (q_pos // CHUNK)

    bias_p, vis_p = bias_mask(past, 0)
    bias_n, vis_n = bias_mask(t, past)
    res = []
    for mi in range(2):
        qm = q[:, mi * hd:(mi + 1) * hd]
        dn = (((1,), (1,)), ((), ()))
        s_p = lax.dot_general(qm, k_past[:, mi * hd:(mi + 1) * hd], dn, preferred_element_type=F32)
        s_n = lax.dot_general(qm, k_new[:, mi * hd:(mi + 1) * hd], dn, preferred_element_type=F32)
        s_p = jnp.where(vis_p, s_p + bias_p, NEG_BIG)
        s_n = jnp.where(vis_n, s_n + bias_n, NEG_BIG)
        m = jnp.maximum(jnp.max(s_p, axis=-1, keepdims=True), jnp.max(s_n, axis=-1, keepdims=True))
        p_p = jnp.exp2(s_p - m)
        p_n = jnp.exp2(s_n - m)
        l = jnp.sum(p_p, axis=-1, keepdims=True) + jnp.sum(p_n, axis=-1, keepdims=True)
        acc = (jnp.dot(p_p.astype(BF16), v_past, preferred_element_type=F32)
               + jnp.dot(p_n.astype(BF16), v_new, preferred_element_type=F32))
        res += [acc, l]
    o_ref[...] = _finish_heads(res[0], res[1], res[2], res[3], lam, g_ref[...], post_scale).astype(o_ref.dtype)


def attention_sample(q_bf, k_bf, v_bf, cache_k, cache_v, lam, slopes2, subln_g, post_scale):
    batch, past, width = cache_k.shape
    m = q_bf.shape[0]
    t = m // batch
    dv = 2 * ATTN_HEAD_DIM
    heads = width // dv
    new_spec = pl.BlockSpec((t, dv), lambda b, h: (b, h))
    cache_spec = pl.BlockSpec((None, past, dv), lambda b, h: (b, 0, h))
    smem = pl.BlockSpec(memory_space=pltpu.SMEM)
    return pl.pallas_call(
        functools.partial(_attn_sample_kernel, post_scale=post_scale),
        grid=(batch, heads),
        in_specs=[smem, smem, new_spec, new_spec, new_spec, cache_spec, cache_spec,
                  pl.BlockSpec((1, dv), lambda b, h: (0, 0))],
        out_specs=new_spec,
        out_shape=jax.ShapeDtypeStruct((m, width), BF16),
        compiler_params=_params(2), name="attention_sample",
    )(lam, slopes2, q_bf, k_bf, v_bf, cache_k, cache_v, subln_g.reshape(1, dv))


def _sgu_gate_kernel(u_ref, v_ref, g_ref, b_ref, w_ref, bs_ref, *out_refs, chunk_len, write_v):
    bt, width = u_ref.shape
    gd = width // SGU_GROUPS
    v = v_ref[...].astype(F32)
    mu = jnp.mean(v, axis=-1, keepdims=True)
    vc = v - mu
    vn = vc * lax.rsqrt(jnp.mean(vc * vc, axis=-1, keepdims=True) + EPS) * g_ref[...] + b_ref[...]
    if write_v:
        out_refs[1][...] = vn
    vn_bf = vn.astype(BF16)
    row = lax.broadcasted_iota(jnp.int32, (SGU_CHUNK, SGU_CHUNK), 0)
    col = lax.broadcasted_iota(jnp.int32, (SGU_CHUNK, SGU_CHUNK), 1)
    keep = ((row // chunk_len) == (col // chunk_len)) & ((col % chunk_len) <= (row % chunk_len))
    for gi in range(SGU_GROUPS):
        w = jnp.where(keep, w_ref[gi], 0.0).astype(BF16)
        bias = bs_ref[gi]
        for c in range(bt // SGU_CHUNK):
            rows = slice(c * SGU_CHUNK, (c + 1) * SGU_CHUNK)
            cols = slice(gi * gd, (gi + 1) * gd)
            mixed = jnp.dot(w, vn_bf[rows, cols], preferred_element_type=F32) + bias
            out_refs[0][rows, cols] = (u_ref[rows, cols].astype(F32) * mixed).astype(BF16)


def sgu_gate(z, ln_g, ln_b, w_s, b_s, chunk_len, write_v):
    m, two_w = z.shape
    width = two_w // 2
    reps = SGU_CHUNK // chunk_len
    w_t = jnp.tile(w_s[:, :chunk_len, :chunk_len], (1, reps, reps))
    b_t = jnp.tile(b_s[:, :chunk_len], (1, reps))[:, :, None]
    bt = _pick(m, 256)
    row_spec = lambda c: pl.BlockSpec((bt, width), lambda i, c=c: (i, c))
    vec_spec = pl.BlockSpec((1, width), lambda i: (0, 0))
    out_specs = [row_spec(0)]
    out_shape = [jax.ShapeDtypeStruct((m, width), BF16)]
    if write_v:
        out_specs.append(row_spec(0))
        out_shape.append(jax.ShapeDtypeStruct((m, width), F32))
    return pl.pallas_call(
        functools.partial(_sgu_gate_kernel, chunk_len=chunk_len, write_v=write_v),
        grid=(m // bt,),
        in_specs=[row_spec(0), row_spec(1), vec_spec, vec_spec,
                  pl.BlockSpec((SGU_GROUPS, SGU_CHUNK, SGU_CHUNK), lambda i: (0, 0, 0)),
                  pl.BlockSpec((SGU_GROUPS, SGU_CHUNK, 1), lambda i: (0, 0, 0))],
        out_specs=out_specs, out_shape=out_shape,
        compiler_params=_params(1), name="sgu_gate",
    )(z, z, ln_g.reshape(1, width), ln_b.reshape(1, width), w_t, b_t)


def _swiglu(x, wg, wu, wd):
    g = jnp.dot(x, wg, preferred_element_type=F32)
    u = jnp.dot(x, wu, preferred_element_type=F32)
    h = (g * jax.nn.sigmoid(g) * u).astype(BF16)
    return jnp.dot(h, wd, preferred_element_type=F32)


def _moe_kernel(blk_e_ref, n_used_ref, x_ref, wg_ref, wu_ref, wd_ref, o_ref, wg_bf, wu_bf, wd_bf):
    b = pl.program_id(0)
    e = blk_e_ref[b]
    e_prev = blk_e_ref[jnp.maximum(b - 1, 0)]

    @pl.when((b == 0) | (e != e_prev))
    def _():
        wg_bf[...] = wg_ref[...].astype(BF16)
        wu_bf[...] = wu_ref[...].astype(BF16)
        wd_bf[...] = wd_ref[...].astype(BF16)

    @pl.when(b < n_used_ref[0])
    def _():
        o_ref[...] = _swiglu(x_ref[...], wg_bf[...], wu_bf[...], wd_bf[...]).astype(o_ref.dtype)

    @pl.when(b >= n_used_ref[0])
    def _():
        o_ref[...] = jnp.zeros_like(o_ref)


def moe_experts(xg, blk_expert, n_used, w_gate, w_up, w_down):
    rows, d = xg.shape
    ed = w_gate.shape[2]
    n_blocks = rows // MOE_BLOCK
    x_spec = pl.BlockSpec((MOE_BLOCK, d), lambda b, be, nu: (b, 0))
    return pl.pallas_call(
        _moe_kernel,
        grid_spec=pltpu.PrefetchScalarGridSpec(
            num_scalar_prefetch=2, grid=(n_blocks,),
            in_specs=[x_spec,
                      pl.BlockSpec((None, d, ed), lambda b, be, nu: (be[b], 0, 0)),
                      pl.BlockSpec((None, d, ed), lambda b, be, nu: (be[b], 0, 0)),
                      pl.BlockSpec((None, ed, d), lambda b, be, nu: (be[b], 0, 0))],
            out_specs=x_spec,
            scratch_shapes=[pltpu.VMEM((d, ed), BF16), pltpu.VMEM((d, ed), BF16), pltpu.VMEM((ed, d), BF16)]),
        out_shape=jax.ShapeDtypeStruct((rows, d), BF16),
        compiler_params=_params(1), name="moe_experts",
    )(blk_expert, n_used, xg, w_gate, w_up, w_down)


def _shared_kernel(h_ref, r_ref, x_ref, gate_ref, wg_ref, wu_ref, wd_ref, o_ref, wg_bf, wu_bf, wd_bf):
    @pl.when(pl.program_id(0) == 0)
    def _():
        wg_bf[...] = wg_ref[...].astype(BF16)
        wu_bf[...] = wu_ref[...].astype(BF16)
        wd_bf[...] = wd_ref[...].astype(BF16)

    ffn = _swiglu(h_ref[...], wg_bf[...], wu_bf[...], wd_bf[...]) + r_ref[...]
    o_ref[...] = x_ref[...] + gate_ref[...] * ffn


def shared_expert_residual(stream, h, routed, x, gate, sw_gate, sw_up, sw_down):
    m, d = x.shape
    ed = sw_gate.shape[1]
    bm = _row_block(stream, m, 256)
    row_spec = pl.BlockSpec((bm, d), lambda i: (i, 0))
    full = lambda a, b: pl.BlockSpec((a, b), lambda i: (0, 0))
    return pl.pallas_call(
        _shared_kernel, grid=(m // bm,),
        in_specs=[row_spec, row_spec, row_spec, _mod_spec(stream, bm, d, lambda i: (i, 0)),
                  full(d, ed), full(d, ed), full(ed, d)],
        out_specs=row_spec,
        out_shape=jax.ShapeDtypeStruct((m, d), F32),
        scratch_shapes=[pltpu.VMEM((d, ed), BF16), pltpu.VMEM((d, ed), BF16), pltpu.VMEM((ed, d), BF16)],
        compiler_params=_params(1), name="shared_expert",
    )(h, routed, x, gate, sw_gate, sw_up, sw_down)


def route(s, b_router):
    t, n_exp = s.shape
    per = n_exp // N_EXPERT_GROUPS
    sb = s + b_router.astype(F32)
    grp_score = lax.top_k(sb.reshape(t, N_EXPERT_GROUPS, per), 2)[0].sum(-1)
    _, gidx = lax.top_k(grp_score, TOPK_GROUPS)
    gmask = jax.nn.one_hot(gidx, N_EXPERT_GROUPS, dtype=F32).sum(1) > 0
    sel = jnp.where(jnp.repeat(gmask, per, axis=1), sb, -jnp.inf)
    _, idx = lax.top_k(sel, TOP_K)
    wts = jnp.take_along_axis(s, idx, axis=1)
    return idx, wts / jnp.sum(wts, axis=-1, keepdims=True) * ROUTED_SCALE


def moe_routed(h_all, s_all, b_router, w_gate, w_up, w_down):
    t, d = h_all.shape
    n_exp = s_all.shape[1]
    idx, wts = route(s_all, b_router)
    n_assign = t * TOP_K
    e_flat = idx.reshape(-1)
    counts = jnp.zeros((n_exp,), jnp.int32).at[e_flat].add(1)
    padded = (counts + MOE_BLOCK - 1) // MOE_BLOCK * MOE_BLOCK
    pad_end = jnp.cumsum(padded)
    pad_start = pad_end - padded
    grp_start = jnp.cumsum(counts) - counts
    order = jnp.argsort(e_flat)
    e_sorted = e_flat[order]
    dest = pad_start[e_sorted] + jnp.arange(n_assign, dtype=jnp.int32) - grp_start[e_sorted]
    n_blocks = -(-(n_assign + n_exp * (MOE_BLOCK - 1)) // MOE_BLOCK)
    src_tok = jnp.zeros((n_blocks * MOE_BLOCK,), jnp.int32).at[dest].set((order // TOP_K).astype(jnp.int32))
    pos = jnp.zeros((n_assign,), jnp.int32).at[order].set(dest).reshape(t, TOP_K)
    blk_expert = jnp.minimum(
        jnp.searchsorted(pad_end, jnp.arange(n_blocks, dtype=jnp.int32) * MOE_BLOCK, side='right'),
        n_exp - 1).astype(jnp.int32)
    n_used = (pad_end[-1:] // MOE_BLOCK).astype(jnp.int32)
    xg = jnp.take(h_all, src_tok, axis=0)
    out = moe_experts(xg, blk_expert, n_used, w_gate, w_up, w_down)
    picked = jnp.take(out, pos.reshape(-1), axis=0).reshape(t, TOP_K, d).astype(F32)
    return jnp.sum(picked * wts[:, :, None], axis=1)


def _stream(rows_per_batch):
    return dict(per_row=rows_per_batch < 1024, rows_per_batch=rows_per_batch)


def _mod_rows(stream, mod):
    b, d = mod.shape
    if not stream["per_row"]:
        return mod.reshape(b, 1, d)
    return jnp.repeat(mod, stream["rows_per_batch"], axis=0).reshape(1, -1, d)


def kernel(x_prompt, x_sample, cache_k_attn, cache_v_attn, c_prompt, c_sample, norm_mix_g, norm_ffn_g, ada_w, ada_b, attn_w_qkv, attn_w_o, attn_q_norm, attn_k_norm, attn_lambda_q1, attn_lambda_k1, attn_lambda_q2, attn_lambda_k2, attn_subln_g, sgu_w_in, sgu_ln_g, sgu_ln_b, sgu_w_s, sgu_b_s, sgu_w_o, moe_w_router, moe_b_router, moe_w_gate, moe_w_up, moe_w_down, shared_w_gate, shared_w_up, shared_w_down):
    bp, sp, d = x_prompt.shape
    bs, ss, _ = x_sample.shape
    depth = ada_w.shape[0]
    dv = 2 * ATTN_HEAD_DIM
    heads = d // dv
    mp, ms = bp * sp, bs * ss
    streams = (_stream(sp), _stream(ss))
    xs = [x_prompt.reshape(mp, d), x_sample.reshape(ms, d)]

    mod = ada_modulation(jnp.concatenate([c_prompt, c_sample], axis=0), ada_w, ada_b)
    slopes2 = (2.0 ** (-8.0 * jnp.arange(1, heads + 1, dtype=F32) / heads)) * LOG2E

    k_out, v_out, sgu_out = [[], []], [[], []], []
    for i in range(depth):
        j = i // N_MIXERS
        mods = []
        for si, (stream, rows) in enumerate(zip(streams, (slice(0, bp), slice(bp, bp + bs)))):
            mods.append([_mod_rows(stream, mod[i, rows, c * d:(c + 1) * d]) for c in range(6)])

        if i % N_MIXERS == 0:
            lam_init = 0.8 - 0.6 * math.exp(-0.3 * i)
            f = lambda a: a[j].astype(F32)
            lam = (jnp.exp(jnp.sum(f(attn_lambda_q1) * f(attn_lambda_k1)))
                   - jnp.exp(jnp.sum(f(attn_lambda_q2) * f(attn_lambda_k2))) + lam_init).reshape(1)
            for si, stream in enumerate(streams):
                h = modulate(stream, xs[si], norm_mix_g[i], mods[si][0], mods[si][1])
                q_bf, k32, k_bf, v32, v_bf = qkv_project(stream, h, attn_w_qkv[j], attn_q_norm[j], attn_k_norm[j])
                if si == 0:
                    o = attention_prompt(q_bf, k_bf, v_bf, bp, lam, slopes2, attn_subln_g[j], 1.0 - lam_init)
                else:
                    past = cache_k_attn.shape[2]
                    o = attention_sample(q_bf, k_bf, v_bf, cache_k_attn[j].reshape(bs, past, d),
                                         cache_v_attn[j].reshape(bs, past, d), lam, slopes2, attn_subln_g[j],
                                         1.0 - lam_init)
                xs[si] = project_residual(stream, o, attn_w_o[j], xs[si], mods[si][2], "attn_out_proj")
                k_out[si].append(k32)
                v_out[si].append(v32)
        else:
            for si, stream in enumerate(streams):
                h = modulate(stream, xs[si], norm_mix_g[i], mods[si][0], mods[si][1])
                (z,) = matmul(h, sgu_w_in[j], bm=_row_block(stream, h.shape[0], 1024),
                              bn=_pick(sgu_w_in.shape[2], 1024),
                              out_dtypes=(BF16,), epilogue=_gelu_epilogue, name="sgu_in_proj")
                chunk_len = SGU_CHUNK if si == 0 else ss
                res = sgu_gate(z, sgu_ln_g[j], sgu_ln_b[j], sgu_w_s[j], sgu_b_s[j], chunk_len, write_v=(si == 1))
                if si == 1:
                    sgu_out.append(res[1])
                xs[si] = project_residual(stream, res[0], sgu_w_o[j], xs[si], mods[si][2], "sgu_out_proj")

        hs, scores = [], []
        for si, stream in enumerate(streams):
            hf, sc = modulate(stream, xs[si], norm_ffn_g[i], mods[si][3], mods[si][4], w_router=moe_w_router[i])
            hs.append(hf)
            scores.append(sc)
        routed = moe_routed(jnp.concatenate(hs, axis=0), jnp.concatenate(scores, axis=0), moe_b_router[i],
                            moe_w_gate[i], moe_w_up[i], moe_w_down[i])
        for si, (stream, rows) in enumerate(zip(streams, (slice(0, mp), slice(mp, mp + ms)))):
            xs[si] = shared_expert_residual(stream, hs[si], routed[rows], xs[si], mods[si][5],
                                            shared_w_gate[i], shared_w_up[i], shared_w_down[i])

    n_attn = len(k_out[0])
    k_prompt = jnp.stack(k_out[0]).reshape(n_attn, bp, sp, heads, 2, ATTN_HEAD_DIM)
    v_prompt = jnp.stack(v_out[0]).reshape(n_attn, bp, sp, heads, dv)
    k_sample = jnp.stack(k_out[1]).reshape(n_attn, bs, ss, heads, 2, ATTN_HEAD_DIM)
    v_sample = jnp.stack(v_out[1]).reshape(n_attn, bs, ss, heads, dv)
    sgu_v = jnp.stack(sgu_out).reshape(len(sgu_out), bs, ss, -1)
    return (xs[0].reshape(bp, sp, d), xs[1].reshape(bs, ss, d), k_prompt, v_prompt, k_sample, v_sample, sgu_v)
```

```python
import functools
import math

import jax
import jax.numpy as jnp
from jax import lax
from jax.experimental import pallas as pl
from jax.experimental.pallas import tpu as pltpu

F32 = jnp.float32
BF16 = jnp.bfloat16

EPS = 1e-6
CHUNK = 64
N_MIXERS = 2
ATTN_HEAD_DIM = 128
SGU_GROUPS = 8
SGU_CHUNK = 128
TOP_K = 8
N_EXPERT_GROUPS = 8
TOPK_GROUPS = 4
ROUTED_SCALE = 2.5
LOG2E = 1.4426950408889634
NEG_BIG = -1e30

VMEM_LIMIT = 56 * 1024 * 1024
MOE_BLOCK = 256


def _params(n_axes):
    return pltpu.CompilerParams(dimension_semantics=("arbitrary",) * n_axes, vmem_limit_bytes=VMEM_LIMIT)


def _pick(n, pref):
    if n <= pref:
        return n
    b = pref
    while n % b:
        b //= 2
    return b


def _ada_kernel(c_ref, w_ref, b_ref, o_ref):
    c = c_ref[...]
    x = (c * jax.nn.sigmoid(c)).astype(BF16)
    o_ref[...] = jnp.dot(x, w_ref[...].astype(BF16), preferred_element_type=F32) + b_ref[...]


def ada_modulation(c, ada_w, ada_b):
    n_layers, d, n = ada_w.shape
    r = c.shape[0]
    bn = _pick(n, 1024)
    return pl.pallas_call(
        _ada_kernel,
        grid=(n_layers, n // bn),
        in_specs=[
            pl.BlockSpec((r, d), lambda l, j: (0, 0)),
            pl.BlockSpec((None, d, bn), lambda l, j: (l, 0, j)),
            pl.BlockSpec((None, 1, bn), lambda l, j: (l, 0, j)),
        ],
        out_specs=pl.BlockSpec((None, r, bn), lambda l, j: (l, 0, j)),
        out_shape=jax.ShapeDtypeStruct((n_layers, r, n), F32),
        compiler_params=_params(2),
        name="ada_modulation",
    )(c, ada_w, ada_b.reshape(n_layers, 1, n))


def _modulated(x, g, shift, scale):
    y = x * lax.rsqrt(jnp.mean(x * x, axis=-1, keepdims=True) + EPS)
    return y * g * (1.0 + scale) + shift


def _modulate_kernel(x_ref, g_ref, sh_ref, sc_ref, o_ref):
    o_ref[...] = _modulated(x_ref[...], g_ref[...], sh_ref[...], sc_ref[...]).astype(o_ref.dtype)


def _split3(x):
    hi = x.astype(BF16)
    lo = (x - hi.astype(F32)).astype(BF16)
    return hi, lo


def _first_max(x, ids, n_ids, axes):
    mx = x
    for ax in axes:
        mx = jnp.max(mx, axis=ax, keepdims=True)
    arg = jnp.where(x == mx, ids, n_ids)
    for ax in axes:
        arg = jnp.min(arg, axis=ax, keepdims=True)
    return mx, arg


def _sum_axes(x, axes):
    for ax in axes:
        x = jnp.sum(x, axis=ax, keepdims=True)
    return x


def _modulate_router_kernel(x_ref, g_ref, sh_ref, sc_ref, wrt_ref, br_ref, cin_ref,
                            o_ref, idx_ref, wts_ref, rank_ref, cnt_ref, carry):
    h = _modulated(x_ref[...], g_ref[...], sh_ref[...], sc_ref[...])
    o_ref[...] = h.astype(o_ref.dtype)
    bm = h.shape[0]
    n_exp = wrt_ref.shape[0]
    per = n_exp // N_EXPERT_GROUPS
    grp_shape = (N_EXPERT_GROUPS, per, bm)

    h_hi, h_lo = _split3(h)
    w_hi, w_lo = _split3(wrt_ref[...])
    nt = (((1,), (1,)), ((), ()))
    logits = (lax.dot_general(w_hi, h_hi, nt, preferred_element_type=F32)
              + lax.dot_general(w_lo, h_hi, nt, preferred_element_type=F32)
              + lax.dot_general(w_hi, h_lo, nt, preferred_element_type=F32))
    s = jax.nn.sigmoid(logits)
    s3 = s.reshape(grp_shape)
    sb3 = (s + br_ref[...]).reshape(grp_shape)

    sub = lax.broadcasted_iota(jnp.int32, grp_shape, 1)
    gid = lax.broadcasted_iota(jnp.int32, (N_EXPERT_GROUPS, 1, bm), 0)
    eid = lax.broadcasted_iota(jnp.int32, grp_shape, 0) * per + sub

    m1, i1 = _first_max(sb3, sub, per, (1,))
    m2 = jnp.max(jnp.where(sub == i1, -jnp.inf, sb3), axis=1, keepdims=True)
    work = m1 + m2
    chosen = jnp.zeros_like(work)
    for _ in range(TOPK_GROUPS):
        _, gi = _first_max(work, gid, N_EXPERT_GROUPS, (0,))
        chosen = jnp.where(gid == gi, 1.0, chosen)
        work = jnp.where(gid == gi, -jnp.inf, work)
    sel = jnp.where(chosen > 0.0, sb3, -jnp.inf)

    hits, ids, raw = [], [], []
    for _ in range(TOP_K):
        _, ei = _first_max(sel, eid, n_exp, (1, 0))
        hit = eid == ei
        raw.append(_sum_axes(jnp.where(hit, s3, 0.0), (1, 0)))
        sel = jnp.where(hit, -jnp.inf, sel)
        hits.append(hit)
        ids.append(ei)
    total = sum(raw)

    member = sum(jnp.where(hit, 1.0, 0.0) for hit in hits).reshape(n_exp, bm)
    r_i = lax.broadcasted_iota(jnp.int32, (bm, bm), 0)
    c_i = lax.broadcasted_iota(jnp.int32, (bm, bm), 1)
    before = jnp.where(r_i < c_i, 1.0, 0.0).astype(BF16)

    @pl.when(pl.program_id(0) == 0)
    def _():
        carry[...] = cin_ref[...]

    rank_all = (jnp.dot(member.astype(BF16), before, preferred_element_type=F32) + carry[...]).reshape(grp_shape)
    carry[...] += jnp.sum(member, axis=1, keepdims=True)
    cnt_ref[...] = carry[...]
    for k in range(TOP_K):
        idx_ref[k:k + 1, :] = ids[k].reshape(1, bm)
        wts_ref[k:k + 1, :] = (raw[k] / total * ROUTED_SCALE).reshape(1, bm)
        rank_ref[k:k + 1, :] = _sum_axes(jnp.where(hits[k], rank_all, 0.0), (1, 0)).reshape(1, bm).astype(jnp.int32)


def _row_block(stream, m, pref):
    return _pick(m if stream["per_row"] else stream["rows_per_batch"], pref)


def _mod_spec(stream, bm, bn, ij):
    if stream["per_row"]:
        return pl.BlockSpec((None, bm, bn), lambda *g: (0,) + tuple(ij(*g)))
    bpg = stream["rows_per_batch"] // bm
    return pl.BlockSpec((None, 1, bn), lambda *g: (ij(*g)[0] // bpg, 0, ij(*g)[1]))


def modulate(stream, x, g, shift, scale, router=None):
    m, d = x.shape
    bm = _row_block(stream, m, 512)
    row_spec = pl.BlockSpec((bm, d), lambda i: (i, 0))
    mod_spec = _mod_spec(stream, bm, d, lambda i: (i, 0))
    g_spec = pl.BlockSpec((1, d), lambda i: (0, 0))
    if router is None:
        return pl.pallas_call(
            _modulate_kernel, grid=(m // bm,),
            in_specs=[row_spec, g_spec, mod_spec, mod_spec], out_specs=row_spec,
            out_shape=jax.ShapeDtypeStruct((m, d), BF16),
            compiler_params=_params(1), name="modulate",
        )(x, g.reshape(1, d), shift, scale)
    w_router, b_router, counts_in = router
    e = w_router.shape[1]
    tok_spec = pl.BlockSpec((TOP_K, bm), lambda i: (0, i))
    cnt_spec = pl.BlockSpec((e, 1), lambda i: (0, 0))
    return pl.pallas_call(
        _modulate_router_kernel, grid=(m // bm,),
        in_specs=[row_spec, g_spec, mod_spec, mod_spec, pl.BlockSpec((e, d), lambda i: (0, 0)), cnt_spec, cnt_spec],
        out_specs=[row_spec, tok_spec, tok_spec, tok_spec, cnt_spec],
        out_shape=[jax.ShapeDtypeStruct((m, d), BF16), jax.ShapeDtypeStruct((TOP_K, m), jnp.int32),
                   jax.ShapeDtypeStruct((TOP_K, m), F32), jax.ShapeDtypeStruct((TOP_K, m), jnp.int32),
                   jax.ShapeDtypeStruct((e, 1), F32)],
        scratch_shapes=[pltpu.VMEM((e, 1), F32)],
        compiler_params=_params(1), name="modulate_router",
    )(x, g.reshape(1, d), shift, scale, w_router.T, b_router.astype(F32).reshape(e, 1), counts_in)


def _mm_kernel(x_ref, w_ref, *refs, n_extra, epilogue):
    extra, outs, w_bf = refs[:n_extra], refs[n_extra:-1], refs[-1]

    @pl.when(pl.program_id(1) == 0)
    def _():
        w_bf[...] = w_ref[...].astype(BF16)

    acc = jnp.dot(x_ref[...], w_bf[...], preferred_element_type=F32)
    epilogue(acc, extra, outs)


def matmul(x, w, *, bm, bn, col_block_off=0, n_cols=None, extra=(), extra_specs=(), out_dtypes, epilogue, name):
    m, k = x.shape
    n = w.shape[1] if n_cols is None else n_cols
    grid = (n // bn, m // bm)
    out_spec = pl.BlockSpec((bm, bn), lambda j, i: (i, j))
    return pl.pallas_call(
        functools.partial(_mm_kernel, n_extra=len(extra), epilogue=epilogue),
        grid=grid,
        in_specs=[pl.BlockSpec((bm, k), lambda j, i: (i, 0)),
                  pl.BlockSpec((k, bn), lambda j, i: (0, j + col_block_off))] + list(extra_specs),
        out_specs=[out_spec] * len(out_dtypes),
        out_shape=[jax.ShapeDtypeStruct((m, n), dt) for dt in out_dtypes],
        scratch_shapes=[pltpu.VMEM((k, bn), BF16)],
        compiler_params=_params(2), name=name,
    )(x, w, *extra)


def _head_rms(acc, gain, post_scale):
    pieces = []
    for c in range(acc.shape[1] // ATTN_HEAD_DIM):
        seg = acc[:, c * ATTN_HEAD_DIM:(c + 1) * ATTN_HEAD_DIM]
        y = seg * lax.rsqrt(jnp.mean(seg * seg, axis=-1, keepdims=True) + EPS)
        pieces.append(y * (gain * post_scale))
    return jnp.concatenate(pieces, axis=-1)


def _q_epilogue(acc, extra, outs):
    outs[0][...] = _head_rms(acc, extra[0][...], ATTN_HEAD_DIM ** -0.5 * LOG2E).astype(BF16)


def _k_epilogue(acc, extra, outs):
    kn = _head_rms(acc, extra[0][...], 1.0)
    outs[0][...] = kn
    outs[1][...] = kn.astype(BF16)


def _v_epilogue(acc, extra, outs):
    outs[0][...] = acc
    outs[1][...] = acc.astype(BF16)


def _gelu_epilogue(acc, extra, outs):
    outs[0][...] = (0.5 * acc * (1.0 + lax.erf(acc * (2.0 ** -0.5)))).astype(BF16)


def _residual_epilogue(acc, extra, outs):
    x_ref, gate_ref = extra
    outs[0][...] = x_ref[...] + gate_ref[...] * acc


def qkv_project(stream, h, w_qkv, q_norm, k_norm):
    m, d = h.shape
    bm = _row_block(stream, m, 1024)
    bn = _pick(d, 1024)
    nb = d // bn
    gain_spec = pl.BlockSpec((1, ATTN_HEAD_DIM), lambda j, i: (0, 0))
    common = dict(bm=bm, bn=bn, n_cols=d)
    (q_bf,) = matmul(h, w_qkv, col_block_off=0, extra=(q_norm.reshape(1, -1),), extra_specs=(gain_spec,),
                     out_dtypes=(BF16,), epilogue=_q_epilogue, name="q_proj", **common)
    k32, k_bf = matmul(h, w_qkv, col_block_off=nb, extra=(k_norm.reshape(1, -1),), extra_specs=(gain_spec,),
                       out_dtypes=(F32, BF16), epilogue=_k_epilogue, name="k_proj", **common)
    v32, v_bf = matmul(h, w_qkv, col_block_off=2 * nb, out_dtypes=(F32, BF16), epilogue=_v_epilogue,
                       name="v_proj", **common)
    return q_bf, k32, k_bf, v32, v_bf


def project_residual(stream, a, w, x, gate, name):
    k, n = w.shape
    deep = k > 4096
    bm = _row_block(stream, a.shape[0], 512 if deep else 1024)
    bn = _pick(n, 512 if deep else 1024)
    (out,) = matmul(a, w, bm=bm, bn=bn, extra=(x, gate),
                    extra_specs=(pl.BlockSpec((bm, bn), lambda j, i: (i, j)),
                                 _mod_spec(stream, bm, bn, lambda j, i: (i, j))),
                    out_dtypes=(F32,), epilogue=_residual_epilogue, name=name)
    return out


def _finish_heads(acc1, l1, acc2, l2, lam, g, post_scale):
    o = acc1 / l1 - lam * (acc2 / l2)
    o = o * lax.rsqrt(jnp.mean(o * o, axis=-1, keepdims=True) + EPS)
    return o * (g * post_scale)


def _attn_prompt_kernel(lam_ref, slope_ref, q_ref, k_ref, v_ref, g_ref, o_ref, *, bq, post_scale):
    hd = ATTN_HEAD_DIM
    qi = pl.program_id(2)
    sl2 = slope_ref[pl.program_id(1)]
    lam = lam_ref[0]
    q = q_ref[...]
    qs = (q[:, :hd], q[:, hd:])

    def scores(qm, k_blk):
        return lax.dot_general(qm, k_blk, (((1,), (1,)), ((), ())), preferred_element_type=F32)

    d0 = pl.multiple_of(qi * bq, bq)
    k_d = k_ref[pl.ds(d0, bq), :]
    v_d = v_ref[pl.ds(d0, bq), :]
    row = lax.broadcasted_iota(jnp.int32, (bq, bq), 0)
    col = lax.broadcasted_iota(jnp.int32, (bq, bq), 1)
    visible = (col // CHUNK) <= (row // CHUNK)
    bias_d = sl2 * (row - jnp.abs(row - col)).astype(F32)
    state = []
    for mi in range(2):
        s = jnp.where(visible, scores(qs[mi], k_d[:, mi * hd:(mi + 1) * hd]) + bias_d, NEG_BIG)
        m = jnp.max(s, axis=-1, keepdims=True)
        p = jnp.exp2(s - m)
        l = jnp.sum(p, axis=-1, keepdims=True)
        acc = jnp.dot(p.astype(BF16), v_d, preferred_element_type=F32)
        state += [m, l, acc]

    col_off = lax.broadcasted_iota(jnp.int32, (1, bq), 1)

    def body(kj, carry):
        k0 = pl.multiple_of(kj * bq, bq)
        k_b = k_ref[pl.ds(k0, bq), :]
        v_b = v_ref[pl.ds(k0, bq), :]
        bias = sl2 * (col_off + (kj - qi) * bq).astype(F32)
        new = []
        for mi in range(2):
            m_old, l_old, acc_old = carry[3 * mi:3 * mi + 3]
            s = scores(qs[mi], k_b[:, mi * hd:(mi + 1) * hd]) + bias
            m_new = jnp.maximum(m_old, jnp.max(s, axis=-1, keepdims=True))
            alpha = jnp.exp2(m_old - m_new)
            p = jnp.exp2(s - m_new)
            l_new = alpha * l_old + jnp.sum(p, axis=-1, keepdims=True)
            acc_new = alpha * acc_old + jnp.dot(p.astype(BF16), v_b, preferred_element_type=F32)
            new += [m_new, l_new, acc_new]
        return tuple(new)

    m1, l1, acc1, m2, l2, acc2 = lax.fori_loop(0, qi, body, tuple(state))
    o_ref[...] = _finish_heads(acc1, l1, acc2, l2, lam, g_ref[...], post_scale).astype(o_ref.dtype)


def attention_prompt(q_bf, k_bf, v_bf, batch, lam, slopes2, subln_g, post_scale):
    m, width = q_bf.shape
    s = m // batch
    dv = 2 * ATTN_HEAD_DIM
    heads = width // dv
    bq = _pick(s, 512)
    assert bq % CHUNK == 0
    nq = s // bq
    kv_spec = pl.BlockSpec((s, dv), lambda b, h, i: (b, h))
    q_spec = pl.BlockSpec((bq, dv), lambda b, h, i: (b * nq + i, h))
    smem = pl.BlockSpec(memory_space=pltpu.SMEM)
    return pl.pallas_call(
        functools.partial(_attn_prompt_kernel, bq=bq, post_scale=post_scale),
        grid=(batch, heads, nq),
        in_specs=[smem, smem, q_spec, kv_spec, kv_spec, pl.BlockSpec((1, dv), lambda b, h, i: (0, 0))],
        out_specs=q_spec,
        out_shape=jax.ShapeDtypeStruct((m, width), BF16),
        compiler_params=_params(3), name="attention_prompt",
    )(lam, slopes2, q_bf, k_bf, v_bf, subln_g.reshape(1, dv))


def _attn_sample_kernel(lam_ref, slope_ref, q_ref, kn_ref, vn_ref, ck_ref, cv_ref, g_ref, o_ref, *, post_scale):
    hd = ATTN_HEAD_DIM
    sl2 = slope_ref[pl.program_id(1)]
    lam = lam_ref[0]
    t = q_ref.shape[0]
    past = ck_ref.shape[0]
    q = q_ref[...]
    k_new = kn_ref[...]
    v_new = vn_ref[...]
    k_past = ck_ref[...].astype(BF16)
    v_past = cv_ref[...].astype(BF16)

    def bias_mask(n_keys, key0):
        q_pos = past + lax.broadcasted_iota(jnp.int32, (t, n_keys), 0)
        k_pos = key0 + lax.broadcasted_iota(jnp.int32, (t, n_keys), 1)
        return -sl2 * jnp.abs(q_pos - k_pos).astype(F32), (k_pos // CHUNK) <= (q_pos // CHUNK)

    bias_p, vis_p = bias_mask(past, 0)
    bias_n, vis_n = bias_mask(t, past)
    res = []
    for mi in range(2):
        qm = q[:, mi * hd:(mi + 1) * hd]
        dn = (((1,), (1,)), ((), ()))
        s_p = lax.dot_general(qm, k_past[:, mi * hd:(mi + 1) * hd], dn, preferred_element_type=F32)
        s_n = lax.dot_general(qm, k_new[:, mi * hd:(mi + 1) * hd], dn, preferred_element_type=F32)
        s_p = jnp.where(vis_p, s_p + bias_p, NEG_BIG)
        s_n = jnp.where(vis_n, s_n + bias_n, NEG_BIG)
        m = jnp.maximum(jnp.max(s_p, axis=-1, keepdims=True), jnp.max(s_n, axis=-1, keepdims=True))
        p_p = jnp.exp2(s_p - m)
        p_n = jnp.exp2(s_n - m)
        l = jnp.sum(p_p, axis=-1, keepdims=True) + jnp.sum(p_n, axis=-1, keepdims=True)
        acc = (jnp.dot(p_p.astype(BF16), v_past, preferred_element_type=F32)
               + jnp.dot(p_n.astype(BF16), v_new, preferred_element_type=F32))
        res += [acc, l]
    o_ref[...] = _finish_heads(res[0], res[1], res[2], res[3], lam, g_ref[...], post_scale).astype(o_ref.dtype)


def attention_sample(q_bf, k_bf, v_bf, cache_k, cache_v, lam, slopes2, subln_g, post_scale):
    batch, past, width = cache_k.shape
    m = q_bf.shape[0]
    t = m // batch
    dv = 2 * ATTN_HEAD_DIM
    heads = width // dv
    new_spec = pl.BlockSpec((t, dv), lambda b, h: (b, h))
    cache_spec = pl.BlockSpec((None, past, dv), lambda b, h: (b, 0, h))
    smem = pl.BlockSpec(memory_space=pltpu.SMEM)
    return pl.pallas_call(
        functools.partial(_attn_sample_kernel, post_scale=post_scale),
        grid=(batch, heads),
        in_specs=[smem, smem, new_spec, new_spec, new_spec, cache_spec, cache_spec,
                  pl.BlockSpec((1, dv), lambda b, h: (0, 0))],
        out_specs=new_spec,
        out_shape=jax.ShapeDtypeStruct((m, width), BF16),
        compiler_params=_params(2), name="attention_sample",
    )(lam, slopes2, q_bf, k_bf, v_bf, cache_k, cache_v, subln_g.reshape(1, dv))


def _sgu_gate_kernel(u_ref, v_ref, g_ref, b_ref, w_ref, bs_ref, *out_refs, chunk_len, write_v):
    bt, width = u_ref.shape
    gd = width // SGU_GROUPS
    v = v_ref[...].astype(F32)
    mu = jnp.mean(v, axis=-1, keepdims=True)
    vc = v - mu
    vn = vc * lax.rsqrt(jnp.mean(vc * vc, axis=-1, keepdims=True) + EPS) * g_ref[...] + b_ref[...]
    if write_v:
        out_refs[1][...] = vn
    vn_bf = vn.astype(BF16)
    row = lax.broadcasted_iota(jnp.int32, (SGU_CHUNK, SGU_CHUNK), 0)
    col = lax.broadcasted_iota(jnp.int32, (SGU_CHUNK, SGU_CHUNK), 1)
    keep = ((row // chunk_len) == (col // chunk_len)) & ((col % chunk_len) <= (row % chunk_len))
    for gi in range(SGU_GROUPS):
        w = jnp.where(keep, w_ref[gi], 0.0).astype(BF16)
        bias = bs_ref[gi]
        for c in range(bt // SGU_CHUNK):
            rows = slice(c * SGU_CHUNK, (c + 1) * SGU_CHUNK)
            cols = slice(gi * gd, (gi + 1) * gd)
            mixed = jnp.dot(w, vn_bf[rows, cols], preferred_element_type=F32) + bias
            out_refs[0][rows, cols] = (u_ref[rows, cols].astype(F32) * mixed).astype(BF16)


def sgu_gate(z, ln_g, ln_b, w_s, b_s, chunk_len, write_v):
    m, two_w = z.shape
    width = two_w // 2
    reps = SGU_CHUNK // chunk_len
    w_t = jnp.tile(w_s[:, :chunk_len, :chunk_len], (1, reps, reps))
    b_t = jnp.tile(b_s[:, :chunk_len], (1, reps))[:, :, None]
    bt = _pick(m, 256)
    row_spec = lambda c: pl.BlockSpec((bt, width), lambda i, c=c: (i, c))
    vec_spec = pl.BlockSpec((1, width), lambda i: (0, 0))
    out_specs = [row_spec(0)]
    out_shape = [jax.ShapeDtypeStruct((m, width), BF16)]
    if write_v:
        out_specs.append(row_spec(0))
        out_shape.append(jax.ShapeDtypeStruct((m, width), F32))
    return pl.pallas_call(
        functools.partial(_sgu_gate_kernel, chunk_len=chunk_len, write_v=write_v),
        grid=(m // bt,),
        in_specs=[row_spec(0), row_spec(1), vec_spec, vec_spec,
                  pl.BlockSpec((SGU_GROUPS, SGU_CHUNK, SGU_CHUNK), lambda i: (0, 0, 0)),
                  pl.BlockSpec((SGU_GROUPS, SGU_CHUNK, 1), lambda i: (0, 0, 0))],
        out_specs=out_specs, out_shape=out_shape,
        compiler_params=_params(1), name="sgu_gate",
    )(z, z, ln_g.reshape(1, width), ln_b.reshape(1, width), w_t, b_t)


def _swiglu(x, wg, wu, wd):
    g = jnp.dot(x, wg, preferred_element_type=F32)
    u = jnp.dot(x, wu, preferred_element_type=F32)
    h = (g * jax.nn.sigmoid(g) * u).astype(BF16)
    return jnp.dot(h, wd, preferred_element_type=F32)


def _moe_kernel(blk_e_ref, n_used_ref, x_ref, wg_ref, wu_ref, wd_ref, o_ref, wg_bf, wu_bf, wd_bf):
    b = pl.program_id(0)
    e = blk_e_ref[b]
    e_prev = blk_e_ref[jnp.maximum(b - 1, 0)]

    @pl.when((b == 0) | (e != e_prev))
    def _():
        wg_bf[...] = wg_ref[...].astype(BF16)
        wu_bf[...] = wu_ref[...].astype(BF16)
        wd_bf[...] = wd_ref[...].astype(BF16)

    @pl.when(b < n_used_ref[0])
    def _():
        o_ref[...] = _swiglu(x_ref[...], wg_bf[...], wu_bf[...], wd_bf[...]).astype(o_ref.dtype)

    @pl.when(b >= n_used_ref[0])
    def _():
        o_ref[...] = jnp.zeros_like(o_ref)


def moe_experts(xg, blk_expert, n_used, layer, w_gate, w_up, w_down):
    rows, d = xg.shape
    ed = w_gate.shape[3]
    n_blocks = rows // MOE_BLOCK
    x_spec = pl.BlockSpec((MOE_BLOCK, d), lambda b, be, nu: (b, 0))
    return pl.pallas_call(
        _moe_kernel,
        grid_spec=pltpu.PrefetchScalarGridSpec(
            num_scalar_prefetch=2, grid=(n_blocks,),
            in_specs=[x_spec,
                      pl.BlockSpec((None, None, d, ed), lambda b, be, nu: (layer, be[b], 0, 0)),
                      pl.BlockSpec((None, None, d, ed), lambda b, be, nu: (layer, be[b], 0, 0)),
                      pl.BlockSpec((None, None, ed, d), lambda b, be, nu: (layer, be[b], 0, 0))],
            out_specs=x_spec,
            scratch_shapes=[pltpu.VMEM((d, ed), BF16), pltpu.VMEM((d, ed), BF16), pltpu.VMEM((ed, d), BF16)]),
        out_shape=jax.ShapeDtypeStruct((rows, d), BF16),
        compiler_params=_params(1), name="moe_experts",
    )(blk_expert, n_used, xg, w_gate, w_up, w_down)


def _shared_kernel(h_ref, p_ref, w_ref, x_ref, gate_ref, wg_ref, wu_ref, wd_ref, o_ref):
    ffn = _swiglu(h_ref[...], wg_ref[...], wu_ref[...], wd_ref[...])
    w = w_ref[...]
    for k in range(TOP_K):
        ffn = ffn + w[:, k:k + 1] * p_ref[k].astype(F32)
    o_ref[...] = x_ref[...] + gate_ref[...] * ffn


def shared_expert_residual(stream, h, picked, wts, row0, x, gate, sw_gate, sw_up, sw_down):
    m, d = x.shape
    ed = sw_gate.shape[1]
    bm = _row_block(stream, m, 256)
    off = row0 // bm
    row_spec = pl.BlockSpec((bm, d), lambda i: (i, 0))
    full = lambda a, b: pl.BlockSpec((a, b), lambda i: (0, 0))
    return pl.pallas_call(
        _shared_kernel, grid=(m // bm,),
        in_specs=[row_spec, pl.BlockSpec((TOP_K, bm, d), lambda i: (0, i + off, 0)),
                  pl.BlockSpec((bm, TOP_K), lambda i: (i + off, 0)), row_spec,
                  _mod_spec(stream, bm, d, lambda i: (i, 0)), full(d, ed), full(d, ed), full(ed, d)],
        out_specs=row_spec,
        out_shape=jax.ShapeDtypeStruct((m, d), F32),
        compiler_params=_params(1), name="shared_expert",
    )(h, picked, wts, x, gate, sw_gate, sw_up, sw_down)


def moe_routed(h_all, idx, rank, counts, layer, w_gate, w_up, w_down):
    t, d = h_all.shape
    n_exp = counts.shape[0]
    n_assign = t * TOP_K
    counts = counts.reshape(n_exp).astype(jnp.int32)
    padded = (counts + MOE_BLOCK - 1) // MOE_BLOCK * MOE_BLOCK
    pad_end = jnp.cumsum(padded)
    pad_start = pad_end - padded
    dest = (rank + jnp.take(pad_start, idx, axis=0)).reshape(-1)
    n_blocks = -(-(n_assign + n_exp * (MOE_BLOCK - 1)) // MOE_BLOCK)
    tok = jnp.tile(jnp.arange(t, dtype=jnp.int32), TOP_K)
    src_tok = jnp.zeros((n_blocks * MOE_BLOCK,), jnp.int32).at[dest].set(tok)
    blk_start = jnp.arange(n_blocks, dtype=jnp.int32) * MOE_BLOCK
    blk_expert = jnp.minimum(jnp.sum((pad_end[None, :] <= blk_start[:, None]).astype(jnp.int32), axis=1), n_exp - 1)
    n_used = pad_end[-1:] // MOE_BLOCK
    xg = jnp.take(h_all, src_tok, axis=0)
    out = moe_experts(xg, blk_expert, n_used, layer, w_gate, w_up, w_down)
    return jnp.take(out, dest, axis=0).reshape(TOP_K, t, d)


def _stream(rows_per_batch):
    return dict(per_row=rows_per_batch < 1024, rows_per_batch=rows_per_batch)


def _mod_rows(stream, mod):
    b, d = mod.shape
    if not stream["per_row"]:
        return mod.reshape(b, 1, d)
    return jnp.repeat(mod, stream["rows_per_batch"], axis=0).reshape(1, -1, d)


def kernel(x_prompt, x_sample, cache_k_attn, cache_v_attn, c_prompt, c_sample, norm_mix_g, norm_ffn_g, ada_w, ada_b, attn_w_qkv, attn_w_o, attn_q_norm, attn_k_norm, attn_lambda_q1, attn_lambda_k1, attn_lambda_q2, attn_lambda_k2, attn_subln_g, sgu_w_in, sgu_ln_g, sgu_ln_b, sgu_w_s, sgu_b_s, sgu_w_o, moe_w_router, moe_b_router, moe_w_gate, moe_w_up, moe_w_down, shared_w_gate, shared_w_up, shared_w_down):
    bp, sp, d = x_prompt.shape
    bs, ss, _ = x_sample.shape
    depth = ada_w.shape[0]
    dv = 2 * ATTN_HEAD_DIM
    heads = d // dv
    mp, ms = bp * sp, bs * ss
    streams = (_stream(sp), _stream(ss))
    xs = [x_prompt.reshape(mp, d), x_sample.reshape(ms, d)]

    mod = ada_modulation(jnp.concatenate([c_prompt, c_sample], axis=0), ada_w, ada_b)
    slopes2 = (2.0 ** (-8.0 * jnp.arange(1, heads + 1, dtype=F32) / heads)) * LOG2E

    k_out, v_out, sgu_out = [[], []], [[], []], []
    for i in range(depth):
        j = i // N_MIXERS
        mods = []
        for si, (stream, rows) in enumerate(zip(streams, (slice(0, bp), slice(bp, bp + bs)))):
            mods.append([_mod_rows(stream, mod[i, rows, c * d:(c + 1) * d]) for c in range(6)])

        if i % N_MIXERS == 0:
            lam_init = 0.8 - 0.6 * math.exp(-0.3 * i)
            f = lambda a: a[j].astype(F32)
            lam = (jnp.exp(jnp.sum(f(attn_lambda_q1) * f(attn_lambda_k1)))
                   - jnp.exp(jnp.sum(f(attn_lambda_q2) * f(attn_lambda_k2))) + lam_init).reshape(1)
            for si, stream in enumerate(streams):
                h = modulate(stream, xs[si], norm_mix_g[i], mods[si][0], mods[si][1])
                q_bf, k32, k_bf, v32, v_bf = qkv_project(stream, h, attn_w_qkv[j], attn_q_norm[j], attn_k_norm[j])
                if si == 0:
                    o = attention_prompt(q_bf, k_bf, v_bf, bp, lam, slopes2, attn_subln_g[j], 1.0 - lam_init)
                else:
                    past = cache_k_attn.shape[2]
                    o = attention_sample(q_bf, k_bf, v_bf, cache_k_attn[j].reshape(bs, past, d),
                                         cache_v_attn[j].reshape(bs, past, d), lam, slopes2, attn_subln_g[j],
                                         1.0 - lam_init)
                xs[si] = project_residual(stream, o, attn_w_o[j], xs[si], mods[si][2], "attn_out_proj")
                k_out[si].append(k32)
                v_out[si].append(v32)
        else:
            for si, stream in enumerate(streams):
                h = modulate(stream, xs[si], norm_mix_g[i], mods[si][0], mods[si][1])
                (z,) = matmul(h, sgu_w_in[j], bm=_row_block(stream, h.shape[0], 1024),
                              bn=_pick(sgu_w_in.shape[2], 1024),
                              out_dtypes=(BF16,), epilogue=_gelu_epilogue, name="sgu_in_proj")
                chunk_len = SGU_CHUNK if si == 0 else ss
                res = sgu_gate(z, sgu_ln_g[j], sgu_ln_b[j], sgu_w_s[j], sgu_b_s[j], chunk_len, write_v=(si == 1))
                if si == 1:
                    sgu_out.append(res[1])
                xs[si] = project_residual(stream, res[0], sgu_w_o[j], xs[si], mods[si][2], "sgu_out_proj")

        hs, routing = [], []
        counts = jnp.zeros((moe_w_router.shape[2], 1), F32)
        for si, stream in enumerate(streams):
            hf, idx, wts, rank, counts = modulate(stream, xs[si], norm_ffn_g[i], mods[si][3], mods[si][4],
                                                  router=(moe_w_router[i], moe_b_router[i], counts))
            hs.append(hf)
            routing.append((idx, wts, rank))
        idx, wts, rank = (jnp.concatenate(parts, axis=1) for parts in zip(*routing))
        picked = moe_routed(jnp.concatenate(hs, axis=0), idx, rank, counts, i, moe_w_gate, moe_w_up, moe_w_down)
        sw = [w[i].astype(BF16) for w in (shared_w_gate, shared_w_up, shared_w_down)]
        for si, (stream, row0) in enumerate(zip(streams, (0, mp))):
            xs[si] = shared_expert_residual(stream, hs[si], picked, wts.T, row0, xs[si], mods[si][5], *sw)

    n_attn = len(k_out[0])
    k_prompt = jnp.stack(k_out[0]).reshape(n_attn, bp, sp, heads, 2, ATTN_HEAD_DIM)
    v_prompt = jnp.stack(v_out[0]).reshape(n_attn, bp, sp, heads, dv)
    k_sample = jnp.stack(k_out[1]).reshape(n_attn, bs, ss, heads, 2, ATTN_HEAD_DIM)
    v_sample = jnp.stack(v_out[1]).reshape(n_attn, bs, ss, heads, dv)
    sgu_v = jnp.stack(sgu_out).reshape(len(sgu_out), bs, ss, -1)
    return (xs[0].reshape(bp, sp, d), xs[1].reshape(bs, ss, d), k_prompt, v_prompt, k_sample, v_sample, sgu_v)
```

```python
import functools
import math

import jax
import jax.numpy as jnp
from jax import lax
from jax.experimental import pallas as pl
from jax.experimental.pallas import tpu as pltpu

F32 = jnp.float32
BF16 = jnp.bfloat16

EPS = 1e-6
CHUNK = 64
N_MIXERS = 2
ATTN_HEAD_DIM = 128
SGU_GROUPS = 8
SGU_CHUNK = 128
TOP_K = 8
N_EXPERT_GROUPS = 8
TOPK_GROUPS = 4
ROUTED_SCALE = 2.5
LOG2E = 1.4426950408889634
NEG_BIG = -1e30

VMEM_LIMIT = 56 * 1024 * 1024
MOE_BLOCK = 256
ATTN_ROW_CHUNK = 64
LANES = 128


def _params(n_axes):
    return pltpu.CompilerParams(dimension_semantics=("arbitrary",) * n_axes, vmem_limit_bytes=VMEM_LIMIT)


def _pick(n, pref):
    if n <= pref:
        return n
    b = pref
    while n % b:
        b //= 2
    return b


def _ada_kernel(c_ref, w_ref, b_ref, o_ref):
    c = c_ref[...]
    x = (c * jax.nn.sigmoid(c)).astype(BF16)
    o_ref[...] = jnp.dot(x, w_ref[...].astype(BF16), preferred_element_type=F32) + b_ref[...]


def ada_modulation(c, ada_w, ada_b):
    n_layers, d, n = ada_w.shape
    r = c.shape[0]
    bn = _pick(n, 1024)
    return pl.pallas_call(
        _ada_kernel,
        grid=(n_layers, n // bn),
        in_specs=[
            pl.BlockSpec((r, d), lambda l, j: (0, 0)),
            pl.BlockSpec((None, d, bn), lambda l, j: (l, 0, j)),
            pl.BlockSpec((None, 1, bn), lambda l, j: (l, 0, j)),
        ],
        out_specs=pl.BlockSpec((None, r, bn), lambda l, j: (l, 0, j)),
        out_shape=jax.ShapeDtypeStruct((n_layers, r, n), F32),
        compiler_params=_params(2),
        name="ada_modulation",
    )(c, ada_w, ada_b.reshape(n_layers, 1, n))


def _modulated(x, g, shift, scale):
    y = x * lax.rsqrt(jnp.mean(x * x, axis=-1, keepdims=True) + EPS)
    return y * g * (1.0 + scale) + shift


def _modulate_kernel(x_ref, g_ref, sh_ref, sc_ref, o_ref):
    o_ref[...] = _modulated(x_ref[...], g_ref[...], sh_ref[...], sc_ref[...]).astype(o_ref.dtype)


def _split3(x):
    hi = x.astype(BF16)
    lo = (x - hi.astype(F32)).astype(BF16)
    return hi, lo


def _first_max(x, ids, n_ids, axes):
    mx = x
    for ax in axes:
        mx = jnp.max(mx, axis=ax, keepdims=True)
    arg = jnp.where(x == mx, ids, n_ids)
    for ax in axes:
        arg = jnp.min(arg, axis=ax, keepdims=True)
    return mx, arg


def _sum_axes(x, axes):
    for ax in axes:
        x = jnp.sum(x, axis=ax, keepdims=True)
    return x


def _modulate_router_kernel(x_ref, g_ref, sh_ref, sc_ref, wrt_ref, br_ref, cin_ref,
                            o_ref, idx_ref, wts_ref, rank_ref, cnt_ref, carry):
    h = _modulated(x_ref[...], g_ref[...], sh_ref[...], sc_ref[...])
    o_ref[...] = h.astype(o_ref.dtype)
    bm = h.shape[0]
    n_exp = wrt_ref.shape[0]
    per = n_exp // N_EXPERT_GROUPS
    grp_shape = (N_EXPERT_GROUPS, per, bm)

    h_hi, h_lo = _split3(h)
    w_hi, w_lo = _split3(wrt_ref[...])
    nt = (((1,), (1,)), ((), ()))
    logits = (lax.dot_general(w_hi, h_hi, nt, preferred_element_type=F32)
              + lax.dot_general(w_lo, h_hi, nt, preferred_element_type=F32)
              + lax.dot_general(w_hi, h_lo, nt, preferred_element_type=F32))
    s = jax.nn.sigmoid(logits)
    s3 = s.reshape(grp_shape)
    sb3 = (s + br_ref[...]).reshape(grp_shape)

    sub = lax.broadcasted_iota(jnp.int32, grp_shape, 1)
    gid = lax.broadcasted_iota(jnp.int32, (N_EXPERT_GROUPS, 1, bm), 0)
    eid = lax.broadcasted_iota(jnp.int32, grp_shape, 0) * per + sub

    m1, i1 = _first_max(sb3, sub, per, (1,))
    m2 = jnp.max(jnp.where(sub == i1, -jnp.inf, sb3), axis=1, keepdims=True)
    work = m1 + m2
    chosen = jnp.zeros_like(work)
    for _ in range(TOPK_GROUPS):
        _, gi = _first_max(work, gid, N_EXPERT_GROUPS, (0,))
        chosen = jnp.where(gid == gi, 1.0, chosen)
        work = jnp.where(gid == gi, -jnp.inf, work)
    sel = jnp.where(chosen > 0.0, sb3, -jnp.inf)

    hits, ids, raw = [], [], []
    for _ in range(TOP_K):
        _, ei = _first_max(sel, eid, n_exp, (1, 0))
        hit = eid == ei
        raw.append(_sum_axes(jnp.where(hit, s3, 0.0), (1, 0)))
        sel = jnp.where(hit, -jnp.inf, sel)
        hits.append(hit)
        ids.append(ei)
    total = sum(raw)

    member = sum(jnp.where(hit, 1.0, 0.0) for hit in hits).reshape(n_exp, bm)
    r_i = lax.broadcasted_iota(jnp.int32, (bm, bm), 0)
    c_i = lax.broadcasted_iota(jnp.int32, (bm, bm), 1)
    before = jnp.where(r_i < c_i, 1.0, 0.0).astype(BF16)

    @pl.when(pl.program_id(0) == 0)
    def _():
        carry[...] = cin_ref[...]

    rank_all = (jnp.dot(member.astype(BF16), before, preferred_element_type=F32) + carry[...]).reshape(grp_shape)
    carry[...] += jnp.sum(member, axis=1, keepdims=True)
    cnt_ref[...] = carry[...]
    for k in range(TOP_K):
        idx_ref[k:k + 1, :] = ids[k].reshape(1, bm)
        wts_ref[k:k + 1, :] = (raw[k] / total * ROUTED_SCALE).reshape(1, bm)
        rank_ref[k:k + 1, :] = _sum_axes(jnp.where(hits[k], rank_all, 0.0), (1, 0)).reshape(1, bm).astype(jnp.int32)


def _row_block(stream, m, pref):
    return _pick(m if stream["per_row"] else stream["rows_per_batch"], pref)


def _mod_spec(stream, bm, bn, ij):
    if stream["per_row"]:
        return pl.BlockSpec((None, bm, bn), lambda *g: (0,) + tuple(ij(*g)))
    bpg = stream["rows_per_batch"] // bm
    return pl.BlockSpec((None, 1, bn), lambda *g: (ij(*g)[0] // bpg, 0, ij(*g)[1]))


def modulate(stream, x, g, shift, scale, router=None):
    m, d = x.shape
    bm = _row_block(stream, m, 512)
    row_spec = pl.BlockSpec((bm, d), lambda i: (i, 0))
    mod_spec = _mod_spec(stream, bm, d, lambda i: (i, 0))
    g_spec = pl.BlockSpec((1, d), lambda i: (0, 0))
    if router is None:
        return pl.pallas_call(
            _modulate_kernel, grid=(m // bm,),
            in_specs=[row_spec, g_spec, mod_spec, mod_spec], out_specs=row_spec,
            out_shape=jax.ShapeDtypeStruct((m, d), BF16),
            compiler_params=_params(1), name="modulate",
        )(x, g.reshape(1, d), shift, scale)
    w_router, b_router, counts_in = router
    e = w_router.shape[1]
    tok_spec = pl.BlockSpec((TOP_K, bm), lambda i: (0, i))
    cnt_spec = pl.BlockSpec((e, 1), lambda i: (0, 0))
    return pl.pallas_call(
        _modulate_router_kernel, grid=(m // bm,),
        in_specs=[row_spec, g_spec, mod_spec, mod_spec, pl.BlockSpec((e, d), lambda i: (0, 0)), cnt_spec, cnt_spec],
        out_specs=[row_spec, tok_spec, tok_spec, tok_spec, cnt_spec],
        out_shape=[jax.ShapeDtypeStruct((m, d), BF16), jax.ShapeDtypeStruct((TOP_K, m), jnp.int32),
                   jax.ShapeDtypeStruct((TOP_K, m), F32), jax.ShapeDtypeStruct((TOP_K, m), jnp.int32),
                   jax.ShapeDtypeStruct((e, 1), F32)],
        scratch_shapes=[pltpu.VMEM((e, 1), F32)],
        compiler_params=_params(1), name="modulate_router",
    )(x, g.reshape(1, d), shift, scale, w_router.T, b_router.astype(F32).reshape(e, 1), counts_in)


def _mm_kernel(x_ref, w_ref, *refs, n_extra, epilogue):
    extra, outs, w_bf = refs[:n_extra], refs[n_extra:-1], refs[-1]

    @pl.when(pl.program_id(1) == 0)
    def _():
        w_bf[...] = w_ref[...].astype(BF16)

    acc = jnp.dot(x_ref[...], w_bf[...], preferred_element_type=F32)
    epilogue(acc, extra, outs)


def matmul(x, w, *, bm, bn, col_block_off=0, n_cols=None, extra=(), extra_specs=(), out_dtypes, epilogue, name):
    m, k = x.shape
    n = w.shape[1] if n_cols is None else n_cols
    grid = (n // bn, m // bm)
    out_spec = pl.BlockSpec((bm, bn), lambda j, i: (i, j))
    return pl.pallas_call(
        functools.partial(_mm_kernel, n_extra=len(extra), epilogue=epilogue),
        grid=grid,
        in_specs=[pl.BlockSpec((bm, k), lambda j, i: (i, 0)),
                  pl.BlockSpec((k, bn), lambda j, i: (0, j + col_block_off))] + list(extra_specs),
        out_specs=[out_spec] * len(out_dtypes),
        out_shape=[jax.ShapeDtypeStruct((m, n), dt) for dt in out_dtypes],
        scratch_shapes=[pltpu.VMEM((k, bn), BF16)],
        compiler_params=_params(2), name=name,
    )(x, w, *extra)


def _head_rms(acc, gain, post_scale):
    pieces = []
    for c in range(acc.shape[1] // ATTN_HEAD_DIM):
        seg = acc[:, c * ATTN_HEAD_DIM:(c + 1) * ATTN_HEAD_DIM]
        y = seg * lax.rsqrt(jnp.mean(seg * seg, axis=-1, keepdims=True) + EPS)
        pieces.append(y * (gain * post_scale))
    return jnp.concatenate(pieces, axis=-1)


def _q_epilogue(acc, extra, outs):
    outs[0][...] = _head_rms(acc, extra[0][...], ATTN_HEAD_DIM ** -0.5 * LOG2E).astype(BF16)


def _k_epilogue(acc, extra, outs):
    kn = _head_rms(acc, extra[0][...], 1.0)
    outs[0][...] = kn
    outs[1][...] = kn.astype(BF16)


def _v_epilogue(acc, extra, outs):
    outs[0][...] = acc
    outs[1][...] = acc.astype(BF16)


def _gelu_epilogue(acc, extra, outs):
    outs[0][...] = (0.5 * acc * (1.0 + lax.erf(acc * (2.0 ** -0.5)))).astype(BF16)


def _residual_epilogue(acc, extra, outs):
    x_ref, gate_ref = extra
    outs[0][...] = x_ref[...] + gate_ref[...] * acc


def qkv_project(stream, h, w_qkv, q_norm, k_norm):
    m, d = h.shape
    bm = _row_block(stream, m, 1024)
    bn = _pick(d, 1024)
    nb = d // bn
    gain_spec = pl.BlockSpec((1, ATTN_HEAD_DIM), lambda j, i: (0, 0))
    common = dict(bm=bm, bn=bn, n_cols=d)
    (q_bf,) = matmul(h, w_qkv, col_block_off=0, extra=(q_norm.reshape(1, -1),), extra_specs=(gain_spec,),
                     out_dtypes=(BF16,), epilogue=_q_epilogue, name="q_proj", **common)
    k32, k_bf = matmul(h, w_qkv, col_block_off=nb, extra=(k_norm.reshape(1, -1),), extra_specs=(gain_spec,),
                       out_dtypes=(F32, BF16), epilogue=_k_epilogue, name="k_proj", **common)
    v32, v_bf = matmul(h, w_qkv, col_block_off=2 * nb, out_dtypes=(F32, BF16), epilogue=_v_epilogue,
                       name="v_proj", **common)
    return q_bf, k32, k_bf, v32, v_bf


def project_residual(stream, a, w, x, gate, name):
    k, n = w.shape
    deep = k > 4096
    bm = _row_block(stream, a.shape[0], 512 if deep else 1024)
    bn = _pick(n, 512 if deep else 1024)
    (out,) = matmul(a, w, bm=bm, bn=bn, extra=(x, gate),
                    extra_specs=(pl.BlockSpec((bm, bn), lambda j, i: (i, j)),
                                 _mod_spec(stream, bm, bn, lambda j, i: (i, j))),
                    out_dtypes=(F32,), epilogue=_residual_epilogue, name=name)
    return out


def _finish_heads(acc1, l1, acc2, l2, lam, g, post_scale):
    o = acc1 / l1 - lam * (acc2 / l2)
    o = o * lax.rsqrt(jnp.mean(o * o, axis=-1, keepdims=True) + EPS)
    return o * (g * post_scale)


def _attn_prompt_kernel(lam_ref, slope_ref, q_ref, k_ref, v_ref, g_ref, o_ref,
                        s_scr, p_scr, acc_scr, m_scr, l_scr, a_scr, *, bq, post_scale):
    hd = ATTN_HEAD_DIM
    rc = ATTN_ROW_CHUNK
    qi = pl.program_id(2)
    sl2 = slope_ref[pl.program_id(1)]
    col = lax.broadcasted_iota(jnp.int32, (1, bq), 1)

    m_scr[...] = jnp.full(m_scr.shape, NEG_BIG, F32)
    l_scr[...] = jnp.zeros(l_scr.shape, F32)
    acc_scr[...] = jnp.zeros(acc_scr.shape, F32)

    def scores(kb, slot):
        k0 = pl.multiple_of(kb * bq, bq)
        for mi in range(2):
            s_scr[2 * slot + mi] = lax.dot_general(
                q_ref[:, mi * hd:(mi + 1) * hd], k_ref[pl.ds(k0, bq), mi * hd:(mi + 1) * hd],
                (((1,), (1,)), ((), ())), preferred_element_type=F32)

    def block(kb, slot, bias_row, diag):
        k0 = pl.multiple_of(kb * bq, bq)
        for mi in range(2):
            s_map = s_scr.at[2 * slot + mi]
            for c in range(bq // rc):
                rows = slice(c * rc, (c + 1) * rc)
                ncol = min(bq, -(-((c + 1) * rc) // CHUNK) * CHUNK) if diag else bq
                tiles = []
                for t in range(-(-ncol // LANES)):
                    lanes = slice(t * LANES, (t + 1) * LANES)
                    s_t = s_map[rows, lanes]
                    if diag:
                        r_i = lax.broadcasted_iota(jnp.int32, (rc, LANES), 0) + c * rc
                        c_i = lax.broadcasted_iota(jnp.int32, (rc, LANES), 1) + t * LANES
                        bias = sl2 * (r_i - jnp.abs(r_i - c_i)).astype(F32)
                        tiles.append(jnp.where((c_i // CHUNK) <= (r_i // CHUNK), s_t + bias, NEG_BIG))
                    else:
                        tiles.append(s_t + bias_row[:, lanes])
                m_old = m_scr[mi, rows, :]
                m_new = jnp.maximum(m_old, jnp.max(functools.reduce(jnp.maximum, tiles), axis=-1, keepdims=True))
                alpha = jnp.exp2(m_old - m_new)
                ps = [jnp.exp2(s_t - m_new) for s_t in tiles]
                l_scr[mi, rows, :] = alpha * l_scr[mi, rows, :] + functools.reduce(jnp.add, ps)
                m_scr[mi, rows, :] = m_new
                a_scr[mi, rows, :] = alpha
                for t, p_t in enumerate(ps):
                    p_scr[mi, rows, t * LANES:(t + 1) * LANES] = p_t.astype(BF16)
                if len(ps) * LANES < bq:
                    p_scr[mi, rows, len(ps) * LANES:] = jnp.zeros((rc, bq - len(ps) * LANES), BF16)
            pv = jnp.dot(p_scr[mi], v_ref[pl.ds(k0, bq), :], preferred_element_type=F32)
            for t in range(pv.shape[1] // LANES):
                lanes = slice(t * LANES, (t + 1) * LANES)
                acc_scr[mi, :, lanes] = a_scr[mi] * acc_scr[mi, :, lanes] + pv[:, lanes]

    def bias_row(kb):
        return sl2 * (col + (kb - qi) * bq).astype(F32)

    def pair(j2, carry):
        kb = 2 * j2
        scores(kb + 1, 1)
        block(kb, 0, bias_row(kb), False)
        scores(kb + 2, 0)
        block(kb + 1, 1, bias_row(kb + 1), False)
        return carry

    scores(0, 0)
    lax.fori_loop(0, qi // 2, pair, 0)

    @pl.when(qi % 2 == 1)
    def _():
        scores(qi, 1)
        block(qi - 1, 0, bias_row(qi - 1), False)
        block(qi, 1, None, True)

    @pl.when(qi % 2 == 0)
    def _():
        block(qi, 0, None, True)
    l1 = jnp.sum(l_scr[0], axis=-1, keepdims=True)
    l2 = jnp.sum(l_scr[1], axis=-1, keepdims=True)
    o_ref[...] = _finish_heads(acc_scr[0], l1, acc_scr[1], l2, lam_ref[0], g_ref[...], post_scale).astype(o_ref.dtype)


def attention_prompt(q_bf, k_bf, v_bf, batch, lam, slopes2, subln_g, post_scale):
    m, width = q_bf.shape
    s = m // batch
    dv = 2 * ATTN_HEAD_DIM
    heads = width // dv
    bq = _pick(s, 512)
    assert bq % CHUNK == 0
    nq = s // bq
    kv_spec = pl.BlockSpec((s, dv), lambda b, h, i: (b, h))
    q_spec = pl.BlockSpec((bq, dv), lambda b, h, i: (b * nq + i, h))
    smem = pl.BlockSpec(memory_space=pltpu.SMEM)
    return pl.pallas_call(
        functools.partial(_attn_prompt_kernel, bq=bq, post_scale=post_scale),
        grid=(batch, heads, nq),
        in_specs=[smem, smem, q_spec, kv_spec, kv_spec, pl.BlockSpec((1, dv), lambda b, h, i: (0, 0))],
        out_specs=q_spec,
        out_shape=jax.ShapeDtypeStruct((m, width), BF16),
        scratch_shapes=[pltpu.VMEM((4, bq, bq), F32), pltpu.VMEM((2, bq, bq), BF16), pltpu.VMEM((2, bq, dv), F32)]
        + [pltpu.VMEM((2, bq, LANES), F32)] * 3,
        compiler_params=_params(3), name="attention_prompt",
    )(lam, slopes2, q_bf, k_bf, v_bf, subln_g.reshape(1, dv))


def _attn_sample_kernel(lam_ref, slope_ref, q_ref, kn_ref, vn_ref, ck_ref, cv_ref, g_ref, o_ref, *, post_scale):
    hd = ATTN_HEAD_DIM
    sl2 = slope_ref[pl.program_id(1)]
    lam = lam_ref[0]
    t = q_ref.shape[0]
    past = ck_ref.shape[0]
    q = q_ref[...]
    k_new = kn_ref[...]
    v_new = vn_ref[...]
    k_past = ck_ref[...].astype(BF16)
    v_past = cv_ref[...].astype(BF16)

    def bias_mask(n_keys, key0):
        q_pos = past + lax.broadcasted_iota(jnp.int32, (t, n_keys), 0)
        k_pos = key0 + lax.broadcasted_iota(jnp.int32, (t, n_keys), 1)
        return -sl2 * jnp.abs(q_pos - k_pos).astype(F32), (k_pos // CHUNK) <= (q_pos // CHUNK)

    bias_p, vis_p = bias_mask(past, 0)
    bias_n, vis_n = bias_mask(t, past)
    res = []
    for mi in range(2):
        qm = q[:, mi * hd:(mi + 1) * hd]
        dn = (((1,), (1,)), ((), ()))
        s_p = lax.dot_general(qm, k_past[:, mi * hd:(mi + 1) * hd], dn, preferred_element_type=F32)
        s_n = lax.dot_general(qm, k_new[:, mi * hd:(mi + 1) * hd], dn, preferred_element_type=F32)
        s_p = jnp.where(vis_p, s_p + bias_p, NEG_BIG)
        s_n = jnp.where(vis_n, s_n + bias_n, NEG_BIG)
        m = jnp.maximum(jnp.max(s_p, axis=-1, keepdims=True), jnp.max(s_n, axis=-1, keepdims=True))
        p_p = jnp.exp2(s_p - m)
        p_n = jnp.exp2(s_n - m)
        l = jnp.sum(p_p, axis=-1, keepdims=True) + jnp.sum(p_n, axis=-1, keepdims=True)
        acc = (jnp.dot(p_p.astype(BF16), v_past, preferred_element_type=F32)
               + jnp.dot(p_n.astype(BF16), v_new, preferred_element_type=F32))
        res += [acc, l]
    o_ref[...] = _finish_heads(res[0], res[1], res[2], res[3], lam, g_ref[...], post_scale).astype(o_ref.dtype)


def attention_sample(q_bf, k_bf, v_bf, cache_k, cache_v, lam, slopes2, subln_g, post_scale):
    batch, past, width = cache_k.shape
    m = q_bf.shape[0]
    t = m // batch
    dv = 2 * ATTN_HEAD_DIM
    heads = width // dv
    new_spec = pl.BlockSpec((t, dv), lambda b, h: (b, h))
    cache_spec = pl.BlockSpec((None, past, dv), lambda b, h: (b, 0, h))
    smem = pl.BlockSpec(memory_space=pltpu.SMEM)
    return pl.pallas_call(
        functools.partial(_attn_sample_kernel, post_scale=post_scale),
        grid=(batch, heads),
        in_specs=[smem, smem, new_spec, new_spec, new_spec, cache_spec, cache_spec,
                  pl.BlockSpec((1, dv), lambda b, h: (0, 0))],
        out_specs=new_spec,
        out_shape=jax.ShapeDtypeStruct((m, width), BF16),
        compiler_params=_params(2), name="attention_sample",
    )(lam, slopes2, q_bf, k_bf, v_bf, cache_k, cache_v, subln_g.reshape(1, dv))


def _sgu_gate_kernel(u_ref, v_ref, g_ref, b_ref, w_ref, bs_ref, *out_refs, chunk_len, write_v):
    bt, width = u_ref.shape
    gd = width // SGU_GROUPS
    v = v_ref[...].astype(F32)
    mu = jnp.mean(v, axis=-1, keepdims=True)
    vc = v - mu
    vn = vc * lax.rsqrt(jnp.mean(vc * vc, axis=-1, keepdims=True) + EPS) * g_ref[...] + b_ref[...]
    if write_v:
        out_refs[1][...] = vn
    vn_bf = vn.astype(BF16)
    row = lax.broadcasted_iota(jnp.int32, (SGU_CHUNK, SGU_CHUNK), 0)
    col = lax.broadcasted_iota(jnp.int32, (SGU_CHUNK, SGU_CHUNK), 1)
    keep = ((row // chunk_len) == (col // chunk_len)) & ((col % chunk_len) <= (row % chunk_len))
    for gi in range(SGU_GROUPS):
        w = jnp.where(keep, w_ref[gi], 0.0).astype(BF16)
        bias = bs_ref[gi]
        for c in range(bt // SGU_CHUNK):
            rows = slice(c * SGU_CHUNK, (c + 1) * SGU_CHUNK)
            cols = slice(gi * gd, (gi + 1) * gd)
            mixed = jnp.dot(w, vn_bf[rows, cols], preferred_element_type=F32) + bias
            out_refs[0][rows, cols] = (u_ref[rows, cols].astype(F32) * mixed).astype(BF16)


def sgu_gate(z, ln_g, ln_b, w_s, b_s, chunk_len, write_v):
    m, two_w = z.shape
    width = two_w // 2
    reps = SGU_CHUNK // chunk_len
    w_t = jnp.tile(w_s[:, :chunk_len, :chunk_len], (1, reps, reps))
    b_t = jnp.tile(b_s[:, :chunk_len], (1, reps))[:, :, None]
    bt = _pick(m, 256)
    row_spec = lambda c: pl.BlockSpec((bt, width), lambda i, c=c: (i, c))
    vec_spec = pl.BlockSpec((1, width), lambda i: (0, 0))
    out_specs = [row_spec(0)]
    out_shape = [jax.ShapeDtypeStruct((m, width), BF16)]
    if write_v:
        out_specs.append(row_spec(0))
        out_shape.append(jax.ShapeDtypeStruct((m, width), F32))
    return pl.pallas_call(
        functools.partial(_sgu_gate_kernel, chunk_len=chunk_len, write_v=write_v),
        grid=(m // bt,),
        in_specs=[row_spec(0), row_spec(1), vec_spec, vec_spec,
                  pl.BlockSpec((SGU_GROUPS, SGU_CHUNK, SGU_CHUNK), lambda i: (0, 0, 0)),
                  pl.BlockSpec((SGU_GROUPS, SGU_CHUNK, 1), lambda i: (0, 0, 0))],
        out_specs=out_specs, out_shape=out_shape,
        compiler_params=_params(1), name="sgu_gate",
    )(z, z, ln_g.reshape(1, width), ln_b.reshape(1, width), w_t, b_t)


def _swiglu(x, wg, wu, wd):
    g = jnp.dot(x, wg, preferred_element_type=F32)
    u = jnp.dot(x, wu, preferred_element_type=F32)
    h = (g * jax.nn.sigmoid(g) * u).astype(BF16)
    return jnp.dot(h, wd, preferred_element_type=F32)


def _moe_kernel(blk_e_ref, n_used_ref, x_ref, wg_ref, wu_ref, wd_ref, o_ref, wg_bf, wu_bf, wd_bf):
    b = pl.program_id(0)
    e = blk_e_ref[b]
    e_prev = blk_e_ref[jnp.maximum(b - 1, 0)]

    @pl.when((b == 0) | (e != e_prev))
    def _():
        wg_bf[...] = wg_ref[...].astype(BF16)
        wu_bf[...] = wu_ref[...].astype(BF16)
        wd_bf[...] = wd_ref[...].astype(BF16)

    @pl.when(b < n_used_ref[0])
    def _():
        o_ref[...] = _swiglu(x_ref[...], wg_bf[...], wu_bf[...], wd_bf[...]).astype(o_ref.dtype)

    @pl.when(b >= n_used_ref[0])
    def _():
        o_ref[...] = jnp.zeros_like(o_ref)


def moe_experts(xg, blk_expert, n_used, layer, w_gate, w_up, w_down):
    rows, d = xg.shape
    ed = w_gate.shape[3]
    n_blocks = rows // MOE_BLOCK
    x_spec = pl.BlockSpec((MOE_BLOCK, d), lambda b, be, nu: (b, 0))
    return pl.pallas_call(
        _moe_kernel,
        grid_spec=pltpu.PrefetchScalarGridSpec(
            num_scalar_prefetch=2, grid=(n_blocks,),
            in_specs=[x_spec,
                      pl.BlockSpec((None, None, d, ed), lambda b, be, nu: (layer, be[b], 0, 0)),
                      pl.BlockSpec((None, None, d, ed), lambda b, be, nu: (layer, be[b], 0, 0)),
                      pl.BlockSpec((None, None, ed, d), lambda b, be, nu: (layer, be[b], 0, 0))],
            out_specs=x_spec,
            scratch_shapes=[pltpu.VMEM((d, ed), BF16), pltpu.VMEM((d, ed), BF16), pltpu.VMEM((ed, d), BF16)]),
        out_shape=jax.ShapeDtypeStruct((rows, d), BF16),
        compiler_params=_params(1), name="moe_experts",
    )(blk_expert, n_used, xg, w_gate, w_up, w_down)


def _shared_kernel(h_ref, p_ref, w_ref, x_ref, gate_ref, wg_ref, wu_ref, wd_ref, o_ref):
    ffn = _swiglu(h_ref[...], wg_ref[...], wu_ref[...], wd_ref[...])
    w = w_ref[...]
    for k in range(TOP_K):
        ffn = ffn + w[:, k:k + 1] * p_ref[k].astype(F32)
    o_ref[...] = x_ref[...] + gate_ref[...] * ffn


def shared_expert_residual(stream, h, picked, wts, row0, x, gate, sw_gate, sw_up, sw_down):
    m, d = x.shape
    ed = sw_gate.shape[1]
    bm = _row_block(stream, m, 256)
    off = row0 // bm
    row_spec = pl.BlockSpec((bm, d), lambda i: (i, 0))
    full = lambda a, b: pl.BlockSpec((a, b), lambda i: (0, 0))
    return pl.pallas_call(
        _shared_kernel, grid=(m // bm,),
        in_specs=[row_spec, pl.BlockSpec((TOP_K, bm, d), lambda i: (0, i + off, 0)),
                  pl.BlockSpec((bm, TOP_K), lambda i: (i + off, 0)), row_spec,
                  _mod_spec(stream, bm, d, lambda i: (i, 0)), full(d, ed), full(d, ed), full(ed, d)],
        out_specs=row_spec,
        out_shape=jax.ShapeDtypeStruct((m, d), F32),
        compiler_params=_params(1), name="shared_expert",
    )(h, picked, wts, x, gate, sw_gate, sw_up, sw_down)


def _dest_kernel(start_ref, idx_ref, rank_ref, dest_ref):
    idx = idx_ref[...]

    def body(e, acc):
        return jnp.where(idx == e, start_ref[e], acc)

    dest_ref[...] = rank_ref[...] + lax.fori_loop(0, start_ref.shape[0], body, jnp.zeros_like(idx))


def assignment_rows(pad_start, idx, rank):
    return pl.pallas_call(
        _dest_kernel,
        in_specs=[pl.BlockSpec(memory_space=pltpu.SMEM), pl.BlockSpec(memory_space=pltpu.VMEM),
                  pl.BlockSpec(memory_space=pltpu.VMEM)],
        out_specs=pl.BlockSpec(memory_space=pltpu.VMEM),
        out_shape=jax.ShapeDtypeStruct(idx.shape, jnp.int32),
        name="assignment_rows",
    )(pad_start, idx, rank)


def moe_routed(h_all, idx, rank, counts, layer, w_gate, w_up, w_down):
    t, d = h_all.shape
    n_exp = counts.shape[0]
    n_assign = t * TOP_K
    counts = counts.reshape(n_exp).astype(jnp.int32)
    padded = (counts + MOE_BLOCK - 1) // MOE_BLOCK * MOE_BLOCK
    pad_end = jnp.cumsum(padded)
    pad_start = pad_end - padded
    dest = assignment_rows(pad_start, idx, rank).reshape(-1)
    n_blocks = -(-(n_assign + n_exp * (MOE_BLOCK - 1)) // MOE_BLOCK)
    n_rows = n_blocks * MOE_BLOCK
    tok = jnp.tile(jnp.arange(t, dtype=jnp.int32), TOP_K)
    src_tok = (jnp.arange(n_rows, dtype=jnp.int32) % t).at[dest].set(tok, unique_indices=True,
                                                                      mode="promise_in_bounds")
    blk_start = jnp.arange(n_blocks, dtype=jnp.int32) * MOE_BLOCK
    blk_expert = jnp.minimum(jnp.sum((pad_end[None, :] <= blk_start[:, None]).astype(jnp.int32), axis=1), n_exp - 1)
    n_used = pad_end[-1:] // MOE_BLOCK
    xg = h_all.at[src_tok].get(mode="promise_in_bounds")
    out = moe_experts(xg, blk_expert, n_used, layer, w_gate, w_up, w_down)
    return out.at[dest].get(mode="promise_in_bounds", unique_indices=True).reshape(TOP_K, t, d)


def _stream(rows_per_batch):
    return dict(per_row=rows_per_batch < 1024, rows_per_batch=rows_per_batch)


def _mod_rows(stream, mod):
    b, d = mod.shape
    if not stream["per_row"]:
        return mod.reshape(b, 1, d)
    return jnp.repeat(mod, stream["rows_per_batch"], axis=0).reshape(1, -1, d)


def kernel(x_prompt, x_sample, cache_k_attn, cache_v_attn, c_prompt, c_sample, norm_mix_g, norm_ffn_g, ada_w, ada_b, attn_w_qkv, attn_w_o, attn_q_norm, attn_k_norm, attn_lambda_q1, attn_lambda_k1, attn_lambda_q2, attn_lambda_k2, attn_subln_g, sgu_w_in, sgu_ln_g, sgu_ln_b, sgu_w_s, sgu_b_s, sgu_w_o, moe_w_router, moe_b_router, moe_w_gate, moe_w_up, moe_w_down, shared_w_gate, shared_w_up, shared_w_down):
    bp, sp, d = x_prompt.shape
    bs, ss, _ = x_sample.shape
    depth = ada_w.shape[0]
    dv = 2 * ATTN_HEAD_DIM
    heads = d // dv
    mp, ms = bp * sp, bs * ss
    streams = (_stream(sp), _stream(ss))
    xs = [x_prompt.reshape(mp, d), x_sample.reshape(ms, d)]

    mod = ada_modulation(jnp.concatenate([c_prompt, c_sample], axis=0), ada_w, ada_b)
    slopes2 = (2.0 ** (-8.0 * jnp.arange(1, heads + 1, dtype=F32) / heads)) * LOG2E

    k_out, v_out, sgu_out = [[], []], [[], []], []
    for i in range(depth):
        j = i // N_MIXERS
        mods = []
        for si, (stream, rows) in enumerate(zip(streams, (slice(0, bp), slice(bp, bp + bs)))):
            mods.append([_mod_rows(stream, mod[i, rows, c * d:(c + 1) * d]) for c in range(6)])

        if i % N_MIXERS == 0:
            lam_init = 0.8 - 0.6 * math.exp(-0.3 * i)
            f = lambda a: a[j].astype(F32)
            lam = (jnp.exp(jnp.sum(f(attn_lambda_q1) * f(attn_lambda_k1)))
                   - jnp.exp(jnp.sum(f(attn_lambda_q2) * f(attn_lambda_k2))) + lam_init).reshape(1)
            for si, stream in enumerate(streams):
                h = modulate(stream, xs[si], norm_mix_g[i], mods[si][0], mods[si][1])
                q_bf, k32, k_bf, v32, v_bf = qkv_project(stream, h, attn_w_qkv[j], attn_q_norm[j], attn_k_norm[j])
                if si == 0:
                    o = attention_prompt(q_bf, k_bf, v_bf, bp, lam, slopes2, attn_subln_g[j], 1.0 - lam_init)
                else:
                    past = cache_k_attn.shape[2]
                    o = attention_sample(q_bf, k_bf, v_bf, cache_k_attn[j].reshape(bs, past, d),
                                         cache_v_attn[j].reshape(bs, past, d), lam, slopes2, attn_subln_g[j],
                                         1.0 - lam_init)
                xs[si] = project_residual(stream, o, attn_w_o[j], xs[si], mods[si][2], "attn_out_proj")
                k_out[si].append(k32)
                v_out[si].append(v32)
        else:
            for si, stream in enumerate(streams):
                h = modulate(stream, xs[si], norm_mix_g[i], mods[si][0], mods[si][1])
                (z,) = matmul(h, sgu_w_in[j], bm=_row_block(stream, h.shape[0], 1024),
                              bn=_pick(sgu_w_in.shape[2], 1024),
                              out_dtypes=(BF16,), epilogue=_gelu_epilogue, name="sgu_in_proj")
                chunk_len = SGU_CHUNK if si == 0 else ss
                res = sgu_gate(z, sgu_ln_g[j], sgu_ln_b[j], sgu_w_s[j], sgu_b_s[j], chunk_len, write_v=(si == 1))
                if si == 1:
                    sgu_out.append(res[1])
                xs[si] = project_residual(stream, res[0], sgu_w_o[j], xs[si], mods[si][2], "sgu_out_proj")

        hs, routing = [], []
        counts = jnp.zeros((moe_w_router.shape[2], 1), F32)
        for si, stream in enumerate(streams):
            hf, idx, wts, rank, counts = modulate(stream, xs[si], norm_ffn_g[i], mods[si][3], mods[si][4],
                                                  router=(moe_w_router[i], moe_b_router[i], counts))
            hs.append(hf)
            routing.append((idx, wts, rank))
        idx, wts, rank = (jnp.concatenate(parts, axis=1) for parts in zip(*routing))
        picked = moe_routed(jnp.concatenate(hs, axis=0), idx, rank, counts, i, moe_w_gate, moe_w_up, moe_w_down)
        sw = [w[i].astype(BF16) for w in (shared_w_gate, shared_w_up, shared_w_down)]
        for si, (stream, row0) in enumerate(zip(streams, (0, mp))):
            xs[si] = shared_expert_residual(stream, hs[si], picked, wts.T, row0, xs[si], mods[si][5], *sw)

    n_attn = len(k_out[0])
    k_prompt = jnp.stack(k_out[0]).reshape(n_attn, bp, sp, heads, 2, ATTN_HEAD_DIM)
    v_prompt = jnp.stack(v_out[0]).reshape(n_attn, bp, sp, heads, dv)
    k_sample = jnp.stack(k_out[1]).reshape(n_attn, bs, ss, heads, 2, ATTN_HEAD_DIM)
    v_sample = jnp.stack(v_out[1]).reshape(n_attn, bs, ss, heads, dv)
    sgu_v = jnp.stack(sgu_out).reshape(len(sgu_out), bs, ss, -1)
    return (xs[0].reshape(bp, sp, d), xs[1].reshape(bs, ss, d), k_prompt, v_prompt, k_sample, v_sample, sgu_v)
```

```python
import functools
import math

import jax
import jax.numpy as jnp
from jax import lax
from jax.experimental import pallas as pl
from jax.experimental.pallas import tpu as pltpu

F32 = jnp.float32
BF16 = jnp.bfloat16

EPS = 1e-6
CHUNK = 64
N_MIXERS = 2
ATTN_HEAD_DIM = 128
SGU_GROUPS = 8
SGU_CHUNK = 128
TOP_K = 8
N_EXPERT_GROUPS = 8
TOPK_GROUPS = 4
ROUTED_SCALE = 2.5
LOG2E = 1.4426950408889634
NEG_BIG = -1e30

VMEM_LIMIT = 56 * 1024 * 1024
MOE_BLOCK = 512
ATTN_ROW_CHUNK = 64
LANES = 128


def _params(n_axes):
    return pltpu.CompilerParams(dimension_semantics=("arbitrary",) * n_axes, vmem_limit_bytes=VMEM_LIMIT)


def _pick(n, pref):
    if n <= pref:
        return n
    b = pref
    while n % b:
        b //= 2
    return b


def _ada_kernel(c_ref, w_ref, b_ref, o_ref):
    c = c_ref[...]
    x = (c * jax.nn.sigmoid(c)).astype(BF16)
    o_ref[...] = jnp.dot(x, w_ref[...].astype(BF16), preferred_element_type=F32) + b_ref[...]


def ada_modulation(c, ada_w, ada_b):
    n_layers, d, n = ada_w.shape
    r = c.shape[0]
    bn = _pick(n, 1024)
    return pl.pallas_call(
        _ada_kernel,
        grid=(n_layers, n // bn),
        in_specs=[
            pl.BlockSpec((r, d), lambda l, j: (0, 0)),
            pl.BlockSpec((None, d, bn), lambda l, j: (l, 0, j)),
            pl.BlockSpec((None, 1, bn), lambda l, j: (l, 0, j)),
        ],
        out_specs=pl.BlockSpec((None, r, bn), lambda l, j: (l, 0, j)),
        out_shape=jax.ShapeDtypeStruct((n_layers, r, n), F32),
        compiler_params=_params(2),
        name="ada_modulation",
    )(c, ada_w, ada_b.reshape(n_layers, 1, n))


def _modulated(x, g, shift, scale):
    y = x * lax.rsqrt(jnp.mean(x * x, axis=-1, keepdims=True) + EPS)
    return y * g * (1.0 + scale) + shift


def _modulate_kernel(x_ref, g_ref, sh_ref, sc_ref, o_ref):
    o_ref[...] = _modulated(x_ref[...], g_ref[...], sh_ref[...], sc_ref[...]).astype(o_ref.dtype)


def _split3(x):
    hi = x.astype(BF16)
    lo = (x - hi.astype(F32)).astype(BF16)
    return hi, lo


def _first_max(x, ids, n_ids, axes):
    mx = x
    for ax in axes:
        mx = jnp.max(mx, axis=ax, keepdims=True)
    arg = jnp.where(x == mx, ids, n_ids)
    for ax in axes:
        arg = jnp.min(arg, axis=ax, keepdims=True)
    return mx, arg


def _sum_axes(x, axes):
    for ax in axes:
        x = jnp.sum(x, axis=ax, keepdims=True)
    return x


def _modulate_router_kernel(x_ref, g_ref, sh_ref, sc_ref, wrt_ref, br_ref, cin_ref,
                            o_ref, idx_ref, wts_ref, rank_ref, cnt_ref, carry):
    h = _modulated(x_ref[...], g_ref[...], sh_ref[...], sc_ref[...])
    o_ref[...] = h.astype(o_ref.dtype)
    bm = h.shape[0]
    n_exp = wrt_ref.shape[0]
    per = n_exp // N_EXPERT_GROUPS
    grp_shape = (N_EXPERT_GROUPS, per, bm)

    h_hi, h_lo = _split3(h)
    w_hi, w_lo = _split3(wrt_ref[...])
    nt = (((1,), (1,)), ((), ()))
    logits = (lax.dot_general(w_hi, h_hi, nt, preferred_element_type=F32)
              + lax.dot_general(w_lo, h_hi, nt, preferred_element_type=F32)
              + lax.dot_general(w_hi, h_lo, nt, preferred_element_type=F32))
    s = jax.nn.sigmoid(logits)
    s3 = s.reshape(grp_shape)
    sb3 = (s + br_ref[...]).reshape(grp_shape)

    sub = lax.broadcasted_iota(jnp.int32, grp_shape, 1)
    gid = lax.broadcasted_iota(jnp.int32, (N_EXPERT_GROUPS, 1, bm), 0)
    eid = lax.broadcasted_iota(jnp.int32, grp_shape, 0) * per + sub

    m1, i1 = _first_max(sb3, sub, per, (1,))
    m2 = jnp.max(jnp.where(sub == i1, -jnp.inf, sb3), axis=1, keepdims=True)
    work = m1 + m2
    chosen = jnp.zeros_like(work)
    for _ in range(TOPK_GROUPS):
        _, gi = _first_max(work, gid, N_EXPERT_GROUPS, (0,))
        chosen = jnp.where(gid == gi, 1.0, chosen)
        work = jnp.where(gid == gi, -jnp.inf, work)
    sel = jnp.where(chosen > 0.0, sb3, -jnp.inf)

    hits, ids, raw = [], [], []
    for _ in range(TOP_K):
        _, ei = _first_max(sel, eid, n_exp, (1, 0))
        hit = eid == ei
        raw.append(_sum_axes(jnp.where(hit, s3, 0.0), (1, 0)))
        sel = jnp.where(hit, -jnp.inf, sel)
        hits.append(hit)
        ids.append(ei)
    total = sum(raw)

    member = sum(jnp.where(hit, 1.0, 0.0) for hit in hits).reshape(n_exp, bm)
    r_i = lax.broadcasted_iota(jnp.int32, (bm, bm), 0)
    c_i = lax.broadcasted_iota(jnp.int32, (bm, bm), 1)
    before = jnp.where(r_i < c_i, 1.0, 0.0).astype(BF16)

    @pl.when(pl.program_id(0) == 0)
    def _():
        carry[...] = cin_ref[...]

    rank_all = (jnp.dot(member.astype(BF16), before, preferred_element_type=F32) + carry[...]).reshape(grp_shape)
    carry[...] += jnp.sum(member, axis=1, keepdims=True)
    cnt_ref[...] = carry[...]
    for k in range(TOP_K):
        idx_ref[k:k + 1, :] = ids[k].reshape(1, bm)
        wts_ref[k:k + 1, :] = (raw[k] / total * ROUTED_SCALE).reshape(1, bm)
        rank_ref[k:k + 1, :] = _sum_axes(jnp.where(hits[k], rank_all, 0.0), (1, 0)).reshape(1, bm).astype(jnp.int32)


def _row_block(stream, m, pref):
    return _pick(m if stream["per_row"] else stream["rows_per_batch"], pref)


def _mod_spec(stream, bm, bn, ij):
    if stream["per_row"]:
        return pl.BlockSpec((None, bm, bn), lambda *g: (0,) + tuple(ij(*g)))
    bpg = stream["rows_per_batch"] // bm
    return pl.BlockSpec((None, 1, bn), lambda *g: (ij(*g)[0] // bpg, 0, ij(*g)[1]))


def modulate(stream, x, g, shift, scale, router=None):
    m, d = x.shape
    bm = _row_block(stream, m, 512)
    row_spec = pl.BlockSpec((bm, d), lambda i: (i, 0))
    mod_spec = _mod_spec(stream, bm, d, lambda i: (i, 0))
    g_spec = pl.BlockSpec((1, d), lambda i: (0, 0))
    if router is None:
        return pl.pallas_call(
            _modulate_kernel, grid=(m // bm,),
            in_specs=[row_spec, g_spec, mod_spec, mod_spec], out_specs=row_spec,
            out_shape=jax.ShapeDtypeStruct((m, d), BF16),
            compiler_params=_params(1), name="modulate",
        )(x, g.reshape(1, d), shift, scale)
    w_router, b_router, counts_in = router
    e = w_router.shape[1]
    tok_spec = pl.BlockSpec((TOP_K, bm), lambda i: (0, i))
    cnt_spec = pl.BlockSpec((e, 1), lambda i: (0, 0))
    return pl.pallas_call(
        _modulate_router_kernel, grid=(m // bm,),
        in_specs=[row_spec, g_spec, mod_spec, mod_spec, pl.BlockSpec((e, d), lambda i: (0, 0)), cnt_spec, cnt_spec],
        out_specs=[row_spec, tok_spec, tok_spec, tok_spec, cnt_spec],
        out_shape=[jax.ShapeDtypeStruct((m, d), BF16), jax.ShapeDtypeStruct((TOP_K, m), jnp.int32),
                   jax.ShapeDtypeStruct((TOP_K, m), F32), jax.ShapeDtypeStruct((TOP_K, m), jnp.int32),
                   jax.ShapeDtypeStruct((e, 1), F32)],
        scratch_shapes=[pltpu.VMEM((e, 1), F32)],
        compiler_params=_params(1), name="modulate_router",
    )(x, g.reshape(1, d), shift, scale, w_router.T, b_router.astype(F32).reshape(e, 1), counts_in)


def _mm_kernel(x_ref, w_ref, *refs, n_extra, epilogue):
    extra, outs, w_bf = refs[:n_extra], refs[n_extra:-1], refs[-1]

    @pl.when(pl.program_id(1) == 0)
    def _():
        w_bf[...] = w_ref[...].astype(BF16)

    acc = jnp.dot(x_ref[...], w_bf[...], preferred_element_type=F32)
    epilogue(acc, extra, outs)


def matmul(x, w, *, bm, bn, col_block_off=0, n_cols=None, extra=(), extra_specs=(), out_dtypes, epilogue, name):
    m, k = x.shape
    n = w.shape[1] if n_cols is None else n_cols
    grid = (n // bn, m // bm)
    out_spec = pl.BlockSpec((bm, bn), lambda j, i: (i, j))
    return pl.pallas_call(
        functools.partial(_mm_kernel, n_extra=len(extra), epilogue=epilogue),
        grid=grid,
        in_specs=[pl.BlockSpec((bm, k), lambda j, i: (i, 0)),
                  pl.BlockSpec((k, bn), lambda j, i: (0, j + col_block_off))] + list(extra_specs),
        out_specs=[out_spec] * len(out_dtypes),
        out_shape=[jax.ShapeDtypeStruct((m, n), dt) for dt in out_dtypes],
        scratch_shapes=[pltpu.VMEM((k, bn), BF16)],
        compiler_params=_params(2), name=name,
    )(x, w, *extra)


def _head_rms(acc, gain, post_scale):
    pieces = []
    for c in range(acc.shape[1] // ATTN_HEAD_DIM):
        seg = acc[:, c * ATTN_HEAD_DIM:(c + 1) * ATTN_HEAD_DIM]
        y = seg * lax.rsqrt(jnp.mean(seg * seg, axis=-1, keepdims=True) + EPS)
        pieces.append(y * (gain * post_scale))
    return jnp.concatenate(pieces, axis=-1)


def _q_epilogue(acc, extra, outs):
    outs[0][...] = _head_rms(acc, extra[0][...], ATTN_HEAD_DIM ** -0.5 * LOG2E).astype(BF16)


def _k_epilogue(acc, extra, outs):
    kn = _head_rms(acc, extra[0][...], 1.0)
    outs[0][...] = kn
    outs[1][...] = kn.astype(BF16)


def _v_epilogue(acc, extra, outs):
    outs[0][...] = acc
    outs[1][...] = acc.astype(BF16)


def _gelu_epilogue(acc, extra, outs):
    outs[0][...] = (0.5 * acc * (1.0 + lax.erf(acc * (2.0 ** -0.5)))).astype(BF16)


def _residual_epilogue(acc, extra, outs):
    x_ref, gate_ref = extra
    outs[0][...] = x_ref[...] + gate_ref[...] * acc


def qkv_project(stream, h, w_qkv, q_norm, k_norm):
    m, d = h.shape
    bm = _row_block(stream, m, 1024)
    bn = _pick(d, 1024)
    nb = d // bn
    gain_spec = pl.BlockSpec((1, ATTN_HEAD_DIM), lambda j, i: (0, 0))
    common = dict(bm=bm, bn=bn, n_cols=d)
    (q_bf,) = matmul(h, w_qkv, col_block_off=0, extra=(q_norm.reshape(1, -1),), extra_specs=(gain_spec,),
                     out_dtypes=(BF16,), epilogue=_q_epilogue, name="q_proj", **common)
    k32, k_bf = matmul(h, w_qkv, col_block_off=nb, extra=(k_norm.reshape(1, -1),), extra_specs=(gain_spec,),
                       out_dtypes=(F32, BF16), epilogue=_k_epilogue, name="k_proj", **common)
    v32, v_bf = matmul(h, w_qkv, col_block_off=2 * nb, out_dtypes=(F32, BF16), epilogue=_v_epilogue,
                       name="v_proj", **common)
    return q_bf, k32, k_bf, v32, v_bf


def project_residual(stream, a, w, x, gate, name):
    k, n = w.shape
    deep = k > 4096
    bm = _row_block(stream, a.shape[0], 512 if deep else 1024)
    bn = _pick(n, 512 if deep else 1024)
    (out,) = matmul(a, w, bm=bm, bn=bn, extra=(x, gate),
                    extra_specs=(pl.BlockSpec((bm, bn), lambda j, i: (i, j)),
                                 _mod_spec(stream, bm, bn, lambda j, i: (i, j))),
                    out_dtypes=(F32,), epilogue=_residual_epilogue, name=name)
    return out


def _finish_heads(acc1, l1, acc2, l2, lam, g, post_scale):
    o = acc1 / l1 - lam * (acc2 / l2)
    o = o * lax.rsqrt(jnp.mean(o * o, axis=-1, keepdims=True) + EPS)
    return o * (g * post_scale)


def _attn_prompt_kernel(lam_ref, slope_ref, q_ref, k_ref, v_ref, g_ref, o_ref,
                        s_scr, p_scr, qa_scr, acc_scr, m_scr, l_scr, a_scr, *, bq, post_scale):
    hd = ATTN_HEAD_DIM
    rc = ATTN_ROW_CHUNK
    qi = pl.program_id(2)
    hh = pl.program_id(1)
    sl2 = slope_ref[hh]

    m_scr[...] = jnp.full(m_scr.shape, NEG_BIG, F32)
    l_scr[...] = jnp.zeros(l_scr.shape, F32)
    acc_scr[...] = jnp.zeros(acc_scr.shape, F32)

    lane = lax.broadcasted_iota(jnp.int32, (bq, LANES), 1)
    key_c = lax.broadcasted_iota(jnp.int32, (bq, LANES), 0)
    k_aug = jnp.where(lane < 3, key_c % 256, jnp.where(lane < 6, key_c // 256, 0)).astype(F32).astype(BF16)
    rest = jnp.full((bq, LANES), sl2, F32)
    piece = jnp.zeros((bq, LANES), F32)
    for i in range(3):
        part = rest.astype(BF16).astype(F32)
        piece = jnp.where(lane % 3 == i, part, piece)
        rest = rest - part
    q_aug = jnp.where(lane < 3, piece, jnp.where(lane < 6, 256.0 * piece, 0.0)).astype(BF16)
    for mi in range(2):
        qa_scr[mi, :, :hd] = q_ref[:, mi * hd:(mi + 1) * hd]
        qa_scr[mi, :, hd:] = q_aug

    def scores(kb, slot):
        k0 = pl.multiple_of(kb * bq, bq)
        for mi in range(2):
            k_blk = jnp.concatenate([k_ref[pl.ds(k0, bq), mi * hd:(mi + 1) * hd], k_aug], axis=1)
            s_scr[2 * slot + mi] = lax.dot_general(qa_scr[mi], k_blk, (((1,), (1,)), ((), ())),
                                                   preferred_element_type=F32)

    def block(kb, slot, diag):
        k0 = pl.multiple_of(kb * bq, bq)
        shift_blk = sl2 * ((kb - qi) * bq + jnp.zeros((1, LANES), jnp.int32)).astype(F32)
        for mi in range(2):
            s_map = s_scr.at[2 * slot + mi]
            for c in range(bq // rc):
                rows = slice(c * rc, (c + 1) * rc)
                ncol = min(bq, -(-((c + 1) * rc) // CHUNK) * CHUNK) if diag else bq
                lane_tiles = [slice(t * LANES, (t + 1) * LANES) for t in range(-(-ncol // LANES))]

                def tile(lanes, t):
                    s_t = s_map[rows, lanes]
                    if not diag:
                        return s_t
                    r_i = lax.broadcasted_iota(jnp.int32, (rc, LANES), 0) + c * rc
                    c_i = lax.broadcasted_iota(jnp.int32, (rc, LANES), 1) + t * LANES
                    s_t = s_t + (2.0 * sl2) * jnp.minimum(r_i - c_i, 0).astype(F32)
                    return jnp.where((c_i // CHUNK) <= (r_i // CHUNK), s_t, NEG_BIG)

                tiles = [tile(lanes, t) for t, lanes in enumerate(lane_tiles)]
                m_old = m_scr[mi, rows, :]
                blk_max = jnp.max(functools.reduce(jnp.maximum, tiles), axis=-1, keepdims=True)
                m_new = jnp.maximum(m_old, blk_max + shift_blk)
                alpha = jnp.exp2(m_old - m_new)
                m_scr[mi, rows, :] = m_new
                a_scr[mi, rows, :] = alpha
                if not diag:
                    tiles = [tile(lanes, t) for t, lanes in enumerate(lane_tiles)]
                off = m_new - shift_blk
                ps = [jnp.exp2(s_t - off) for s_t in tiles]
                l_scr[mi, rows, :] = alpha * l_scr[mi, rows, :] + functools.reduce(jnp.add, ps)
                for t, p_t in enumerate(ps):
                    p_scr[mi, rows, t * LANES:(t + 1) * LANES] = p_t.astype(BF16)
                if len(ps) * LANES < bq:
                    p_scr[mi, rows, len(ps) * LANES:] = jnp.zeros((rc, bq - len(ps) * LANES), BF16)
            pv = jnp.dot(p_scr[mi], v_ref[pl.ds(k0, bq), :], preferred_element_type=F32)
            for t in range(pv.shape[1] // LANES):
                lanes = slice(t * LANES, (t + 1) * LANES)
                acc_scr[mi, :, lanes] = a_scr[mi] * acc_scr[mi, :, lanes] + pv[:, lanes]

    def pair(j2, carry):
        kb = 2 * j2
        scores(kb + 1, 1)
        block(kb, 0, False)
        scores(kb + 2, 0)
        block(kb + 1, 1, False)
        return carry

    scores(0, 0)
    lax.fori_loop(0, qi // 2, pair, 0)

    @pl.when(qi % 2 == 1)
    def _():
        scores(qi, 1)
        block(qi - 1, 0, False)
        block(qi, 1, True)

    @pl.when(qi % 2 == 0)
    def _():
        block(qi, 0, True)
    l1 = jnp.sum(l_scr[0], axis=-1, keepdims=True)
    l2 = jnp.sum(l_scr[1], axis=-1, keepdims=True)
    o_ref[...] = _finish_heads(acc_scr[0], l1, acc_scr[1], l2, lam_ref[0], g_ref[...], post_scale).astype(o_ref.dtype)


def attention_prompt(q_bf, k_bf, v_bf, batch, lam, slopes2, subln_g, post_scale):
    m, width = q_bf.shape
    s = m // batch
    dv = 2 * ATTN_HEAD_DIM
    heads = width // dv
    bq = _pick(s, 512)
    assert bq % CHUNK == 0
    nq = s // bq
    kv_spec = pl.BlockSpec((s, dv), lambda b, h, i: (b, h))
    q_spec = pl.BlockSpec((bq, dv), lambda b, h, i: (b * nq + i, h))
    smem = pl.BlockSpec(memory_space=pltpu.SMEM)
    return pl.pallas_call(
        functools.partial(_attn_prompt_kernel, bq=bq, post_scale=post_scale),
        grid=(batch, heads, nq),
        in_specs=[smem, smem, q_spec, kv_spec, kv_spec, pl.BlockSpec((1, dv), lambda b, h, i: (0, 0))],
        out_specs=q_spec,
        out_shape=jax.ShapeDtypeStruct((m, width), BF16),
        scratch_shapes=[pltpu.VMEM((4, bq, bq), F32), pltpu.VMEM((2, bq, bq), BF16), pltpu.VMEM((2, bq, dv), BF16),
                        pltpu.VMEM((2, bq, dv), F32)] + [pltpu.VMEM((2, bq, LANES), F32)] * 3,
        compiler_params=_params(3), name="attention_prompt",
    )(lam, slopes2, q_bf, k_bf, v_bf, subln_g.reshape(1, dv))


def _attn_sample_kernel(lam_ref, slope_ref, q_ref, kn_ref, vn_ref, ck_ref, cv_ref, g_ref, o_ref, *, post_scale):
    hd = ATTN_HEAD_DIM
    dv = 2 * hd
    lam = lam_ref[0]
    t = q_ref.shape[0]
    past, heads = ck_ref.shape[0], ck_ref.shape[1]

    def dist_mask(n_keys, key0):
        q_pos = past + lax.broadcasted_iota(jnp.int32, (t, n_keys), 0)
        k_pos = key0 + lax.broadcasted_iota(jnp.int32, (t, n_keys), 1)
        return jnp.abs(q_pos - k_pos).astype(F32), (k_pos // CHUNK) <= (q_pos // CHUNK)

    dist_p, vis_p = dist_mask(past, 0)
    dist_n, vis_n = dist_mask(t, past)
    dn = (((1,), (1,)), ((), ()))
    for h in range(heads):
        cols = slice(h * dv, (h + 1) * dv)
        sl2 = slope_ref[h]
        v_past = cv_ref[:, h, :].astype(BF16)
        res = []
        for mi in range(2):
            qm = q_ref[:, h * dv + mi * hd:h * dv + (mi + 1) * hd]
            s_p = lax.dot_general(qm, ck_ref[:, h, mi, :].astype(BF16), dn, preferred_element_type=F32)
            s_n = lax.dot_general(qm, kn_ref[:, h * dv + mi * hd:h * dv + (mi + 1) * hd], dn,
                                  preferred_element_type=F32)
            s_p = jnp.where(vis_p, s_p - sl2 * dist_p, NEG_BIG)
            s_n = jnp.where(vis_n, s_n - sl2 * dist_n, NEG_BIG)
            m = jnp.maximum(jnp.max(s_p, axis=-1, keepdims=True), jnp.max(s_n, axis=-1, keepdims=True))
            p_p = jnp.exp2(s_p - m)
            p_n = jnp.exp2(s_n - m)
            l = jnp.sum(p_p, axis=-1, keepdims=True) + jnp.sum(p_n, axis=-1, keepdims=True)
            acc = (jnp.dot(p_p.astype(BF16), v_past, preferred_element_type=F32)
                   + jnp.dot(p_n.astype(BF16), vn_ref[:, cols], preferred_element_type=F32))
            res += [acc, l]
        o_ref[:, cols] = _finish_heads(res[0], res[1], res[2], res[3], lam, g_ref[...],
                                       post_scale).astype(o_ref.dtype)


def attention_sample(q_bf, k_bf, v_bf, cache_k, cache_v, layer, lam, slopes2, subln_g, post_scale):
    _, batch, past, heads, _, hd = cache_k.shape
    m, width = q_bf.shape
    t = m // batch
    dv = 2 * hd
    new_spec = pl.BlockSpec((t, width), lambda b: (b, 0))
    ck_spec = pl.BlockSpec((None, None, past, heads, 2, hd), lambda b: (layer, b, 0, 0, 0, 0))
    cv_spec = pl.BlockSpec((None, None, past, heads, dv), lambda b: (layer, b, 0, 0, 0))
    smem = pl.BlockSpec(memory_space=pltpu.SMEM)
    return pl.pallas_call(
        functools.partial(_attn_sample_kernel, post_scale=post_scale),
        grid=(batch,),
        in_specs=[smem, smem, new_spec, new_spec, new_spec, ck_spec, cv_spec,
                  pl.BlockSpec((1, dv), lambda b: (0, 0))],
        out_specs=new_spec,
        out_shape=jax.ShapeDtypeStruct((m, width), BF16),
        compiler_params=_params(1), name="attention_sample",
    )(lam, slopes2, q_bf, k_bf, v_bf, cache_k, cache_v, subln_g.reshape(1, dv))


def _sgu_gate_kernel(u_ref, v_ref, g_ref, b_ref, w_ref, bs_ref, *out_refs, chunk_len, write_v):
    bt, width = u_ref.shape
    gd = width // SGU_GROUPS
    v = v_ref[...].astype(F32)
    mu = jnp.mean(v, axis=-1, keepdims=True)
    vc = v - mu
    vn = vc * lax.rsqrt(jnp.mean(vc * vc, axis=-1, keepdims=True) + EPS) * g_ref[...] + b_ref[...]
    if write_v:
        out_refs[1][...] = vn
    vn_bf = vn.astype(BF16)
    row = lax.broadcasted_iota(jnp.int32, (SGU_CHUNK, SGU_CHUNK), 0)
    col = lax.broadcasted_iota(jnp.int32, (SGU_CHUNK, SGU_CHUNK), 1)
    keep = ((row // chunk_len) == (col // chunk_len)) & ((col % chunk_len) <= (row % chunk_len))
    for gi in range(SGU_GROUPS):
        w = jnp.where(keep, w_ref[gi], 0.0).astype(BF16)
        bias = bs_ref[gi]
        for c in range(bt // SGU_CHUNK):
            rows = slice(c * SGU_CHUNK, (c + 1) * SGU_CHUNK)
            cols = slice(gi * gd, (gi + 1) * gd)
            mixed = jnp.dot(w, vn_bf[rows, cols], preferred_element_type=F32) + bias
            out_refs[0][rows, cols] = (u_ref[rows, cols].astype(F32) * mixed).astype(BF16)


def sgu_gate(z, ln_g, ln_b, w_s, b_s, chunk_len, write_v):
    m, two_w = z.shape
    width = two_w // 2
    reps = SGU_CHUNK // chunk_len
    w_t = jnp.tile(w_s[:, :chunk_len, :chunk_len], (1, reps, reps))
    b_t = jnp.tile(b_s[:, :chunk_len], (1, reps))[:, :, None]
    bt = _pick(m, 256)
    row_spec = lambda c: pl.BlockSpec((bt, width), lambda i, c=c: (i, c))
    vec_spec = pl.BlockSpec((1, width), lambda i: (0, 0))
    out_specs = [row_spec(0)]
    out_shape = [jax.ShapeDtypeStruct((m, width), BF16)]
    if write_v:
        out_specs.append(row_spec(0))
        out_shape.append(jax.ShapeDtypeStruct((m, width), F32))
    return pl.pallas_call(
        functools.partial(_sgu_gate_kernel, chunk_len=chunk_len, write_v=write_v),
        grid=(m // bt,),
        in_specs=[row_spec(0), row_spec(1), vec_spec, vec_spec,
                  pl.BlockSpec((SGU_GROUPS, SGU_CHUNK, SGU_CHUNK), lambda i: (0, 0, 0)),
                  pl.BlockSpec((SGU_GROUPS, SGU_CHUNK, 1), lambda i: (0, 0, 0))],
        out_specs=out_specs, out_shape=out_shape,
        compiler_params=_params(1), name="sgu_gate",
    )(z, z, ln_g.reshape(1, width), ln_b.reshape(1, width), w_t, b_t)


def _swiglu(x, wg, wu, wd):
    g = jnp.dot(x, wg, preferred_element_type=F32)
    u = jnp.dot(x, wu, preferred_element_type=F32)
    h = (g * jax.nn.sigmoid(g) * u).astype(BF16)
    return jnp.dot(h, wd, preferred_element_type=F32)


def _moe_kernel(blk_e_ref, n_used_ref, x_ref, wg_ref, wu_ref, wd_ref, o_ref, wg_bf, wu_bf, wd_bf):
    b = pl.program_id(0)
    e = blk_e_ref[b]
    e_prev = blk_e_ref[jnp.maximum(b - 1, 0)]

    @pl.when((b == 0) | (e != e_prev))
    def _():
        wg_bf[...] = wg_ref[...].astype(BF16)
        wu_bf[...] = wu_ref[...].astype(BF16)
        wd_bf[...] = wd_ref[...].astype(BF16)

    @pl.when(b < n_used_ref[0])
    def _():
        o_ref[...] = _swiglu(x_ref[...], wg_bf[...], wu_bf[...], wd_bf[...]).astype(o_ref.dtype)

    @pl.when(b >= n_used_ref[0])
    def _():
        o_ref[...] = jnp.zeros_like(o_ref)


def moe_experts(xg, blk_expert, n_used, layer, w_gate, w_up, w_down):
    rows, d = xg.shape
    ed = w_gate.shape[3]
    n_blocks = rows // MOE_BLOCK
    x_spec = pl.BlockSpec((MOE_BLOCK, d), lambda b, be, nu: (b, 0))
    return pl.pallas_call(
        _moe_kernel,
        grid_spec=pltpu.PrefetchScalarGridSpec(
            num_scalar_prefetch=2, grid=(n_blocks,),
            in_specs=[x_spec,
                      pl.BlockSpec((None, None, d, ed), lambda b, be, nu: (layer, be[b], 0, 0)),
                      pl.BlockSpec((None, None, d, ed), lambda b, be, nu: (layer, be[b], 0, 0)),
                      pl.BlockSpec((None, None, ed, d), lambda b, be, nu: (layer, be[b], 0, 0))],
            out_specs=x_spec,
            scratch_shapes=[pltpu.VMEM((d, ed), BF16), pltpu.VMEM((d, ed), BF16), pltpu.VMEM((ed, d), BF16)]),
        out_shape=jax.ShapeDtypeStruct((rows, d), BF16),
        compiler_params=_params(1), name="moe_experts",
    )(blk_expert, n_used, xg, w_gate, w_up, w_down)


def _shared_kernel(h_ref, wg_ref, wu_ref, wd_ref, o_ref):
    o_ref[...] = _swiglu(h_ref[...], wg_ref[...], wu_ref[...], wd_ref[...]).astype(o_ref.dtype)


def shared_expert(h, sw_gate, sw_up, sw_down):
    m, d = h.shape
    ed = sw_gate.shape[1]
    bm = _pick(m, 512)
    row_spec = pl.BlockSpec((bm, d), lambda i: (i, 0))
    full = lambda a, b: pl.BlockSpec((a, b), lambda i: (0, 0))
    return pl.pallas_call(
        _shared_kernel, grid=(m // bm,),
        in_specs=[row_spec, full(d, ed), full(d, ed), full(ed, d)],
        out_specs=row_spec,
        out_shape=jax.ShapeDtypeStruct((m, d), BF16),
        compiler_params=_params(1), name="shared_expert",
    )(h, sw_gate, sw_up, sw_down)


def _combine_kernel(s_ref, p_ref, w_ref, x_ref, gate_ref, o_ref):
    ffn = s_ref[...].astype(F32)
    w = w_ref[...]
    for k in range(TOP_K):
        ffn = ffn + w[:, k:k + 1] * p_ref[k].astype(F32)
    o_ref[...] = x_ref[...] + gate_ref[...] * ffn


def moe_combine_residual(stream, shared, picked, wts, row0, x, gate):
    m, d = x.shape
    bm = _row_block(stream, m, 256)
    assert row0 % bm == 0
    off = row0 // bm
    row_spec = pl.BlockSpec((bm, d), lambda i: (i, 0))
    return pl.pallas_call(
        _combine_kernel, grid=(m // bm,),
        in_specs=[pl.BlockSpec((bm, d), lambda i: (i + off, 0)),
                  pl.BlockSpec((TOP_K, bm, d), lambda i: (0, i + off, 0)),
                  pl.BlockSpec((bm, TOP_K), lambda i: (i + off, 0)), row_spec,
                  _mod_spec(stream, bm, d, lambda i: (i, 0))],
        out_specs=row_spec,
        out_shape=jax.ShapeDtypeStruct((m, d), F32),
        compiler_params=_params(1), name="moe_combine",
    )(shared, picked, wts, x, gate)


def _dest_kernel(start_ref, idx_ref, rank_ref, dest_ref):
    idx = idx_ref[...]

    def body(e, acc):
        return jnp.where(idx == e, start_ref[e], acc)

    dest_ref[...] = rank_ref[...] + lax.fori_loop(0, start_ref.shape[0], body, jnp.zeros_like(idx))


def assignment_rows(pad_start, idx, rank):
    return pl.pallas_call(
        _dest_kernel,
        in_specs=[pl.BlockSpec(memory_space=pltpu.SMEM), pl.BlockSpec(memory_space=pltpu.VMEM),
                  pl.BlockSpec(memory_space=pltpu.VMEM)],
        out_specs=pl.BlockSpec(memory_space=pltpu.VMEM),
        out_shape=jax.ShapeDtypeStruct(idx.shape, jnp.int32),
        name="assignment_rows",
    )(pad_start, idx, rank)


def moe_routed(h_all, idx, rank, counts, layer, w_gate, w_up, w_down):
    t, d = h_all.shape
    n_exp = counts.shape[0]
    n_assign = t * TOP_K
    counts = counts.reshape(n_exp).astype(jnp.int32)
    padded = (counts + MOE_BLOCK - 1) // MOE_BLOCK * MOE_BLOCK
    pad_end = jnp.cumsum(padded)
    pad_start = pad_end - padded
    dest = assignment_rows(pad_start, idx, rank).reshape(-1)
    n_blocks = -(-(n_assign + n_exp * (MOE_BLOCK - 1)) // MOE_BLOCK)
    n_rows = n_blocks * MOE_BLOCK
    tok = jnp.tile(jnp.arange(t, dtype=jnp.int32), TOP_K)
    src_tok = jnp.arange(n_rows, dtype=jnp.int32) % t
    src_tok = src_tok.at[dest].add(tok - dest % t, unique_indices=True, mode="promise_in_bounds")
    blk_start = jnp.arange(n_blocks, dtype=jnp.int32) * MOE_BLOCK
    blk_expert = jnp.minimum(jnp.sum((pad_end[None, :] <= blk_start[:, None]).astype(jnp.int32), axis=1), n_exp - 1)
    n_used = pad_end[-1:] // MOE_BLOCK
    xg = h_all.at[src_tok].get(mode="promise_in_bounds")
    out = moe_experts(xg, blk_expert, n_used, layer, w_gate, w_up, w_down)
    return out.at[dest].get(mode="promise_in_bounds", unique_indices=True).reshape(TOP_K, t, d)


def _stream(rows_per_batch):
    return dict(per_row=rows_per_batch < 1024, rows_per_batch=rows_per_batch)


def _mod_rows(stream, mod):
    b, d = mod.shape
    if not stream["per_row"]:
        return mod.reshape(b, 1, d)
    return jnp.repeat(mod, stream["rows_per_batch"], axis=0).reshape(1, -1, d)


def kernel(x_prompt, x_sample, cache_k_attn, cache_v_attn, c_prompt, c_sample, norm_mix_g, norm_ffn_g, ada_w, ada_b, attn_w_qkv, attn_w_o, attn_q_norm, attn_k_norm, attn_lambda_q1, attn_lambda_k1, attn_lambda_q2, attn_lambda_k2, attn_subln_g, sgu_w_in, sgu_ln_g, sgu_ln_b, sgu_w_s, sgu_b_s, sgu_w_o, moe_w_router, moe_b_router, moe_w_gate, moe_w_up, moe_w_down, shared_w_gate, shared_w_up, shared_w_down):
    bp, sp, d = x_prompt.shape
    bs, ss, _ = x_sample.shape
    depth = ada_w.shape[0]
    dv = 2 * ATTN_HEAD_DIM
    heads = d // dv
    mp, ms = bp * sp, bs * ss
    streams = (_stream(sp), _stream(ss))
    xs = [x_prompt.reshape(mp, d), x_sample.reshape(ms, d)]

    mod = ada_modulation(jnp.concatenate([c_prompt, c_sample], axis=0), ada_w, ada_b)
    slopes2 = (2.0 ** (-8.0 * jnp.arange(1, heads + 1, dtype=F32) / heads)) * LOG2E

    k_out, v_out, sgu_out = [[], []], [[], []], []
    for i in range(depth):
        j = i // N_MIXERS
        mods = []
        for si, (stream, rows) in enumerate(zip(streams, (slice(0, bp), slice(bp, bp + bs)))):
            mods.append([_mod_rows(stream, mod[i, rows, c * d:(c + 1) * d]) for c in range(6)])

        if i % N_MIXERS == 0:
            lam_init = 0.8 - 0.6 * math.exp(-0.3 * i)
            f = lambda a: a[j].astype(F32)
            lam = (jnp.exp(jnp.sum(f(attn_lambda_q1) * f(attn_lambda_k1)))
                   - jnp.exp(jnp.sum(f(attn_lambda_q2) * f(attn_lambda_k2))) + lam_init).reshape(1)
            for si, stream in enumerate(streams):
                h = modulate(stream, xs[si], norm_mix_g[i], mods[si][0], mods[si][1])
                q_bf, k32, k_bf, v32, v_bf = qkv_project(stream, h, attn_w_qkv[j], attn_q_norm[j], attn_k_norm[j])
                if si == 0:
                    o = attention_prompt(q_bf, k_bf, v_bf, bp, lam, slopes2, attn_subln_g[j], 1.0 - lam_init)
                else:
                    o = attention_sample(q_bf, k_bf, v_bf, cache_k_attn, cache_v_attn, j, lam, slopes2,
                                         attn_subln_g[j], 1.0 - lam_init)
                xs[si] = project_residual(stream, o, attn_w_o[j], xs[si], mods[si][2], "attn_out_proj")
                k_out[si].append(k32)
                v_out[si].append(v32)
        else:
            for si, stream in enumerate(streams):
                h = modulate(stream, xs[si], norm_mix_g[i], mods[si][0], mods[si][1])
                (z,) = matmul(h, sgu_w_in[j], bm=_row_block(stream, h.shape[0], 1024),
                              bn=_pick(sgu_w_in.shape[2], 1024),
                              out_dtypes=(BF16,), epilogue=_gelu_epilogue, name="sgu_in_proj")
                chunk_len = SGU_CHUNK if si == 0 else ss
                res = sgu_gate(z, sgu_ln_g[j], sgu_ln_b[j], sgu_w_s[j], sgu_b_s[j], chunk_len, write_v=(si == 1))
                if si == 1:
                    sgu_out.append(res[1])
                xs[si] = project_residual(stream, res[0], sgu_w_o[j], xs[si], mods[si][2], "sgu_out_proj")

        hs, routing = [], []
        counts = jnp.zeros((moe_w_router.shape[2], 1), F32)
        for si, stream in enumerate(streams):
            hf, idx, wts, rank, counts = modulate(stream, xs[si], norm_ffn_g[i], mods[si][3], mods[si][4],
                                                  router=(moe_w_router[i], moe_b_router[i], counts))
            hs.append(hf)
            routing.append((idx, wts, rank))
        idx, wts, rank = (jnp.concatenate(parts, axis=1) for parts in zip(*routing))
        h_all = jnp.concatenate(hs, axis=0)
        shared = shared_expert(h_all, *[w[i].astype(BF16) for w in (shared_w_gate, shared_w_up, shared_w_down)])
        picked = moe_routed(h_all, idx, rank, counts, i, moe_w_gate, moe_w_up, moe_w_down)
        for si, (stream, row0) in enumerate(zip(streams, (0, mp))):
            xs[si] = moe_combine_residual(stream, shared, picked, wts.T, row0, xs[si], mods[si][5])

    n_attn = len(k_out[0])
    k_prompt = jnp.stack(k_out[0]).reshape(n_attn, bp, sp, heads, 2, ATTN_HEAD_DIM)
    v_prompt = jnp.stack(v_out[0]).reshape(n_attn, bp, sp, heads, dv)
    k_sample = jnp.stack(k_out[1]).reshape(n_attn, bs, ss, heads, 2, ATTN_HEAD_DIM)
    v_sample = jnp.stack(v_out[1]).reshape(n_attn, bs, ss, heads, dv)
    sgu_v = jnp.stack(sgu_out).reshape(len(sgu_out), bs, ss, -1)
    return (xs[0].reshape(bp, sp, d), xs[1].reshape(bs, ss, d), k_prompt, v_prompt, k_sample, v_sample, sgu_v)
```

```python
import functools
import math

import jax
import jax.numpy as jnp
from jax import lax
from jax.experimental import pallas as pl
from jax.experimental.pallas import tpu as pltpu

F32 = jnp.float32
BF16 = jnp.bfloat16

EPS = 1e-6
CHUNK = 64
N_MIXERS = 2
ATTN_HEAD_DIM = 128
SGU_GROUPS = 8
SGU_CHUNK = 128
TOP_K = 8
N_EXPERT_GROUPS = 8
TOPK_GROUPS = 4
ROUTED_SCALE = 2.5
LOG2E = 1.4426950408889634
NEG_BIG = -1e30

VMEM_LIMIT = 56 * 1024 * 1024
MOE_BLOCK = 512
ATTN_ROW_CHUNK = 64
LANES = 128


def _params(n_axes):
    return pltpu.CompilerParams(dimension_semantics=("arbitrary",) * n_axes, vmem_limit_bytes=VMEM_LIMIT)


def _pick(n, pref):
    if n <= pref:
        return n
    b = pref
    while n % b:
        b //= 2
    return b


def _ada_kernel(c_ref, w_ref, b_ref, o_ref):
    c = c_ref[...]
    x = (c * jax.nn.sigmoid(c)).astype(BF16)
    o_ref[...] = jnp.dot(x, w_ref[...].astype(BF16), preferred_element_type=F32) + b_ref[...]


def ada_modulation(c, ada_w, ada_b):
    n_layers, d, n = ada_w.shape
    r = c.shape[0]
    bn = _pick(n, 1024)
    return pl.pallas_call(
        _ada_kernel,
        grid=(n_layers, n // bn),
        in_specs=[
            pl.BlockSpec((r, d), lambda l, j: (0, 0)),
            pl.BlockSpec((None, d, bn), lambda l, j: (l, 0, j)),
            pl.BlockSpec((None, 1, bn), lambda l, j: (l, 0, j)),
        ],
        out_specs=pl.BlockSpec((None, r, bn), lambda l, j: (l, 0, j)),
        out_shape=jax.ShapeDtypeStruct((n_layers, r, n), F32),
        compiler_params=_params(2),
        name="ada_modulation",
    )(c, ada_w, ada_b.reshape(n_layers, 1, n))


def _modulated(x, g, shift, scale):
    y = x * lax.rsqrt(jnp.mean(x * x, axis=-1, keepdims=True) + EPS)
    return y * g * (1.0 + scale) + shift


def _modulate_kernel(x_ref, g_ref, sh_ref, sc_ref, o_ref):
    o_ref[...] = _modulated(x_ref[...], g_ref[...], sh_ref[...], sc_ref[...]).astype(o_ref.dtype)


def _split3(x):
    hi = x.astype(BF16)
    lo = (x - hi.astype(F32)).astype(BF16)
    return hi, lo


def _first_max(x, ids, n_ids, axes):
    mx = x
    for ax in axes:
        mx = jnp.max(mx, axis=ax, keepdims=True)
    arg = jnp.where(x == mx, ids, n_ids)
    for ax in axes:
        arg = jnp.min(arg, axis=ax, keepdims=True)
    return mx, arg


def _sum_axes(x, axes):
    for ax in axes:
        x = jnp.sum(x, axis=ax, keepdims=True)
    return x


def _modulate_router_kernel(x_ref, g_ref, sh_ref, sc_ref, wrt_ref, br_ref, cin_ref,
                            o_ref, idx_ref, wts_ref, rank_ref, cnt_ref, carry):
    h = _modulated(x_ref[...], g_ref[...], sh_ref[...], sc_ref[...])
    o_ref[...] = h.astype(o_ref.dtype)
    bm = h.shape[0]
    n_exp = wrt_ref.shape[0]
    per = n_exp // N_EXPERT_GROUPS
    grp_shape = (N_EXPERT_GROUPS, per, bm)

    h_hi, h_lo = _split3(h)
    w_hi, w_lo = _split3(wrt_ref[...])
    nt = (((1,), (1,)), ((), ()))
    logits = (lax.dot_general(w_hi, h_hi, nt, preferred_element_type=F32)
              + lax.dot_general(w_lo, h_hi, nt, preferred_element_type=F32)
              + lax.dot_general(w_hi, h_lo, nt, preferred_element_type=F32))
    s = jax.nn.sigmoid(logits)
    s3 = s.reshape(grp_shape)
    sb3 = (s + br_ref[...]).reshape(grp_shape)

    sub = lax.broadcasted_iota(jnp.int32, grp_shape, 1)
    gid = lax.broadcasted_iota(jnp.int32, (N_EXPERT_GROUPS, 1, bm), 0)
    eid = lax.broadcasted_iota(jnp.int32, grp_shape, 0) * per + sub

    m1, i1 = _first_max(sb3, sub, per, (1,))
    m2 = jnp.max(jnp.where(sub == i1, -jnp.inf, sb3), axis=1, keepdims=True)
    work = m1 + m2
    chosen = jnp.zeros_like(work)
    for _ in range(TOPK_GROUPS):
        _, gi = _first_max(work, gid, N_EXPERT_GROUPS, (0,))
        chosen = jnp.where(gid == gi, 1.0, chosen)
        work = jnp.where(gid == gi, -jnp.inf, work)
    sel = jnp.where(chosen > 0.0, sb3, -jnp.inf)

    hits, ids, raw = [], [], []
    for _ in range(TOP_K):
        _, ei = _first_max(sel, eid, n_exp, (1, 0))
        hit = eid == ei
        raw.append(_sum_axes(jnp.where(hit, s3, 0.0), (1, 0)))
        sel = jnp.where(hit, -jnp.inf, sel)
        hits.append(hit)
        ids.append(ei)
    total = sum(raw)

    member = sum(jnp.where(hit, 1.0, 0.0) for hit in hits).reshape(n_exp, bm)
    r_i = lax.broadcasted_iota(jnp.int32, (bm, bm), 0)
    c_i = lax.broadcasted_iota(jnp.int32, (bm, bm), 1)
    before = jnp.where(r_i < c_i, 1.0, 0.0).astype(BF16)

    @pl.when(pl.program_id(0) == 0)
    def _():
        carry[...] = cin_ref[...]

    rank_all = (jnp.dot(member.astype(BF16), before, preferred_element_type=F32) + carry[...]).reshape(grp_shape)
    carry[...] += jnp.sum(member, axis=1, keepdims=True)
    cnt_ref[...] = carry[...]
    for k in range(TOP_K):
        idx_ref[k:k + 1, :] = ids[k].reshape(1, bm)
        wts_ref[k:k + 1, :] = (raw[k] / total * ROUTED_SCALE).reshape(1, bm)
        rank_ref[k:k + 1, :] = _sum_axes(jnp.where(hits[k], rank_all, 0.0), (1, 0)).reshape(1, bm).astype(jnp.int32)


def _row_block(stream, m, pref):
    return _pick(m if stream["per_row"] else stream["rows_per_batch"], pref)


def _mod_spec(stream, bm, bn, ij):
    if stream["per_row"]:
        return pl.BlockSpec((None, bm, bn), lambda *g: (0,) + tuple(ij(*g)))
    bpg = stream["rows_per_batch"] // bm
    return pl.BlockSpec((None, 1, bn), lambda *g: (ij(*g)[0] // bpg, 0, ij(*g)[1]))


def modulate(stream, x, g, shift, scale, router=None):
    m, d = x.shape
    bm = _row_block(stream, m, 512)
    row_spec = pl.BlockSpec((bm, d), lambda i: (i, 0))
    mod_spec = _mod_spec(stream, bm, d, lambda i: (i, 0))
    g_spec = pl.BlockSpec((1, d), lambda i: (0, 0))
    if router is None:
        return pl.pallas_call(
            _modulate_kernel, grid=(m // bm,),
            in_specs=[row_spec, g_spec, mod_spec, mod_spec], out_specs=row_spec,
            out_shape=jax.ShapeDtypeStruct((m, d), BF16),
            compiler_params=_params(1), name="modulate",
        )(x, g.reshape(1, d), shift, scale)
    w_router, b_router, counts_in = router
    e = w_router.shape[1]
    tok_spec = pl.BlockSpec((TOP_K, bm), lambda i: (0, i))
    cnt_spec = pl.BlockSpec((e, 1), lambda i: (0, 0))
    return pl.pallas_call(
        _modulate_router_kernel, grid=(m // bm,),
        in_specs=[row_spec, g_spec, mod_spec, mod_spec, pl.BlockSpec((e, d), lambda i: (0, 0)), cnt_spec, cnt_spec],
        out_specs=[row_spec, tok_spec, tok_spec, tok_spec, cnt_spec],
        out_shape=[jax.ShapeDtypeStruct((m, d), BF16), jax.ShapeDtypeStruct((TOP_K, m), jnp.int32),
                   jax.ShapeDtypeStruct((TOP_K, m), F32), jax.ShapeDtypeStruct((TOP_K, m), jnp.int32),
                   jax.ShapeDtypeStruct((e, 1), F32)],
        scratch_shapes=[pltpu.VMEM((e, 1), F32)],
        compiler_params=_params(1), name="modulate_router",
    )(x, g.reshape(1, d), shift, scale, w_router.T, b_router.astype(F32).reshape(e, 1), counts_in)


def _mm_kernel(x_ref, w_ref, *refs, n_extra, epilogue):
    extra, outs, w_bf = refs[:n_extra], refs[n_extra:-1], refs[-1]

    @pl.when(pl.program_id(1) == 0)
    def _():
        w_bf[...] = w_ref[...].astype(BF16)

    acc = jnp.dot(x_ref[...], w_bf[...], preferred_element_type=F32)
    epilogue(acc, extra, outs)


def matmul(x, w, *, bm, bn, col_block_off=0, n_cols=None, extra=(), extra_specs=(), out_dtypes, epilogue, name):
    m, k = x.shape
    n = w.shape[1] if n_cols is None else n_cols
    grid = (n // bn, m // bm)
    out_spec = pl.BlockSpec((bm, bn), lambda j, i: (i, j))
    return pl.pallas_call(
        functools.partial(_mm_kernel, n_extra=len(extra), epilogue=epilogue),
        grid=grid,
        in_specs=[pl.BlockSpec((bm, k), lambda j, i: (i, 0)),
                  pl.BlockSpec((k, bn), lambda j, i: (0, j + col_block_off))] + list(extra_specs),
        out_specs=[out_spec] * len(out_dtypes),
        out_shape=[jax.ShapeDtypeStruct((m, n), dt) for dt in out_dtypes],
        scratch_shapes=[pltpu.VMEM((k, bn), BF16)],
        compiler_params=_params(2), name=name,
    )(x, w, *extra)


def _head_rms(acc, gain, post_scale):
    pieces = []
    for c in range(acc.shape[1] // ATTN_HEAD_DIM):
        seg = acc[:, c * ATTN_HEAD_DIM:(c + 1) * ATTN_HEAD_DIM]
        y = seg * lax.rsqrt(jnp.mean(seg * seg, axis=-1, keepdims=True) + EPS)
        pieces.append(y * (gain * post_scale))
    return jnp.concatenate(pieces, axis=-1)


def _q_epilogue(acc, extra, outs):
    outs[0][...] = _head_rms(acc, extra[0][...], ATTN_HEAD_DIM ** -0.5 * LOG2E).astype(BF16)


def _k_epilogue(acc, extra, outs):
    kn = _head_rms(acc, extra[0][...], 1.0)
    outs[0][...] = kn
    outs[1][...] = kn.astype(BF16)


def _v_epilogue(acc, extra, outs):
    outs[0][...] = acc
    outs[1][...] = acc.astype(BF16)


def _gelu_epilogue(acc, extra, outs):
    outs[0][...] = (0.5 * acc * (1.0 + lax.erf(acc * (2.0 ** -0.5)))).astype(BF16)


def _residual_epilogue(acc, extra, outs):
    x_ref, gate_ref = extra
    outs[0][...] = x_ref[...] + gate_ref[...] * acc


def qkv_project(stream, h, w_qkv, q_norm, k_norm):
    m, d = h.shape
    bm = _row_block(stream, m, 1024)
    bn = _pick(d, 1024)
    nb = d // bn
    gain_spec = pl.BlockSpec((1, ATTN_HEAD_DIM), lambda j, i: (0, 0))
    common = dict(bm=bm, bn=bn, n_cols=d)
    (q_bf,) = matmul(h, w_qkv, col_block_off=0, extra=(q_norm.reshape(1, -1),), extra_specs=(gain_spec,),
                     out_dtypes=(BF16,), epilogue=_q_epilogue, name="q_proj", **common)
    k32, k_bf = matmul(h, w_qkv, col_block_off=nb, extra=(k_norm.reshape(1, -1),), extra_specs=(gain_spec,),
                       out_dtypes=(F32, BF16), epilogue=_k_epilogue, name="k_proj", **common)
    v32, v_bf = matmul(h, w_qkv, col_block_off=2 * nb, out_dtypes=(F32, BF16), epilogue=_v_epilogue,
                       name="v_proj", **common)
    return q_bf, k32, k_bf, v32, v_bf


def project_residual(stream, a, w, x, gate, name):
    k, n = w.shape
    deep = k > 4096
    bm = _row_block(stream, a.shape[0], 512 if deep else 1024)
    bn = _pick(n, 512 if deep else 1024)
    (out,) = matmul(a, w, bm=bm, bn=bn, extra=(x, gate),
                    extra_specs=(pl.BlockSpec((bm, bn), lambda j, i: (i, j)),
                                 _mod_spec(stream, bm, bn, lambda j, i: (i, j))),
                    out_dtypes=(F32,), epilogue=_residual_epilogue, name=name)
    return out


def _finish_heads(acc1, l1, acc2, l2, lam, g, post_scale):
    o = acc1 / l1 - lam * (acc2 / l2)
    o = o * lax.rsqrt(jnp.mean(o * o, axis=-1, keepdims=True) + EPS)
    return o * (g * post_scale)


def _attn_prompt_kernel(lam_ref, slope_ref, q_ref, k_ref, v_ref, g_ref, o_ref,
                        s_scr, p_scr, qa_scr, acc_scr, m_scr, l_scr, a_scr, *, bq, post_scale):
    hd = ATTN_HEAD_DIM
    rc = ATTN_ROW_CHUNK
    qi = pl.program_id(2)
    hh = pl.program_id(1)
    sl2 = slope_ref[hh]

    m_scr[...] = jnp.full(m_scr.shape, NEG_BIG, F32)
    l_scr[...] = jnp.zeros(l_scr.shape, F32)
    acc_scr[...] = jnp.zeros(acc_scr.shape, F32)

    lane = lax.broadcasted_iota(jnp.int32, (bq, LANES), 1)
    key_c = lax.broadcasted_iota(jnp.int32, (bq, LANES), 0)
    k_aug = jnp.where(lane < 3, key_c % 256, jnp.where(lane < 6, key_c // 256, 0)).astype(F32).astype(BF16)
    rest = jnp.full((bq, LANES), sl2, F32)
    piece = jnp.zeros((bq, LANES), F32)
    for i in range(3):
        part = rest.astype(BF16).astype(F32)
        piece = jnp.where(lane % 3 == i, part, piece)
        rest = rest - part
    q_aug = jnp.where(lane < 3, piece, jnp.where(lane < 6, 256.0 * piece, 0.0)).astype(BF16)
    for mi in range(2):
        qa_scr[mi, :, :hd] = q_ref[:, mi * hd:(mi + 1) * hd]
        qa_scr[mi, :, hd:] = q_aug

    def scores(kb, slot):
        k0 = pl.multiple_of(kb * bq, bq)
        for mi in range(2):
            k_blk = jnp.concatenate([k_ref[pl.ds(k0, bq), mi * hd:(mi + 1) * hd], k_aug], axis=1)
            s_scr[2 * slot + mi] = lax.dot_general(qa_scr[mi], k_blk, (((1,), (1,)), ((), ())),
                                                   preferred_element_type=F32)

    def softmax(kb, slot, diag):
        shift_blk = sl2 * ((kb - qi) * bq + jnp.zeros((1, LANES), jnp.int32)).astype(F32)
        for mi in range(2):
            s_map = s_scr.at[2 * slot + mi]
            p_map = p_scr.at[2 * slot + mi]
            for c in range(bq // rc):
                rows = slice(c * rc, (c + 1) * rc)
                ncol = min(bq, -(-((c + 1) * rc) // CHUNK) * CHUNK) if diag else bq
                lane_tiles = [slice(t * LANES, (t + 1) * LANES) for t in range(-(-ncol // LANES))]

                def tile(lanes, t):
                    s_t = s_map[rows, lanes]
                    if not diag:
                        return s_t
                    r_i = lax.broadcasted_iota(jnp.int32, (rc, LANES), 0) + c * rc
                    c_i = lax.broadcasted_iota(jnp.int32, (rc, LANES), 1) + t * LANES
                    s_t = s_t + (2.0 * sl2) * jnp.minimum(r_i - c_i, 0).astype(F32)
                    return jnp.where((c_i // CHUNK) <= (r_i // CHUNK), s_t, NEG_BIG)

                tiles = [tile(lanes, t) for t, lanes in enumerate(lane_tiles)]
                m_old = m_scr[mi, rows, :]
                blk_max = jnp.max(functools.reduce(jnp.maximum, tiles), axis=-1, keepdims=True)
                m_new = jnp.maximum(m_old, blk_max + shift_blk)
                alpha = jnp.exp2(m_old - m_new)
                m_scr[mi, rows, :] = m_new
                a_scr[2 * slot + mi, rows, :] = alpha
                if not diag:
                    tiles = [tile(lanes, t) for t, lanes in enumerate(lane_tiles)]
                off = m_new - shift_blk
                ps = [jnp.exp2(s_t - off) for s_t in tiles]
                l_scr[mi, rows, :] = alpha * l_scr[mi, rows, :] + functools.reduce(jnp.add, ps)
                for t, p_t in enumerate(ps):
                    p_map[rows, t * LANES:(t + 1) * LANES] = p_t.astype(BF16)
                if len(ps) * LANES < bq:
                    p_map[rows, len(ps) * LANES:] = jnp.zeros((rc, bq - len(ps) * LANES), BF16)

    def weighted_values(kb, slot):
        k0 = pl.multiple_of(jnp.maximum(kb, 0) * bq, bq)
        for mi in range(2):
            pv = jnp.dot(p_scr[2 * slot + mi], v_ref[pl.ds(k0, bq), :], preferred_element_type=F32)
            for t in range(pv.shape[1] // LANES):
                lanes = slice(t * LANES, (t + 1) * LANES)
                acc_scr[mi, :, lanes] = a_scr[2 * slot + mi] * acc_scr[mi, :, lanes] + pv[:, lanes]

    p_scr[2] = jnp.zeros((bq, bq), BF16)
    p_scr[3] = jnp.zeros((bq, bq), BF16)
    a_scr[2] = jnp.ones((bq, LANES), F32)
    a_scr[3] = jnp.ones((bq, LANES), F32)

    def pair(j2, carry):
        kb = 2 * j2
        scores(kb + 1, 1)
        weighted_values(kb - 1, 1)
        softmax(kb, 0, False)
        scores(kb + 2, 0)
        weighted_values(kb, 0)
        softmax(kb + 1, 1, False)
        return carry

    scores(0, 0)
    lax.fori_loop(0, qi // 2, pair, 0)

    @pl.when(qi % 2 == 1)
    def _():
        scores(qi, 1)
        weighted_values(qi - 2, 1)
        softmax(qi - 1, 0, False)
        weighted_values(qi - 1, 0)
        softmax(qi, 1, True)
        weighted_values(qi, 1)

    @pl.when(qi % 2 == 0)
    def _():
        weighted_values(qi - 1, 1)
        softmax(qi, 0, True)
        weighted_values(qi, 0)
    l1 = jnp.sum(l_scr[0], axis=-1, keepdims=True)
    l2 = jnp.sum(l_scr[1], axis=-1, keepdims=True)
    o_ref[...] = _finish_heads(acc_scr[0], l1, acc_scr[1], l2, lam_ref[0], g_ref[...], post_scale).astype(o_ref.dtype)


def attention_prompt(q_bf, k_bf, v_bf, batch, lam, slopes2, subln_g, post_scale):
    m, width = q_bf.shape
    s = m // batch
    dv = 2 * ATTN_HEAD_DIM
    heads = width // dv
    bq = _pick(s, 512)
    assert bq % CHUNK == 0
    nq = s // bq
    kv_spec = pl.BlockSpec((s, dv), lambda b, h, i: (b, h))
    q_spec = pl.BlockSpec((bq, dv), lambda b, h, i: (b * nq + i, h))
    smem = pl.BlockSpec(memory_space=pltpu.SMEM)
    return pl.pallas_call(
        functools.partial(_attn_prompt_kernel, bq=bq, post_scale=post_scale),
        grid=(batch, heads, nq),
        in_specs=[smem, smem, q_spec, kv_spec, kv_spec, pl.BlockSpec((1, dv), lambda b, h, i: (0, 0))],
        out_specs=q_spec,
        out_shape=jax.ShapeDtypeStruct((m, width), BF16),
        scratch_shapes=[pltpu.VMEM((4, bq, bq), F32), pltpu.VMEM((4, bq, bq), BF16), pltpu.VMEM((2, bq, dv), BF16),
                        pltpu.VMEM((2, bq, dv), F32), pltpu.VMEM((2, bq, LANES), F32),
                        pltpu.VMEM((2, bq, LANES), F32), pltpu.VMEM((4, bq, LANES), F32)],
        compiler_params=_params(3), name="attention_prompt",
    )(lam, slopes2, q_bf, k_bf, v_bf, subln_g.reshape(1, dv))


def _attn_sample_kernel(lam_ref, slope_ref, q_ref, kn_ref, vn_ref, ck_ref, cv_ref, g_ref, o_ref, *, post_scale):
    hd = ATTN_HEAD_DIM
    dv = 2 * hd
    lam = lam_ref[0]
    t = q_ref.shape[0]
    heads = q_ref.shape[1] // dv
    past = ck_ref.shape[0] // (2 * heads)

    def dist_mask(n_keys, key0):
        q_pos = past + lax.broadcasted_iota(jnp.int32, (t, n_keys), 0)
        k_pos = key0 + lax.broadcasted_iota(jnp.int32, (t, n_keys), 1)
        return jnp.abs(q_pos - k_pos).astype(F32), (k_pos // CHUNK) <= (q_pos // CHUNK)

    dist_p, vis_p = dist_mask(past, 0)
    dist_n, vis_n = dist_mask(t, past)
    dn = (((1,), (1,)), ((), ()))
    for h in range(heads):
        cols = slice(h * dv, (h + 1) * dv)
        sl2 = slope_ref[h]
        v_past = jnp.concatenate(
            [cv_ref[pl.ds(c * heads + h, past, stride=heads * (dv // LANES)), :] for c in range(dv // LANES)],
            axis=1).astype(BF16)
        res = []
        for mi in range(2):
            qm = q_ref[:, h * dv + mi * hd:h * dv + (mi + 1) * hd]
            k_past = ck_ref[pl.ds(2 * h + mi, past, stride=2 * heads), :].astype(BF16)
            s_p = lax.dot_general(qm, k_past, dn, preferred_element_type=F32)
            s_n = lax.dot_general(qm, kn_ref[:, h * dv + mi * hd:h * dv + (mi + 1) * hd], dn,
                                  preferred_element_type=F32)
            s_p = jnp.where(vis_p, s_p - sl2 * dist_p, NEG_BIG)
            s_n = jnp.where(vis_n, s_n - sl2 * dist_n, NEG_BIG)
            m = jnp.maximum(jnp.max(s_p, axis=-1, keepdims=True), jnp.max(s_n, axis=-1, keepdims=True))
            p_p = jnp.exp2(s_p - m)
            p_n = jnp.exp2(s_n - m)
            l = jnp.sum(p_p, axis=-1, keepdims=True) + jnp.sum(p_n, axis=-1, keepdims=True)
            acc = (jnp.dot(p_p.astype(BF16), v_past, preferred_element_type=F32)
                   + jnp.dot(p_n.astype(BF16), vn_ref[:, cols], preferred_element_type=F32))
            res += [acc, l]
        o_ref[:, cols] = _finish_heads(res[0], res[1], res[2], res[3], lam, g_ref[...],
                                       post_scale).astype(o_ref.dtype)


def attention_sample(q_bf, k_bf, v_bf, cache_k, cache_v, layer, lam, slopes2, subln_g, post_scale):
    _, batch, past, heads, _, hd = cache_k.shape
    m, width = q_bf.shape
    t = m // batch
    dv = 2 * hd
    new_spec = pl.BlockSpec((t, width), lambda b: (b, 0))
    n_layers = cache_k.shape[0]
    cache_k = cache_k.reshape(n_layers, batch, past * heads * 2, hd)
    cache_v = cache_v.reshape(n_layers, batch, past, heads, dv // LANES, LANES)
    cache_v = cache_v.transpose(0, 1, 2, 4, 3, 5).reshape(n_layers, batch, past * (dv // LANES) * heads, LANES)
    ck_spec = pl.BlockSpec((None, None, past * heads * 2, hd), lambda b: (layer, b, 0, 0))
    cv_spec = pl.BlockSpec((None, None, past * (dv // LANES) * heads, LANES), lambda b: (layer, b, 0, 0))
    smem = pl.BlockSpec(memory_space=pltpu.SMEM)
    return pl.pallas_call(
        functools.partial(_attn_sample_kernel, post_scale=post_scale),
        grid=(batch,),
        in_specs=[smem, smem, new_spec, new_spec, new_spec, ck_spec, cv_spec,
                  pl.BlockSpec((1, dv), lambda b: (0, 0))],
        out_specs=new_spec,
        out_shape=jax.ShapeDtypeStruct((m, width), BF16),
        compiler_params=_params(1), name="attention_sample",
    )(lam, slopes2, q_bf, k_bf, v_bf, cache_k, cache_v, subln_g.reshape(1, dv))


def _sgu_gate_kernel(u_ref, v_ref, g_ref, b_ref, w_ref, bs_ref, *out_refs, chunk_len, write_v):
    bt, width = u_ref.shape
    gd = width // SGU_GROUPS
    v = v_ref[...].astype(F32)
    mu = jnp.mean(v, axis=-1, keepdims=True)
    vc = v - mu
    vn = vc * lax.rsqrt(jnp.mean(vc * vc, axis=-1, keepdims=True) + EPS) * g_ref[...] + b_ref[...]
    if write_v:
        out_refs[1][...] = vn
    vn_bf = vn.astype(BF16)
    row = lax.broadcasted_iota(jnp.int32, (SGU_CHUNK, SGU_CHUNK), 0)
    col = lax.broadcasted_iota(jnp.int32, (SGU_CHUNK, SGU_CHUNK), 1)
    keep = ((row // chunk_len) == (col // chunk_len)) & ((col % chunk_len) <= (row % chunk_len))
    for gi in range(SGU_GROUPS):
        w = jnp.where(keep, w_ref[gi], 0.0).astype(BF16)
        bias = bs_ref[gi]
        for c in range(bt // SGU_CHUNK):
            rows = slice(c * SGU_CHUNK, (c + 1) * SGU_CHUNK)
            cols = slice(gi * gd, (gi + 1) * gd)
            mixed = jnp.dot(w, vn_bf[rows, cols], preferred_element_type=F32) + bias
            out_refs[0][rows, cols] = (u_ref[rows, cols].astype(F32) * mixed).astype(BF16)


def sgu_gate(z, ln_g, ln_b, w_s, b_s, chunk_len, write_v):
    m, two_w = z.shape
    width = two_w // 2
    reps = SGU_CHUNK // chunk_len
    w_t = jnp.tile(w_s[:, :chunk_len, :chunk_len], (1, reps, reps))
    b_t = jnp.tile(b_s[:, :chunk_len], (1, reps))[:, :, None]
    bt = _pick(m, 256)
    row_spec = lambda c: pl.BlockSpec((bt, width), lambda i, c=c: (i, c))
    vec_spec = pl.BlockSpec((1, width), lambda i: (0, 0))
    out_specs = [row_spec(0)]
    out_shape = [jax.ShapeDtypeStruct((m, width), BF16)]
    if write_v:
        out_specs.append(row_spec(0))
        out_shape.append(jax.ShapeDtypeStruct((m, width), F32))
    return pl.pallas_call(
        functools.partial(_sgu_gate_kernel, chunk_len=chunk_len, write_v=write_v),
        grid=(m // bt,),
        in_specs=[row_spec(0), row_spec(1), vec_spec, vec_spec,
                  pl.BlockSpec((SGU_GROUPS, SGU_CHUNK, SGU_CHUNK), lambda i: (0, 0, 0)),
                  pl.BlockSpec((SGU_GROUPS, SGU_CHUNK, 1), lambda i: (0, 0, 0))],
        out_specs=out_specs, out_shape=out_shape,
        compiler_params=_params(1), name="sgu_gate",
    )(z, z, ln_g.reshape(1, width), ln_b.reshape(1, width), w_t, b_t)


def _swiglu(x, wg, wu, wd):
    g = jnp.dot(x, wg, preferred_element_type=F32)
    u = jnp.dot(x, wu, preferred_element_type=F32)
    h = (g * jax.nn.sigmoid(g) * u).astype(BF16)
    return jnp.dot(h, wd, preferred_element_type=F32)


def _moe_kernel(blk_e_ref, n_used_ref, x_ref, wg_ref, wu_ref, wd_ref, o_ref, wg_bf, wu_bf, wd_bf):
    b = pl.program_id(0)
    e = blk_e_ref[b]
    e_prev = blk_e_ref[jnp.maximum(b - 1, 0)]

    @pl.when((b == 0) | (e != e_prev))
    def _():
        wg_bf[...] = wg_ref[...].astype(BF16)
        wu_bf[...] = wu_ref[...].astype(BF16)
        wd_bf[...] = wd_ref[...].astype(BF16)

    @pl.when(b < n_used_ref[0])
    def _():
        o_ref[...] = _swiglu(x_ref[...], wg_bf[...], wu_bf[...], wd_bf[...]).astype(o_ref.dtype)

    @pl.when(b >= n_used_ref[0])
    def _():
        o_ref[...] = jnp.zeros_like(o_ref)


def moe_experts(xg, blk_expert, n_used, layer, w_gate, w_up, w_down):
    rows, d = xg.shape
    ed = w_gate.shape[3]
    n_blocks = rows // MOE_BLOCK
    x_spec = pl.BlockSpec((MOE_BLOCK, d), lambda b, be, nu: (b, 0))
    return pl.pallas_call(
        _moe_kernel,
        grid_spec=pltpu.PrefetchScalarGridSpec(
            num_scalar_prefetch=2, grid=(n_blocks,),
            in_specs=[x_spec,
                      pl.BlockSpec((None, None, d, ed), lambda b, be, nu: (layer, be[b], 0, 0)),
                      pl.BlockSpec((None, None, d, ed), lambda b, be, nu: (layer, be[b], 0, 0)),
                      pl.BlockSpec((None, None, ed, d), lambda b, be, nu: (layer, be[b], 0, 0))],
            out_specs=x_spec,
            scratch_shapes=[pltpu.VMEM((d, ed), BF16), pltpu.VMEM((d, ed), BF16), pltpu.VMEM((ed, d), BF16)]),
        out_shape=jax.ShapeDtypeStruct((rows, d), BF16),
        compiler_params=_params(1), name="moe_experts",
    )(blk_expert, n_used, xg, w_gate, w_up, w_down)


def _shared_kernel(h_ref, wg_ref, wu_ref, wd_ref, o_ref):
    o_ref[...] = _swiglu(h_ref[...], wg_ref[...], wu_ref[...], wd_ref[...]).astype(o_ref.dtype)


def shared_expert(h, sw_gate, sw_up, sw_down):
    m, d = h.shape
    ed = sw_gate.shape[1]
    bm = _pick(m, 512)
    row_spec = pl.BlockSpec((bm, d), lambda i: (i, 0))
    full = lambda a, b: pl.BlockSpec((a, b), lambda i: (0, 0))
    return pl.pallas_call(
        _shared_kernel, grid=(m // bm,),
        in_specs=[row_spec, full(d, ed), full(d, ed), full(ed, d)],
        out_specs=row_spec,
        out_shape=jax.ShapeDtypeStruct((m, d), BF16),
        compiler_params=_params(1), name="shared_expert",
    )(h, sw_gate, sw_up, sw_down)


def _combine_kernel(s_ref, p_ref, w_ref, x_ref, gate_ref, o_ref):
    ffn = s_ref[...].astype(F32)
    w = w_ref[...]
    for k in range(TOP_K):
        ffn = ffn + w[:, k:k + 1] * p_ref[k].astype(F32)
    o_ref[...] = x_ref[...] + gate_ref[...] * ffn


def moe_combine_residual(stream, shared, picked, wts, row0, x, gate):
    m, d = x.shape
    bm = _row_block(stream, m, 256)
    assert row0 % bm == 0
    off = row0 // bm
    row_spec = pl.BlockSpec((bm, d), lambda i: (i, 0))
    return pl.pallas_call(
        _combine_kernel, grid=(m // bm,),
        in_specs=[pl.BlockSpec((bm, d), lambda i: (i + off, 0)),
                  pl.BlockSpec((TOP_K, bm, d), lambda i: (0, i + off, 0)),
                  pl.BlockSpec((bm, TOP_K), lambda i: (i + off, 0)), row_spec,
                  _mod_spec(stream, bm, d, lambda i: (i, 0))],
        out_specs=row_spec,
        out_shape=jax.ShapeDtypeStruct((m, d), F32),
        compiler_params=_params(1), name="moe_combine",
    )(shared, picked, wts, x, gate)


def _dest_kernel(start_ref, idx_ref, rank_ref, dest_ref):
    idx = idx_ref[...]

    def body(e, acc):
        return jnp.where(idx == e, start_ref[e], acc)

    dest_ref[...] = rank_ref[...] + lax.fori_loop(0, start_ref.shape[0], body, jnp.zeros_like(idx))


def assignment_rows(pad_start, idx, rank):
    return pl.pallas_call(
        _dest_kernel,
        in_specs=[pl.BlockSpec(memory_space=pltpu.SMEM), pl.BlockSpec(memory_space=pltpu.VMEM),
                  pl.BlockSpec(memory_space=pltpu.VMEM)],
        out_specs=pl.BlockSpec(memory_space=pltpu.VMEM),
        out_shape=jax.ShapeDtypeStruct(idx.shape, jnp.int32),
        name="assignment_rows",
    )(pad_start, idx, rank)


def moe_routed(h_all, idx, rank, counts, layer, w_gate, w_up, w_down):
    t, d = h_all.shape
    n_exp = counts.shape[0]
    n_assign = t * TOP_K
    counts = counts.reshape(n_exp).astype(jnp.int32)
    padded = (counts + MOE_BLOCK - 1) // MOE_BLOCK * MOE_BLOCK
    pad_end = jnp.cumsum(padded)
    pad_start = pad_end - padded
    dest = assignment_rows(pad_start, idx, rank).reshape(-1)
    n_blocks = -(-(n_assign + n_exp * (MOE_BLOCK - 1)) // MOE_BLOCK)
    n_rows = n_blocks * MOE_BLOCK
    tok = jnp.tile(jnp.arange(t, dtype=jnp.int32), TOP_K)
    src_tok = jnp.arange(n_rows, dtype=jnp.int32) % t
    src_tok = src_tok.at[dest].add(tok - dest % t, unique_indices=True, mode="promise_in_bounds")
    blk_start = jnp.arange(n_blocks, dtype=jnp.int32) * MOE_BLOCK
    blk_expert = jnp.minimum(jnp.sum((pad_end[None, :] <= blk_start[:, None]).astype(jnp.int32), axis=1), n_exp - 1)
    n_used = pad_end[-1:] // MOE_BLOCK
    xg = h_all.at[src_tok].get(mode="promise_in_bounds")
    out = moe_experts(xg, blk_expert, n_used, layer, w_gate, w_up, w_down)
    return out.at[dest].get(mode="promise_in_bounds", unique_indices=True).reshape(TOP_K, t, d)


def _stream(rows_per_batch):
    return dict(per_row=rows_per_batch < 1024, rows_per_batch=rows_per_batch)


def _mod_rows(stream, mod):
    b, d = mod.shape
    if not stream["per_row"]:
        return mod.reshape(b, 1, d)
    return jnp.repeat(mod, stream["rows_per_batch"], axis=0).reshape(1, -1, d)


def kernel(x_prompt, x_sample, cache_k_attn, cache_v_attn, c_prompt, c_sample, norm_mix_g, norm_ffn_g, ada_w, ada_b, attn_w_qkv, attn_w_o, attn_q_norm, attn_k_norm, attn_lambda_q1, attn_lambda_k1, attn_lambda_q2, attn_lambda_k2, attn_subln_g, sgu_w_in, sgu_ln_g, sgu_ln_b, sgu_w_s, sgu_b_s, sgu_w_o, moe_w_router, moe_b_router, moe_w_gate, moe_w_up, moe_w_down, shared_w_gate, shared_w_up, shared_w_down):
    bp, sp, d = x_prompt.shape
    bs, ss, _ = x_sample.shape
    depth = ada_w.shape[0]
    dv = 2 * ATTN_HEAD_DIM
    heads = d // dv
    mp, ms = bp * sp, bs * ss
    streams = (_stream(sp), _stream(ss))
    xs = [x_prompt.reshape(mp, d), x_sample.reshape(ms, d)]

    mod = ada_modulation(jnp.concatenate([c_prompt, c_sample], axis=0), ada_w, ada_b)
    slopes2 = (2.0 ** (-8.0 * jnp.arange(1, heads + 1, dtype=F32) / heads)) * LOG2E

    k_out, v_out, sgu_out = [[], []], [[], []], []
    for i in range(depth):
        j = i // N_MIXERS
        mods = []
        for si, (stream, rows) in enumerate(zip(streams, (slice(0, bp), slice(bp, bp + bs)))):
            mods.append([_mod_rows(stream, mod[i, rows, c * d:(c + 1) * d]) for c in range(6)])

        if i % N_MIXERS == 0:
            lam_init = 0.8 - 0.6 * math.exp(-0.3 * i)
            f = lambda a: a[j].astype(F32)
            lam = (jnp.exp(jnp.sum(f(attn_lambda_q1) * f(attn_lambda_k1)))
                   - jnp.exp(jnp.sum(f(attn_lambda_q2) * f(attn_lambda_k2))) + lam_init).reshape(1)
            for si, stream in enumerate(streams):
                h = modulate(stream, xs[si], norm_mix_g[i], mods[si][0], mods[si][1])
                q_bf, k32, k_bf, v32, v_bf = qkv_project(stream, h, attn_w_qkv[j], attn_q_norm[j], attn_k_norm[j])
                if si == 0:
                    o = attention_prompt(q_bf, k_bf, v_bf, bp, lam, slopes2, attn_subln_g[j], 1.0 - lam_init)
                else:
                    o = attention_sample(q_bf, k_bf, v_bf, cache_k_attn, cache_v_attn, j, lam, slopes2,
                                         attn_subln_g[j], 1.0 - lam_init)
                xs[si] = project_residual(stream, o, attn_w_o[j], xs[si], mods[si][2], "attn_out_proj")
                k_out[si].append(k32)
                v_out[si].append(v32)
        else:
            for si, stream in enumerate(streams):
                h = modulate(stream, xs[si], norm_mix_g[i], mods[si][0], mods[si][1])
                (z,) = matmul(h, sgu_w_in[j], bm=_row_block(stream, h.shape[0], 1024),
                              bn=_pick(sgu_w_in.shape[2], 1024),
                              out_dtypes=(BF16,), epilogue=_gelu_epilogue, name="sgu_in_proj")
                chunk_len = SGU_CHUNK if si == 0 else ss
                res = sgu_gate(z, sgu_ln_g[j], sgu_ln_b[j], sgu_w_s[j], sgu_b_s[j], chunk_len, write_v=(si == 1))
                if si == 1:
                    sgu_out.append(res[1])
                xs[si] = project_residual(stream, res[0], sgu_w_o[j], xs[si], mods[si][2], "sgu_out_proj")

        hs, routing = [], []
        counts = jnp.zeros((moe_w_router.shape[2], 1), F32)
        for si, stream in enumerate(streams):
            hf, idx, wts, rank, counts = modulate(stream, xs[si], norm_ffn_g[i], mods[si][3], mods[si][4],
                                                  router=(moe_w_router[i], moe_b_router[i], counts))
            hs.append(hf)
            routing.append((idx, wts, rank))
        idx, wts, rank = (jnp.concatenate(parts, axis=1) for parts in zip(*routing))
        h_all = jnp.concatenate(hs, axis=0)
        shared = shared_expert(h_all, *[w[i].astype(BF16) for w in (shared_w_gate, shared_w_up, shared_w_down)])
        picked = moe_routed(h_all, idx, rank, counts, i, moe_w_gate, moe_w_up, moe_w_down)
        for si, (stream, row0) in enumerate(zip(streams, (0, mp))):
            xs[si] = moe_combine_residual(stream, shared, picked, wts.T, row0, xs[si], mods[si][5])

    n_attn = len(k_out[0])
    k_prompt = jnp.stack(k_out[0]).reshape(n_attn, bp, sp, heads, 2, ATTN_HEAD_DIM)
    v_prompt = jnp.stack(v_out[0]).reshape(n_attn, bp, sp, heads, dv)
    k_sample = jnp.stack(k_out[1]).reshape(n_attn, bs, ss, heads, 2, ATTN_HEAD_DIM)
    v_sample = jnp.stack(v_out[1]).reshape(n_attn, bs, ss, heads, dv)
    sgu_v = jnp.stack(sgu_out).reshape(len(sgu_out), bs, ss, -1)
    return (xs[0].reshape(bp, sp, d), xs[1].reshape(bs, ss, d), k_prompt, v_prompt, k_sample, v_sample, sgu_v)
```

```python
import functools
import math

import jax
import jax.numpy as jnp
from jax import lax
from jax.experimental import pallas as pl
from jax.experimental.pallas import tpu as pltpu

F32 = jnp.float32
BF16 = jnp.bfloat16

EPS = 1e-6
CHUNK = 64
N_MIXERS = 2
ATTN_HEAD_DIM = 128
SGU_GROUPS = 8
SGU_CHUNK = 128
TOP_K = 8
N_EXPERT_GROUPS = 8
TOPK_GROUPS = 4
ROUTED_SCALE = 2.5
LOG2E = 1.4426950408889634
NEG_BIG = -1e30

VMEM_LIMIT = 56 * 1024 * 1024
MOE_BLOCK = 512
MOE_CHUNKS = 4
ATTN_ROW_CHUNK = 64
LANES = 128


def _params(n_axes):
    return pltpu.CompilerParams(dimension_semantics=("arbitrary",) * n_axes, vmem_limit_bytes=VMEM_LIMIT)


def _pick(n, pref):
    if n <= pref:
        return n
    b = pref
    while n % b:
        b //= 2
    return b


def _ada_kernel(c_ref, w_ref, b_ref, o_ref):
    c = c_ref[...]
    x = (c * jax.nn.sigmoid(c)).astype(BF16)
    o_ref[...] = jnp.dot(x, w_ref[...].astype(BF16), preferred_element_type=F32) + b_ref[...]


def ada_modulation(c, ada_w, ada_b):
    n_layers, d, n = ada_w.shape
    r = c.shape[0]
    bn = _pick(n, 1024)
    return pl.pallas_call(
        _ada_kernel,
        grid=(n_layers, n // bn),
        in_specs=[
            pl.BlockSpec((r, d), lambda l, j: (0, 0)),
            pl.BlockSpec((None, d, bn), lambda l, j: (l, 0, j)),
            pl.BlockSpec((None, 1, bn), lambda l, j: (l, 0, j)),
        ],
        out_specs=pl.BlockSpec((None, r, bn), lambda l, j: (l, 0, j)),
        out_shape=jax.ShapeDtypeStruct((n_layers, r, n), F32),
        compiler_params=_params(2),
        name="ada_modulation",
    )(c, ada_w, ada_b.reshape(n_layers, 1, n))


def _modulated(x, g, shift, scale):
    y = x * lax.rsqrt(jnp.mean(x * x, axis=-1, keepdims=True) + EPS)
    return y * g * (1.0 + scale) + shift


def _modulate_kernel(x_ref, g_ref, sh_ref, sc_ref, o_ref):
    o_ref[...] = _modulated(x_ref[...], g_ref[...], sh_ref[...], sc_ref[...]).astype(o_ref.dtype)


def _split3(x):
    hi = x.astype(BF16)
    lo = (x - hi.astype(F32)).astype(BF16)
    return hi, lo


def _first_max(x, ids, n_ids, axes):
    mx = x
    for ax in axes:
        mx = jnp.max(mx, axis=ax, keepdims=True)
    arg = jnp.where(x == mx, ids, n_ids)
    for ax in axes:
        arg = jnp.min(arg, axis=ax, keepdims=True)
    return mx, arg


def _sum_axes(x, axes):
    for ax in axes:
        x = jnp.sum(x, axis=ax, keepdims=True)
    return x


def _modulate_router_kernel(x_ref, g_ref, sh_ref, sc_ref, wrt_ref, br_ref, cin_ref,
                            o_ref, idx_ref, wts_ref, rank_ref, cnt_ref, carry):
    h = _modulated(x_ref[...], g_ref[...], sh_ref[...], sc_ref[...])
    o_ref[...] = h.astype(o_ref.dtype)
    bm = h.shape[0]
    n_exp = wrt_ref.shape[0]
    per = n_exp // N_EXPERT_GROUPS
    grp_shape = (N_EXPERT_GROUPS, per, bm)

    h_hi, h_lo = _split3(h)
    w_hi, w_lo = _split3(wrt_ref[...])
    nt = (((1,), (1,)), ((), ()))
    logits = (lax.dot_general(w_hi, h_hi, nt, preferred_element_type=F32)
              + lax.dot_general(w_lo, h_hi, nt, preferred_element_type=F32)
              + lax.dot_general(w_hi, h_lo, nt, preferred_element_type=F32))
    s = jax.nn.sigmoid(logits)
    s3 = s.reshape(grp_shape)
    sb3 = (s + br_ref[...]).reshape(grp_shape)

    sub = lax.broadcasted_iota(jnp.int32, grp_shape, 1)
    gid = lax.broadcasted_iota(jnp.int32, (N_EXPERT_GROUPS, 1, bm), 0)
    eid = lax.broadcasted_iota(jnp.int32, grp_shape, 0) * per + sub

    m1, i1 = _first_max(sb3, sub, per, (1,))
    m2 = jnp.max(jnp.where(sub == i1, -jnp.inf, sb3), axis=1, keepdims=True)
    work = m1 + m2
    chosen = jnp.zeros_like(work)
    for _ in range(TOPK_GROUPS):
        _, gi = _first_max(work, gid, N_EXPERT_GROUPS, (0,))
        chosen = jnp.where(gid == gi, 1.0, chosen)
        work = jnp.where(gid == gi, -jnp.inf, work)
    sel = jnp.where(chosen > 0.0, sb3, -jnp.inf)

    hits, ids, raw = [], [], []
    for _ in range(TOP_K):
        _, ei = _first_max(sel, eid, n_exp, (1, 0))
        hit = eid == ei
        raw.append(_sum_axes(jnp.where(hit, s3, 0.0), (1, 0)))
        sel = jnp.where(hit, -jnp.inf, sel)
        hits.append(hit)
        ids.append(ei)
    total = sum(raw)

    member = sum(jnp.where(hit, 1.0, 0.0) for hit in hits).reshape(n_exp, bm)
    r_i = lax.broadcasted_iota(jnp.int32, (bm, bm), 0)
    c_i = lax.broadcasted_iota(jnp.int32, (bm, bm), 1)
    before = jnp.where(r_i < c_i, 1.0, 0.0).astype(BF16)

    @pl.when(pl.program_id(0) == 0)
    def _():
        carry[...] = cin_ref[...]

    rank_all = (jnp.dot(member.astype(BF16), before, preferred_element_type=F32) + carry[...]).reshape(grp_shape)
    carry[...] += jnp.sum(member, axis=1, keepdims=True)
    cnt_ref[...] = carry[...]
    for k in range(TOP_K):
        idx_ref[k:k + 1, :] = ids[k].reshape(1, bm)
        wts_ref[k:k + 1, :] = (raw[k] / total * ROUTED_SCALE).reshape(1, bm)
        rank_ref[k:k + 1, :] = _sum_axes(jnp.where(hits[k], rank_all, 0.0), (1, 0)).reshape(1, bm).astype(jnp.int32)


def _row_block(stream, m, pref):
    return _pick(m if stream["per_row"] else stream["rows_per_batch"], pref)


def _mod_spec(stream, bm, bn, ij):
    if stream["per_row"]:
        return pl.BlockSpec((None, bm, bn), lambda *g: (0,) + tuple(ij(*g)))
    bpg = stream["rows_per_batch"] // bm
    return pl.BlockSpec((None, 1, bn), lambda *g: (ij(*g)[0] // bpg, 0, ij(*g)[1]))


def modulate(stream, x, g, shift, scale, router=None):
    m, d = x.shape
    bm = _row_block(stream, m, 512)
    row_spec = pl.BlockSpec((bm, d), lambda i: (i, 0))
    mod_spec = _mod_spec(stream, bm, d, lambda i: (i, 0))
    g_spec = pl.BlockSpec((1, d), lambda i: (0, 0))
    if router is None:
        return pl.pallas_call(
            _modulate_kernel, grid=(m // bm,),
            in_specs=[row_spec, g_spec, mod_spec, mod_spec], out_specs=row_spec,
            out_shape=jax.ShapeDtypeStruct((m, d), BF16),
            compiler_params=_params(1), name="modulate",
        )(x, g.reshape(1, d), shift, scale)
    w_router, b_router, counts_in = router
    e = w_router.shape[1]
    tok_spec = pl.BlockSpec((TOP_K, bm), lambda i: (0, i))
    cnt_spec = pl.BlockSpec((e, 1), lambda i: (0, 0))
    return pl.pallas_call(
        _modulate_router_kernel, grid=(m // bm,),
        in_specs=[row_spec, g_spec, mod_spec, mod_spec, pl.BlockSpec((e, d), lambda i: (0, 0)), cnt_spec, cnt_spec],
        out_specs=[row_spec, tok_spec, tok_spec, tok_spec, cnt_spec],
        out_shape=[jax.ShapeDtypeStruct((m, d), BF16), jax.ShapeDtypeStruct((TOP_K, m), jnp.int32),
                   jax.ShapeDtypeStruct((TOP_K, m), F32), jax.ShapeDtypeStruct((TOP_K, m), jnp.int32),
                   jax.ShapeDtypeStruct((e, 1), F32)],
        scratch_shapes=[pltpu.VMEM((e, 1), F32)],
        compiler_params=_params(1), name="modulate_router",
    )(x, g.reshape(1, d), shift, scale, w_router.T, b_router.astype(F32).reshape(e, 1), counts_in)


def _mm_kernel(x_ref, w_ref, *refs, n_extra, epilogue):
    extra, outs, w_bf = refs[:n_extra], refs[n_extra:-1], refs[-1]

    @pl.when(pl.program_id(1) == 0)
    def _():
        w_bf[...] = w_ref[...].astype(BF16)

    acc = jnp.dot(x_ref[...], w_bf[...], preferred_element_type=F32)
    epilogue(acc, extra, outs)


def matmul(x, w, *, bm, bn, col_block_off=0, n_cols=None, extra=(), extra_specs=(), out_dtypes, epilogue, name):
    m, k = x.shape
    n = w.shape[1] if n_cols is None else n_cols
    grid = (n // bn, m // bm)
    out_spec = pl.BlockSpec((bm, bn), lambda j, i: (i, j))
    return pl.pallas_call(
        functools.partial(_mm_kernel, n_extra=len(extra), epilogue=epilogue),
        grid=grid,
        in_specs=[pl.BlockSpec((bm, k), lambda j, i: (i, 0)),
                  pl.BlockSpec((k, bn), lambda j, i: (0, j + col_block_off))] + list(extra_specs),
        out_specs=[out_spec] * len(out_dtypes),
        out_shape=[jax.ShapeDtypeStruct((m, n), dt) for dt in out_dtypes],
        scratch_shapes=[pltpu.VMEM((k, bn), BF16)],
        compiler_params=_params(2), name=name,
    )(x, w, *extra)


def _head_rms(acc, gain, post_scale):
    pieces = []
    for c in range(acc.shape[1] // ATTN_HEAD_DIM):
        seg = acc[:, c * ATTN_HEAD_DIM:(c + 1) * ATTN_HEAD_DIM]
        y = seg * lax.rsqrt(jnp.mean(seg * seg, axis=-1, keepdims=True) + EPS)
        pieces.append(y * (gain * post_scale))
    return jnp.concatenate(pieces, axis=-1)


def _q_epilogue(acc, extra, outs):
    outs[0][...] = _head_rms(acc, extra[0][...], ATTN_HEAD_DIM ** -0.5 * LOG2E).astype(BF16)


def _k_epilogue(acc, extra, outs):
    kn = _head_rms(acc, extra[0][...], 1.0)
    outs[0][...] = kn
    outs[1][...] = kn.astype(BF16)


def _v_epilogue(acc, extra, outs):
    outs[0][...] = acc
    outs[1][...] = acc.astype(BF16)


def _gelu_epilogue(acc, extra, outs):
    outs[0][...] = (0.5 * acc * (1.0 + lax.erf(acc * (2.0 ** -0.5)))).astype(BF16)


def _residual_epilogue(acc, extra, outs):
    x_ref, gate_ref = extra
    outs[0][...] = x_ref[...] + gate_ref[...] * acc


def qkv_project(stream, h, w_qkv, q_norm, k_norm):
    m, d = h.shape
    bm = _row_block(stream, m, 1024)
    bn = _pick(d, 1024)
    nb = d // bn
    gain_spec = pl.BlockSpec((1, ATTN_HEAD_DIM), lambda j, i: (0, 0))
    common = dict(bm=bm, bn=bn, n_cols=d)
    (q_bf,) = matmul(h, w_qkv, col_block_off=0, extra=(q_norm.reshape(1, -1),), extra_specs=(gain_spec,),
                     out_dtypes=(BF16,), epilogue=_q_epilogue, name="q_proj", **common)
    k32, k_bf = matmul(h, w_qkv, col_block_off=nb, extra=(k_norm.reshape(1, -1),), extra_specs=(gain_spec,),
                       out_dtypes=(F32, BF16), epilogue=_k_epilogue, name="k_proj", **common)
    v32, v_bf = matmul(h, w_qkv, col_block_off=2 * nb, out_dtypes=(F32, BF16), epilogue=_v_epilogue,
                       name="v_proj", **common)
    return q_bf, k32, k_bf, v32, v_bf


def project_residual(stream, a, w, x, gate, name):
    k, n = w.shape
    deep = k > 4096
    bm = _row_block(stream, a.shape[0], 512 if deep else 1024)
    bn = _pick(n, 512 if deep else 1024)
    (out,) = matmul(a, w, bm=bm, bn=bn, extra=(x, gate),
                    extra_specs=(pl.BlockSpec((bm, bn), lambda j, i: (i, j)),
                                 _mod_spec(stream, bm, bn, lambda j, i: (i, j))),
                    out_dtypes=(F32,), epilogue=_residual_epilogue, name=name)
    return out


def _finish_heads(acc1, l1, acc2, l2, lam, g, post_scale):
    o = acc1 / l1 - lam * (acc2 / l2)
    o = o * lax.rsqrt(jnp.mean(o * o, axis=-1, keepdims=True) + EPS)
    return o * (g * post_scale)


def _attn_prompt_kernel(lam_ref, slope_ref, q_ref, k_ref, v_ref, g_ref, o_ref,
                        s_scr, p_scr, qa_scr, acc_scr, m_scr, l_scr, a_scr, *, bq, post_scale):
    hd = ATTN_HEAD_DIM
    rc = ATTN_ROW_CHUNK
    qi = pl.program_id(2)
    hh = pl.program_id(1)
    sl2 = slope_ref[hh]

    m_scr[...] = jnp.full(m_scr.shape, NEG_BIG, F32)
    l_scr[...] = jnp.zeros(l_scr.shape, F32)
    acc_scr[...] = jnp.zeros(acc_scr.shape, F32)

    lane = lax.broadcasted_iota(jnp.int32, (bq, LANES), 1)
    key_c = lax.broadcasted_iota(jnp.int32, (bq, LANES), 0)
    k_aug = jnp.where(lane < 3, key_c % 256, jnp.where(lane < 6, key_c // 256, 0)).astype(F32).astype(BF16)
    rest = jnp.full((bq, LANES), sl2, F32)
    piece = jnp.zeros((bq, LANES), F32)
    for i in range(3):
        part = rest.astype(BF16).astype(F32)
        piece = jnp.where(lane % 3 == i, part, piece)
        rest = rest - part
    q_aug = jnp.where(lane < 3, piece, jnp.where(lane < 6, 256.0 * piece, 0.0)).astype(BF16)
    for mi in range(2):
        qa_scr[mi, :, :hd] = q_ref[:, mi * hd:(mi + 1) * hd]
        qa_scr[mi, :, hd:] = q_aug

    def scores(kb, slot):
        k0 = pl.multiple_of(kb * bq, bq)
        for mi in range(2):
            k_blk = jnp.concatenate([k_ref[pl.ds(k0, bq), mi * hd:(mi + 1) * hd], k_aug], axis=1)
            s_scr[2 * slot + mi] = lax.dot_general(qa_scr[mi], k_blk, (((1,), (1,)), ((), ())),
                                                   preferred_element_type=F32)

    def softmax(kb, slot, diag):
        shift_blk = sl2 * ((kb - qi) * bq + jnp.zeros((1, LANES), jnp.int32)).astype(F32)
        for mi in range(2):
            s_map = s_scr.at[2 * slot + mi]
            p_map = p_scr.at[2 * slot + mi]
            for c in range(bq // rc):
                rows = slice(c * rc, (c + 1) * rc)
                ncol = min(bq, -(-((c + 1) * rc) // CHUNK) * CHUNK) if diag else bq
                lane_tiles = [slice(t * LANES, (t + 1) * LANES) for t in range(-(-ncol // LANES))]

                def tile(lanes, t):
                    s_t = s_map[rows, lanes]
                    if not diag:
                        return s_t
                    r_i = lax.broadcasted_iota(jnp.int32, (rc, LANES), 0) + c * rc
                    c_i = lax.broadcasted_iota(jnp.int32, (rc, LANES), 1) + t * LANES
                    s_t = s_t + (2.0 * sl2) * jnp.minimum(r_i - c_i, 0).astype(F32)
                    return jnp.where((c_i // CHUNK) <= (r_i // CHUNK), s_t, NEG_BIG)

                tiles = [tile(lanes, t) for t, lanes in enumerate(lane_tiles)]
                m_old = m_scr[mi, rows, :]
                blk_max = jnp.max(functools.reduce(jnp.maximum, tiles), axis=-1, keepdims=True)
                m_new = jnp.maximum(m_old, blk_max + shift_blk)
                alpha = jnp.exp2(m_old - m_new)
                m_scr[mi, rows, :] = m_new
                a_scr[2 * slot + mi, rows, :] = alpha
                if not diag:
                    tiles = [tile(lanes, t) for t, lanes in enumerate(lane_tiles)]
                off = m_new - shift_blk
                ps = [jnp.exp2(s_t - off) for s_t in tiles]
                l_scr[mi, rows, :] = alpha * l_scr[mi, rows, :] + functools.reduce(jnp.add, ps)
                for t, p_t in enumerate(ps):
                    p_map[rows, t * LANES:(t + 1) * LANES] = p_t.astype(BF16)
                if len(ps) * LANES < bq:
                    p_map[rows, len(ps) * LANES:] = jnp.zeros((rc, bq - len(ps) * LANES), BF16)

    def weighted_values(kb, slot):
        k0 = pl.multiple_of(jnp.maximum(kb, 0) * bq, bq)
        for mi in range(2):
            pv = jnp.dot(p_scr[2 * slot + mi], v_ref[pl.ds(k0, bq), :], preferred_element_type=F32)
            for t in range(pv.shape[1] // LANES):
                lanes = slice(t * LANES, (t + 1) * LANES)
                acc_scr[mi, :, lanes] = a_scr[2 * slot + mi] * acc_scr[mi, :, lanes] + pv[:, lanes]

    p_scr[2] = jnp.zeros((bq, bq), BF16)
    p_scr[3] = jnp.zeros((bq, bq), BF16)
    a_scr[2] = jnp.ones((bq, LANES), F32)
    a_scr[3] = jnp.ones((bq, LANES), F32)

    def pair(j2, carry):
        kb = 2 * j2
        scores(kb + 1, 1)
        weighted_values(kb - 1, 1)
        softmax(kb, 0, False)
        scores(kb + 2, 0)
        weighted_values(kb, 0)
        softmax(kb + 1, 1, False)
        return carry

    scores(0, 0)
    lax.fori_loop(0, qi // 2, pair, 0)

    @pl.when(qi % 2 == 1)
    def _():
        scores(qi, 1)
        weighted_values(qi - 2, 1)
        softmax(qi - 1, 0, False)
        weighted_values(qi - 1, 0)
        softmax(qi, 1, True)
        weighted_values(qi, 1)

    @pl.when(qi % 2 == 0)
    def _():
        weighted_values(qi - 1, 1)
        softmax(qi, 0, True)
        weighted_values(qi, 0)
    l1 = jnp.sum(l_scr[0], axis=-1, keepdims=True)
    l2 = jnp.sum(l_scr[1], axis=-1, keepdims=True)
    o_ref[...] = _finish_heads(acc_scr[0], l1, acc_scr[1], l2, lam_ref[0], g_ref[...], post_scale).astype(o_ref.dtype)


def attention_prompt(q_bf, k_bf, v_bf, batch, lam, slopes2, subln_g, post_scale):
    m, width = q_bf.shape
    s = m // batch
    dv = 2 * ATTN_HEAD_DIM
    heads = width // dv
    bq = _pick(s, 512)
    assert bq % CHUNK == 0
    nq = s // bq
    kv_spec = pl.BlockSpec((s, dv), lambda b, h, i: (b, h))
    q_spec = pl.BlockSpec((bq, dv), lambda b, h, i: (b * nq + i, h))
    smem = pl.BlockSpec(memory_space=pltpu.SMEM)
    return pl.pallas_call(
        functools.partial(_attn_prompt_kernel, bq=bq, post_scale=post_scale),
        grid=(batch, heads, nq),
        in_specs=[smem, smem, q_spec, kv_spec, kv_spec, pl.BlockSpec((1, dv), lambda b, h, i: (0, 0))],
        out_specs=q_spec,
        out_shape=jax.ShapeDtypeStruct((m, width), BF16),
        scratch_shapes=[pltpu.VMEM((4, bq, bq), F32), pltpu.VMEM((4, bq, bq), BF16), pltpu.VMEM((2, bq, dv), BF16),
                        pltpu.VMEM((2, bq, dv), F32), pltpu.VMEM((2, bq, LANES), F32),
                        pltpu.VMEM((2, bq, LANES), F32), pltpu.VMEM((4, bq, LANES), F32)],
        compiler_params=_params(3), name="attention_prompt",
    )(lam, slopes2, q_bf, k_bf, v_bf, subln_g.reshape(1, dv))


def _attn_sample_kernel(lam_ref, slope_ref, q_ref, kn_ref, vn_ref, ck_ref, cv_ref, g_ref, o_ref, *, post_scale):
    hd = ATTN_HEAD_DIM
    dv = 2 * hd
    lam = lam_ref[0]
    t = q_ref.shape[0]
    heads = q_ref.shape[1] // dv
    past = ck_ref.shape[0] // (2 * heads)

    def dist_mask(n_keys, key0):
        q_pos = past + lax.broadcasted_iota(jnp.int32, (t, n_keys), 0)
        k_pos = key0 + lax.broadcasted_iota(jnp.int32, (t, n_keys), 1)
        return jnp.abs(q_pos - k_pos).astype(F32), (k_pos // CHUNK) <= (q_pos // CHUNK)

    dist_p, vis_p = dist_mask(past, 0)
    dist_n, vis_n = dist_mask(t, past)
    dn = (((1,), (1,)), ((), ()))
    for h in range(heads):
        cols = slice(h * dv, (h + 1) * dv)
        sl2 = slope_ref[h]
        v_past = jnp.concatenate(
            [cv_ref[pl.ds(c * heads + h, past, stride=heads * (dv // LANES)), :] for c in range(dv // LANES)],
            axis=1).astype(BF16)
        res = []
        for mi in range(2):
            qm = q_ref[:, h * dv + mi * hd:h * dv + (mi + 1) * hd]
            k_past = ck_ref[pl.ds(2 * h + mi, past, stride=2 * heads), :].astype(BF16)
            s_p = lax.dot_general(qm, k_past, dn, preferred_element_type=F32)
            s_n = lax.dot_general(qm, kn_ref[:, h * dv + mi * hd:h * dv + (mi + 1) * hd], dn,
                                  preferred_element_type=F32)
            s_p = jnp.where(vis_p, s_p - sl2 * dist_p, NEG_BIG)
            s_n = jnp.where(vis_n, s_n - sl2 * dist_n, NEG_BIG)
            m = jnp.maximum(jnp.max(s_p, axis=-1, keepdims=True), jnp.max(s_n, axis=-1, keepdims=True))
            p_p = jnp.exp2(s_p - m)
            p_n = jnp.exp2(s_n - m)
            l = jnp.sum(p_p, axis=-1, keepdims=True) + jnp.sum(p_n, axis=-1, keepdims=True)
            acc = (jnp.dot(p_p.astype(BF16), v_past, preferred_element_type=F32)
                   + jnp.dot(p_n.astype(BF16), vn_ref[:, cols], preferred_element_type=F32))
            res += [acc, l]
        o_ref[:, cols] = _finish_heads(res[0], res[1], res[2], res[3], lam, g_ref[...],
                                       post_scale).astype(o_ref.dtype)


def attention_sample(q_bf, k_bf, v_bf, cache_k, cache_v, layer, lam, slopes2, subln_g, post_scale):
    _, batch, past, heads, _, hd = cache_k.shape
    m, width = q_bf.shape
    t = m // batch
    dv = 2 * hd
    new_spec = pl.BlockSpec((t, width), lambda b: (b, 0))
    n_layers = cache_k.shape[0]
    cache_k = cache_k.reshape(n_layers, batch, past * heads * 2, hd)
    cache_v = cache_v.reshape(n_layers, batch, past, heads, dv // LANES, LANES)
    cache_v = cache_v.transpose(0, 1, 2, 4, 3, 5).reshape(n_layers, batch, past * (dv // LANES) * heads, LANES)
    ck_spec = pl.BlockSpec((None, None, past * heads * 2, hd), lambda b: (layer, b, 0, 0))
    cv_spec = pl.BlockSpec((None, None, past * (dv // LANES) * heads, LANES), lambda b: (layer, b, 0, 0))
    smem = pl.BlockSpec(memory_space=pltpu.SMEM)
    return pl.pallas_call(
        functools.partial(_attn_sample_kernel, post_scale=post_scale),
        grid=(batch,),
        in_specs=[smem, smem, new_spec, new_spec, new_spec, ck_spec, cv_spec,
                  pl.BlockSpec((1, dv), lambda b: (0, 0))],
        out_specs=new_spec,
        out_shape=jax.ShapeDtypeStruct((m, width), BF16),
        compiler_params=_params(1), name="attention_sample",
    )(lam, slopes2, q_bf, k_bf, v_bf, cache_k, cache_v, subln_g.reshape(1, dv))


def _sgu_gate_kernel(u_ref, v_ref, g_ref, b_ref, w_ref, bs_ref, *out_refs, chunk_len, write_v):
    bt, width = u_ref.shape
    gd = width // SGU_GROUPS
    v = v_ref[...].astype(F32)
    mu = jnp.mean(v, axis=-1, keepdims=True)
    vc = v - mu
    vn = vc * lax.rsqrt(jnp.mean(vc * vc, axis=-1, keepdims=True) + EPS) * g_ref[...] + b_ref[...]
    if write_v:
        out_refs[1][...] = vn
    vn_bf = vn.astype(BF16)
    row = lax.broadcasted_iota(jnp.int32, (SGU_CHUNK, SGU_CHUNK), 0)
    col = lax.broadcasted_iota(jnp.int32, (SGU_CHUNK, SGU_CHUNK), 1)
    keep = ((row // chunk_len) == (col // chunk_len)) & ((col % chunk_len) <= (row % chunk_len))
    for gi in range(SGU_GROUPS):
        w = jnp.where(keep, w_ref[gi], 0.0).astype(BF16)
        bias = bs_ref[gi]
        for c in range(bt // SGU_CHUNK):
            rows = slice(c * SGU_CHUNK, (c + 1) * SGU_CHUNK)
            cols = slice(gi * gd, (gi + 1) * gd)
            mixed = jnp.dot(w, vn_bf[rows, cols], preferred_element_type=F32) + bias
            out_refs[0][rows, cols] = (u_ref[rows, cols].astype(F32) * mixed).astype(BF16)


def sgu_gate(z, ln_g, ln_b, w_s, b_s, chunk_len, write_v):
    m, two_w = z.shape
    width = two_w // 2
    reps = SGU_CHUNK // chunk_len
    w_t = jnp.tile(w_s[:, :chunk_len, :chunk_len], (1, reps, reps))
    b_t = jnp.tile(b_s[:, :chunk_len], (1, reps))[:, :, None]
    bt = _pick(m, 256)
    row_spec = lambda c: pl.BlockSpec((bt, width), lambda i, c=c: (i, c))
    vec_spec = pl.BlockSpec((1, width), lambda i: (0, 0))
    out_specs = [row_spec(0)]
    out_shape = [jax.ShapeDtypeStruct((m, width), BF16)]
    if write_v:
        out_specs.append(row_spec(0))
        out_shape.append(jax.ShapeDtypeStruct((m, width), F32))
    return pl.pallas_call(
        functools.partial(_sgu_gate_kernel, chunk_len=chunk_len, write_v=write_v),
        grid=(m // bt,),
        in_specs=[row_spec(0), row_spec(1), vec_spec, vec_spec,
                  pl.BlockSpec((SGU_GROUPS, SGU_CHUNK, SGU_CHUNK), lambda i: (0, 0, 0)),
                  pl.BlockSpec((SGU_GROUPS, SGU_CHUNK, 1), lambda i: (0, 0, 0))],
        out_specs=out_specs, out_shape=out_shape,
        compiler_params=_params(1), name="sgu_gate",
    )(z, z, ln_g.reshape(1, width), ln_b.reshape(1, width), w_t, b_t)


def _swiglu(x, wg, wu, wd):
    g = jnp.dot(x, wg, preferred_element_type=F32)
    u = jnp.dot(x, wu, preferred_element_type=F32)
    h = (g * jax.nn.sigmoid(g) * u).astype(BF16)
    return jnp.dot(h, wd, preferred_element_type=F32)


def _moe_kernel(blk_e_ref, n_used_ref, x_ref, wg_ref, wu_ref, wd_ref, *refs, blk0):
    o_ref, wg_bf, wu_bf, wd_bf = refs[-4:]
    b = pl.program_id(0)
    e = blk_e_ref[blk0 + b]
    e_prev = blk_e_ref[blk0 + jnp.maximum(b - 1, 0)]

    @pl.when((b == 0) | (e != e_prev))
    def _():
        wg_bf[...] = wg_ref[...].astype(BF16)
        wu_bf[...] = wu_ref[...].astype(BF16)
        wd_bf[...] = wd_ref[...].astype(BF16)

    @pl.when(blk0 + b < n_used_ref[0])
    def _():
        o_ref[...] = _swiglu(x_ref[...], wg_bf[...], wu_bf[...], wd_bf[...]).astype(o_ref.dtype)

    @pl.when(blk0 + b >= n_used_ref[0])
    def _():
        o_ref[...] = jnp.zeros_like(o_ref)


def moe_experts(xg, blk_expert, n_used, layer, w_gate, w_up, w_down, out, blk0):
    rows, d = xg.shape
    ed = w_gate.shape[3]
    x_spec = pl.BlockSpec((MOE_BLOCK, d), lambda b, be, nu: (b, 0))
    w_idx = lambda b, be, nu: (layer, be[blk0 + b], 0, 0)
    in_specs = [x_spec, pl.BlockSpec((None, None, d, ed), w_idx), pl.BlockSpec((None, None, d, ed), w_idx),
                pl.BlockSpec((None, None, ed, d), w_idx)]
    args = [blk_expert, n_used, xg, w_gate, w_up, w_down]
    aliases = {}
    if out is not None:
        in_specs.append(pl.BlockSpec(memory_space=pl.ANY))
        args.append(out)
        aliases = {len(args) - 1: 0}
    return pl.pallas_call(
        functools.partial(_moe_kernel, blk0=blk0),
        grid_spec=pltpu.PrefetchScalarGridSpec(
            num_scalar_prefetch=2, grid=(rows // MOE_BLOCK,),
            in_specs=in_specs,
            out_specs=pl.BlockSpec((MOE_BLOCK, d), lambda b, be, nu: (blk0 + b, 0)),
            scratch_shapes=[pltpu.VMEM((d, ed), BF16), pltpu.VMEM((d, ed), BF16), pltpu.VMEM((ed, d), BF16)]),
        out_shape=jax.ShapeDtypeStruct((blk_expert.shape[0] * MOE_BLOCK, d), BF16),
        input_output_aliases=aliases,
        compiler_params=_params(1), name="moe_experts",
    )(*args)


def _shared_kernel(h_ref, wg_ref, wu_ref, wd_ref, o_ref):
    o_ref[...] = _swiglu(h_ref[...], wg_ref[...], wu_ref[...], wd_ref[...]).astype(o_ref.dtype)


def shared_expert(h, sw_gate, sw_up, sw_down):
    m, d = h.shape
    ed = sw_gate.shape[1]
    bm = _pick(m, 512)
    row_spec = pl.BlockSpec((bm, d), lambda i: (i, 0))
    full = lambda a, b: pl.BlockSpec((a, b), lambda i: (0, 0))
    return pl.pallas_call(
        _shared_kernel, grid=(m // bm,),
        in_specs=[row_spec, full(d, ed), full(d, ed), full(ed, d)],
        out_specs=row_spec,
        out_shape=jax.ShapeDtypeStruct((m, d), BF16),
        compiler_params=_params(1), name="shared_expert",
    )(h, sw_gate, sw_up, sw_down)


def _combine_kernel(s_ref, p_ref, w_ref, x_ref, gate_ref, o_ref):
    ffn = s_ref[...].astype(F32)
    w = w_ref[...]
    for k in range(TOP_K):
        ffn = ffn + w[:, k:k + 1] * p_ref[k].astype(F32)
    o_ref[...] = x_ref[...] + gate_ref[...] * ffn


def moe_combine_residual(stream, shared, picked, wts, row0, x, gate):
    m, d = x.shape
    bm = _row_block(stream, m, 256)
    assert row0 % bm == 0
    off = row0 // bm
    row_spec = pl.BlockSpec((bm, d), lambda i: (i, 0))
    return pl.pallas_call(
        _combine_kernel, grid=(m // bm,),
        in_specs=[pl.BlockSpec((bm, d), lambda i: (i + off, 0)),
                  pl.BlockSpec((TOP_K, bm, d), lambda i: (0, i + off, 0)),
                  pl.BlockSpec((bm, TOP_K), lambda i: (i + off, 0)), row_spec,
                  _mod_spec(stream, bm, d, lambda i: (i, 0))],
        out_specs=row_spec,
        out_shape=jax.ShapeDtypeStruct((m, d), F32),
        compiler_params=_params(1), name="moe_combine",
    )(shared, picked, wts, x, gate)


def _dest_kernel(start_ref, idx_ref, rank_ref, dest_ref):
    idx = idx_ref[...]

    def body(e, acc):
        return jnp.where(idx == e, start_ref[e], acc)

    dest_ref[...] = rank_ref[...] + lax.fori_loop(0, start_ref.shape[0], body, jnp.zeros_like(idx))


def assignment_rows(pad_start, idx, rank):
    return pl.pallas_call(
        _dest_kernel,
        in_specs=[pl.BlockSpec(memory_space=pltpu.SMEM), pl.BlockSpec(memory_space=pltpu.VMEM),
                  pl.BlockSpec(memory_space=pltpu.VMEM)],
        out_specs=pl.BlockSpec(memory_space=pltpu.VMEM),
        out_shape=jax.ShapeDtypeStruct(idx.shape, jnp.int32),
        name="assignment_rows",
    )(pad_start, idx, rank)


def moe_routed(h_all, idx, rank, counts, layer, w_gate, w_up, w_down):
    t, d = h_all.shape
    n_exp = counts.shape[0]
    n_assign = t * TOP_K
    counts = counts.reshape(n_exp).astype(jnp.int32)
    padded = (counts + MOE_BLOCK - 1) // MOE_BLOCK * MOE_BLOCK
    pad_end = jnp.cumsum(padded)
    pad_start = pad_end - padded
    dest = assignment_rows(pad_start, idx, rank).reshape(-1)
    n_blocks = -(-(n_assign + n_exp * (MOE_BLOCK - 1)) // MOE_BLOCK)
    n_blocks = -(-n_blocks // MOE_CHUNKS) * MOE_CHUNKS
    n_rows = n_blocks * MOE_BLOCK
    tok = jnp.tile(jnp.arange(t, dtype=jnp.int32), TOP_K)
    src_tok = jnp.arange(n_rows, dtype=jnp.int32) % t
    src_tok = src_tok.at[dest].add(tok - dest % t, unique_indices=True, mode="promise_in_bounds")
    blk_start = jnp.arange(n_blocks, dtype=jnp.int32) * MOE_BLOCK
    blk_expert = jnp.minimum(jnp.sum((pad_end[None, :] <= blk_start[:, None]).astype(jnp.int32), axis=1), n_exp - 1)
    n_used = pad_end[-1:] // MOE_BLOCK
    out = None
    chunk_rows = n_rows // MOE_CHUNKS
    for c in range(MOE_CHUNKS):
        xg = h_all.at[src_tok[c * chunk_rows:(c + 1) * chunk_rows]].get(mode="promise_in_bounds")
        out = moe_experts(xg, blk_expert, n_used, layer, w_gate, w_up, w_down, out, c * (n_blocks // MOE_CHUNKS))
    return out.at[dest].get(mode="promise_in_bounds", unique_indices=True).reshape(TOP_K, t, d)


def _stream(rows_per_batch):
    return dict(per_row=rows_per_batch < 1024, rows_per_batch=rows_per_batch)


def _mod_rows(stream, mod):
    b, d = mod.shape
    if not stream["per_row"]:
        return mod.reshape(b, 1, d)
    return jnp.repeat(mod, stream["rows_per_batch"], axis=0).reshape(1, -1, d)


def kernel(x_prompt, x_sample, cache_k_attn, cache_v_attn, c_prompt, c_sample, norm_mix_g, norm_ffn_g, ada_w, ada_b, attn_w_qkv, attn_w_o, attn_q_norm, attn_k_norm, attn_lambda_q1, attn_lambda_k1, attn_lambda_q2, attn_lambda_k2, attn_subln_g, sgu_w_in, sgu_ln_g, sgu_ln_b, sgu_w_s, sgu_b_s, sgu_w_o, moe_w_router, moe_b_router, moe_w_gate, moe_w_up, moe_w_down, shared_w_gate, shared_w_up, shared_w_down):
    bp, sp, d = x_prompt.shape
    bs, ss, _ = x_sample.shape
    depth = ada_w.shape[0]
    dv = 2 * ATTN_HEAD_DIM
    heads = d // dv
    mp, ms = bp * sp, bs * ss
    streams = (_stream(sp), _stream(ss))
    xs = [x_prompt.reshape(mp, d), x_sample.reshape(ms, d)]

    mod = ada_modulation(jnp.concatenate([c_prompt, c_sample], axis=0), ada_w, ada_b)
    slopes2 = (2.0 ** (-8.0 * jnp.arange(1, heads + 1, dtype=F32) / heads)) * LOG2E

    k_out, v_out, sgu_out = [[], []], [[], []], []
    for i in range(depth):
        j = i // N_MIXERS
        mods = []
        for si, (stream, rows) in enumerate(zip(streams, (slice(0, bp), slice(bp, bp + bs)))):
            mods.append([_mod_rows(stream, mod[i, rows, c * d:(c + 1) * d]) for c in range(6)])

        if i % N_MIXERS == 0:
            lam_init = 0.8 - 0.6 * math.exp(-0.3 * i)
            f = lambda a: a[j].astype(F32)
            lam = (jnp.exp(jnp.sum(f(attn_lambda_q1) * f(attn_lambda_k1)))
                   - jnp.exp(jnp.sum(f(attn_lambda_q2) * f(attn_lambda_k2))) + lam_init).reshape(1)
            for si, stream in enumerate(streams):
                h = modulate(stream, xs[si], norm_mix_g[i], mods[si][0], mods[si][1])
                q_bf, k32, k_bf, v32, v_bf = qkv_project(stream, h, attn_w_qkv[j], attn_q_norm[j], attn_k_norm[j])
                if si == 0:
                    o = attention_prompt(q_bf, k_bf, v_bf, bp, lam, slopes2, attn_subln_g[j], 1.0 - lam_init)
                else:
                    o = attention_sample(q_bf, k_bf, v_bf, cache_k_attn, cache_v_attn, j, lam, slopes2,
                                         attn_subln_g[j], 1.0 - lam_init)
                xs[si] = project_residual(stream, o, attn_w_o[j], xs[si], mods[si][2], "attn_out_proj")
                k_out[si].append(k32)
                v_out[si].append(v32)
        else:
            for si, stream in enumerate(streams):
                h = modulate(stream, xs[si], norm_mix_g[i], mods[si][0], mods[si][1])
                (z,) = matmul(h, sgu_w_in[j], bm=_row_block(stream, h.shape[0], 1024),
                              bn=_pick(sgu_w_in.shape[2], 1024),
                              out_dtypes=(BF16,), epilogue=_gelu_epilogue, name="sgu_in_proj")
                chunk_len = SGU_CHUNK if si == 0 else ss
                res = sgu_gate(z, sgu_ln_g[j], sgu_ln_b[j], sgu_w_s[j], sgu_b_s[j], chunk_len, write_v=(si == 1))
                if si == 1:
                    sgu_out.append(res[1])
                xs[si] = project_residual(stream, res[0], sgu_w_o[j], xs[si], mods[si][2], "sgu_out_proj")

        hs, routing = [], []
        counts = jnp.zeros((moe_w_router.shape[2], 1), F32)
        for si, stream in enumerate(streams):
            hf, idx, wts, rank, counts = modulate(stream, xs[si], norm_ffn_g[i], mods[si][3], mods[si][4],
                                                  router=(moe_w_router[i], moe_b_router[i], counts))
            hs.append(hf)
            routing.append((idx, wts, rank))
        idx, wts, rank = (jnp.concatenate(parts, axis=1) for parts in zip(*routing))
        h_all = jnp.concatenate(hs, axis=0)
        shared = shared_expert(h_all, *[w[i].astype(BF16) for w in (shared_w_gate, shared_w_up, shared_w_down)])
        picked = moe_routed(h_all, idx, rank, counts, i, moe_w_gate, moe_w_up, moe_w_down)
        for si, (stream, row0) in enumerate(zip(streams, (0, mp))):
            xs[si] = moe_combine_residual(stream, shared, picked, wts.T, row0, xs[si], mods[si][5])

    n_attn = len(k_out[0])
    k_prompt = jnp.stack(k_out[0]).reshape(n_attn, bp, sp, heads, 2, ATTN_HEAD_DIM)
    v_prompt = jnp.stack(v_out[0]).reshape(n_attn, bp, sp, heads, dv)
    k_sample = jnp.stack(k_out[1]).reshape(n_attn, bs, ss, heads, 2, ATTN_HEAD_DIM)
    v_sample = jnp.stack(v_out[1]).reshape(n_attn, bs, ss, heads, dv)
    sgu_v = jnp.stack(sgu_out).reshape(len(sgu_out), bs, ss, -1)
    return (xs[0].reshape(bp, sp, d), xs[1].reshape(bs, ss, d), k_prompt, v_prompt, k_sample, v_sample, sgu_v)
```

```python
import functools
import math

import jax
import jax.numpy as jnp
from jax import lax
from jax.experimental import pallas as pl
from jax.experimental.pallas import tpu as pltpu

F32 = jnp.float32
BF16 = jnp.bfloat16

EPS = 1e-6
CHUNK = 64
N_MIXERS = 2
ATTN_HEAD_DIM = 128
SGU_GROUPS = 8
SGU_CHUNK = 128
TOP_K = 8
N_EXPERT_GROUPS = 8
TOPK_GROUPS = 4
ROUTED_SCALE = 2.5
LOG2E = 1.4426950408889634
NEG_BIG = -1e30

VMEM_LIMIT = 56 * 1024 * 1024
MOE_BLOCK = 512
MOE_CHUNKS = 4
COMBINE_TOKENS = 4096
ATTN_ROW_CHUNK = 64
LANES = 128


def _params(n_axes):
    return pltpu.CompilerParams(dimension_semantics=("arbitrary",) * n_axes, vmem_limit_bytes=VMEM_LIMIT)


def _pick(n, pref):
    if n <= pref:
        return n
    b = pref
    while n % b:
        b //= 2
    return b


def _ada_kernel(c_ref, w_ref, b_ref, o_ref):
    c = c_ref[...]
    x = (c * jax.nn.sigmoid(c)).astype(BF16)
    o_ref[...] = jnp.dot(x, w_ref[...].astype(BF16), preferred_element_type=F32) + b_ref[...]


def ada_modulation(c, ada_w, ada_b):
    n_layers, d, n = ada_w.shape
    r = c.shape[0]
    bn = _pick(n, 1024)
    return pl.pallas_call(
        _ada_kernel,
        grid=(n_layers, n // bn),
        in_specs=[
            pl.BlockSpec((r, d), lambda l, j: (0, 0)),
            pl.BlockSpec((None, d, bn), lambda l, j: (l, 0, j)),
            pl.BlockSpec((None, 1, bn), lambda l, j: (l, 0, j)),
        ],
        out_specs=pl.BlockSpec((None, r, bn), lambda l, j: (l, 0, j)),
        out_shape=jax.ShapeDtypeStruct((n_layers, r, n), F32),
        compiler_params=_params(2),
        name="ada_modulation",
    )(c, ada_w, ada_b.reshape(n_layers, 1, n))


def _modulated(x, g, shift, scale):
    y = x * lax.rsqrt(jnp.mean(x * x, axis=-1, keepdims=True) + EPS)
    return y * g * (1.0 + scale) + shift


def _modulate_kernel(x_ref, g_ref, sh_ref, sc_ref, o_ref):
    o_ref[...] = _modulated(x_ref[...], g_ref[...], sh_ref[...], sc_ref[...]).astype(o_ref.dtype)


def _split3(x):
    hi = x.astype(BF16)
    lo = (x - hi.astype(F32)).astype(BF16)
    return hi, lo


def _first_max(x, ids, n_ids, axes):
    mx = x
    for ax in axes:
        mx = jnp.max(mx, axis=ax, keepdims=True)
    arg = jnp.where(x == mx, ids, n_ids)
    for ax in axes:
        arg = jnp.min(arg, axis=ax, keepdims=True)
    return mx, arg


def _sum_axes(x, axes):
    for ax in axes:
        x = jnp.sum(x, axis=ax, keepdims=True)
    return x


def _modulate_router_kernel(x_ref, g_ref, sh_ref, sc_ref, wrt_ref, br_ref, cin_ref,
                            o_ref, idx_ref, wts_ref, rank_ref, cnt_ref, carry):
    h = _modulated(x_ref[...], g_ref[...], sh_ref[...], sc_ref[...])
    o_ref[...] = h.astype(o_ref.dtype)
    bm = h.shape[0]
    n_exp = wrt_ref.shape[0]
    per = n_exp // N_EXPERT_GROUPS
    grp_shape = (N_EXPERT_GROUPS, per, bm)

    h_hi, h_lo = _split3(h)
    w_hi, w_lo = _split3(wrt_ref[...])
    nt = (((1,), (1,)), ((), ()))
    logits = (lax.dot_general(w_hi, h_hi, nt, preferred_element_type=F32)
              + lax.dot_general(w_lo, h_hi, nt, preferred_element_type=F32)
              + lax.dot_general(w_hi, h_lo, nt, preferred_element_type=F32))
    s = jax.nn.sigmoid(logits)
    s3 = s.reshape(grp_shape)
    sb3 = (s + br_ref[...]).reshape(grp_shape)

    sub = lax.broadcasted_iota(jnp.int32, grp_shape, 1)
    gid = lax.broadcasted_iota(jnp.int32, (N_EXPERT_GROUPS, 1, bm), 0)
    eid = lax.broadcasted_iota(jnp.int32, grp_shape, 0) * per + sub

    m1, i1 = _first_max(sb3, sub, per, (1,))
    m2 = jnp.max(jnp.where(sub == i1, -jnp.inf, sb3), axis=1, keepdims=True)
    work = m1 + m2
    chosen = jnp.zeros_like(work)
    for _ in range(TOPK_GROUPS):
        _, gi = _first_max(work, gid, N_EXPERT_GROUPS, (0,))
        chosen = jnp.where(gid == gi, 1.0, chosen)
        work = jnp.where(gid == gi, -jnp.inf, work)
    sel = jnp.where(chosen > 0.0, sb3, -jnp.inf)

    hits, ids, raw = [], [], []
    for _ in range(TOP_K):
        _, ei = _first_max(sel, eid, n_exp, (1, 0))
        hit = eid == ei
        raw.append(_sum_axes(jnp.where(hit, s3, 0.0), (1, 0)))
        sel = jnp.where(hit, -jnp.inf, sel)
        hits.append(hit)
        ids.append(ei)
    total = sum(raw)

    member = sum(jnp.where(hit, 1.0, 0.0) for hit in hits).reshape(n_exp, bm)
    r_i = lax.broadcasted_iota(jnp.int32, (bm, bm), 0)
    c_i = lax.broadcasted_iota(jnp.int32, (bm, bm), 1)
    before = jnp.where(r_i < c_i, 1.0, 0.0).astype(BF16)

    @pl.when(pl.program_id(0) == 0)
    def _():
        carry[...] = cin_ref[...]

    rank_all = (jnp.dot(member.astype(BF16), before, preferred_element_type=F32) + carry[...]).reshape(grp_shape)
    carry[...] += jnp.sum(member, axis=1, keepdims=True)
    cnt_ref[...] = carry[...]
    for k in range(TOP_K):
        idx_ref[k:k + 1, :] = ids[k].reshape(1, bm)
        wts_ref[k:k + 1, :] = (raw[k] / total * ROUTED_SCALE).reshape(1, bm)
        rank_ref[k:k + 1, :] = _sum_axes(jnp.where(hits[k], rank_all, 0.0), (1, 0)).reshape(1, bm).astype(jnp.int32)


def _row_block(stream, m, pref):
    return _pick(m if stream["per_row"] else stream["rows_per_batch"], pref)


def _mod_spec(stream, bm, bn, ij):
    if stream["per_row"]:
        return pl.BlockSpec((None, bm, bn), lambda *g: (0,) + tuple(ij(*g)))
    bpg = stream["rows_per_batch"] // bm
    return pl.BlockSpec((None, 1, bn), lambda *g: (ij(*g)[0] // bpg, 0, ij(*g)[1]))


def modulate(stream, x, g, shift, scale, router=None):
    m, d = x.shape
    bm = _row_block(stream, m, 512)
    row_spec = pl.BlockSpec((bm, d), lambda i: (i, 0))
    mod_spec = _mod_spec(stream, bm, d, lambda i: (i, 0))
    g_spec = pl.BlockSpec((1, d), lambda i: (0, 0))
    if router is None:
        return pl.pallas_call(
            _modulate_kernel, grid=(m // bm,),
            in_specs=[row_spec, g_spec, mod_spec, mod_spec], out_specs=row_spec,
            out_shape=jax.ShapeDtypeStruct((m, d), BF16),
            compiler_params=_params(1), name="modulate",
        )(x, g.reshape(1, d), shift, scale)
    w_router, b_router, counts_in = router
    e = w_router.shape[1]
    tok_spec = pl.BlockSpec((TOP_K, bm), lambda i: (0, i))
    cnt_spec = pl.BlockSpec((e, 1), lambda i: (0, 0))
    return pl.pallas_call(
        _modulate_router_kernel, grid=(m // bm,),
        in_specs=[row_spec, g_spec, mod_spec, mod_spec, pl.BlockSpec((e, d), lambda i: (0, 0)), cnt_spec, cnt_spec],
        out_specs=[row_spec, tok_spec, tok_spec, tok_spec, cnt_spec],
        out_shape=[jax.ShapeDtypeStruct((m, d), BF16), jax.ShapeDtypeStruct((TOP_K, m), jnp.int32),
                   jax.ShapeDtypeStruct((TOP_K, m), F32), jax.ShapeDtypeStruct((TOP_K, m), jnp.int32),
                   jax.ShapeDtypeStruct((e, 1), F32)],
        scratch_shapes=[pltpu.VMEM((e, 1), F32)],
        compiler_params=_params(1), name="modulate_router",
    )(x, g.reshape(1, d), shift, scale, w_router.T, b_router.astype(F32).reshape(e, 1), counts_in)


def _mm_kernel(x_ref, w_ref, *refs, n_extra, epilogue):
    extra, outs, w_bf = refs[:n_extra], refs[n_extra:-1], refs[-1]

    @pl.when(pl.program_id(1) == 0)
    def _():
        w_bf[...] = w_ref[...].astype(BF16)

    acc = jnp.dot(x_ref[...], w_bf[...], preferred_element_type=F32)
    epilogue(acc, extra, outs)


def matmul(x, w, *, bm, bn, col_block_off=0, n_cols=None, extra=(), extra_specs=(), out_dtypes, epilogue, name):
    m, k = x.shape
    n = w.shape[1] if n_cols is None else n_cols
    grid = (n // bn, m // bm)
    out_spec = pl.BlockSpec((bm, bn), lambda j, i: (i, j))
    return pl.pallas_call(
        functools.partial(_mm_kernel, n_extra=len(extra), epilogue=epilogue),
        grid=grid,
        in_specs=[pl.BlockSpec((bm, k), lambda j, i: (i, 0)),
                  pl.BlockSpec((k, bn), lambda j, i: (0, j + col_block_off))] + list(extra_specs),
        out_specs=[out_spec] * len(out_dtypes),
        out_shape=[jax.ShapeDtypeStruct((m, n), dt) for dt in out_dtypes],
        scratch_shapes=[pltpu.VMEM((k, bn), BF16)],
        compiler_params=_params(2), name=name,
    )(x, w, *extra)


def _head_rms(acc, gain, post_scale):
    pieces = []
    for c in range(acc.shape[1] // ATTN_HEAD_DIM):
        seg = acc[:, c * ATTN_HEAD_DIM:(c + 1) * ATTN_HEAD_DIM]
        y = seg * lax.rsqrt(jnp.mean(seg * seg, axis=-1, keepdims=True) + EPS)
        pieces.append(y * (gain * post_scale))
    return jnp.concatenate(pieces, axis=-1)


def _q_epilogue(acc, extra, outs):
    outs[0][...] = _head_rms(acc, extra[0][...], ATTN_HEAD_DIM ** -0.5 * LOG2E).astype(BF16)


def _k_epilogue(acc, extra, outs):
    kn = _head_rms(acc, extra[0][...], 1.0)
    outs[0][...] = kn
    outs[1][...] = kn.astype(BF16)


def _v_epilogue(acc, extra, outs):
    outs[0][...] = acc
    outs[1][...] = acc.astype(BF16)


def _gelu_epilogue(acc, extra, outs):
    outs[0][...] = (0.5 * acc * (1.0 + lax.erf(acc * (2.0 ** -0.5)))).astype(BF16)


def _residual_epilogue(acc, extra, outs):
    x_ref, gate_ref = extra
    outs[0][...] = x_ref[...] + gate_ref[...] * acc


def qkv_project(stream, h, w_qkv, q_norm, k_norm):
    m, d = h.shape
    bm = _row_block(stream, m, 1024)
    bn = _pick(d, 1024)
    nb = d // bn
    gain_spec = pl.BlockSpec((1, ATTN_HEAD_DIM), lambda j, i: (0, 0))
    common = dict(bm=bm, bn=bn, n_cols=d)
    (q_bf,) = matmul(h, w_qkv, col_block_off=0, extra=(q_norm.reshape(1, -1),), extra_specs=(gain_spec,),
                     out_dtypes=(BF16,), epilogue=_q_epilogue, name="q_proj", **common)
    k32, k_bf = matmul(h, w_qkv, col_block_off=nb, extra=(k_norm.reshape(1, -1),), extra_specs=(gain_spec,),
                       out_dtypes=(F32, BF16), epilogue=_k_epilogue, name="k_proj", **common)
    v32, v_bf = matmul(h, w_qkv, col_block_off=2 * nb, out_dtypes=(F32, BF16), epilogue=_v_epilogue,
                       name="v_proj", **common)
    return q_bf, k32, k_bf, v32, v_bf


def project_residual(stream, a, w, x, gate, name):
    k, n = w.shape
    deep = k > 4096
    bm = _row_block(stream, a.shape[0], 512 if deep else 1024)
    bn = _pick(n, 512 if deep else 1024)
    (out,) = matmul(a, w, bm=bm, bn=bn, extra=(x, gate),
                    extra_specs=(pl.BlockSpec((bm, bn), lambda j, i: (i, j)),
                                 _mod_spec(stream, bm, bn, lambda j, i: (i, j))),
                    out_dtypes=(F32,), epilogue=_residual_epilogue, name=name)
    return out


def _finish_heads(acc1, l1, acc2, l2, lam, g, post_scale):
    o = acc1 / l1 - lam * (acc2 / l2)
    o = o * lax.rsqrt(jnp.mean(o * o, axis=-1, keepdims=True) + EPS)
    return o * (g * post_scale)


def _attn_prompt_kernel(lam_ref, slope_ref, q_ref, k_ref, v_ref, g_ref, o_ref,
                        s_scr, p_scr, qa_scr, acc_scr, m_scr, l_scr, a_scr, *, bq, post_scale):
    hd = ATTN_HEAD_DIM
    rc = ATTN_ROW_CHUNK
    qi = pl.program_id(2)
    hh = pl.program_id(1)
    sl2 = slope_ref[hh]

    m_scr[...] = jnp.full(m_scr.shape, NEG_BIG, F32)
    l_scr[...] = jnp.zeros(l_scr.shape, F32)
    acc_scr[...] = jnp.zeros(acc_scr.shape, F32)

    lane = lax.broadcasted_iota(jnp.int32, (bq, LANES), 1)
    key_c = lax.broadcasted_iota(jnp.int32, (bq, LANES), 0)
    k_aug = jnp.where(lane < 3, key_c % 256, jnp.where(lane < 6, key_c // 256, 0)).astype(F32).astype(BF16)
    rest = jnp.full((bq, LANES), sl2, F32)
    piece = jnp.zeros((bq, LANES), F32)
    for i in range(3):
        part = rest.astype(BF16).astype(F32)
        piece = jnp.where(lane % 3 == i, part, piece)
        rest = rest - part
    q_aug = jnp.where(lane < 3, piece, jnp.where(lane < 6, 256.0 * piece, 0.0)).astype(BF16)
    for mi in range(2):
        qa_scr[mi, :, :hd] = q_ref[:, mi * hd:(mi + 1) * hd]
        qa_scr[mi, :, hd:] = q_aug

    def scores(kb, slot):
        k0 = pl.multiple_of(kb * bq, bq)
        for mi in range(2):
            k_blk = jnp.concatenate([k_ref[pl.ds(k0, bq), mi * hd:(mi + 1) * hd], k_aug], axis=1)
            s_scr[2 * slot + mi] = lax.dot_general(qa_scr[mi], k_blk, (((1,), (1,)), ((), ())),
                                                   preferred_element_type=F32)

    def softmax(kb, slot, diag):
        shift_blk = sl2 * ((kb - qi) * bq + jnp.zeros((1, LANES), jnp.int32)).astype(F32)
        for mi in range(2):
            s_map = s_scr.at[2 * slot + mi]
            p_map = p_scr.at[2 * slot + mi]
            for c in range(bq // rc):
                rows = slice(c * rc, (c + 1) * rc)
                ncol = min(bq, -(-((c + 1) * rc) // CHUNK) * CHUNK) if diag else bq
                lane_tiles = [slice(t * LANES, (t + 1) * LANES) for t in range(-(-ncol // LANES))]

                def tile(lanes, t):
                    s_t = s_map[rows, lanes]
                    if not diag:
                        return s_t
                    r_i = lax.broadcasted_iota(jnp.int32, (rc, LANES), 0) + c * rc
                    c_i = lax.broadcasted_iota(jnp.int32, (rc, LANES), 1) + t * LANES
                    s_t = s_t + (2.0 * sl2) * jnp.minimum(r_i - c_i, 0).astype(F32)
                    return jnp.where((c_i // CHUNK) <= (r_i // CHUNK), s_t, NEG_BIG)

                tiles = [tile(lanes, t) for t, lanes in enumerate(lane_tiles)]
                m_old = m_scr[mi, rows, :]
                blk_max = jnp.max(functools.reduce(jnp.maximum, tiles), axis=-1, keepdims=True)
                m_new = jnp.maximum(m_old, blk_max + shift_blk)
                alpha = jnp.exp2(m_old - m_new)
                m_scr[mi, rows, :] = m_new
                a_scr[2 * slot + mi, rows, :] = alpha
                if not diag:
                    tiles = [tile(lanes, t) for t, lanes in enumerate(lane_tiles)]
                off = m_new - shift_blk
                ps = [jnp.exp2(s_t - off) for s_t in tiles]
                l_scr[mi, rows, :] = alpha * l_scr[mi, rows, :] + functools.reduce(jnp.add, ps)
                for t, p_t in enumerate(ps):
                    p_map[rows, t * LANES:(t + 1) * LANES] = p_t.astype(BF16)
                if len(ps) * LANES < bq:
                    p_map[rows, len(ps) * LANES:] = jnp.zeros((rc, bq - len(ps) * LANES), BF16)

    def weighted_values(kb, slot):
        k0 = pl.multiple_of(jnp.maximum(kb, 0) * bq, bq)
        for mi in range(2):
            pv = jnp.dot(p_scr[2 * slot + mi], v_ref[pl.ds(k0, bq), :], preferred_element_type=F32)
            for t in range(pv.shape[1] // LANES):
                lanes = slice(t * LANES, (t + 1) * LANES)
                acc_scr[mi, :, lanes] = a_scr[2 * slot + mi] * acc_scr[mi, :, lanes] + pv[:, lanes]

    p_scr[2] = jnp.zeros((bq, bq), BF16)
    p_scr[3] = jnp.zeros((bq, bq), BF16)
    a_scr[2] = jnp.ones((bq, LANES), F32)
    a_scr[3] = jnp.ones((bq, LANES), F32)

    def pair(j2, carry):
        kb = 2 * j2
        scores(kb + 1, 1)
        weighted_values(kb - 1, 1)
        softmax(kb, 0, False)
        scores(kb + 2, 0)
        weighted_values(kb, 0)
        softmax(kb + 1, 1, False)
        return carry

    scores(0, 0)
    lax.fori_loop(0, qi // 2, pair, 0)

    @pl.when(qi % 2 == 1)
    def _():
        scores(qi, 1)
        weighted_values(qi - 2, 1)
        softmax(qi - 1, 0, False)
        weighted_values(qi - 1, 0)
        softmax(qi, 1, True)
        weighted_values(qi, 1)

    @pl.when(qi % 2 == 0)
    def _():
        weighted_values(qi - 1, 1)
        softmax(qi, 0, True)
        weighted_values(qi, 0)
    l1 = jnp.sum(l_scr[0], axis=-1, keepdims=True)
    l2 = jnp.sum(l_scr[1], axis=-1, keepdims=True)
    o_ref[...] = _finish_heads(acc_scr[0], l1, acc_scr[1], l2, lam_ref[0], g_ref[...], post_scale).astype(o_ref.dtype)


def attention_prompt(q_bf, k_bf, v_bf, batch, lam, slopes2, subln_g, post_scale):
    m, width = q_bf.shape
    s = m // batch
    dv = 2 * ATTN_HEAD_DIM
    heads = width // dv
    bq = _pick(s, 512)
    assert bq % CHUNK == 0
    nq = s // bq
    kv_spec = pl.BlockSpec((s, dv), lambda b, h, i: (b, h))
    q_spec = pl.BlockSpec((bq, dv), lambda b, h, i: (b * nq + i, h))
    smem = pl.BlockSpec(memory_space=pltpu.SMEM)
    return pl.pallas_call(
        functools.partial(_attn_prompt_kernel, bq=bq, post_scale=post_scale),
        grid=(batch, heads, nq),
        in_specs=[smem, smem, q_spec, kv_spec, kv_spec, pl.BlockSpec((1, dv), lambda b, h, i: (0, 0))],
        out_specs=q_spec,
        out_shape=jax.ShapeDtypeStruct((m, width), BF16),
        scratch_shapes=[pltpu.VMEM((4, bq, bq), F32), pltpu.VMEM((4, bq, bq), BF16), pltpu.VMEM((2, bq, dv), BF16),
                        pltpu.VMEM((2, bq, dv), F32), pltpu.VMEM((2, bq, LANES), F32),
                        pltpu.VMEM((2, bq, LANES), F32), pltpu.VMEM((4, bq, LANES), F32)],
        compiler_params=_params(3), name="attention_prompt",
    )(lam, slopes2, q_bf, k_bf, v_bf, subln_g.reshape(1, dv))


def _attn_sample_kernel(lam_ref, slope_ref, q_ref, kn_ref, vn_ref, ck_ref, cv_ref, g_ref, o_ref, *, post_scale):
    hd = ATTN_HEAD_DIM
    dv = 2 * hd
    lam = lam_ref[0]
    t = q_ref.shape[0]
    heads = q_ref.shape[1] // dv
    past = ck_ref.shape[0] // (2 * heads)

    def dist_mask(n_keys, key0):
        q_pos = past + lax.broadcasted_iota(jnp.int32, (t, n_keys), 0)
        k_pos = key0 + lax.broadcasted_iota(jnp.int32, (t, n_keys), 1)
        return jnp.abs(q_pos - k_pos).astype(F32), (k_pos // CHUNK) <= (q_pos // CHUNK)

    dist_p, vis_p = dist_mask(past, 0)
    dist_n, vis_n = dist_mask(t, past)
    dn = (((1,), (1,)), ((), ()))
    for h in range(heads):
        cols = slice(h * dv, (h + 1) * dv)
        sl2 = slope_ref[h]
        v_past = jnp.concatenate(
            [cv_ref[pl.ds(c * heads + h, past, stride=heads * (dv // LANES)), :] for c in range(dv // LANES)],
            axis=1).astype(BF16)
        res = []
        for mi in range(2):
            qm = q_ref[:, h * dv + mi * hd:h * dv + (mi + 1) * hd]
            k_past = ck_ref[pl.ds(2 * h + mi, past, stride=2 * heads), :].astype(BF16)
            s_p = lax.dot_general(qm, k_past, dn, preferred_element_type=F32)
            s_n = lax.dot_general(qm, kn_ref[:, h * dv + mi * hd:h * dv + (mi + 1) * hd], dn,
                                  preferred_element_type=F32)
            s_p = jnp.where(vis_p, s_p - sl2 * dist_p, NEG_BIG)
            s_n = jnp.where(vis_n, s_n - sl2 * dist_n, NEG_BIG)
            m = jnp.maximum(jnp.max(s_p, axis=-1, keepdims=True), jnp.max(s_n, axis=-1, keepdims=True))
            p_p = jnp.exp2(s_p - m)
            p_n = jnp.exp2(s_n - m)
            l = jnp.sum(p_p, axis=-1, keepdims=True) + jnp.sum(p_n, axis=-1, keepdims=True)
            acc = (jnp.dot(p_p.astype(BF16), v_past, preferred_element_type=F32)
                   + jnp.dot(p_n.astype(BF16), vn_ref[:, cols], preferred_element_type=F32))
            res += [acc, l]
        o_ref[:, cols] = _finish_heads(res[0], res[1], res[2], res[3], lam, g_ref[...],
                                       post_scale).astype(o_ref.dtype)


def attention_sample(q_bf, k_bf, v_bf, cache_k, cache_v, layer, lam, slopes2, subln_g, post_scale):
    _, batch, past, heads, _, hd = cache_k.shape
    m, width = q_bf.shape
    t = m // batch
    dv = 2 * hd
    new_spec = pl.BlockSpec((t, width), lambda b: (b, 0))
    n_layers = cache_k.shape[0]
    cache_k = cache_k.reshape(n_layers, batch, past * heads * 2, hd)
    cache_v = cache_v.reshape(n_layers, batch, past, heads, dv // LANES, LANES)
    cache_v = cache_v.transpose(0, 1, 2, 4, 3, 5).reshape(n_layers, batch, past * (dv // LANES) * heads, LANES)
    ck_spec = pl.BlockSpec((None, None, past * heads * 2, hd), lambda b: (layer, b, 0, 0))
    cv_spec = pl.BlockSpec((None, None, past * (dv // LANES) * heads, LANES), lambda b: (layer, b, 0, 0))
    smem = pl.BlockSpec(memory_space=pltpu.SMEM)
    return pl.pallas_call(
        functools.partial(_attn_sample_kernel, post_scale=post_scale),
        grid=(batch,),
        in_specs=[smem, smem, new_spec, new_spec, new_spec, ck_spec, cv_spec,
                  pl.BlockSpec((1, dv), lambda b: (0, 0))],
        out_specs=new_spec,
        out_shape=jax.ShapeDtypeStruct((m, width), BF16),
        compiler_params=_params(1), name="attention_sample",
    )(lam, slopes2, q_bf, k_bf, v_bf, cache_k, cache_v, subln_g.reshape(1, dv))


def _sgu_gate_kernel(u_ref, v_ref, g_ref, b_ref, w_ref, bs_ref, *out_refs, chunk_len, write_v):
    bt, width = u_ref.shape
    gd = width // SGU_GROUPS
    v = v_ref[...].astype(F32)
    mu = jnp.mean(v, axis=-1, keepdims=True)
    vc = v - mu
    vn = vc * lax.rsqrt(jnp.mean(vc * vc, axis=-1, keepdims=True) + EPS) * g_ref[...] + b_ref[...]
    if write_v:
        out_refs[1][...] = vn
    vn_bf = vn.astype(BF16)
    row = lax.broadcasted_iota(jnp.int32, (SGU_CHUNK, SGU_CHUNK), 0)
    col = lax.broadcasted_iota(jnp.int32, (SGU_CHUNK, SGU_CHUNK), 1)
    keep = ((row // chunk_len) == (col // chunk_len)) & ((col % chunk_len) <= (row % chunk_len))
    for gi in range(SGU_GROUPS):
        w = jnp.where(keep, w_ref[gi], 0.0).astype(BF16)
        bias = bs_ref[gi]
        for c in range(bt // SGU_CHUNK):
            rows = slice(c * SGU_CHUNK, (c + 1) * SGU_CHUNK)
            cols = slice(gi * gd, (gi + 1) * gd)
            mixed = jnp.dot(w, vn_bf[rows, cols], preferred_element_type=F32) + bias
            out_refs[0][rows, cols] = (u_ref[rows, cols].astype(F32) * mixed).astype(BF16)


def sgu_gate(z, ln_g, ln_b, w_s, b_s, chunk_len, write_v):
    m, two_w = z.shape
    width = two_w // 2
    reps = SGU_CHUNK // chunk_len
    w_t = jnp.tile(w_s[:, :chunk_len, :chunk_len], (1, reps, reps))
    b_t = jnp.tile(b_s[:, :chunk_len], (1, reps))[:, :, None]
    bt = _pick(m, 256)
    row_spec = lambda c: pl.BlockSpec((bt, width), lambda i, c=c: (i, c))
    vec_spec = pl.BlockSpec((1, width), lambda i: (0, 0))
    out_specs = [row_spec(0)]
    out_shape = [jax.ShapeDtypeStruct((m, width), BF16)]
    if write_v:
        out_specs.append(row_spec(0))
        out_shape.append(jax.ShapeDtypeStruct((m, width), F32))
    return pl.pallas_call(
        functools.partial(_sgu_gate_kernel, chunk_len=chunk_len, write_v=write_v),
        grid=(m // bt,),
        in_specs=[row_spec(0), row_spec(1), vec_spec, vec_spec,
                  pl.BlockSpec((SGU_GROUPS, SGU_CHUNK, SGU_CHUNK), lambda i: (0, 0, 0)),
                  pl.BlockSpec((SGU_GROUPS, SGU_CHUNK, 1), lambda i: (0, 0, 0))],
        out_specs=out_specs, out_shape=out_shape,
        compiler_params=_params(1), name="sgu_gate",
    )(z, z, ln_g.reshape(1, width), ln_b.reshape(1, width), w_t, b_t)


def _swiglu(x, wg, wu, wd):
    g = jnp.dot(x, wg, preferred_element_type=F32)
    u = jnp.dot(x, wu, preferred_element_type=F32)
    h = (g * jax.nn.sigmoid(g) * u).astype(BF16)
    return jnp.dot(h, wd, preferred_element_type=F32)


def _moe_kernel(blk_e_ref, n_used_ref, x_ref, wg_ref, wu_ref, wd_ref, *refs, blk0):
    o_ref, wg_bf, wu_bf, wd_bf = refs[-4:]
    b = pl.program_id(0)
    e = blk_e_ref[blk0 + b]
    e_prev = blk_e_ref[blk0 + jnp.maximum(b - 1, 0)]

    @pl.when((b == 0) | (e != e_prev))
    def _():
        wg_bf[...] = wg_ref[...].astype(BF16)
        wu_bf[...] = wu_ref[...].astype(BF16)
        wd_bf[...] = wd_ref[...].astype(BF16)

    @pl.when(blk0 + b < n_used_ref[0])
    def _():
        o_ref[...] = _swiglu(x_ref[...], wg_bf[...], wu_bf[...], wd_bf[...]).astype(o_ref.dtype)

    @pl.when(blk0 + b >= n_used_ref[0])
    def _():
        o_ref[...] = jnp.zeros_like(o_ref)


def moe_experts(xg, blk_expert, n_used, layer, w_gate, w_up, w_down, out, blk0):
    rows, d = xg.shape
    ed = w_gate.shape[3]
    x_spec = pl.BlockSpec((MOE_BLOCK, d), lambda b, be, nu: (b, 0))
    w_idx = lambda b, be, nu: (layer, be[blk0 + b], 0, 0)
    in_specs = [x_spec, pl.BlockSpec((None, None, d, ed), w_idx), pl.BlockSpec((None, None, d, ed), w_idx),
                pl.BlockSpec((None, None, ed, d), w_idx)]
    args = [blk_expert, n_used, xg, w_gate, w_up, w_down]
    aliases = {}
    if out is not None:
        in_specs.append(pl.BlockSpec(memory_space=pl.ANY))
        args.append(out)
        aliases = {len(args) - 1: 0}
    return pl.pallas_call(
        functools.partial(_moe_kernel, blk0=blk0),
        grid_spec=pltpu.PrefetchScalarGridSpec(
            num_scalar_prefetch=2, grid=(rows // MOE_BLOCK,),
            in_specs=in_specs,
            out_specs=pl.BlockSpec((MOE_BLOCK, d), lambda b, be, nu: (blk0 + b, 0)),
            scratch_shapes=[pltpu.VMEM((d, ed), BF16), pltpu.VMEM((d, ed), BF16), pltpu.VMEM((ed, d), BF16)]),
        out_shape=jax.ShapeDtypeStruct((blk_expert.shape[0] * MOE_BLOCK, d), BF16),
        input_output_aliases=aliases,
        compiler_params=_params(1), name="moe_experts",
    )(*args)


def _shared_kernel(h_ref, wg_ref, wu_ref, wd_ref, o_ref):
    o_ref[...] = _swiglu(h_ref[...], wg_ref[...], wu_ref[...], wd_ref[...]).astype(o_ref.dtype)


def shared_expert(h, sw_gate, sw_up, sw_down):
    m, d = h.shape
    ed = sw_gate.shape[1]
    bm = _pick(m, 512)
    row_spec = pl.BlockSpec((bm, d), lambda i: (i, 0))
    full = lambda a, b: pl.BlockSpec((a, b), lambda i: (0, 0))
    return pl.pallas_call(
        _shared_kernel, grid=(m // bm,),
        in_specs=[row_spec, full(d, ed), full(d, ed), full(ed, d)],
        out_specs=row_spec,
        out_shape=jax.ShapeDtypeStruct((m, d), BF16),
        compiler_params=_params(1), name="shared_expert",
    )(h, sw_gate, sw_up, sw_down)


def _combine_kernel(s_ref, p_ref, w_ref, x_ref, gate_ref, o_ref):
    ffn = s_ref[...].astype(F32)
    w = w_ref[...]
    for k in range(TOP_K):
        ffn = ffn + w[:, k:k + 1] * p_ref[k].astype(F32)
    o_ref[...] = x_ref[...] + gate_ref[...] * ffn


def moe_combine_residual(stream, shared, picked, wts, row0, x, gate, tok0):
    m, d = x.shape
    n = picked.shape[1]
    bm = _row_block(stream, n, 256)
    assert row0 % bm == 0 and tok0 % bm == 0 and n % bm == 0
    off, off_all = tok0 // bm, (row0 + tok0) // bm
    row_spec = pl.BlockSpec((bm, d), lambda i: (i + off, 0))
    all_rows = lambda width: pl.BlockSpec((bm, width), lambda i: (i + off_all, 0))
    return pl.pallas_call(
        _combine_kernel, grid=(n // bm,),
        in_specs=[all_rows(d), pl.BlockSpec((TOP_K, bm, d), lambda i: (0, i, 0)), all_rows(TOP_K), row_spec,
                  _mod_spec(stream, bm, d, lambda i: (i + off, 0))],
        out_specs=row_spec,
        out_shape=jax.ShapeDtypeStruct((m, d), F32),
        input_output_aliases={3: 0},
        compiler_params=_params(1), name="moe_combine",
    )(shared, picked, wts, x, gate)


def _dest_kernel(start_ref, idx_ref, rank_ref, dest_ref):
    idx = idx_ref[...]

    def body(e, acc):
        return jnp.where(idx == e, start_ref[e], acc)

    dest_ref[...] = rank_ref[...] + lax.fori_loop(0, start_ref.shape[0], body, jnp.zeros_like(idx))


def assignment_rows(pad_start, idx, rank):
    return pl.pallas_call(
        _dest_kernel,
        in_specs=[pl.BlockSpec(memory_space=pltpu.SMEM), pl.BlockSpec(memory_space=pltpu.VMEM),
                  pl.BlockSpec(memory_space=pltpu.VMEM)],
        out_specs=pl.BlockSpec(memory_space=pltpu.VMEM),
        out_shape=jax.ShapeDtypeStruct(idx.shape, jnp.int32),
        name="assignment_rows",
    )(pad_start, idx, rank)


def moe_routed(h_all, idx, rank, counts, layer, w_gate, w_up, w_down):
    t, d = h_all.shape
    n_exp = counts.shape[0]
    n_assign = t * TOP_K
    counts = counts.reshape(n_exp).astype(jnp.int32)
    padded = (counts + MOE_BLOCK - 1) // MOE_BLOCK * MOE_BLOCK
    pad_end = jnp.cumsum(padded)
    pad_start = pad_end - padded
    dest = assignment_rows(pad_start, idx, rank).reshape(-1)
    n_blocks = -(-(n_assign + n_exp * (MOE_BLOCK - 1)) // MOE_BLOCK)
    n_blocks = -(-n_blocks // MOE_CHUNKS) * MOE_CHUNKS
    n_rows = n_blocks * MOE_BLOCK
    tok = jnp.tile(jnp.arange(t, dtype=jnp.int32), TOP_K)
    src_tok = jnp.arange(n_rows, dtype=jnp.int32) % t
    src_tok = src_tok.at[dest].add(tok - dest % t, unique_indices=True, mode="promise_in_bounds")
    blk_start = jnp.arange(n_blocks, dtype=jnp.int32) * MOE_BLOCK
    blk_expert = jnp.minimum(jnp.sum((pad_end[None, :] <= blk_start[:, None]).astype(jnp.int32), axis=1), n_exp - 1)
    n_used = pad_end[-1:] // MOE_BLOCK
    out = None
    chunk_rows = n_rows // MOE_CHUNKS
    for c in range(MOE_CHUNKS):
        xg = h_all.at[src_tok[c * chunk_rows:(c + 1) * chunk_rows]].get(mode="promise_in_bounds")
        out = moe_experts(xg, blk_expert, n_used, layer, w_gate, w_up, w_down, out, c * (n_blocks // MOE_CHUNKS))
    return out, dest.reshape(TOP_K, t)


def _stream(rows_per_batch):
    return dict(per_row=rows_per_batch < 1024, rows_per_batch=rows_per_batch)


def _mod_rows(stream, mod):
    b, d = mod.shape
    if not stream["per_row"]:
        return mod.reshape(b, 1, d)
    return jnp.repeat(mod, stream["rows_per_batch"], axis=0).reshape(1, -1, d)


def kernel(x_prompt, x_sample, cache_k_attn, cache_v_attn, c_prompt, c_sample, norm_mix_g, norm_ffn_g, ada_w, ada_b, attn_w_qkv, attn_w_o, attn_q_norm, attn_k_norm, attn_lambda_q1, attn_lambda_k1, attn_lambda_q2, attn_lambda_k2, attn_subln_g, sgu_w_in, sgu_ln_g, sgu_ln_b, sgu_w_s, sgu_b_s, sgu_w_o, moe_w_router, moe_b_router, moe_w_gate, moe_w_up, moe_w_down, shared_w_gate, shared_w_up, shared_w_down):
    bp, sp, d = x_prompt.shape
    bs, ss, _ = x_sample.shape
    depth = ada_w.shape[0]
    dv = 2 * ATTN_HEAD_DIM
    heads = d // dv
    mp, ms = bp * sp, bs * ss
    streams = (_stream(sp), _stream(ss))
    xs = [x_prompt.reshape(mp, d), x_sample.reshape(ms, d)]

    mod = ada_modulation(jnp.concatenate([c_prompt, c_sample], axis=0), ada_w, ada_b)
    slopes2 = (2.0 ** (-8.0 * jnp.arange(1, heads + 1, dtype=F32) / heads)) * LOG2E

    k_out, v_out, sgu_out = [[], []], [[], []], []
    for i in range(depth):
        j = i // N_MIXERS
        mods = []
        for si, (stream, rows) in enumerate(zip(streams, (slice(0, bp), slice(bp, bp + bs)))):
            mods.append([_mod_rows(stream, mod[i, rows, c * d:(c + 1) * d]) for c in range(6)])

        if i % N_MIXERS == 0:
            lam_init = 0.8 - 0.6 * math.exp(-0.3 * i)
            f = lambda a: a[j].astype(F32)
            lam = (jnp.exp(jnp.sum(f(attn_lambda_q1) * f(attn_lambda_k1)))
                   - jnp.exp(jnp.sum(f(attn_lambda_q2) * f(attn_lambda_k2))) + lam_init).reshape(1)
            for si, stream in enumerate(streams):
                h = modulate(stream, xs[si], norm_mix_g[i], mods[si][0], mods[si][1])
                q_bf, k32, k_bf, v32, v_bf = qkv_project(stream, h, attn_w_qkv[j], attn_q_norm[j], attn_k_norm[j])
                if si == 0:
                    o = attention_prompt(q_bf, k_bf, v_bf, bp, lam, slopes2, attn_subln_g[j], 1.0 - lam_init)
                else:
                    o = attention_sample(q_bf, k_bf, v_bf, cache_k_attn, cache_v_attn, j, lam, slopes2,
                                         attn_subln_g[j], 1.0 - lam_init)
                xs[si] = project_residual(stream, o, attn_w_o[j], xs[si], mods[si][2], "attn_out_proj")
                k_out[si].append(k32)
                v_out[si].append(v32)
        else:
            for si, stream in enumerate(streams):
                h = modulate(stream, xs[si], norm_mix_g[i], mods[si][0], mods[si][1])
                (z,) = matmul(h, sgu_w_in[j], bm=_row_block(stream, h.shape[0], 1024),
                              bn=_pick(sgu_w_in.shape[2], 1024),
                              out_dtypes=(BF16,), epilogue=_gelu_epilogue, name="sgu_in_proj")
                chunk_len = SGU_CHUNK if si == 0 else ss
                res = sgu_gate(z, sgu_ln_g[j], sgu_ln_b[j], sgu_w_s[j], sgu_b_s[j], chunk_len, write_v=(si == 1))
                if si == 1:
                    sgu_out.append(res[1])
                xs[si] = project_residual(stream, res[0], sgu_w_o[j], xs[si], mods[si][2], "sgu_out_proj")

        hs, routing = [], []
        counts = jnp.zeros((moe_w_router.shape[2], 1), F32)
        for si, stream in enumerate(streams):
            hf, idx, wts, rank, counts = modulate(stream, xs[si], norm_ffn_g[i], mods[si][3], mods[si][4],
                                                  router=(moe_w_router[i], moe_b_router[i], counts))
            hs.append(hf)
            routing.append((idx, wts, rank))
        idx, wts, rank = (jnp.concatenate(parts, axis=1) for parts in zip(*routing))
        h_all = jnp.concatenate(hs, axis=0)
        shared = shared_expert(h_all, *[w[i].astype(BF16) for w in (shared_w_gate, shared_w_up, shared_w_down)])
        out, dest = moe_routed(h_all, idx, rank, counts, i, moe_w_gate, moe_w_up, moe_w_down)
        for si, (stream, row0, m_rows) in enumerate(zip(streams, (0, mp), (mp, ms))):
            piece = _pick(m_rows, COMBINE_TOKENS)
            for tok0 in range(0, m_rows, piece):
                rows = dest[:, row0 + tok0:row0 + tok0 + piece].reshape(-1)
                picked = out.at[rows].get(mode="promise_in_bounds", unique_indices=True).reshape(TOP_K, piece, d)
                xs[si] = moe_combine_residual(stream, shared, picked, wts.T, row0, xs[si], mods[si][5], tok0)

    n_attn = len(k_out[0])
    k_prompt = jnp.stack(k_out[0]).reshape(n_attn, bp, sp, heads, 2, ATTN_HEAD_DIM)
    v_prompt = jnp.stack(v_out[0]).reshape(n_attn, bp, sp, heads, dv)
    k_sample = jnp.stack(k_out[1]).reshape(n_attn, bs, ss, heads, 2, ATTN_HEAD_DIM)
    v_sample = jnp.stack(v_out[1]).reshape(n_attn, bs, ss, heads, dv)
    sgu_v = jnp.stack(sgu_out).reshape(len(sgu_out), bs, ss, -1)
    return (xs[0].reshape(bp, sp, d), xs[1].reshape(bs, ss, d), k_prompt, v_prompt, k_sample, v_sample, sgu_v)
```

```python
import functools
import math

import jax
import jax.numpy as jnp
from jax import lax
from jax.experimental import pallas as pl
from jax.experimental.pallas import tpu as pltpu

F32 = jnp.float32
BF16 = jnp.bfloat16

EPS = 1e-6
CHUNK = 64
N_MIXERS = 2
ATTN_HEAD_DIM = 128
SGU_GROUPS = 8
SGU_CHUNK = 128
TOP_K = 8
N_EXPERT_GROUPS = 8
TOPK_GROUPS = 4
ROUTED_SCALE = 2.5
LOG2E = 1.4426950408889634
NEG_BIG = -1e30

VMEM_LIMIT = 56 * 1024 * 1024
MOE_BLOCK = 512
MOE_CHUNKS = 4
COMBINE_TOKENS = 4096
ATTN_ROW_CHUNK = 64
LANES = 128


def _params(n_axes):
    return pltpu.CompilerParams(dimension_semantics=("arbitrary",) * n_axes, vmem_limit_bytes=VMEM_LIMIT)


def _pick(n, pref):
    if n <= pref:
        return n
    b = pref
    while n % b:
        b //= 2
    return b


def _ada_kernel(c_ref, w_ref, b_ref, o_ref):
    c = c_ref[...]
    x = (c * jax.nn.sigmoid(c)).astype(BF16)
    o_ref[...] = jnp.dot(x, w_ref[...].astype(BF16), preferred_element_type=F32) + b_ref[...]


def ada_modulation(c, ada_w, ada_b):
    n_layers, d, n = ada_w.shape
    r = c.shape[0]
    bn = _pick(n, 1024)
    return pl.pallas_call(
        _ada_kernel,
        grid=(n_layers, n // bn),
        in_specs=[
            pl.BlockSpec((r, d), lambda l, j: (0, 0)),
            pl.BlockSpec((None, d, bn), lambda l, j: (l, 0, j)),
            pl.BlockSpec((None, 1, bn), lambda l, j: (l, 0, j)),
        ],
        out_specs=pl.BlockSpec((None, r, bn), lambda l, j: (l, 0, j)),
        out_shape=jax.ShapeDtypeStruct((n_layers, r, n), F32),
        compiler_params=_params(2),
        name="ada_modulation",
    )(c, ada_w, ada_b.reshape(n_layers, 1, n))


def _modulated(x, g, shift, scale):
    y = x * lax.rsqrt(jnp.mean(x * x, axis=-1, keepdims=True) + EPS)
    return y * g * (1.0 + scale) + shift


def _modulate_kernel(x_ref, g_ref, sh_ref, sc_ref, o_ref):
    o_ref[...] = _modulated(x_ref[...], g_ref[...], sh_ref[...], sc_ref[...]).astype(o_ref.dtype)


def _split3(x):
    hi = x.astype(BF16)
    lo = (x - hi.astype(F32)).astype(BF16)
    return hi, lo


def _first_max(x, ids, n_ids, axes):
    mx = x
    for ax in axes:
        mx = jnp.max(mx, axis=ax, keepdims=True)
    arg = jnp.where(x == mx, ids, n_ids)
    for ax in axes:
        arg = jnp.min(arg, axis=ax, keepdims=True)
    return mx, arg


def _sum_axes(x, axes):
    for ax in axes:
        x = jnp.sum(x, axis=ax, keepdims=True)
    return x


def _modulate_router_kernel(x_ref, g_ref, sh_ref, sc_ref, wrt_ref, br_ref, cin_ref,
                            o_ref, idx_ref, wts_ref, rank_ref, cnt_ref, carry):
    h = _modulated(x_ref[...], g_ref[...], sh_ref[...], sc_ref[...])
    o_ref[...] = h.astype(o_ref.dtype)
    bm = h.shape[0]
    n_exp = wrt_ref.shape[0]
    per = n_exp // N_EXPERT_GROUPS
    grp_shape = (N_EXPERT_GROUPS, per, bm)

    h_hi, h_lo = _split3(h)
    w_hi, w_lo = _split3(wrt_ref[...])
    nt = (((1,), (1,)), ((), ()))
    logits = (lax.dot_general(w_hi, h_hi, nt, preferred_element_type=F32)
              + lax.dot_general(w_lo, h_hi, nt, preferred_element_type=F32)
              + lax.dot_general(w_hi, h_lo, nt, preferred_element_type=F32))
    s = jax.nn.sigmoid(logits)
    s3 = s.reshape(grp_shape)
    sb3 = (s + br_ref[...]).reshape(grp_shape)

    sub = lax.broadcasted_iota(jnp.int32, grp_shape, 1)
    gid = lax.broadcasted_iota(jnp.int32, (N_EXPERT_GROUPS, 1, bm), 0)
    eid = lax.broadcasted_iota(jnp.int32, grp_shape, 0) * per + sub

    m1, i1 = _first_max(sb3, sub, per, (1,))
    m2 = jnp.max(jnp.where(sub == i1, -jnp.inf, sb3), axis=1, keepdims=True)
    work = m1 + m2
    chosen = jnp.zeros_like(work)
    for _ in range(TOPK_GROUPS):
        _, gi = _first_max(work, gid, N_EXPERT_GROUPS, (0,))
        chosen = jnp.where(gid == gi, 1.0, chosen)
        work = jnp.where(gid == gi, -jnp.inf, work)
    sel = jnp.where(chosen > 0.0, sb3, -jnp.inf)

    hits, ids, raw = [], [], []
    for _ in range(TOP_K):
        _, ei = _first_max(sel, eid, n_exp, (1, 0))
        hit = eid == ei
        raw.append(_sum_axes(jnp.where(hit, s3, 0.0), (1, 0)))
        sel = jnp.where(hit, -jnp.inf, sel)
        hits.append(hit)
        ids.append(ei)
    total = sum(raw)

    member = sum(jnp.where(hit, 1.0, 0.0) for hit in hits).reshape(n_exp, bm)
    r_i = lax.broadcasted_iota(jnp.int32, (bm, bm), 0)
    c_i = lax.broadcasted_iota(jnp.int32, (bm, bm), 1)
    before = jnp.where(r_i < c_i, 1.0, 0.0).astype(BF16)

    @pl.when(pl.program_id(0) == 0)
    def _():
        carry[...] = cin_ref[...]

    rank_all = (jnp.dot(member.astype(BF16), before, preferred_element_type=F32) + carry[...]).reshape(grp_shape)
    carry[...] += jnp.sum(member, axis=1, keepdims=True)
    cnt_ref[...] = carry[...]
    for k in range(TOP_K):
        idx_ref[k:k + 1, :] = ids[k].reshape(1, bm)
        wts_ref[k:k + 1, :] = (raw[k] / total * ROUTED_SCALE).reshape(1, bm)
        rank_ref[k:k + 1, :] = _sum_axes(jnp.where(hits[k], rank_all, 0.0), (1, 0)).reshape(1, bm).astype(jnp.int32)


def _row_block(stream, m, pref):
    return _pick(m if stream["per_row"] else stream["rows_per_batch"], pref)


def _mod_spec(stream, bm, bn, ij):
    if stream["per_row"]:
        return pl.BlockSpec((None, bm, bn), lambda *g: (0,) + tuple(ij(*g)))
    bpg = stream["rows_per_batch"] // bm
    return pl.BlockSpec((None, 1, bn), lambda *g: (ij(*g)[0] // bpg, 0, ij(*g)[1]))


def modulate(stream, x, g, shift, scale, router=None):
    m, d = x.shape
    bm = _row_block(stream, m, 512)
    row_spec = pl.BlockSpec((bm, d), lambda i: (i, 0))
    mod_spec = _mod_spec(stream, bm, d, lambda i: (i, 0))
    g_spec = pl.BlockSpec((1, d), lambda i: (0, 0))
    if router is None:
        return pl.pallas_call(
            _modulate_kernel, grid=(m // bm,),
            in_specs=[row_spec, g_spec, mod_spec, mod_spec], out_specs=row_spec,
            out_shape=jax.ShapeDtypeStruct((m, d), BF16),
            compiler_params=_params(1), name="modulate",
        )(x, g.reshape(1, d), shift, scale)
    w_router, b_router, counts_in = router
    e = w_router.shape[1]
    tok_spec = pl.BlockSpec((TOP_K, bm), lambda i: (0, i))
    cnt_spec = pl.BlockSpec((e, 1), lambda i: (0, 0))
    return pl.pallas_call(
        _modulate_router_kernel, grid=(m // bm,),
        in_specs=[row_spec, g_spec, mod_spec, mod_spec, pl.BlockSpec((e, d), lambda i: (0, 0)), cnt_spec, cnt_spec],
        out_specs=[row_spec, tok_spec, tok_spec, tok_spec, cnt_spec],
        out_shape=[jax.ShapeDtypeStruct((m, d), BF16), jax.ShapeDtypeStruct((TOP_K, m), jnp.int32),
                   jax.ShapeDtypeStruct((TOP_K, m), F32), jax.ShapeDtypeStruct((TOP_K, m), jnp.int32),
                   jax.ShapeDtypeStruct((e, 1), F32)],
        scratch_shapes=[pltpu.VMEM((e, 1), F32)],
        compiler_params=_params(1), name="modulate_router",
    )(x, g.reshape(1, d), shift, scale, w_router.T, b_router.astype(F32).reshape(e, 1), counts_in)


def _mm_kernel(x_ref, w_ref, *refs, n_extra, epilogue):
    extra, outs, w_bf = refs[:n_extra], refs[n_extra:-1], refs[-1]

    @pl.when(pl.program_id(1) == 0)
    def _():
        w_bf[...] = w_ref[...].astype(BF16)

    acc = jnp.dot(x_ref[...], w_bf[...], preferred_element_type=F32)
    epilogue(acc, extra, outs)


def matmul(x, w, *, bm, bn, col_block_off=0, n_cols=None, extra=(), extra_specs=(), out_dtypes, epilogue, name):
    m, k = x.shape
    n = w.shape[1] if n_cols is None else n_cols
    grid = (n // bn, m // bm)
    out_spec = pl.BlockSpec((bm, bn), lambda j, i: (i, j))
    return pl.pallas_call(
        functools.partial(_mm_kernel, n_extra=len(extra), epilogue=epilogue),
        grid=grid,
        in_specs=[pl.BlockSpec((bm, k), lambda j, i: (i, 0)),
                  pl.BlockSpec((k, bn), lambda j, i: (0, j + col_block_off))] + list(extra_specs),
        out_specs=[out_spec] * len(out_dtypes),
        out_shape=[jax.ShapeDtypeStruct((m, n), dt) for dt in out_dtypes],
        scratch_shapes=[pltpu.VMEM((k, bn), BF16)],
        compiler_params=_params(2), name=name,
    )(x, w, *extra)


def _head_rms(acc, gain, post_scale):
    pieces = []
    for c in range(acc.shape[1] // ATTN_HEAD_DIM):
        seg = acc[:, c * ATTN_HEAD_DIM:(c + 1) * ATTN_HEAD_DIM]
        y = seg * lax.rsqrt(jnp.mean(seg * seg, axis=-1, keepdims=True) + EPS)
        pieces.append(y * (gain * post_scale))
    return jnp.concatenate(pieces, axis=-1)


def _q_epilogue(acc, extra, outs):
    outs[0][...] = _head_rms(acc, extra[0][...], ATTN_HEAD_DIM ** -0.5 * LOG2E).astype(BF16)


def _v_epilogue(acc, extra, outs):
    outs[0][...] = acc
    outs[1][...] = acc.astype(BF16)


def _gelu_epilogue(acc, extra, outs):
    outs[0][...] = (0.5 * acc * (1.0 + lax.erf(acc * (2.0 ** -0.5)))).astype(BF16)


def _residual_epilogue(acc, extra, outs):
    x_ref, gate_ref = extra
    outs[0][...] = x_ref[...] + gate_ref[...] * acc


def _k_proj_kernel(h_ref, w_ref, g_ref, k32_ref, kbf_ref):
    acc = jnp.dot(h_ref[...], w_ref[...], preferred_element_type=F32)
    bm = acc.shape[0]
    groups = acc.shape[1] // ATTN_HEAD_DIM
    gain = g_ref[...]
    for g in range(groups):
        cols = slice(g * ATTN_HEAD_DIM, (g + 1) * ATTN_HEAD_DIM)
        seg = acc[:, cols]
        kn = seg * lax.rsqrt(jnp.mean(seg * seg, axis=-1, keepdims=True) + EPS) * gain
        k32_ref[pl.ds(g, bm, stride=groups), :] = kn
        kbf_ref[:, cols] = kn.astype(BF16)


def k_project(stream, h, w_k, k_norm):
    m, d = h.shape
    n = w_k.shape[1]
    groups = n // ATTN_HEAD_DIM
    bm = _row_block(stream, m, 512)
    return pl.pallas_call(
        _k_proj_kernel, grid=(m // bm,),
        in_specs=[pl.BlockSpec((bm, d), lambda i: (i, 0)), pl.BlockSpec((d, n), lambda i: (0, 0)),
                  pl.BlockSpec((1, ATTN_HEAD_DIM), lambda i: (0, 0))],
        out_specs=[pl.BlockSpec((bm * groups, ATTN_HEAD_DIM), lambda i: (i, 0)),
                   pl.BlockSpec((bm, n), lambda i: (i, 0))],
        out_shape=[jax.ShapeDtypeStruct((m * groups, ATTN_HEAD_DIM), F32), jax.ShapeDtypeStruct((m, n), BF16)],
        compiler_params=_params(1), name="k_proj",
    )(h, w_k, k_norm.reshape(1, -1))


def qkv_project(stream, h, w_qkv, q_norm, k_norm):
    m, d = h.shape
    bm = _row_block(stream, m, 1024)
    bn = _pick(d, 1024)
    nb = d // bn
    gain_spec = pl.BlockSpec((1, ATTN_HEAD_DIM), lambda j, i: (0, 0))
    common = dict(bm=bm, bn=bn, n_cols=d)
    (q_bf,) = matmul(h, w_qkv, col_block_off=0, extra=(q_norm.reshape(1, -1),), extra_specs=(gain_spec,),
                     out_dtypes=(BF16,), epilogue=_q_epilogue, name="q_proj", **common)
    k32, k_bf = k_project(stream, h, w_qkv[:, d:2 * d].astype(BF16), k_norm)
    v32, v_bf = matmul(h, w_qkv, col_block_off=2 * nb, out_dtypes=(F32, BF16), epilogue=_v_epilogue,
                       name="v_proj", **common)
    return q_bf, k32, k_bf, v32, v_bf


def project_residual(stream, a, w, x, gate, name):
    k, n = w.shape
    deep = k > 4096
    bm = _row_block(stream, a.shape[0], 512 if deep else 1024)
    bn = _pick(n, 512 if deep else 1024)
    (out,) = matmul(a, w, bm=bm, bn=bn, extra=(x, gate),
                    extra_specs=(pl.BlockSpec((bm, bn), lambda j, i: (i, j)),
                                 _mod_spec(stream, bm, bn, lambda j, i: (i, j))),
                    out_dtypes=(F32,), epilogue=_residual_epilogue, name=name)
    return out


def _finish_heads(acc1, l1, acc2, l2, lam, g, post_scale):
    o = acc1 / l1 - lam * (acc2 / l2)
    o = o * lax.rsqrt(jnp.mean(o * o, axis=-1, keepdims=True) + EPS)
    return o * (g * post_scale)


def _attn_prompt_kernel(lam_ref, slope_ref, q_ref, k_ref, v_ref, g_ref, o_ref,
                        s_scr, p_scr, qa_scr, acc_scr, m_scr, l_scr, a_scr, *, bq, post_scale):
    hd = ATTN_HEAD_DIM
    rc = ATTN_ROW_CHUNK
    qi = pl.program_id(2)
    hh = pl.program_id(1)
    sl2 = slope_ref[hh]

    m_scr[...] = jnp.full(m_scr.shape, NEG_BIG, F32)
    l_scr[...] = jnp.zeros(l_scr.shape, F32)
    acc_scr[...] = jnp.zeros(acc_scr.shape, F32)

    lane = lax.broadcasted_iota(jnp.int32, (bq, LANES), 1)
    key_c = lax.broadcasted_iota(jnp.int32, (bq, LANES), 0)
    k_aug = jnp.where(lane < 3, key_c % 256, jnp.where(lane < 6, key_c // 256, 0)).astype(F32).astype(BF16)
    rest = jnp.full((bq, LANES), sl2, F32)
    piece = jnp.zeros((bq, LANES), F32)
    for i in range(3):
        part = rest.astype(BF16).astype(F32)
        piece = jnp.where(lane % 3 == i, part, piece)
        rest = rest - part
    q_aug = jnp.where(lane < 3, piece, jnp.where(lane < 6, 256.0 * piece, 0.0)).astype(BF16)
    for mi in range(2):
        qa_scr[mi, :, :hd] = q_ref[:, mi * hd:(mi + 1) * hd]
        qa_scr[mi, :, hd:] = q_aug

    def scores(kb, slot):
        k0 = pl.multiple_of(kb * bq, bq)
        for mi in range(2):
            k_blk = jnp.concatenate([k_ref[pl.ds(k0, bq), mi * hd:(mi + 1) * hd], k_aug], axis=1)
            s_scr[2 * slot + mi] = lax.dot_general(qa_scr[mi], k_blk, (((1,), (1,)), ((), ())),
                                                   preferred_element_type=F32)

    def softmax(kb, slot, diag):
        shift_blk = sl2 * ((kb - qi) * bq + jnp.zeros((1, LANES), jnp.int32)).astype(F32)
        for mi in range(2):
            s_map = s_scr.at[2 * slot + mi]
            p_map = p_scr.at[2 * slot + mi]
            for c in range(bq // rc):
                rows = slice(c * rc, (c + 1) * rc)
                ncol = min(bq, -(-((c + 1) * rc) // CHUNK) * CHUNK) if diag else bq
                lane_tiles = [slice(t * LANES, (t + 1) * LANES) for t in range(-(-ncol // LANES))]

                def tile(lanes, t):
                    s_t = s_map[rows, lanes]
                    if not diag:
                        return s_t
                    r_i = lax.broadcasted_iota(jnp.int32, (rc, LANES), 0) + c * rc
                    c_i = lax.broadcasted_iota(jnp.int32, (rc, LANES), 1) + t * LANES
                    s_t = s_t + (2.0 * sl2) * jnp.minimum(r_i - c_i, 0).astype(F32)
                    return jnp.where((c_i // CHUNK) <= (r_i // CHUNK), s_t, NEG_BIG)

                tiles = [tile(lanes, t) for t, lanes in enumerate(lane_tiles)]
                m_old = m_scr[mi, rows, :]
                blk_max = jnp.max(functools.reduce(jnp.maximum, tiles), axis=-1, keepdims=True)
                m_new = jnp.maximum(m_old, blk_max + shift_blk)
                alpha = jnp.exp2(m_old - m_new)
                m_scr[mi, rows, :] = m_new
                a_scr[2 * slot + mi, rows, :] = alpha
                if not diag:
                    tiles = [tile(lanes, t) for t, lanes in enumerate(lane_tiles)]
                off = m_new - shift_blk
                ps = [jnp.exp2(s_t - off) for s_t in tiles]
                l_scr[mi, rows, :] = alpha * l_scr[mi, rows, :] + functools.reduce(jnp.add, ps)
                for t, p_t in enumerate(ps):
                    p_map[rows, t * LANES:(t + 1) * LANES] = p_t.astype(BF16)
                if len(ps) * LANES < bq:
                    p_map[rows, len(ps) * LANES:] = jnp.zeros((rc, bq - len(ps) * LANES), BF16)

    def weighted_values(kb, slot):
        k0 = pl.multiple_of(jnp.maximum(kb, 0) * bq, bq)
        for mi in range(2):
            pv = jnp.dot(p_scr[2 * slot + mi], v_ref[pl.ds(k0, bq), :], preferred_element_type=F32)
            for t in range(pv.shape[1] // LANES):
                lanes = slice(t * LANES, (t + 1) * LANES)
                acc_scr[mi, :, lanes] = a_scr[2 * slot + mi] * acc_scr[mi, :, lanes] + pv[:, lanes]

    p_scr[2] = jnp.zeros((bq, bq), BF16)
    p_scr[3] = jnp.zeros((bq, bq), BF16)
    a_scr[2] = jnp.ones((bq, LANES), F32)
    a_scr[3] = jnp.ones((bq, LANES), F32)

    def pair(j2, carry):
        kb = 2 * j2
        scores(kb + 1, 1)
        weighted_values(kb - 1, 1)
        softmax(kb, 0, False)
        scores(kb + 2, 0)
        weighted_values(kb, 0)
        softmax(kb + 1, 1, False)
        return carry

    scores(0, 0)
    lax.fori_loop(0, qi // 2, pair, 0)

    @pl.when(qi % 2 == 1)
    def _():
        scores(qi, 1)
        weighted_values(qi - 2, 1)
        softmax(qi - 1, 0, False)
        weighted_values(qi - 1, 0)
        softmax(qi, 1, True)
        weighted_values(qi, 1)

    @pl.when(qi % 2 == 0)
    def _():
        weighted_values(qi - 1, 1)
        softmax(qi, 0, True)
        weighted_values(qi, 0)
    l1 = jnp.sum(l_scr[0], axis=-1, keepdims=True)
    l2 = jnp.sum(l_scr[1], axis=-1, keepdims=True)
    o_ref[...] = _finish_heads(acc_scr[0], l1, acc_scr[1], l2, lam_ref[0], g_ref[...], post_scale).astype(o_ref.dtype)


def attention_prompt(q_bf, k_bf, v_bf, batch, lam, slopes2, subln_g, post_scale):
    m, width = q_bf.shape
    s = m // batch
    dv = 2 * ATTN_HEAD_DIM
    heads = width // dv
    bq = _pick(s, 512)
    assert bq % CHUNK == 0
    nq = s // bq
    kv_spec = pl.BlockSpec((s, dv), lambda b, h, i: (b, h))
    q_spec = pl.BlockSpec((bq, dv), lambda b, h, i: (b * nq + i, h))
    smem = pl.BlockSpec(memory_space=pltpu.SMEM)
    return pl.pallas_call(
        functools.partial(_attn_prompt_kernel, bq=bq, post_scale=post_scale),
        grid=(batch, heads, nq),
        in_specs=[smem, smem, q_spec, kv_spec, kv_spec, pl.BlockSpec((1, dv), lambda b, h, i: (0, 0))],
        out_specs=q_spec,
        out_shape=jax.ShapeDtypeStruct((m, width), BF16),
        scratch_shapes=[pltpu.VMEM((4, bq, bq), F32), pltpu.VMEM((4, bq, bq), BF16), pltpu.VMEM((2, bq, dv), BF16),
                        pltpu.VMEM((2, bq, dv), F32), pltpu.VMEM((2, bq, LANES), F32),
                        pltpu.VMEM((2, bq, LANES), F32), pltpu.VMEM((4, bq, LANES), F32)],
        compiler_params=_params(3), name="attention_prompt",
    )(lam, slopes2, q_bf, k_bf, v_bf, subln_g.reshape(1, dv))


def _attn_sample_kernel(lam_ref, slope_ref, q_ref, kn_ref, vn_ref, ck_ref, cv_ref, g_ref, o_ref, *, post_scale):
    hd = ATTN_HEAD_DIM
    dv = 2 * hd
    lam = lam_ref[0]
    t = q_ref.shape[0]
    heads = q_ref.shape[1] // dv
    past = ck_ref.shape[0] // (2 * heads)

    def dist_mask(n_keys, key0):
        q_pos = past + lax.broadcasted_iota(jnp.int32, (t, n_keys), 0)
        k_pos = key0 + lax.broadcasted_iota(jnp.int32, (t, n_keys), 1)
        return jnp.abs(q_pos - k_pos).astype(F32), (k_pos // CHUNK) <= (q_pos // CHUNK)

    dist_p, vis_p = dist_mask(past, 0)
    dist_n, vis_n = dist_mask(t, past)
    dn = (((1,), (1,)), ((), ()))
    for h in range(heads):
        cols = slice(h * dv, (h + 1) * dv)
        sl2 = slope_ref[h]
        v_past = jnp.concatenate(
            [cv_ref[pl.ds(c * heads + h, past, stride=heads * (dv // LANES)), :] for c in range(dv // LANES)],
            axis=1).astype(BF16)
        res = []
        for mi in range(2):
            qm = q_ref[:, h * dv + mi * hd:h * dv + (mi + 1) * hd]
            k_past = ck_ref[pl.ds(2 * h + mi, past, stride=2 * heads), :].astype(BF16)
            s_p = lax.dot_general(qm, k_past, dn, preferred_element_type=F32)
            s_n = lax.dot_general(qm, kn_ref[:, h * dv + mi * hd:h * dv + (mi + 1) * hd], dn,
                                  preferred_element_type=F32)
            s_p = jnp.where(vis_p, s_p - sl2 * dist_p, NEG_BIG)
            s_n = jnp.where(vis_n, s_n - sl2 * dist_n, NEG_BIG)
            m = jnp.maximum(jnp.max(s_p, axis=-1, keepdims=True), jnp.max(s_n, axis=-1, keepdims=True))
            p_p = jnp.exp2(s_p - m)
            p_n = jnp.exp2(s_n - m)
            l = jnp.sum(p_p, axis=-1, keepdims=True) + jnp.sum(p_n, axis=-1, keepdims=True)
            acc = (jnp.dot(p_p.astype(BF16), v_past, preferred_element_type=F32)
                   + jnp.dot(p_n.astype(BF16), vn_ref[:, cols], preferred_element_type=F32))
            res += [acc, l]
        o_ref[:, cols] = _finish_heads(res[0], res[1], res[2], res[3], lam, g_ref[...],
                                       post_scale).astype(o_ref.dtype)


def attention_sample(q_bf, k_bf, v_bf, cache_k, cache_v, layer, lam, slopes2, subln_g, post_scale):
    _, batch, past, heads, _, hd = cache_k.shape
    m, width = q_bf.shape
    t = m // batch
    dv = 2 * hd
    new_spec = pl.BlockSpec((t, width), lambda b: (b, 0))
    n_layers = cache_k.shape[0]
    cache_k = cache_k.reshape(n_layers, batch, past * heads * 2, hd)
    cache_v = cache_v.reshape(n_layers, batch, past, heads, dv // LANES, LANES)
    cache_v = cache_v.transpose(0, 1, 2, 4, 3, 5).reshape(n_layers, batch, past * (dv // LANES) * heads, LANES)
    ck_spec = pl.BlockSpec((None, None, past * heads * 2, hd), lambda b: (layer, b, 0, 0))
    cv_spec = pl.BlockSpec((None, None, past * (dv // LANES) * heads, LANES), lambda b: (layer, b, 0, 0))
    smem = pl.BlockSpec(memory_space=pltpu.SMEM)
    return pl.pallas_call(
        functools.partial(_attn_sample_kernel, post_scale=post_scale),
        grid=(batch,),
        in_specs=[smem, smem, new_spec, new_spec, new_spec, ck_spec, cv_spec,
                  pl.BlockSpec((1, dv), lambda b: (0, 0))],
        out_specs=new_spec,
        out_shape=jax.ShapeDtypeStruct((m, width), BF16),
        compiler_params=_params(1), name="attention_sample",
    )(lam, slopes2, q_bf, k_bf, v_bf, cache_k, cache_v, subln_g.reshape(1, dv))


def _sgu_gate_kernel(u_ref, v_ref, g_ref, b_ref, w_ref, bs_ref, *out_refs, chunk_len, write_v):
    bt, width = u_ref.shape
    gd = width // SGU_GROUPS
    v = v_ref[...].astype(F32)
    mu = jnp.mean(v, axis=-1, keepdims=True)
    vc = v - mu
    vn = vc * lax.rsqrt(jnp.mean(vc * vc, axis=-1, keepdims=True) + EPS) * g_ref[...] + b_ref[...]
    if write_v:
        out_refs[1][...] = vn
    vn_bf = vn.astype(BF16)
    row = lax.broadcasted_iota(jnp.int32, (SGU_CHUNK, SGU_CHUNK), 0)
    col = lax.broadcasted_iota(jnp.int32, (SGU_CHUNK, SGU_CHUNK), 1)
    keep = ((row // chunk_len) == (col // chunk_len)) & ((col % chunk_len) <= (row % chunk_len))
    for gi in range(SGU_GROUPS):
        w = jnp.where(keep, w_ref[gi], 0.0).astype(BF16)
        bias = bs_ref[gi]
        for c in range(bt // SGU_CHUNK):
            rows = slice(c * SGU_CHUNK, (c + 1) * SGU_CHUNK)
            cols = slice(gi * gd, (gi + 1) * gd)
            mixed = jnp.dot(w, vn_bf[rows, cols], preferred_element_type=F32) + bias
            out_refs[0][rows, cols] = (u_ref[rows, cols].astype(F32) * mixed).astype(BF16)


def sgu_gate(z, ln_g, ln_b, w_s, b_s, chunk_len, write_v):
    m, two_w = z.shape
    width = two_w // 2
    reps = SGU_CHUNK // chunk_len
    w_t = jnp.tile(w_s[:, :chunk_len, :chunk_len], (1, reps, reps))
    b_t = jnp.tile(b_s[:, :chunk_len], (1, reps))[:, :, None]
    bt = _pick(m, 256)
    row_spec = lambda c: pl.BlockSpec((bt, width), lambda i, c=c: (i, c))
    vec_spec = pl.BlockSpec((1, width), lambda i: (0, 0))
    out_specs = [row_spec(0)]
    out_shape = [jax.ShapeDtypeStruct((m, width), BF16)]
    if write_v:
        out_specs.append(row_spec(0))
        out_shape.append(jax.ShapeDtypeStruct((m, width), F32))
    return pl.pallas_call(
        functools.partial(_sgu_gate_kernel, chunk_len=chunk_len, write_v=write_v),
        grid=(m // bt,),
        in_specs=[row_spec(0), row_spec(1), vec_spec, vec_spec,
                  pl.BlockSpec((SGU_GROUPS, SGU_CHUNK, SGU_CHUNK), lambda i: (0, 0, 0)),
                  pl.BlockSpec((SGU_GROUPS, SGU_CHUNK, 1), lambda i: (0, 0, 0))],
        out_specs=out_specs, out_shape=out_shape,
        compiler_params=_params(1), name="sgu_gate",
    )(z, z, ln_g.reshape(1, width), ln_b.reshape(1, width), w_t, b_t)


def _swiglu(x, wg, wu, wd):
    g = jnp.dot(x, wg, preferred_element_type=F32)
    u = jnp.dot(x, wu, preferred_element_type=F32)
    h = (g * jax.nn.sigmoid(g) * u).astype(BF16)
    return jnp.dot(h, wd, preferred_element_type=F32)


def _moe_kernel(blk_e_ref, n_used_ref, x_ref, wg_ref, wu_ref, wd_ref, *refs, blk0):
    o_ref, wg_bf, wu_bf, wd_bf = refs[-4:]
    b = pl.program_id(0)
    e = blk_e_ref[blk0 + b]
    e_prev = blk_e_ref[blk0 + jnp.maximum(b - 1, 0)]

    @pl.when((b == 0) | (e != e_prev))
    def _():
        wg_bf[...] = wg_ref[...].astype(BF16)
        wu_bf[...] = wu_ref[...].astype(BF16)
        wd_bf[...] = wd_ref[...].astype(BF16)

    @pl.when(blk0 + b < n_used_ref[0])
    def _():
        o_ref[...] = _swiglu(x_ref[...], wg_bf[...], wu_bf[...], wd_bf[...]).astype(o_ref.dtype)

    @pl.when(blk0 + b >= n_used_ref[0])
    def _():
        o_ref[...] = jnp.zeros_like(o_ref)


def moe_experts(xg, blk_expert, n_used, layer, w_gate, w_up, w_down, out, blk0):
    rows, d = xg.shape
    ed = w_gate.shape[3]
    x_spec = pl.BlockSpec((MOE_BLOCK, d), lambda b, be, nu: (b, 0))
    w_idx = lambda b, be, nu: (layer, be[blk0 + b], 0, 0)
    in_specs = [x_spec, pl.BlockSpec((None, None, d, ed), w_idx), pl.BlockSpec((None, None, d, ed), w_idx),
                pl.BlockSpec((None, None, ed, d), w_idx)]
    args = [blk_expert, n_used, xg, w_gate, w_up, w_down]
    aliases = {}
    if out is not None:
        in_specs.append(pl.BlockSpec(memory_space=pl.ANY))
        args.append(out)
        aliases = {len(args) - 1: 0}
    return pl.pallas_call(
        functools.partial(_moe_kernel, blk0=blk0),
        grid_spec=pltpu.PrefetchScalarGridSpec(
            num_scalar_prefetch=2, grid=(rows // MOE_BLOCK,),
            in_specs=in_specs,
            out_specs=pl.BlockSpec((MOE_BLOCK, d), lambda b, be, nu: (blk0 + b, 0)),
            scratch_shapes=[pltpu.VMEM((d, ed), BF16), pltpu.VMEM((d, ed), BF16), pltpu.VMEM((ed, d), BF16)]),
        out_shape=jax.ShapeDtypeStruct((blk_expert.shape[0] * MOE_BLOCK, d), BF16),
        input_output_aliases=aliases,
        compiler_params=_params(1), name="moe_experts",
    )(*args)


def _shared_kernel(h_ref, wg_ref, wu_ref, wd_ref, o_ref):
    o_ref[...] = _swiglu(h_ref[...], wg_ref[...], wu_ref[...], wd_ref[...]).astype(o_ref.dtype)


def shared_expert(h, sw_gate, sw_up, sw_down):
    m, d = h.shape
    ed = sw_gate.shape[1]
    bm = _pick(m, 512)
    row_spec = pl.BlockSpec((bm, d), lambda i: (i, 0))
    full = lambda a, b: pl.BlockSpec((a, b), lambda i: (0, 0))
    return pl.pallas_call(
        _shared_kernel, grid=(m // bm,),
        in_specs=[row_spec, full(d, ed), full(d, ed), full(ed, d)],
        out_specs=row_spec,
        out_shape=jax.ShapeDtypeStruct((m, d), BF16),
        compiler_params=_params(1), name="shared_expert",
    )(h, sw_gate, sw_up, sw_down)


def _combine_kernel(s_ref, p_ref, w_ref, x_ref, gate_ref, o_ref):
    ffn = s_ref[...].astype(F32)
    w = w_ref[...]
    for k in range(TOP_K):
        ffn = ffn + w[:, k:k + 1] * p_ref[k].astype(F32)
    o_ref[...] = x_ref[...] + gate_ref[...] * ffn


def moe_combine_residual(stream, shared, picked, wts, row0, x, gate, tok0):
    m, d = x.shape
    n = picked.shape[1]
    bm = _row_block(stream, n, 256)
    assert row0 % bm == 0 and tok0 % bm == 0 and n % bm == 0
    off, off_all = tok0 // bm, (row0 + tok0) // bm
    row_spec = pl.BlockSpec((bm, d), lambda i: (i + off, 0))
    all_rows = lambda width: pl.BlockSpec((bm, width), lambda i: (i + off_all, 0))
    return pl.pallas_call(
        _combine_kernel, grid=(n // bm,),
        in_specs=[all_rows(d), pl.BlockSpec((TOP_K, bm, d), lambda i: (0, i, 0)), all_rows(TOP_K), row_spec,
                  _mod_spec(stream, bm, d, lambda i: (i + off, 0))],
        out_specs=row_spec,
        out_shape=jax.ShapeDtypeStruct((m, d), F32),
        input_output_aliases={3: 0},
        compiler_params=_params(1), name="moe_combine",
    )(shared, picked, wts, x, gate)


def _dest_kernel(start_ref, idx_ref, rank_ref, dest_ref):
    idx = idx_ref[...]

    def body(e, acc):
        return jnp.where(idx == e, start_ref[e], acc)

    dest_ref[...] = rank_ref[...] + lax.fori_loop(0, start_ref.shape[0], body, jnp.zeros_like(idx))


def assignment_rows(pad_start, idx, rank):
    return pl.pallas_call(
        _dest_kernel,
        in_specs=[pl.BlockSpec(memory_space=pltpu.SMEM), pl.BlockSpec(memory_space=pltpu.VMEM),
                  pl.BlockSpec(memory_space=pltpu.VMEM)],
        out_specs=pl.BlockSpec(memory_space=pltpu.VMEM),
        out_shape=jax.ShapeDtypeStruct(idx.shape, jnp.int32),
        name="assignment_rows",
    )(pad_start, idx, rank)


def moe_routed(h_all, idx, rank, counts, layer, w_gate, w_up, w_down):
    t, d = h_all.shape
    n_exp = counts.shape[0]
    n_assign = t * TOP_K
    counts = counts.reshape(n_exp).astype(jnp.int32)
    padded = (counts + MOE_BLOCK - 1) // MOE_BLOCK * MOE_BLOCK
    pad_end = jnp.cumsum(padded)
    pad_start = pad_end - padded
    dest = assignment_rows(pad_start, idx, rank).reshape(-1)
    n_blocks = -(-(n_assign + n_exp * (MOE_BLOCK - 1)) // MOE_BLOCK)
    n_blocks = -(-n_blocks // MOE_CHUNKS) * MOE_CHUNKS
    n_rows = n_blocks * MOE_BLOCK
    tok = jnp.tile(jnp.arange(t, dtype=jnp.int32), TOP_K)
    src_tok = jnp.arange(n_rows, dtype=jnp.int32) % t
    src_tok = src_tok.at[dest].add(tok - dest % t, unique_indices=True, mode="promise_in_bounds")
    blk_start = jnp.arange(n_blocks, dtype=jnp.int32) * MOE_BLOCK
    blk_expert = jnp.minimum(jnp.sum((pad_end[None, :] <= blk_start[:, None]).astype(jnp.int32), axis=1), n_exp - 1)
    n_used = pad_end[-1:] // MOE_BLOCK
    out = None
    chunk_rows = n_rows // MOE_CHUNKS
    for c in range(MOE_CHUNKS):
        xg = h_all.at[src_tok[c * chunk_rows:(c + 1) * chunk_rows]].get(mode="promise_in_bounds")
        out = moe_experts(xg, blk_expert, n_used, layer, w_gate, w_up, w_down, out, c * (n_blocks // MOE_CHUNKS))
    return out, dest.reshape(TOP_K, t)


def _stream(rows_per_batch):
    return dict(per_row=rows_per_batch < 1024, rows_per_batch=rows_per_batch)


def _mod_rows(stream, mod):
    b, d = mod.shape
    if not stream["per_row"]:
        return mod.reshape(b, 1, d)
    return jnp.repeat(mod, stream["rows_per_batch"], axis=0).reshape(1, -1, d)


def kernel(x_prompt, x_sample, cache_k_attn, cache_v_attn, c_prompt, c_sample, norm_mix_g, norm_ffn_g, ada_w, ada_b, attn_w_qkv, attn_w_o, attn_q_norm, attn_k_norm, attn_lambda_q1, attn_lambda_k1, attn_lambda_q2, attn_lambda_k2, attn_subln_g, sgu_w_in, sgu_ln_g, sgu_ln_b, sgu_w_s, sgu_b_s, sgu_w_o, moe_w_router, moe_b_router, moe_w_gate, moe_w_up, moe_w_down, shared_w_gate, shared_w_up, shared_w_down):
    bp, sp, d = x_prompt.shape
    bs, ss, _ = x_sample.shape
    depth = ada_w.shape[0]
    dv = 2 * ATTN_HEAD_DIM
    heads = d // dv
    mp, ms = bp * sp, bs * ss
    streams = (_stream(sp), _stream(ss))
    xs = [x_prompt.reshape(mp, d), x_sample.reshape(ms, d)]

    mod = ada_modulation(jnp.concatenate([c_prompt, c_sample], axis=0), ada_w, ada_b)
    slopes2 = (2.0 ** (-8.0 * jnp.arange(1, heads + 1, dtype=F32) / heads)) * LOG2E

    k_out, v_out, sgu_out = [[], []], [[], []], []
    for i in range(depth):
        j = i // N_MIXERS
        mods = []
        for si, (stream, rows) in enumerate(zip(streams, (slice(0, bp), slice(bp, bp + bs)))):
            mods.append([_mod_rows(stream, mod[i, rows, c * d:(c + 1) * d]) for c in range(6)])

        if i % N_MIXERS == 0:
            lam_init = 0.8 - 0.6 * math.exp(-0.3 * i)
            f = lambda a: a[j].astype(F32)
            lam = (jnp.exp(jnp.sum(f(attn_lambda_q1) * f(attn_lambda_k1)))
                   - jnp.exp(jnp.sum(f(attn_lambda_q2) * f(attn_lambda_k2))) + lam_init).reshape(1)
            for si, stream in enumerate(streams):
                h = modulate(stream, xs[si], norm_mix_g[i], mods[si][0], mods[si][1])
                q_bf, k32, k_bf, v32, v_bf = qkv_project(stream, h, attn_w_qkv[j], attn_q_norm[j], attn_k_norm[j])
                if si == 0:
                    o = attention_prompt(q_bf, k_bf, v_bf, bp, lam, slopes2, attn_subln_g[j], 1.0 - lam_init)
                else:
                    o = attention_sample(q_bf, k_bf, v_bf, cache_k_attn, cache_v_attn, j, lam, slopes2,
                                         attn_subln_g[j], 1.0 - lam_init)
                xs[si] = project_residual(stream, o, attn_w_o[j], xs[si], mods[si][2], "attn_out_proj")
                k_out[si].append(k32)
                v_out[si].append(v32)
        else:
            for si, stream in enumerate(streams):
                h = modulate(stream, xs[si], norm_mix_g[i], mods[si][0], mods[si][1])
                (z,) = matmul(h, sgu_w_in[j], bm=_row_block(stream, h.shape[0], 1024),
                              bn=_pick(sgu_w_in.shape[2], 1024),
                              out_dtypes=(BF16,), epilogue=_gelu_epilogue, name="sgu_in_proj")
                chunk_len = SGU_CHUNK if si == 0 else ss
                res = sgu_gate(z, sgu_ln_g[j], sgu_ln_b[j], sgu_w_s[j], sgu_b_s[j], chunk_len, write_v=(si == 1))
                if si == 1:
                    sgu_out.append(res[1])
                xs[si] = project_residual(stream, res[0], sgu_w_o[j], xs[si], mods[si][2], "sgu_out_proj")

        hs, routing = [], []
        counts = jnp.zeros((moe_w_router.shape[2], 1), F32)
        for si, stream in enumerate(streams):
            hf, idx, wts, rank, counts = modulate(stream, xs[si], norm_ffn_g[i], mods[si][3], mods[si][4],
                                                  router=(moe_w_router[i], moe_b_router[i], counts))
            hs.append(hf)
            routing.append((idx, wts, rank))
        idx, wts, rank = (jnp.concatenate(parts, axis=1) for parts in zip(*routing))
        h_all = jnp.concatenate(hs, axis=0)
        shared = shared_expert(h_all, *[w[i].astype(BF16) for w in (shared_w_gate, shared_w_up, shared_w_down)])
        out, dest = moe_routed(h_all, idx, rank, counts, i, moe_w_gate, moe_w_up, moe_w_down)
        for si, (stream, row0, m_rows) in enumerate(zip(streams, (0, mp), (mp, ms))):
            piece = _pick(m_rows, COMBINE_TOKENS)
            for tok0 in range(0, m_rows, piece):
                rows = dest[:, row0 + tok0:row0 + tok0 + piece].reshape(-1)
                picked = out.at[rows].get(mode="promise_in_bounds", unique_indices=True).reshape(TOP_K, piece, d)
                xs[si] = moe_combine_residual(stream, shared, picked, wts.T, row0, xs[si], mods[si][5], tok0)

    n_attn = len(k_out[0])
    k_prompt = jnp.stack(k_out[0]).reshape(n_attn, bp, sp, heads, 2, ATTN_HEAD_DIM)
    v_prompt = jnp.stack(v_out[0]).reshape(n_attn, bp, sp, heads, dv)
    k_sample = jnp.stack(k_out[1]).reshape(n_attn, bs, ss, heads, 2, ATTN_HEAD_DIM)
    v_sample = jnp.stack(v_out[1]).reshape(n_attn, bs, ss, heads, dv)
    sgu_v = jnp.stack(sgu_out).reshape(len(sgu_out), bs, ss, -1)
    return (xs[0].reshape(bp, sp, d), xs[1].reshape(bs, ss, d), k_prompt, v_prompt, k_sample, v_sample, sgu_v)
```

```python
import functools
import math

import jax
import jax.numpy as jnp
from jax import lax
from jax.experimental import pallas as pl
from jax.experimental.pallas import tpu as pltpu

F32 = jnp.float32
BF16 = jnp.bfloat16

EPS = 1e-6
CHUNK = 64
N_MIXERS = 2
ATTN_HEAD_DIM = 128
SGU_GROUPS = 8
SGU_CHUNK = 128
TOP_K = 8
N_EXPERT_GROUPS = 8
TOPK_GROUPS = 4
ROUTED_SCALE = 2.5
LOG2E = 1.4426950408889634
NEG_BIG = -1e30

VMEM_LIMIT = 56 * 1024 * 1024
MOE_BLOCK = 512
MOE_CHUNKS = 8
SGU_MIX_ROWS = 256
COMBINE_TOKENS = 2048
ATTN_ROW_CHUNK = 64
LANES = 128


def _params(n_axes):
    return pltpu.CompilerParams(dimension_semantics=("arbitrary",) * n_axes, vmem_limit_bytes=VMEM_LIMIT)


def _pick(n, pref):
    if n <= pref:
        return n
    b = pref
    while n % b:
        b //= 2
    return b


def _ada_kernel(c_ref, w_ref, b_ref, o_ref):
    c = c_ref[...]
    x = (c * jax.nn.sigmoid(c)).astype(BF16)
    o_ref[...] = jnp.dot(x, w_ref[...].astype(BF16), preferred_element_type=F32) + b_ref[...]


def ada_modulation(c, ada_w, ada_b):
    n_layers, d, n = ada_w.shape
    r = c.shape[0]
    bn = _pick(n, 1024)
    return pl.pallas_call(
        _ada_kernel,
        grid=(n_layers, n // bn),
        in_specs=[
            pl.BlockSpec((r, d), lambda l, j: (0, 0)),
            pl.BlockSpec((None, d, bn), lambda l, j: (l, 0, j)),
            pl.BlockSpec((None, 1, bn), lambda l, j: (l, 0, j)),
        ],
        out_specs=pl.BlockSpec((None, r, bn), lambda l, j: (l, 0, j)),
        out_shape=jax.ShapeDtypeStruct((n_layers, r, n), F32),
        compiler_params=_params(2),
        name="ada_modulation",
    )(c, ada_w, ada_b.reshape(n_layers, 1, n))


def _modulated(x, g, shift, scale):
    y = x * lax.rsqrt(jnp.mean(x * x, axis=-1, keepdims=True) + EPS)
    return y * g * (1.0 + scale) + shift


def _modulate_kernel(x_ref, g_ref, sh_ref, sc_ref, o_ref):
    o_ref[...] = _modulated(x_ref[...], g_ref[...], sh_ref[...], sc_ref[...]).astype(o_ref.dtype)


def _split3(x):
    hi = x.astype(BF16)
    lo = (x - hi.astype(F32)).astype(BF16)
    return hi, lo


def _first_max(x, ids, n_ids, axes):
    mx = x
    for ax in axes:
        mx = jnp.max(mx, axis=ax, keepdims=True)
    arg = jnp.where(x == mx, ids, n_ids)
    for ax in axes:
        arg = jnp.min(arg, axis=ax, keepdims=True)
    return mx, arg


def _sum_axes(x, axes):
    for ax in axes:
        x = jnp.sum(x, axis=ax, keepdims=True)
    return x


def _modulate_router_kernel(x_ref, g_ref, sh_ref, sc_ref, wrt_ref, br_ref, cin_ref,
                            o_ref, idx_ref, wts_ref, rank_ref, cnt_ref, carry):
    h = _modulated(x_ref[...], g_ref[...], sh_ref[...], sc_ref[...])
    o_ref[...] = h.astype(o_ref.dtype)
    bm = h.shape[0]
    n_exp = wrt_ref.shape[0]
    per = n_exp // N_EXPERT_GROUPS
    grp_shape = (N_EXPERT_GROUPS, per, bm)

    h_hi, h_lo = _split3(h)
    w_hi, w_lo = _split3(wrt_ref[...])
    nt = (((1,), (1,)), ((), ()))
    logits = (lax.dot_general(w_hi, h_hi, nt, preferred_element_type=F32)
              + lax.dot_general(w_lo, h_hi, nt, preferred_element_type=F32)
              + lax.dot_general(w_hi, h_lo, nt, preferred_element_type=F32))
    s = jax.nn.sigmoid(logits)
    s3 = s.reshape(grp_shape)
    sb3 = (s + br_ref[...]).reshape(grp_shape)

    sub = lax.broadcasted_iota(jnp.int32, grp_shape, 1)
    gid = lax.broadcasted_iota(jnp.int32, (N_EXPERT_GROUPS, 1, bm), 0)
    eid = lax.broadcasted_iota(jnp.int32, grp_shape, 0) * per + sub

    m1, i1 = _first_max(sb3, sub, per, (1,))
    m2 = jnp.max(jnp.where(sub == i1, -jnp.inf, sb3), axis=1, keepdims=True)
    work = m1 + m2
    chosen = jnp.zeros_like(work)
    for _ in range(TOPK_GROUPS):
        _, gi = _first_max(work, gid, N_EXPERT_GROUPS, (0,))
        chosen = jnp.where(gid == gi, 1.0, chosen)
        work = jnp.where(gid == gi, -jnp.inf, work)
    sel = jnp.where(chosen > 0.0, sb3, -jnp.inf)

    hits, ids, raw = [], [], []
    for _ in range(TOP_K):
        _, ei = _first_max(sel, eid, n_exp, (1, 0))
        hit = eid == ei
        raw.append(_sum_axes(jnp.where(hit, s3, 0.0), (1, 0)))
        sel = jnp.where(hit, -jnp.inf, sel)
        hits.append(hit)
        ids.append(ei)
    total = sum(raw)

    member = sum(jnp.where(hit, 1.0, 0.0) for hit in hits).reshape(n_exp, bm)
    r_i = lax.broadcasted_iota(jnp.int32, (bm, bm), 0)
    c_i = lax.broadcasted_iota(jnp.int32, (bm, bm), 1)
    before = jnp.where(r_i < c_i, 1.0, 0.0).astype(BF16)

    @pl.when(pl.program_id(0) == 0)
    def _():
        carry[...] = cin_ref[...]

    rank_all = (jnp.dot(member.astype(BF16), before, preferred_element_type=F32) + carry[...]).reshape(grp_shape)
    carry[...] += jnp.sum(member, axis=1, keepdims=True)
    cnt_ref[...] = carry[...]
    for k in range(TOP_K):
        idx_ref[k:k + 1, :] = ids[k].reshape(1, bm)
        wts_ref[k:k + 1, :] = (raw[k] / total * ROUTED_SCALE).reshape(1, bm)
        rank_ref[k:k + 1, :] = _sum_axes(jnp.where(hits[k], rank_all, 0.0), (1, 0)).reshape(1, bm).astype(jnp.int32)


def _row_block(stream, m, pref):
    return _pick(m if stream["per_row"] else stream["rows_per_batch"], pref)


def _mod_spec(stream, bm, bn, ij):
    if stream["per_row"]:
        return pl.BlockSpec((None, bm, bn), lambda *g: (0,) + tuple(ij(*g)))
    bpg = stream["rows_per_batch"] // bm
    return pl.BlockSpec((None, 1, bn), lambda *g: (ij(*g)[0] // bpg, 0, ij(*g)[1]))


def modulate(stream, x, g, shift, scale, router=None):
    m, d = x.shape
    bm = _row_block(stream, m, 512)
    row_spec = pl.BlockSpec((bm, d), lambda i: (i, 0))
    mod_spec = _mod_spec(stream, bm, d, lambda i: (i, 0))
    g_spec = pl.BlockSpec((1, d), lambda i: (0, 0))
    if router is None:
        return pl.pallas_call(
            _modulate_kernel, grid=(m // bm,),
            in_specs=[row_spec, g_spec, mod_spec, mod_spec], out_specs=row_spec,
            out_shape=jax.ShapeDtypeStruct((m, d), BF16),
            compiler_params=_params(1), name="modulate",
        )(x, g.reshape(1, d), shift, scale)
    w_router, b_router, counts_in = router
    e = w_router.shape[1]
    tok_spec = pl.BlockSpec((TOP_K, bm), lambda i: (0, i))
    cnt_spec = pl.BlockSpec((e, 1), lambda i: (0, 0))
    return pl.pallas_call(
        _modulate_router_kernel, grid=(m // bm,),
        in_specs=[row_spec, g_spec, mod_spec, mod_spec, pl.BlockSpec((e, d), lambda i: (0, 0)), cnt_spec, cnt_spec],
        out_specs=[row_spec, tok_spec, tok_spec, tok_spec, cnt_spec],
        out_shape=[jax.ShapeDtypeStruct((m, d), BF16), jax.ShapeDtypeStruct((TOP_K, m), jnp.int32),
                   jax.ShapeDtypeStruct((TOP_K, m), F32), jax.ShapeDtypeStruct((TOP_K, m), jnp.int32),
                   jax.ShapeDtypeStruct((e, 1), F32)],
        scratch_shapes=[pltpu.VMEM((e, 1), F32)],
        compiler_params=_params(1), name="modulate_router",
    )(x, g.reshape(1, d), shift, scale, w_router.T, b_router.astype(F32).reshape(e, 1), counts_in)


def _mm_kernel(x_ref, w_ref, *refs, n_extra, epilogue):
    extra, outs, w_bf = refs[:n_extra], refs[n_extra:-1], refs[-1]

    @pl.when(pl.program_id(1) == 0)
    def _():
        w_bf[...] = w_ref[...].astype(BF16)

    acc = jnp.dot(x_ref[...], w_bf[...], preferred_element_type=F32)
    epilogue(acc, extra, outs)


def matmul(x, w, *, bm, bn, col_block_off=0, n_cols=None, extra=(), extra_specs=(), out_dtypes, epilogue, name):
    m, k = x.shape
    n = w.shape[1] if n_cols is None else n_cols
    grid = (n // bn, m // bm)
    out_spec = pl.BlockSpec((bm, bn), lambda j, i: (i, j))
    return pl.pallas_call(
        functools.partial(_mm_kernel, n_extra=len(extra), epilogue=epilogue),
        grid=grid,
        in_specs=[pl.BlockSpec((bm, k), lambda j, i: (i, 0)),
                  pl.BlockSpec((k, bn), lambda j, i: (0, j + col_block_off))] + list(extra_specs),
        out_specs=[out_spec] * len(out_dtypes),
        out_shape=[jax.ShapeDtypeStruct((m, n), dt) for dt in out_dtypes],
        scratch_shapes=[pltpu.VMEM((k, bn), BF16)],
        compiler_params=_params(2), name=name,
    )(x, w, *extra)


def _head_rms(acc, gain, post_scale):
    pieces = []
    for c in range(acc.shape[1] // ATTN_HEAD_DIM):
        seg = acc[:, c * ATTN_HEAD_DIM:(c + 1) * ATTN_HEAD_DIM]
        y = seg * lax.rsqrt(jnp.mean(seg * seg, axis=-1, keepdims=True) + EPS)
        pieces.append(y * (gain * post_scale))
    return jnp.concatenate(pieces, axis=-1)


def _q_epilogue(acc, extra, outs):
    outs[0][...] = _head_rms(acc, extra[0][...], ATTN_HEAD_DIM ** -0.5 * LOG2E).astype(BF16)


def _v_epilogue(acc, extra, outs):
    outs[0][...] = acc
    outs[1][...] = acc.astype(BF16)


def _gelu_epilogue(acc, extra, outs):
    outs[0][...] = (0.5 * acc * (1.0 + lax.erf(acc * (2.0 ** -0.5)))).astype(BF16)


def _residual_epilogue(acc, extra, outs):
    x_ref, gate_ref = extra
    outs[0][...] = x_ref[...] + gate_ref[...] * acc


def _k_proj_kernel(h_ref, w_ref, g_ref, k32_ref, kbf_ref):
    acc = jnp.dot(h_ref[...], w_ref[...], preferred_element_type=F32)
    bm = acc.shape[0]
    groups = acc.shape[1] // ATTN_HEAD_DIM
    gain = g_ref[...]
    for g in range(groups):
        cols = slice(g * ATTN_HEAD_DIM, (g + 1) * ATTN_HEAD_DIM)
        seg = acc[:, cols]
        kn = seg * lax.rsqrt(jnp.mean(seg * seg, axis=-1, keepdims=True) + EPS) * gain
        k32_ref[pl.ds(g, bm, stride=groups), :] = kn
        kbf_ref[:, cols] = kn.astype(BF16)


def k_project(stream, h, w_k, k_norm):
    m, d = h.shape
    n = w_k.shape[1]
    groups = n // ATTN_HEAD_DIM
    bm = _row_block(stream, m, 512)
    return pl.pallas_call(
        _k_proj_kernel, grid=(m // bm,),
        in_specs=[pl.BlockSpec((bm, d), lambda i: (i, 0)), pl.BlockSpec((d, n), lambda i: (0, 0)),
                  pl.BlockSpec((1, ATTN_HEAD_DIM), lambda i: (0, 0))],
        out_specs=[pl.BlockSpec((bm * groups, ATTN_HEAD_DIM), lambda i: (i, 0)),
                   pl.BlockSpec((bm, n), lambda i: (i, 0))],
        out_shape=[jax.ShapeDtypeStruct((m * groups, ATTN_HEAD_DIM), F32), jax.ShapeDtypeStruct((m, n), BF16)],
        compiler_params=_params(1), name="k_proj",
    )(h, w_k, k_norm.reshape(1, -1))


def qkv_project(stream, h, w_qkv, q_norm, k_norm):
    m, d = h.shape
    bm = _row_block(stream, m, 1024)
    bn = _pick(d, 1024)
    nb = d // bn
    gain_spec = pl.BlockSpec((1, ATTN_HEAD_DIM), lambda j, i: (0, 0))
    common = dict(bm=bm, bn=bn, n_cols=d)
    (q_bf,) = matmul(h, w_qkv, col_block_off=0, extra=(q_norm.reshape(1, -1),), extra_specs=(gain_spec,),
                     out_dtypes=(BF16,), epilogue=_q_epilogue, name="q_proj", **common)
    k32, k_bf = k_project(stream, h, w_qkv[:, d:2 * d].astype(BF16), k_norm)
    v32, v_bf = matmul(h, w_qkv, col_block_off=2 * nb, out_dtypes=(F32, BF16), epilogue=_v_epilogue,
                       name="v_proj", **common)
    return q_bf, k32, k_bf, v32, v_bf


def project_residual(stream, a, w, x, gate, name):
    k, n = w.shape
    deep = k > 4096
    bm = _row_block(stream, a.shape[0], 512 if deep else 1024)
    bn = _pick(n, 512 if deep else 1024)
    (out,) = matmul(a, w, bm=bm, bn=bn, extra=(x, gate),
                    extra_specs=(pl.BlockSpec((bm, bn), lambda j, i: (i, j)),
                                 _mod_spec(stream, bm, bn, lambda j, i: (i, j))),
                    out_dtypes=(F32,), epilogue=_residual_epilogue, name=name)
    return out


def _finish_heads(acc1, l1, acc2, l2, lam, g, post_scale):
    o = acc1 / l1 - lam * (acc2 / l2)
    o = o * lax.rsqrt(jnp.mean(o * o, axis=-1, keepdims=True) + EPS)
    return o * (g * post_scale)


def _attn_prompt_kernel(lam_ref, slope_ref, q_ref, k_ref, v_ref, g_ref, o_ref,
                        s_scr, p_scr, qa_scr, acc_scr, m_scr, l_scr, a_scr, *, bq, post_scale):
    hd = ATTN_HEAD_DIM
    rc = ATTN_ROW_CHUNK
    qi = pl.program_id(2)
    hh = pl.program_id(1)
    sl2 = slope_ref[hh]

    m_scr[...] = jnp.full(m_scr.shape, NEG_BIG, F32)
    l_scr[...] = jnp.zeros(l_scr.shape, F32)
    acc_scr[...] = jnp.zeros(acc_scr.shape, F32)

    lane = lax.broadcasted_iota(jnp.int32, (bq, LANES), 1)
    key_c = lax.broadcasted_iota(jnp.int32, (bq, LANES), 0)
    k_aug = jnp.where(lane < 3, key_c % 256, jnp.where(lane < 6, key_c // 256, 0)).astype(F32).astype(BF16)
    rest = jnp.full((bq, LANES), sl2, F32)
    piece = jnp.zeros((bq, LANES), F32)
    for i in range(3):
        part = rest.astype(BF16).astype(F32)
        piece = jnp.where(lane % 3 == i, part, piece)
        rest = rest - part
    q_aug = jnp.where(lane < 3, piece, jnp.where(lane < 6, 256.0 * piece, 0.0)).astype(BF16)
    for mi in range(2):
        qa_scr[mi, :, :hd] = q_ref[:, mi * hd:(mi + 1) * hd]
        qa_scr[mi, :, hd:] = q_aug

    def scores(kb, slot):
        k0 = pl.multiple_of(kb * bq, bq)
        for mi in range(2):
            k_blk = jnp.concatenate([k_ref[pl.ds(k0, bq), mi * hd:(mi + 1) * hd], k_aug], axis=1)
            s_scr[2 * slot + mi] = lax.dot_general(qa_scr[mi], k_blk, (((1,), (1,)), ((), ())),
                                                   preferred_element_type=F32)

    def softmax(kb, slot, diag):
        shift_blk = sl2 * ((kb - qi) * bq + jnp.zeros((1, LANES), jnp.int32)).astype(F32)
        for mi in range(2):
            s_map = s_scr.at[2 * slot + mi]
            p_map = p_scr.at[2 * slot + mi]
            for c in range(bq // rc):
                rows = slice(c * rc, (c + 1) * rc)
                ncol = min(bq, -(-((c + 1) * rc) // CHUNK) * CHUNK) if diag else bq
                lane_tiles = [slice(t * LANES, (t + 1) * LANES) for t in range(-(-ncol // LANES))]

                def tile(lanes, t):
                    s_t = s_map[rows, lanes]
                    if not diag:
                        return s_t
                    r_i = lax.broadcasted_iota(jnp.int32, (rc, LANES), 0) + c * rc
                    c_i = lax.broadcasted_iota(jnp.int32, (rc, LANES), 1) + t * LANES
                    s_t = s_t + (2.0 * sl2) * jnp.minimum(r_i - c_i, 0).astype(F32)
                    return jnp.where((c_i // CHUNK) <= (r_i // CHUNK), s_t, NEG_BIG)

                tiles = [tile(lanes, t) for t, lanes in enumerate(lane_tiles)]
                m_old = m_scr[mi, rows, :]
                blk_max = jnp.max(functools.reduce(jnp.maximum, tiles), axis=-1, keepdims=True)
                m_new = jnp.maximum(m_old, blk_max + shift_blk)
                alpha = jnp.exp2(m_old - m_new)
                m_scr[mi, rows, :] = m_new
                a_scr[2 * slot + mi, rows, :] = alpha
                if not diag:
                    tiles = [tile(lanes, t) for t, lanes in enumerate(lane_tiles)]
                off = m_new - shift_blk
                ps = [jnp.exp2(s_t - off) for s_t in tiles]
                l_scr[mi, rows, :] = alpha * l_scr[mi, rows, :] + functools.reduce(jnp.add, ps)
                for t, p_t in enumerate(ps):
                    p_map[rows, t * LANES:(t + 1) * LANES] = p_t.astype(BF16)
                if len(ps) * LANES < bq:
                    p_map[rows, len(ps) * LANES:] = jnp.zeros((rc, bq - len(ps) * LANES), BF16)

    def weighted_values(kb, slot):
        k0 = pl.multiple_of(jnp.maximum(kb, 0) * bq, bq)
        for mi in range(2):
            pv = jnp.dot(p_scr[2 * slot + mi], v_ref[pl.ds(k0, bq), :], preferred_element_type=F32)
            for t in range(pv.shape[1] // LANES):
                lanes = slice(t * LANES, (t + 1) * LANES)
                acc_scr[mi, :, lanes] = a_scr[2 * slot + mi] * acc_scr[mi, :, lanes] + pv[:, lanes]

    p_scr[2] = jnp.zeros((bq, bq), BF16)
    p_scr[3] = jnp.zeros((bq, bq), BF16)
    a_scr[2] = jnp.ones((bq, LANES), F32)
    a_scr[3] = jnp.ones((bq, LANES), F32)

    def pair(j2, carry):
        kb = 2 * j2
        scores(kb + 1, 1)
        weighted_values(kb - 1, 1)
        softmax(kb, 0, False)
        scores(kb + 2, 0)
        weighted_values(kb, 0)
        softmax(kb + 1, 1, False)
        return carry

    scores(0, 0)
    lax.fori_loop(0, qi // 2, pair, 0)

    @pl.when(qi % 2 == 1)
    def _():
        scores(qi, 1)
        weighted_values(qi - 2, 1)
        softmax(qi - 1, 0, False)
        weighted_values(qi - 1, 0)
        softmax(qi, 1, True)
        weighted_values(qi, 1)

    @pl.when(qi % 2 == 0)
    def _():
        weighted_values(qi - 1, 1)
        softmax(qi, 0, True)
        weighted_values(qi, 0)
    l1 = jnp.sum(l_scr[0], axis=-1, keepdims=True)
    l2 = jnp.sum(l_scr[1], axis=-1, keepdims=True)
    o_ref[...] = _finish_heads(acc_scr[0], l1, acc_scr[1], l2, lam_ref[0], g_ref[...], post_scale).astype(o_ref.dtype)


def attention_prompt(q_bf, k_bf, v_bf, batch, lam, slopes2, subln_g, post_scale):
    m, width = q_bf.shape
    s = m // batch
    dv = 2 * ATTN_HEAD_DIM
    heads = width // dv
    bq = _pick(s, 512)
    assert bq % CHUNK == 0
    nq = s // bq
    kv_spec = pl.BlockSpec((s, dv), lambda b, h, i: (b, h))
    q_spec = pl.BlockSpec((bq, dv), lambda b, h, i: (b * nq + i, h))
    smem = pl.BlockSpec(memory_space=pltpu.SMEM)
    return pl.pallas_call(
        functools.partial(_attn_prompt_kernel, bq=bq, post_scale=post_scale),
        grid=(batch, heads, nq),
        in_specs=[smem, smem, q_spec, kv_spec, kv_spec, pl.BlockSpec((1, dv), lambda b, h, i: (0, 0))],
        out_specs=q_spec,
        out_shape=jax.ShapeDtypeStruct((m, width), BF16),
        scratch_shapes=[pltpu.VMEM((4, bq, bq), F32), pltpu.VMEM((4, bq, bq), BF16), pltpu.VMEM((2, bq, dv), BF16),
                        pltpu.VMEM((2, bq, dv), F32), pltpu.VMEM((2, bq, LANES), F32),
                        pltpu.VMEM((2, bq, LANES), F32), pltpu.VMEM((4, bq, LANES), F32)],
        compiler_params=_params(3), name="attention_prompt",
    )(lam, slopes2, q_bf, k_bf, v_bf, subln_g.reshape(1, dv))


def _attn_sample_kernel(lam_ref, slope_ref, q_ref, kn_ref, vn_ref, ck_ref, cv_ref, g_ref, o_ref, *, post_scale):
    hd = ATTN_HEAD_DIM
    dv = 2 * hd
    lam = lam_ref[0]
    t = q_ref.shape[0]
    heads = q_ref.shape[1] // dv
    past = ck_ref.shape[0] // (2 * heads)

    def dist_mask(n_keys, key0):
        q_pos = past + lax.broadcasted_iota(jnp.int32, (t, n_keys), 0)
        k_pos = key0 + lax.broadcasted_iota(jnp.int32, (t, n_keys), 1)
        return jnp.abs(q_pos - k_pos).astype(F32), (k_pos // CHUNK) <= (q_pos // CHUNK)

    dist_p, vis_p = dist_mask(past, 0)
    dist_n, vis_n = dist_mask(t, past)
    dn = (((1,), (1,)), ((), ()))
    for h in range(heads):
        cols = slice(h * dv, (h + 1) * dv)
        sl2 = slope_ref[h]
        v_past = jnp.concatenate(
            [cv_ref[pl.ds(c * heads + h, past, stride=heads * (dv // LANES)), :] for c in range(dv // LANES)],
            axis=1).astype(BF16)
        res = []
        for mi in range(2):
            qm = q_ref[:, h * dv + mi * hd:h * dv + (mi + 1) * hd]
            k_past = ck_ref[pl.ds(2 * h + mi, past, stride=2 * heads), :].astype(BF16)
            s_p = lax.dot_general(qm, k_past, dn, preferred_element_type=F32)
            s_n = lax.dot_general(qm, kn_ref[:, h * dv + mi * hd:h * dv + (mi + 1) * hd], dn,
                                  preferred_element_type=F32)
            s_p = jnp.where(vis_p, s_p - sl2 * dist_p, NEG_BIG)
            s_n = jnp.where(vis_n, s_n - sl2 * dist_n, NEG_BIG)
            m = jnp.maximum(jnp.max(s_p, axis=-1, keepdims=True), jnp.max(s_n, axis=-1, keepdims=True))
            p_p = jnp.exp2(s_p - m)
            p_n = jnp.exp2(s_n - m)
            l = jnp.sum(p_p, axis=-1, keepdims=True) + jnp.sum(p_n, axis=-1, keepdims=True)
            acc = (jnp.dot(p_p.astype(BF16), v_past, preferred_element_type=F32)
                   + jnp.dot(p_n.astype(BF16), vn_ref[:, cols], preferred_element_type=F32))
            res += [acc, l]
        o_ref[:, cols] = _finish_heads(res[0], res[1], res[2], res[3], lam, g_ref[...],
                                       post_scale).astype(o_ref.dtype)


def attention_sample(q_bf, k_bf, v_bf, cache_k, cache_v, layer, lam, slopes2, subln_g, post_scale):
    _, batch, past, heads, _, hd = cache_k.shape
    m, width = q_bf.shape
    t = m // batch
    dv = 2 * hd
    new_spec = pl.BlockSpec((t, width), lambda b: (b, 0))
    n_layers = cache_k.shape[0]
    cache_k = cache_k.reshape(n_layers, batch, past * heads * 2, hd)
    cache_v = cache_v.reshape(n_layers, batch, past, heads, dv // LANES, LANES)
    cache_v = cache_v.transpose(0, 1, 2, 4, 3, 5).reshape(n_layers, batch, past * (dv // LANES) * heads, LANES)
    ck_spec = pl.BlockSpec((None, None, past * heads * 2, hd), lambda b: (layer, b, 0, 0))
    cv_spec = pl.BlockSpec((None, None, past * (dv // LANES) * heads, LANES), lambda b: (layer, b, 0, 0))
    smem = pl.BlockSpec(memory_space=pltpu.SMEM)
    return pl.pallas_call(
        functools.partial(_attn_sample_kernel, post_scale=post_scale),
        grid=(batch,),
        in_specs=[smem, smem, new_spec, new_spec, new_spec, ck_spec, cv_spec,
                  pl.BlockSpec((1, dv), lambda b: (0, 0))],
        out_specs=new_spec,
        out_shape=jax.ShapeDtypeStruct((m, width), BF16),
        compiler_params=_params(1), name="attention_sample",
    )(lam, slopes2, q_bf, k_bf, v_bf, cache_k, cache_v, subln_g.reshape(1, dv))


def _sgu_gate_kernel(u_ref, v_ref, g_ref, b_ref, w_ref, bs_ref, *out_refs, chunk_len, write_v):
    bt, width = u_ref.shape
    gd = width // SGU_GROUPS
    v = v_ref[...].astype(F32)
    mu = jnp.mean(v, axis=-1, keepdims=True)
    vc = v - mu
    vn = vc * lax.rsqrt(jnp.mean(vc * vc, axis=-1, keepdims=True) + EPS) * g_ref[...] + b_ref[...]
    if write_v:
        out_refs[1][...] = vn
    vn_bf = vn.astype(BF16)
    mr = w_ref.shape[1]
    row = lax.broadcasted_iota(jnp.int32, (mr, mr), 0)
    col = lax.broadcasted_iota(jnp.int32, (mr, mr), 1)
    keep = ((row // chunk_len) == (col // chunk_len)) & ((col % chunk_len) <= (row % chunk_len))
    for gi in range(SGU_GROUPS):
        w = jnp.where(keep, w_ref[gi], 0.0).astype(BF16)
        bias = bs_ref[gi]
        for c in range(bt // mr):
            rows = slice(c * mr, (c + 1) * mr)
            cols = slice(gi * gd, (gi + 1) * gd)
            mixed = jnp.dot(w, vn_bf[rows, cols], preferred_element_type=F32) + bias
            out_refs[0][rows, cols] = (u_ref[rows, cols].astype(F32) * mixed).astype(BF16)


def sgu_gate(z, ln_g, ln_b, w_s, b_s, chunk_len, write_v):
    m, two_w = z.shape
    width = two_w // 2
    bt = _pick(m, SGU_MIX_ROWS)
    reps = bt // chunk_len
    w_t = jnp.tile(w_s[:, :chunk_len, :chunk_len], (1, reps, reps))
    b_t = jnp.tile(b_s[:, :chunk_len], (1, reps))[:, :, None]
    row_spec = lambda c: pl.BlockSpec((bt, width), lambda i, c=c: (i, c))
    vec_spec = pl.BlockSpec((1, width), lambda i: (0, 0))
    out_specs = [row_spec(0)]
    out_shape = [jax.ShapeDtypeStruct((m, width), BF16)]
    if write_v:
        out_specs.append(row_spec(0))
        out_shape.append(jax.ShapeDtypeStruct((m, width), F32))
    return pl.pallas_call(
        functools.partial(_sgu_gate_kernel, chunk_len=chunk_len, write_v=write_v),
        grid=(m // bt,),
        in_specs=[row_spec(0), row_spec(1), vec_spec, vec_spec,
                  pl.BlockSpec((SGU_GROUPS, bt, bt), lambda i: (0, 0, 0)),
                  pl.BlockSpec((SGU_GROUPS, bt, 1), lambda i: (0, 0, 0))],
        out_specs=out_specs, out_shape=out_shape,
        compiler_params=_params(1), name="sgu_gate",
    )(z, z, ln_g.reshape(1, width), ln_b.reshape(1, width), w_t, b_t)


def _swiglu(x, wg, wu, wd):
    g = jnp.dot(x, wg, preferred_element_type=F32)
    u = jnp.dot(x, wu, preferred_element_type=F32)
    h = (g * jax.nn.sigmoid(g) * u).astype(BF16)
    return jnp.dot(h, wd, preferred_element_type=F32)


def _moe_kernel(blk_e_ref, n_used_ref, x_ref, wg_ref, wu_ref, wd_ref, *refs, blk0):
    o_ref, wg_bf, wu_bf, wd_bf = refs[-4:]
    b = pl.program_id(0)
    e = blk_e_ref[blk0 + b]
    e_prev = blk_e_ref[blk0 + jnp.maximum(b - 1, 0)]

    @pl.when((b == 0) | (e != e_prev))
    def _():
        wg_bf[...] = wg_ref[...].astype(BF16)
        wu_bf[...] = wu_ref[...].astype(BF16)
        wd_bf[...] = wd_ref[...].astype(BF16)

    @pl.when(blk0 + b < n_used_ref[0])
    def _():
        o_ref[...] = _swiglu(x_ref[...], wg_bf[...], wu_bf[...], wd_bf[...]).astype(o_ref.dtype)

    @pl.when(blk0 + b >= n_used_ref[0])
    def _():
        o_ref[...] = jnp.zeros_like(o_ref)


def moe_experts(xg, blk_expert, n_used, layer, w_gate, w_up, w_down, out, blk0):
    rows, d = xg.shape
    ed = w_gate.shape[3]
    x_spec = pl.BlockSpec((MOE_BLOCK, d), lambda b, be, nu: (b, 0))
    w_idx = lambda b, be, nu: (layer, be[blk0 + b], 0, 0)
    in_specs = [x_spec, pl.BlockSpec((None, None, d, ed), w_idx), pl.BlockSpec((None, None, d, ed), w_idx),
                pl.BlockSpec((None, None, ed, d), w_idx)]
    args = [blk_expert, n_used, xg, w_gate, w_up, w_down]
    aliases = {}
    if out is not None:
        in_specs.append(pl.BlockSpec(memory_space=pl.ANY))
        args.append(out)
        aliases = {len(args) - 1: 0}
    return pl.pallas_call(
        functools.partial(_moe_kernel, blk0=blk0),
        grid_spec=pltpu.PrefetchScalarGridSpec(
            num_scalar_prefetch=2, grid=(rows // MOE_BLOCK,),
            in_specs=in_specs,
            out_specs=pl.BlockSpec((MOE_BLOCK, d), lambda b, be, nu: (blk0 + b, 0)),
            scratch_shapes=[pltpu.VMEM((d, ed), BF16), pltpu.VMEM((d, ed), BF16), pltpu.VMEM((ed, d), BF16)]),
        out_shape=jax.ShapeDtypeStruct((blk_expert.shape[0] * MOE_BLOCK, d), BF16),
        input_output_aliases=aliases,
        compiler_params=_params(1), name="moe_experts",
    )(*args)


def _shared_kernel(h_ref, wg_ref, wu_ref, wd_ref, o_ref):
    o_ref[...] = _swiglu(h_ref[...], wg_ref[...], wu_ref[...], wd_ref[...]).astype(o_ref.dtype)


def shared_expert(h, sw_gate, sw_up, sw_down):
    m, d = h.shape
    ed = sw_gate.shape[1]
    bm = _pick(m, 512)
    row_spec = pl.BlockSpec((bm, d), lambda i: (i, 0))
    full = lambda a, b: pl.BlockSpec((a, b), lambda i: (0, 0))
    return pl.pallas_call(
        _shared_kernel, grid=(m // bm,),
        in_specs=[row_spec, full(d, ed), full(d, ed), full(ed, d)],
        out_specs=row_spec,
        out_shape=jax.ShapeDtypeStruct((m, d), BF16),
        compiler_params=_params(1), name="shared_expert",
    )(h, sw_gate, sw_up, sw_down)


def _combine_kernel(s_ref, p_ref, w_ref, x_ref, gate_ref, o_ref):
    ffn = s_ref[...].astype(F32)
    w = w_ref[...]
    for k in range(TOP_K):
        ffn = ffn + w[:, k:k + 1] * p_ref[k].astype(F32)
    o_ref[...] = x_ref[...] + gate_ref[...] * ffn


def moe_combine_residual(stream, shared, picked, wts, row0, x, gate, tok0):
    m, d = x.shape
    n = picked.shape[1]
    bm = _row_block(stream, n, 256)
    assert row0 % bm == 0 and tok0 % bm == 0 and n % bm == 0
    off, off_all = tok0 // bm, (row0 + tok0) // bm
    row_spec = pl.BlockSpec((bm, d), lambda i: (i + off, 0))
    all_rows = lambda width: pl.BlockSpec((bm, width), lambda i: (i + off_all, 0))
    return pl.pallas_call(
        _combine_kernel, grid=(n // bm,),
        in_specs=[all_rows(d), pl.BlockSpec((TOP_K, bm, d), lambda i: (0, i, 0)), all_rows(TOP_K), row_spec,
                  _mod_spec(stream, bm, d, lambda i: (i + off, 0))],
        out_specs=row_spec,
        out_shape=jax.ShapeDtypeStruct((m, d), F32),
        input_output_aliases={3: 0},
        compiler_params=_params(1), name="moe_combine",
    )(shared, picked, wts, x, gate)


def _dest_kernel(start_ref, idx_ref, rank_ref, dest_ref):
    idx = idx_ref[...]

    def body(e, acc):
        return jnp.where(idx == e, start_ref[e], acc)

    dest_ref[...] = rank_ref[...] + lax.fori_loop(0, start_ref.shape[0], body, jnp.zeros_like(idx))


def assignment_rows(pad_start, idx, rank):
    return pl.pallas_call(
        _dest_kernel,
        in_specs=[pl.BlockSpec(memory_space=pltpu.SMEM), pl.BlockSpec(memory_space=pltpu.VMEM),
                  pl.BlockSpec(memory_space=pltpu.VMEM)],
        out_specs=pl.BlockSpec(memory_space=pltpu.VMEM),
        out_shape=jax.ShapeDtypeStruct(idx.shape, jnp.int32),
        name="assignment_rows",
    )(pad_start, idx, rank)


def moe_routed(h_all, idx, rank, counts, layer, w_gate, w_up, w_down):
    t, d = h_all.shape
    n_exp = counts.shape[0]
    n_assign = t * TOP_K
    counts = counts.reshape(n_exp).astype(jnp.int32)
    padded = (counts + MOE_BLOCK - 1) // MOE_BLOCK * MOE_BLOCK
    pad_end = jnp.cumsum(padded)
    pad_start = pad_end - padded
    dest = assignment_rows(pad_start, idx, rank).reshape(-1)
    n_blocks = -(-(n_assign + n_exp * (MOE_BLOCK - 1)) // MOE_BLOCK)
    n_blocks = -(-n_blocks // MOE_CHUNKS) * MOE_CHUNKS
    n_rows = n_blocks * MOE_BLOCK
    tok = jnp.tile(jnp.arange(t, dtype=jnp.int32), TOP_K)
    src_tok = jnp.arange(n_rows, dtype=jnp.int32) % t
    src_tok = src_tok.at[dest].add(tok - dest % t, unique_indices=True, mode="promise_in_bounds")
    blk_start = jnp.arange(n_blocks, dtype=jnp.int32) * MOE_BLOCK
    blk_expert = jnp.minimum(jnp.sum((pad_end[None, :] <= blk_start[:, None]).astype(jnp.int32), axis=1), n_exp - 1)
    n_used = pad_end[-1:] // MOE_BLOCK
    out = None
    chunk_rows = n_rows // MOE_CHUNKS
    for c in range(MOE_CHUNKS):
        xg = h_all.at[src_tok[c * chunk_rows:(c + 1) * chunk_rows]].get(mode="promise_in_bounds")
        out = moe_experts(xg, blk_expert, n_used, layer, w_gate, w_up, w_down, out, c * (n_blocks // MOE_CHUNKS))
    return out, dest.reshape(TOP_K, t)


def _stream(rows_per_batch):
    return dict(per_row=rows_per_batch < 1024, rows_per_batch=rows_per_batch)


def _mod_rows(stream, mod):
    b, d = mod.shape
    if not stream["per_row"]:
        return mod.reshape(b, 1, d)
    return jnp.repeat(mod, stream["rows_per_batch"], axis=0).reshape(1, -1, d)


def kernel(x_prompt, x_sample, cache_k_attn, cache_v_attn, c_prompt, c_sample, norm_mix_g, norm_ffn_g, ada_w, ada_b, attn_w_qkv, attn_w_o, attn_q_norm, attn_k_norm, attn_lambda_q1, attn_lambda_k1, attn_lambda_q2, attn_lambda_k2, attn_subln_g, sgu_w_in, sgu_ln_g, sgu_ln_b, sgu_w_s, sgu_b_s, sgu_w_o, moe_w_router, moe_b_router, moe_w_gate, moe_w_up, moe_w_down, shared_w_gate, shared_w_up, shared_w_down):
    bp, sp, d = x_prompt.shape
    bs, ss, _ = x_sample.shape
    depth = ada_w.shape[0]
    dv = 2 * ATTN_HEAD_DIM
    heads = d // dv
    mp, ms = bp * sp, bs * ss
    streams = (_stream(sp), _stream(ss))
    xs = [x_prompt.reshape(mp, d), x_sample.reshape(ms, d)]

    mod = ada_modulation(jnp.concatenate([c_prompt, c_sample], axis=0), ada_w, ada_b)
    slopes2 = (2.0 ** (-8.0 * jnp.arange(1, heads + 1, dtype=F32) / heads)) * LOG2E

    k_out, v_out, sgu_out = [[], []], [[], []], []
    for i in range(depth):
        j = i // N_MIXERS
        mods = []
        for si, (stream, rows) in enumerate(zip(streams, (slice(0, bp), slice(bp, bp + bs)))):
            mods.append([_mod_rows(stream, mod[i, rows, c * d:(c + 1) * d]) for c in range(6)])

        if i % N_MIXERS == 0:
            lam_init = 0.8 - 0.6 * math.exp(-0.3 * i)
            f = lambda a: a[j].astype(F32)
            lam = (jnp.exp(jnp.sum(f(attn_lambda_q1) * f(attn_lambda_k1)))
                   - jnp.exp(jnp.sum(f(attn_lambda_q2) * f(attn_lambda_k2))) + lam_init).reshape(1)
            for si, stream in enumerate(streams):
                h = modulate(stream, xs[si], norm_mix_g[i], mods[si][0], mods[si][1])
                q_bf, k32, k_bf, v32, v_bf = qkv_project(stream, h, attn_w_qkv[j], attn_q_norm[j], attn_k_norm[j])
                if si == 0:
                    o = attention_prompt(q_bf, k_bf, v_bf, bp, lam, slopes2, attn_subln_g[j], 1.0 - lam_init)
                else:
                    o = attention_sample(q_bf, k_bf, v_bf, cache_k_attn, cache_v_attn, j, lam, slopes2,
                                         attn_subln_g[j], 1.0 - lam_init)
                xs[si] = project_residual(stream, o, attn_w_o[j], xs[si], mods[si][2], "attn_out_proj")
                k_out[si].append(k32)
                v_out[si].append(v32)
        else:
            for si, stream in enumerate(streams):
                h = modulate(stream, xs[si], norm_mix_g[i], mods[si][0], mods[si][1])
                (z,) = matmul(h, sgu_w_in[j], bm=_row_block(stream, h.shape[0], 1024),
                              bn=_pick(sgu_w_in.shape[2], 1024),
                              out_dtypes=(BF16,), epilogue=_gelu_epilogue, name="sgu_in_proj")
                chunk_len = SGU_CHUNK if si == 0 else ss
                res = sgu_gate(z, sgu_ln_g[j], sgu_ln_b[j], sgu_w_s[j], sgu_b_s[j], chunk_len, write_v=(si == 1))
                if si == 1:
                    sgu_out.append(res[1])
                xs[si] = project_residual(stream, res[0], sgu_w_o[j], xs[si], mods[si][2], "sgu_out_proj")

        hs, routing = [], []
        counts = jnp.zeros((moe_w_router.shape[2], 1), F32)
        for si, stream in enumerate(streams):
            hf, idx, wts, rank, counts = modulate(stream, xs[si], norm_ffn_g[i], mods[si][3], mods[si][4],
                                                  router=(moe_w_router[i], moe_b_router[i], counts))
            hs.append(hf)
            routing.append((idx, wts, rank))
        idx, wts, rank = (jnp.concatenate(parts, axis=1) for parts in zip(*routing))
        h_all = jnp.concatenate(hs, axis=0)
        shared = shared_expert(h_all, *[w[i].astype(BF16) for w in (shared_w_gate, shared_w_up, shared_w_down)])
        out, dest = moe_routed(h_all, idx, rank, counts, i, moe_w_gate, moe_w_up, moe_w_down)
        for si, (stream, row0, m_rows) in enumerate(zip(streams, (0, mp), (mp, ms))):
            piece = _pick(m_rows, COMBINE_TOKENS)
            for tok0 in range(0, m_rows, piece):
                rows = dest[:, row0 + tok0:row0 + tok0 + piece].reshape(-1)
                picked = out.at[rows].get(mode="promise_in_bounds", unique_indices=True).reshape(TOP_K, piece, d)
                xs[si] = moe_combine_residual(stream, shared, picked, wts.T, row0, xs[si], mods[si][5], tok0)

    n_attn = len(k_out[0])
    k_prompt = jnp.stack(k_out[0]).reshape(n_attn, bp, sp, heads, 2, ATTN_HEAD_DIM)
    v_prompt = jnp.stack(v_out[0]).reshape(n_attn, bp, sp, heads, dv)
    k_sample = jnp.stack(k_out[1]).reshape(n_attn, bs, ss, heads, 2, ATTN_HEAD_DIM)
    v_sample = jnp.stack(v_out[1]).reshape(n_attn, bs, ss, heads, dv)
    sgu_v = jnp.stack(sgu_out).reshape(len(sgu_out), bs, ss, -1)
    return (xs[0].reshape(bp, sp, d), xs[1].reshape(bs, ss, d), k_prompt, v_prompt, k_sample, v_sample, sgu_v)
```

```python
import functools
import math

import jax
import jax.numpy as jnp
from jax import lax
from jax.experimental import pallas as pl
from jax.experimental.pallas import tpu as pltpu

F32 = jnp.float32
BF16 = jnp.bfloat16

EPS = 1e-6
CHUNK = 64
N_MIXERS = 2
ATTN_HEAD_DIM = 128
SGU_GROUPS = 8
SGU_CHUNK = 128
TOP_K = 8
N_EXPERT_GROUPS = 8
TOPK_GROUPS = 4
ROUTED_SCALE = 2.5
LOG2E = 1.4426950408889634
NEG_BIG = -1e30

VMEM_LIMIT = 56 * 1024 * 1024
MOE_BLOCK = 512
MOE_CHUNKS = 8
SGU_MIX_ROWS = 256
COMBINE_TOKENS = 2048
ATTN_ROW_CHUNK = 64
LANES = 128


def _params(n_axes):
    return pltpu.CompilerParams(dimension_semantics=("arbitrary",) * n_axes, vmem_limit_bytes=VMEM_LIMIT)


def _pick(n, pref):
    if n <= pref:
        return n
    b = pref
    while n % b:
        b //= 2
    return b


def _ada_kernel(c_ref, w_ref, b_ref, o_ref):
    c = c_ref[...]
    x = (c * jax.nn.sigmoid(c)).astype(BF16)
    o_ref[...] = jnp.dot(x, w_ref[...].astype(BF16), preferred_element_type=F32) + b_ref[...]


def ada_modulation(c, ada_w, ada_b):
    n_layers, d, n = ada_w.shape
    r = c.shape[0]
    bn = _pick(n, 1024)
    return pl.pallas_call(
        _ada_kernel,
        grid=(n_layers, n // bn),
        in_specs=[
            pl.BlockSpec((r, d), lambda l, j: (0, 0)),
            pl.BlockSpec((None, d, bn), lambda l, j: (l, 0, j)),
            pl.BlockSpec((None, 1, bn), lambda l, j: (l, 0, j)),
        ],
        out_specs=pl.BlockSpec((None, r, bn), lambda l, j: (l, 0, j)),
        out_shape=jax.ShapeDtypeStruct((n_layers, r, n), F32),
        compiler_params=_params(2),
        name="ada_modulation",
    )(c, ada_w, ada_b.reshape(n_layers, 1, n))


def _modulated(x, g, shift, scale):
    y = x * lax.rsqrt(jnp.mean(x * x, axis=-1, keepdims=True) + EPS)
    return y * g * (1.0 + scale) + shift


def _modulate_kernel(x_ref, g_ref, sh_ref, sc_ref, o_ref):
    o_ref[...] = _modulated(x_ref[...], g_ref[...], sh_ref[...], sc_ref[...]).astype(o_ref.dtype)


def _split3(x):
    hi = x.astype(BF16)
    lo = (x - hi.astype(F32)).astype(BF16)
    return hi, lo


def _first_max(x, ids, n_ids, axes):
    mx = x
    for ax in axes:
        mx = jnp.max(mx, axis=ax, keepdims=True)
    arg = jnp.where(x == mx, ids, n_ids)
    for ax in axes:
        arg = jnp.min(arg, axis=ax, keepdims=True)
    return mx, arg


def _sum_axes(x, axes):
    for ax in axes:
        x = jnp.sum(x, axis=ax, keepdims=True)
    return x


def _modulate_router_kernel(x_ref, g_ref, sh_ref, sc_ref, wrt_ref, br_ref, cin_ref, *refs, n_prev):
    o_ref, idx_ref, wts_ref, rank_ref, cnt_ref, carry = refs[n_prev:]
    h = _modulated(x_ref[...], g_ref[...], sh_ref[...], sc_ref[...])
    o_ref[...] = h.astype(o_ref.dtype)
    bm = h.shape[0]
    n_exp = wrt_ref.shape[0]
    per = n_exp // N_EXPERT_GROUPS
    grp_shape = (N_EXPERT_GROUPS, per, bm)

    h_hi, h_lo = _split3(h)
    w_hi, w_lo = _split3(wrt_ref[...])
    nt = (((1,), (1,)), ((), ()))
    logits = (lax.dot_general(w_hi, h_hi, nt, preferred_element_type=F32)
              + lax.dot_general(w_lo, h_hi, nt, preferred_element_type=F32)
              + lax.dot_general(w_hi, h_lo, nt, preferred_element_type=F32))
    s = jax.nn.sigmoid(logits)
    s3 = s.reshape(grp_shape)
    sb3 = (s + br_ref[...]).reshape(grp_shape)

    sub = lax.broadcasted_iota(jnp.int32, grp_shape, 1)
    gid = lax.broadcasted_iota(jnp.int32, (N_EXPERT_GROUPS, 1, bm), 0)
    eid = lax.broadcasted_iota(jnp.int32, grp_shape, 0) * per + sub

    m1, i1 = _first_max(sb3, sub, per, (1,))
    m2 = jnp.max(jnp.where(sub == i1, -jnp.inf, sb3), axis=1, keepdims=True)
    work = m1 + m2
    chosen = jnp.zeros_like(work)
    for _ in range(TOPK_GROUPS):
        _, gi = _first_max(work, gid, N_EXPERT_GROUPS, (0,))
        chosen = jnp.where(gid == gi, 1.0, chosen)
        work = jnp.where(gid == gi, -jnp.inf, work)
    sel = jnp.where(chosen > 0.0, sb3, -jnp.inf)

    hits, ids, raw = [], [], []
    for _ in range(TOP_K):
        _, ei = _first_max(sel, eid, n_exp, (1, 0))
        hit = eid == ei
        raw.append(_sum_axes(jnp.where(hit, s3, 0.0), (1, 0)))
        sel = jnp.where(hit, -jnp.inf, sel)
        hits.append(hit)
        ids.append(ei)
    total = sum(raw)

    member = sum(jnp.where(hit, 1.0, 0.0) for hit in hits).reshape(n_exp, bm)
    r_i = lax.broadcasted_iota(jnp.int32, (bm, bm), 0)
    c_i = lax.broadcasted_iota(jnp.int32, (bm, bm), 1)
    before = jnp.where(r_i < c_i, 1.0, 0.0).astype(BF16)

    @pl.when(pl.program_id(0) == 0)
    def _():
        carry[...] = cin_ref[...]

    rank_all = (jnp.dot(member.astype(BF16), before, preferred_element_type=F32) + carry[...]).reshape(grp_shape)
    carry[...] += jnp.sum(member, axis=1, keepdims=True)
    cnt_ref[...] = carry[...]
    for k in range(TOP_K):
        idx_ref[k:k + 1, :] = ids[k].reshape(1, bm)
        wts_ref[k:k + 1, :] = (raw[k] / total * ROUTED_SCALE).reshape(1, bm)
        rank_ref[k:k + 1, :] = _sum_axes(jnp.where(hits[k], rank_all, 0.0), (1, 0)).reshape(1, bm).astype(jnp.int32)


def _row_block(stream, m, pref):
    return _pick(m if stream["per_row"] else stream["rows_per_batch"], pref)


def _mod_spec(stream, bm, bn, ij):
    if stream["per_row"]:
        return pl.BlockSpec((None, bm, bn), lambda *g: (0,) + tuple(ij(*g)))
    bpg = stream["rows_per_batch"] // bm
    return pl.BlockSpec((None, 1, bn), lambda *g: (ij(*g)[0] // bpg, 0, ij(*g)[1]))


def modulate(stream, x, g, shift, scale, router=None):
    m, d = x.shape
    bm = _row_block(stream, m, 512)
    row_spec = pl.BlockSpec((bm, d), lambda i: (i, 0))
    mod_spec = _mod_spec(stream, bm, d, lambda i: (i, 0))
    g_spec = pl.BlockSpec((1, d), lambda i: (0, 0))
    if router is None:
        return pl.pallas_call(
            _modulate_kernel, grid=(m // bm,),
            in_specs=[row_spec, g_spec, mod_spec, mod_spec], out_specs=row_spec,
            out_shape=jax.ShapeDtypeStruct((m, d), BF16),
            compiler_params=_params(1), name="modulate",
        )(x, g.reshape(1, d), shift, scale)
    w_router, b_router, counts_in, t_all, row0, prev = router
    assert row0 % bm == 0
    off = row0 // bm
    e = w_router.shape[1]
    out_rows = pl.BlockSpec((bm, d), lambda i: (i + off, 0))
    tok_spec = pl.BlockSpec((TOP_K, bm), lambda i: (0, i + off))
    cnt_spec = pl.BlockSpec((e, 1), lambda i: (0, 0))
    n_in = 7
    return pl.pallas_call(
        functools.partial(_modulate_router_kernel, n_prev=len(prev)), grid=(m // bm,),
        in_specs=[row_spec, g_spec, mod_spec, mod_spec, pl.BlockSpec((e, d), lambda i: (0, 0)), cnt_spec, cnt_spec]
        + [pl.BlockSpec(memory_space=pl.ANY)] * len(prev),
        out_specs=[out_rows, tok_spec, tok_spec, tok_spec, cnt_spec],
        out_shape=[jax.ShapeDtypeStruct((t_all, d), BF16), jax.ShapeDtypeStruct((TOP_K, t_all), jnp.int32),
                   jax.ShapeDtypeStruct((TOP_K, t_all), F32), jax.ShapeDtypeStruct((TOP_K, t_all), jnp.int32),
                   jax.ShapeDtypeStruct((e, 1), F32)],
        input_output_aliases={n_in + k: k for k in range(len(prev))},
        scratch_shapes=[pltpu.VMEM((e, 1), F32)],
        compiler_params=_params(1), name="modulate_router",
    )(x, g.reshape(1, d), shift, scale, w_router.T, b_router.astype(F32).reshape(e, 1), counts_in, *prev)


def _mm_kernel(x_ref, w_ref, *refs, n_extra, epilogue):
    extra, outs, w_bf = refs[:n_extra], refs[n_extra:-1], refs[-1]

    @pl.when(pl.program_id(1) == 0)
    def _():
        w_bf[...] = w_ref[...].astype(BF16)

    acc = jnp.dot(x_ref[...], w_bf[...], preferred_element_type=F32)
    epilogue(acc, extra, outs)


def matmul(x, w, *, bm, bn, col_block_off=0, n_cols=None, extra=(), extra_specs=(), out_dtypes, epilogue, name):
    m, k = x.shape
    n = w.shape[1] if n_cols is None else n_cols
    grid = (n // bn, m // bm)
    out_spec = pl.BlockSpec((bm, bn), lambda j, i: (i, j))
    return pl.pallas_call(
        functools.partial(_mm_kernel, n_extra=len(extra), epilogue=epilogue),
        grid=grid,
        in_specs=[pl.BlockSpec((bm, k), lambda j, i: (i, 0)),
                  pl.BlockSpec((k, bn), lambda j, i: (0, j + col_block_off))] + list(extra_specs),
        out_specs=[out_spec] * len(out_dtypes),
        out_shape=[jax.ShapeDtypeStruct((m, n), dt) for dt in out_dtypes],
        scratch_shapes=[pltpu.VMEM((k, bn), BF16)],
        compiler_params=_params(2), name=name,
    )(x, w, *extra)


def _head_rms(acc, gain, post_scale):
    pieces = []
    for c in range(acc.shape[1] // ATTN_HEAD_DIM):
        seg = acc[:, c * ATTN_HEAD_DIM:(c + 1) * ATTN_HEAD_DIM]
        y = seg * lax.rsqrt(jnp.mean(seg * seg, axis=-1, keepdims=True) + EPS)
        pieces.append(y * (gain * post_scale))
    return jnp.concatenate(pieces, axis=-1)


def _q_epilogue(acc, extra, outs):
    outs[0][...] = _head_rms(acc, extra[0][...], ATTN_HEAD_DIM ** -0.5 * LOG2E).astype(BF16)


def _v_epilogue(acc, extra, outs):
    outs[0][...] = acc
    outs[1][...] = acc.astype(BF16)


def _gelu_epilogue(acc, extra, outs):
    outs[0][...] = (0.5 * acc * (1.0 + lax.erf(acc * (2.0 ** -0.5)))).astype(BF16)


def _residual_epilogue(acc, extra, outs):
    x_ref, gate_ref = extra
    outs[0][...] = x_ref[...] + gate_ref[...] * acc


def _k_proj_kernel(h_ref, w_ref, g_ref, k32_ref, kbf_ref):
    acc = jnp.dot(h_ref[...], w_ref[...], preferred_element_type=F32)
    bm = acc.shape[0]
    groups = acc.shape[1] // ATTN_HEAD_DIM
    gain = g_ref[...]
    for g in range(groups):
        cols = slice(g * ATTN_HEAD_DIM, (g + 1) * ATTN_HEAD_DIM)
        seg = acc[:, cols]
        kn = seg * lax.rsqrt(jnp.mean(seg * seg, axis=-1, keepdims=True) + EPS) * gain
        k32_ref[pl.ds(g, bm, stride=groups), :] = kn
        kbf_ref[:, cols] = kn.astype(BF16)


def k_project(stream, h, w_k, k_norm):
    m, d = h.shape
    n = w_k.shape[1]
    groups = n // ATTN_HEAD_DIM
    bm = _row_block(stream, m, 512)
    return pl.pallas_call(
        _k_proj_kernel, grid=(m // bm,),
        in_specs=[pl.BlockSpec((bm, d), lambda i: (i, 0)), pl.BlockSpec((d, n), lambda i: (0, 0)),
                  pl.BlockSpec((1, ATTN_HEAD_DIM), lambda i: (0, 0))],
        out_specs=[pl.BlockSpec((bm * groups, ATTN_HEAD_DIM), lambda i: (i, 0)),
                   pl.BlockSpec((bm, n), lambda i: (i, 0))],
        out_shape=[jax.ShapeDtypeStruct((m * groups, ATTN_HEAD_DIM), F32), jax.ShapeDtypeStruct((m, n), BF16)],
        compiler_params=_params(1), name="k_proj",
    )(h, w_k, k_norm.reshape(1, -1))


def qkv_project(stream, h, w_qkv, q_norm, k_norm):
    m, d = h.shape
    bm = _row_block(stream, m, 1024)
    bn = _pick(d, 1024)
    nb = d // bn
    gain_spec = pl.BlockSpec((1, ATTN_HEAD_DIM), lambda j, i: (0, 0))
    common = dict(bm=bm, bn=bn, n_cols=d)
    (q_bf,) = matmul(h, w_qkv, col_block_off=0, extra=(q_norm.reshape(1, -1),), extra_specs=(gain_spec,),
                     out_dtypes=(BF16,), epilogue=_q_epilogue, name="q_proj", **common)
    k32, k_bf = k_project(stream, h, w_qkv[:, d:2 * d].astype(BF16), k_norm)
    v32, v_bf = matmul(h, w_qkv, col_block_off=2 * nb, out_dtypes=(F32, BF16), epilogue=_v_epilogue,
                       name="v_proj", **common)
    return q_bf, k32, k_bf, v32, v_bf


def project_residual(stream, a, w, x, gate, name):
    k, n = w.shape
    deep = k > 4096
    bm = _row_block(stream, a.shape[0], 512 if deep else 1024)
    bn = _pick(n, 512 if deep else 1024)
    (out,) = matmul(a, w, bm=bm, bn=bn, extra=(x, gate),
                    extra_specs=(pl.BlockSpec((bm, bn), lambda j, i: (i, j)),
                                 _mod_spec(stream, bm, bn, lambda j, i: (i, j))),
                    out_dtypes=(F32,), epilogue=_residual_epilogue, name=name)
    return out


def _finish_heads(acc1, l1, acc2, l2, lam, g, post_scale):
    o = acc1 / l1 - lam * (acc2 / l2)
    o = o * lax.rsqrt(jnp.mean(o * o, axis=-1, keepdims=True) + EPS)
    return o * (g * post_scale)


def _attn_prompt_kernel(lam_ref, slope_ref, q_ref, k_ref, v_ref, g_ref, o_ref,
                        s_scr, p_scr, qa_scr, acc_scr, m_scr, l_scr, a_scr, *, bq, post_scale):
    hd = ATTN_HEAD_DIM
    rc = ATTN_ROW_CHUNK
    qi = pl.program_id(2)
    hh = pl.program_id(1)
    sl2 = slope_ref[hh]

    m_scr[...] = jnp.full(m_scr.shape, NEG_BIG, F32)
    l_scr[...] = jnp.zeros(l_scr.shape, F32)
    acc_scr[...] = jnp.zeros(acc_scr.shape, F32)

    lane = lax.broadcasted_iota(jnp.int32, (bq, LANES), 1)
    key_c = lax.broadcasted_iota(jnp.int32, (bq, LANES), 0)
    k_aug = jnp.where(lane < 3, key_c % 256, jnp.where(lane < 6, key_c // 256, 0)).astype(F32).astype(BF16)
    rest = jnp.full((bq, LANES), sl2, F32)
    piece = jnp.zeros((bq, LANES), F32)
    for i in range(3):
        part = rest.astype(BF16).astype(F32)
        piece = jnp.where(lane % 3 == i, part, piece)
        rest = rest - part
    q_aug = jnp.where(lane < 3, piece, jnp.where(lane < 6, 256.0 * piece, 0.0)).astype(BF16)
    for mi in range(2):
        qa_scr[mi, :, :hd] = q_ref[:, mi * hd:(mi + 1) * hd]
        qa_scr[mi, :, hd:] = q_aug

    def scores(kb, slot):
        k0 = pl.multiple_of(kb * bq, bq)
        for mi in range(2):
            k_blk = jnp.concatenate([k_ref[pl.ds(k0, bq), mi * hd:(mi + 1) * hd], k_aug], axis=1)
            s_scr[2 * slot + mi] = lax.dot_general(qa_scr[mi], k_blk, (((1,), (1,)), ((), ())),
                                                   preferred_element_type=F32)

    def softmax(kb, slot, diag):
        shift_blk = sl2 * ((kb - qi) * bq + jnp.zeros((1, LANES), jnp.int32)).astype(F32)
        for mi in range(2):
            s_map = s_scr.at[2 * slot + mi]
            p_map = p_scr.at[2 * slot + mi]
            for c in range(bq // rc):
                rows = slice(c * rc, (c + 1) * rc)
                ncol = min(bq, -(-((c + 1) * rc) // CHUNK) * CHUNK) if diag else bq
                lane_tiles = [slice(t * LANES, (t + 1) * LANES) for t in range(-(-ncol // LANES))]

                def tile(lanes, t):
                    s_t = s_map[rows, lanes]
                    if not diag:
                        return s_t
                    r_i = lax.broadcasted_iota(jnp.int32, (rc, LANES), 0) + c * rc
                    c_i = lax.broadcasted_iota(jnp.int32, (rc, LANES), 1) + t * LANES
                    s_t = s_t + (2.0 * sl2) * jnp.minimum(r_i - c_i, 0).astype(F32)
                    return jnp.where((c_i // CHUNK) <= (r_i // CHUNK), s_t, NEG_BIG)

                tiles = [tile(lanes, t) for t, lanes in enumerate(lane_tiles)]
                m_old = m_scr[mi, rows, :]
                blk_max = jnp.max(functools.reduce(jnp.maximum, tiles), axis=-1, keepdims=True)
                m_new = jnp.maximum(m_old, blk_max + shift_blk)
                alpha = jnp.exp2(m_old - m_new)
                m_scr[mi, rows, :] = m_new
                a_scr[2 * slot + mi, rows, :] = alpha
                if not diag:
                    tiles = [tile(lanes, t) for t, lanes in enumerate(lane_tiles)]
                off = m_new - shift_blk
                ps = [jnp.exp2(s_t - off) for s_t in tiles]
                l_scr[mi, rows, :] = alpha * l_scr[mi, rows, :] + functools.reduce(jnp.add, ps)
                for t, p_t in enumerate(ps):
                    p_map[rows, t * LANES:(t + 1) * LANES] = p_t.astype(BF16)
                if len(ps) * LANES < bq:
                    p_map[rows, len(ps) * LANES:] = jnp.zeros((rc, bq - len(ps) * LANES), BF16)

    def weighted_values(kb, slot):
        k0 = pl.multiple_of(jnp.maximum(kb, 0) * bq, bq)
        for mi in range(2):
            pv = jnp.dot(p_scr[2 * slot + mi], v_ref[pl.ds(k0, bq), :], preferred_element_type=F32)
            for t in range(pv.shape[1] // LANES):
                lanes = slice(t * LANES, (t + 1) * LANES)
                acc_scr[mi, :, lanes] = a_scr[2 * slot + mi] * acc_scr[mi, :, lanes] + pv[:, lanes]

    p_scr[2] = jnp.zeros((bq, bq), BF16)
    p_scr[3] = jnp.zeros((bq, bq), BF16)
    a_scr[2] = jnp.ones((bq, LANES), F32)
    a_scr[3] = jnp.ones((bq, LANES), F32)

    def pair(j2, carry):
        kb = 2 * j2
        scores(kb + 1, 1)
        weighted_values(kb - 1, 1)
        softmax(kb, 0, False)
        scores(kb + 2, 0)
        weighted_values(kb, 0)
        softmax(kb + 1, 1, False)
        return carry

    scores(0, 0)
    lax.fori_loop(0, qi // 2, pair, 0)

    @pl.when(qi % 2 == 1)
    def _():
        scores(qi, 1)
        weighted_values(qi - 2, 1)
        softmax(qi - 1, 0, False)
        weighted_values(qi - 1, 0)
        softmax(qi, 1, True)
        weighted_values(qi, 1)

    @pl.when(qi % 2 == 0)
    def _():
        weighted_values(qi - 1, 1)
        softmax(qi, 0, True)
        weighted_values(qi, 0)
    l1 = jnp.sum(l_scr[0], axis=-1, keepdims=True)
    l2 = jnp.sum(l_scr[1], axis=-1, keepdims=True)
    o_ref[...] = _finish_heads(acc_scr[0], l1, acc_scr[1], l2, lam_ref[0], g_ref[...], post_scale).astype(o_ref.dtype)


def attention_prompt(q_bf, k_bf, v_bf, batch, lam, slopes2, subln_g, post_scale):
    m, width = q_bf.shape
    s = m // batch
    dv = 2 * ATTN_HEAD_DIM
    heads = width // dv
    bq = _pick(s, 512)
    assert bq % CHUNK == 0
    nq = s // bq
    kv_spec = pl.BlockSpec((s, dv), lambda b, h, i: (b, h))
    q_spec = pl.BlockSpec((bq, dv), lambda b, h, i: (b * nq + i, h))
    smem = pl.BlockSpec(memory_space=pltpu.SMEM)
    return pl.pallas_call(
        functools.partial(_attn_prompt_kernel, bq=bq, post_scale=post_scale),
        grid=(batch, heads, nq),
        in_specs=[smem, smem, q_spec, kv_spec, kv_spec, pl.BlockSpec((1, dv), lambda b, h, i: (0, 0))],
        out_specs=q_spec,
        out_shape=jax.ShapeDtypeStruct((m, width), BF16),
        scratch_shapes=[pltpu.VMEM((4, bq, bq), F32), pltpu.VMEM((4, bq, bq), BF16), pltpu.VMEM((2, bq, dv), BF16),
                        pltpu.VMEM((2, bq, dv), F32), pltpu.VMEM((2, bq, LANES), F32),
                        pltpu.VMEM((2, bq, LANES), F32), pltpu.VMEM((4, bq, LANES), F32)],
        compiler_params=_params(3), name="attention_prompt",
    )(lam, slopes2, q_bf, k_bf, v_bf, subln_g.reshape(1, dv))


def _attn_sample_kernel(lam_ref, slope_ref, q_ref, kn_ref, vn_ref, ck_ref, cv_ref, g_ref, o_ref, *, post_scale):
    hd = ATTN_HEAD_DIM
    dv = 2 * hd
    lam = lam_ref[0]
    t = q_ref.shape[0]
    heads = q_ref.shape[1] // dv
    past = ck_ref.shape[0] // (2 * heads)

    def dist_mask(n_keys, key0):
        q_pos = past + lax.broadcasted_iota(jnp.int32, (t, n_keys), 0)
        k_pos = key0 + lax.broadcasted_iota(jnp.int32, (t, n_keys), 1)
        return jnp.abs(q_pos - k_pos).astype(F32), (k_pos // CHUNK) <= (q_pos // CHUNK)

    dist_p, vis_p = dist_mask(past, 0)
    dist_n, vis_n = dist_mask(t, past)
    dn = (((1,), (1,)), ((), ()))
    for h in range(heads):
        cols = slice(h * dv, (h + 1) * dv)
        sl2 = slope_ref[h]
        v_past = jnp.concatenate(
            [cv_ref[pl.ds(c * heads + h, past, stride=heads * (dv // LANES)), :] for c in range(dv // LANES)],
            axis=1).astype(BF16)
        res = []
        for mi in range(2):
            qm = q_ref[:, h * dv + mi * hd:h * dv + (mi + 1) * hd]
            k_past = ck_ref[pl.ds(2 * h + mi, past, stride=2 * heads), :].astype(BF16)
            s_p = lax.dot_general(qm, k_past, dn, preferred_element_type=F32)
            s_n = lax.dot_general(qm, kn_ref[:, h * dv + mi * hd:h * dv + (mi + 1) * hd], dn,
                                  preferred_element_type=F32)
            s_p = jnp.where(vis_p, s_p - sl2 * dist_p, NEG_BIG)
            s_n = jnp.where(vis_n, s_n - sl2 * dist_n, NEG_BIG)
            m = jnp.maximum(jnp.max(s_p, axis=-1, keepdims=True), jnp.max(s_n, axis=-1, keepdims=True))
            p_p = jnp.exp2(s_p - m)
            p_n = jnp.exp2(s_n - m)
            l = jnp.sum(p_p, axis=-1, keepdims=True) + jnp.sum(p_n, axis=-1, keepdims=True)
            acc = (jnp.dot(p_p.astype(BF16), v_past, preferred_element_type=F32)
                   + jnp.dot(p_n.astype(BF16), vn_ref[:, cols], preferred_element_type=F32))
            res += [acc, l]
        o_ref[:, cols] = _finish_heads(res[0], res[1], res[2], res[3], lam, g_ref[...],
                                       post_scale).astype(o_ref.dtype)


def attention_sample(q_bf, k_bf, v_bf, cache_k, cache_v, layer, lam, slopes2, subln_g, post_scale):
    _, batch, past, heads, _, hd = cache_k.shape
    m, width = q_bf.shape
    t = m // batch
    dv = 2 * hd
    new_spec = pl.BlockSpec((t, width), lambda b: (b, 0))
    n_layers = cache_k.shape[0]
    cache_k = cache_k.reshape(n_layers, batch, past * heads * 2, hd)
    cache_v = cache_v.reshape(n_layers, batch, past, heads, dv // LANES, LANES)
    cache_v = cache_v.transpose(0, 1, 2, 4, 3, 5).reshape(n_layers, batch, past * (dv // LANES) * heads, LANES)
    ck_spec = pl.BlockSpec((None, None, past * heads * 2, hd), lambda b: (layer, b, 0, 0))
    cv_spec = pl.BlockSpec((None, None, past * (dv // LANES) * heads, LANES), lambda b: (layer, b, 0, 0))
    smem = pl.BlockSpec(memory_space=pltpu.SMEM)
    return pl.pallas_call(
        functools.partial(_attn_sample_kernel, post_scale=post_scale),
        grid=(batch,),
        in_specs=[smem, smem, new_spec, new_spec, new_spec, ck_spec, cv_spec,
                  pl.BlockSpec((1, dv), lambda b: (0, 0))],
        out_specs=new_spec,
        out_shape=jax.ShapeDtypeStruct((m, width), BF16),
        compiler_params=_params(1), name="attention_sample",
    )(lam, slopes2, q_bf, k_bf, v_bf, cache_k, cache_v, subln_g.reshape(1, dv))


def _sgu_gate_kernel(u_ref, v_ref, g_ref, b_ref, w_ref, bs_ref, *out_refs, chunk_len, write_v):
    bt, width = u_ref.shape
    gd = width // SGU_GROUPS
    v = v_ref[...].astype(F32)
    mu = jnp.mean(v, axis=-1, keepdims=True)
    vc = v - mu
    vn = vc * lax.rsqrt(jnp.mean(vc * vc, axis=-1, keepdims=True) + EPS) * g_ref[...] + b_ref[...]
    if write_v:
        out_refs[1][...] = vn
    vn_bf = vn.astype(BF16)
    mr = w_ref.shape[1]
    row = lax.broadcasted_iota(jnp.int32, (mr, mr), 0)
    col = lax.broadcasted_iota(jnp.int32, (mr, mr), 1)
    keep = ((row // chunk_len) == (col // chunk_len)) & ((col % chunk_len) <= (row % chunk_len))
    for gi in range(SGU_GROUPS):
        w = jnp.where(keep, w_ref[gi], 0.0).astype(BF16)
        bias = bs_ref[gi]
        for c in range(bt // mr):
            rows = slice(c * mr, (c + 1) * mr)
            cols = slice(gi * gd, (gi + 1) * gd)
            mixed = jnp.dot(w, vn_bf[rows, cols], preferred_element_type=F32) + bias
            out_refs[0][rows, cols] = (u_ref[rows, cols].astype(F32) * mixed).astype(BF16)


def sgu_gate(z, ln_g, ln_b, w_s, b_s, chunk_len, write_v):
    m, two_w = z.shape
    width = two_w // 2
    bt = _pick(m, SGU_MIX_ROWS)
    reps = bt // chunk_len
    w_t = jnp.tile(w_s[:, :chunk_len, :chunk_len], (1, reps, reps))
    b_t = jnp.tile(b_s[:, :chunk_len], (1, reps))[:, :, None]
    row_spec = lambda c: pl.BlockSpec((bt, width), lambda i, c=c: (i, c))
    vec_spec = pl.BlockSpec((1, width), lambda i: (0, 0))
    out_specs = [row_spec(0)]
    out_shape = [jax.ShapeDtypeStruct((m, width), BF16)]
    if write_v:
        out_specs.append(row_spec(0))
        out_shape.append(jax.ShapeDtypeStruct((m, width), F32))
    return pl.pallas_call(
        functools.partial(_sgu_gate_kernel, chunk_len=chunk_len, write_v=write_v),
        grid=(m // bt,),
        in_specs=[row_spec(0), row_spec(1), vec_spec, vec_spec,
                  pl.BlockSpec((SGU_GROUPS, bt, bt), lambda i: (0, 0, 0)),
                  pl.BlockSpec((SGU_GROUPS, bt, 1), lambda i: (0, 0, 0))],
        out_specs=out_specs, out_shape=out_shape,
        compiler_params=_params(1), name="sgu_gate",
    )(z, z, ln_g.reshape(1, width), ln_b.reshape(1, width), w_t, b_t)


def _swiglu(x, wg, wu, wd):
    g = jnp.dot(x, wg, preferred_element_type=F32)
    u = jnp.dot(x, wu, preferred_element_type=F32)
    h = (g * jax.nn.sigmoid(g) * u).astype(BF16)
    return jnp.dot(h, wd, preferred_element_type=F32)


def _moe_kernel(blk_e_ref, n_used_ref, x_ref, wg_ref, wu_ref, wd_ref, *refs, blk0):
    o_ref, wg_bf, wu_bf, wd_bf = refs[-4:]
    b = pl.program_id(0)
    e = blk_e_ref[blk0 + b]
    e_prev = blk_e_ref[blk0 + jnp.maximum(b - 1, 0)]

    @pl.when((b == 0) | (e != e_prev))
    def _():
        wg_bf[...] = wg_ref[...].astype(BF16)
        wu_bf[...] = wu_ref[...].astype(BF16)
        wd_bf[...] = wd_ref[...].astype(BF16)

    @pl.when(blk0 + b < n_used_ref[0])
    def _():
        o_ref[...] = _swiglu(x_ref[...], wg_bf[...], wu_bf[...], wd_bf[...]).astype(o_ref.dtype)

    @pl.when(blk0 + b >= n_used_ref[0])
    def _():
        o_ref[...] = jnp.zeros_like(o_ref)


def moe_experts(xg, blk_expert, n_used, layer, w_gate, w_up, w_down, out, blk0):
    rows, d = xg.shape
    ed = w_gate.shape[3]
    x_spec = pl.BlockSpec((MOE_BLOCK, d), lambda b, be, nu: (b, 0))
    w_idx = lambda b, be, nu: (layer, be[blk0 + b], 0, 0)
    in_specs = [x_spec, pl.BlockSpec((None, None, d, ed), w_idx), pl.BlockSpec((None, None, d, ed), w_idx),
                pl.BlockSpec((None, None, ed, d), w_idx)]
    args = [blk_expert, n_used, xg, w_gate, w_up, w_down]
    aliases = {}
    if out is not None:
        in_specs.append(pl.BlockSpec(memory_space=pl.ANY))
        args.append(out)
        aliases = {len(args) - 1: 0}
    return pl.pallas_call(
        functools.partial(_moe_kernel, blk0=blk0),
        grid_spec=pltpu.PrefetchScalarGridSpec(
            num_scalar_prefetch=2, grid=(rows // MOE_BLOCK,),
            in_specs=in_specs,
            out_specs=pl.BlockSpec((MOE_BLOCK, d), lambda b, be, nu: (blk0 + b, 0)),
            scratch_shapes=[pltpu.VMEM((d, ed), BF16), pltpu.VMEM((d, ed), BF16), pltpu.VMEM((ed, d), BF16)]),
        out_shape=jax.ShapeDtypeStruct((blk_expert.shape[0] * MOE_BLOCK, d), BF16),
        input_output_aliases=aliases,
        compiler_params=_params(1), name="moe_experts",
    )(*args)


def _shared_kernel(h_ref, wg_ref, wu_ref, wd_ref, o_ref):
    o_ref[...] = _swiglu(h_ref[...], wg_ref[...], wu_ref[...], wd_ref[...]).astype(o_ref.dtype)


def shared_expert(h, sw_gate, sw_up, sw_down):
    m, d = h.shape
    ed = sw_gate.shape[1]
    bm = _pick(m, 512)
    row_spec = pl.BlockSpec((bm, d), lambda i: (i, 0))
    full = lambda a, b: pl.BlockSpec((a, b), lambda i: (0, 0))
    return pl.pallas_call(
        _shared_kernel, grid=(m // bm,),
        in_specs=[row_spec, full(d, ed), full(d, ed), full(ed, d)],
        out_specs=row_spec,
        out_shape=jax.ShapeDtypeStruct((m, d), BF16),
        compiler_params=_params(1), name="shared_expert",
    )(h, sw_gate, sw_up, sw_down)


def _combine_kernel(s_ref, p_ref, w_ref, x_ref, gate_ref, o_ref):
    ffn = s_ref[...].astype(F32)
    w = w_ref[...]
    for k in range(TOP_K):
        ffn = ffn + w[:, k:k + 1] * p_ref[k].astype(F32)
    o_ref[...] = x_ref[...] + gate_ref[...] * ffn


def moe_combine_residual(stream, shared, picked, wts, row0, x, gate, tok0):
    m, d = x.shape
    n = picked.shape[1]
    bm = _row_block(stream, n, 256)
    assert row0 % bm == 0 and tok0 % bm == 0 and n % bm == 0
    off, off_all = tok0 // bm, (row0 + tok0) // bm
    row_spec = pl.BlockSpec((bm, d), lambda i: (i + off, 0))
    all_rows = lambda width: pl.BlockSpec((bm, width), lambda i: (i + off_all, 0))
    return pl.pallas_call(
        _combine_kernel, grid=(n // bm,),
        in_specs=[all_rows(d), pl.BlockSpec((TOP_K, bm, d), lambda i: (0, i, 0)), all_rows(TOP_K), row_spec,
                  _mod_spec(stream, bm, d, lambda i: (i + off, 0))],
        out_specs=row_spec,
        out_shape=jax.ShapeDtypeStruct((m, d), F32),
        input_output_aliases={3: 0},
        compiler_params=_params(1), name="moe_combine",
    )(shared, picked, wts, x, gate)


def _dest_kernel(start_ref, idx_ref, rank_ref, dest_ref):
    idx = idx_ref[...]

    def body(e, acc):
        return jnp.where(idx == e, start_ref[e], acc)

    dest_ref[...] = rank_ref[...] + lax.fori_loop(0, start_ref.shape[0], body, jnp.zeros_like(idx))


def assignment_rows(pad_start, idx, rank):
    return pl.pallas_call(
        _dest_kernel,
        in_specs=[pl.BlockSpec(memory_space=pltpu.SMEM), pl.BlockSpec(memory_space=pltpu.VMEM),
                  pl.BlockSpec(memory_space=pltpu.VMEM)],
        out_specs=pl.BlockSpec(memory_space=pltpu.VMEM),
        out_shape=jax.ShapeDtypeStruct(idx.shape, jnp.int32),
        name="assignment_rows",
    )(pad_start, idx, rank)


def moe_routed(h_all, idx, rank, counts, layer, w_gate, w_up, w_down):
    t, d = h_all.shape
    n_exp = counts.shape[0]
    n_assign = t * TOP_K
    counts = counts.reshape(n_exp).astype(jnp.int32)
    padded = (counts + MOE_BLOCK - 1) // MOE_BLOCK * MOE_BLOCK
    pad_end = jnp.cumsum(padded)
    pad_start = pad_end - padded
    dest = assignment_rows(pad_start, idx, rank).reshape(-1)
    n_blocks = -(-(n_assign + n_exp * (MOE_BLOCK - 1)) // MOE_BLOCK)
    n_blocks = -(-n_blocks // MOE_CHUNKS) * MOE_CHUNKS
    n_rows = n_blocks * MOE_BLOCK
    tok = jnp.tile(jnp.arange(t, dtype=jnp.int32), TOP_K)
    src_tok = jnp.arange(n_rows, dtype=jnp.int32) % t
    src_tok = src_tok.at[dest].add(tok - dest % t, unique_indices=True, mode="promise_in_bounds")
    blk_start = jnp.arange(n_blocks, dtype=jnp.int32) * MOE_BLOCK
    blk_expert = jnp.minimum(jnp.sum((pad_end[None, :] <= blk_start[:, None]).astype(jnp.int32), axis=1), n_exp - 1)
    n_used = pad_end[-1:] // MOE_BLOCK
    out = None
    chunk_rows = n_rows // MOE_CHUNKS
    for c in range(MOE_CHUNKS):
        xg = h_all.at[src_tok[c * chunk_rows:(c + 1) * chunk_rows]].get(mode="promise_in_bounds")
        out = moe_experts(xg, blk_expert, n_used, layer, w_gate, w_up, w_down, out, c * (n_blocks // MOE_CHUNKS))
    return out, dest.reshape(TOP_K, t)


def _stream(rows_per_batch):
    return dict(per_row=rows_per_batch < 1024, rows_per_batch=rows_per_batch)


def _mod_rows(stream, mod):
    b, d = mod.shape
    if not stream["per_row"]:
        return mod.reshape(b, 1, d)
    return jnp.repeat(mod, stream["rows_per_batch"], axis=0).reshape(1, -1, d)


def kernel(x_prompt, x_sample, cache_k_attn, cache_v_attn, c_prompt, c_sample, norm_mix_g, norm_ffn_g, ada_w, ada_b, attn_w_qkv, attn_w_o, attn_q_norm, attn_k_norm, attn_lambda_q1, attn_lambda_k1, attn_lambda_q2, attn_lambda_k2, attn_subln_g, sgu_w_in, sgu_ln_g, sgu_ln_b, sgu_w_s, sgu_b_s, sgu_w_o, moe_w_router, moe_b_router, moe_w_gate, moe_w_up, moe_w_down, shared_w_gate, shared_w_up, shared_w_down):
    bp, sp, d = x_prompt.shape
    bs, ss, _ = x_sample.shape
    depth = ada_w.shape[0]
    dv = 2 * ATTN_HEAD_DIM
    heads = d // dv
    mp, ms = bp * sp, bs * ss
    streams = (_stream(sp), _stream(ss))
    xs = [x_prompt.reshape(mp, d), x_sample.reshape(ms, d)]

    mod = ada_modulation(jnp.concatenate([c_prompt, c_sample], axis=0), ada_w, ada_b)
    slopes2 = (2.0 ** (-8.0 * jnp.arange(1, heads + 1, dtype=F32) / heads)) * LOG2E

    k_out, v_out, sgu_out = [[], []], [[], []], []
    for i in range(depth):
        j = i // N_MIXERS
        mods = []
        for si, (stream, rows) in enumerate(zip(streams, (slice(0, bp), slice(bp, bp + bs)))):
            mods.append([_mod_rows(stream, mod[i, rows, c * d:(c + 1) * d]) for c in range(6)])

        if i % N_MIXERS == 0:
            lam_init = 0.8 - 0.6 * math.exp(-0.3 * i)
            f = lambda a: a[j].astype(F32)
            lam = (jnp.exp(jnp.sum(f(attn_lambda_q1) * f(attn_lambda_k1)))
                   - jnp.exp(jnp.sum(f(attn_lambda_q2) * f(attn_lambda_k2))) + lam_init).reshape(1)
            for si, stream in enumerate(streams):
                h = modulate(stream, xs[si], norm_mix_g[i], mods[si][0], mods[si][1])
                q_bf, k32, k_bf, v32, v_bf = qkv_project(stream, h, attn_w_qkv[j], attn_q_norm[j], attn_k_norm[j])
                if si == 0:
                    o = attention_prompt(q_bf, k_bf, v_bf, bp, lam, slopes2, attn_subln_g[j], 1.0 - lam_init)
                else:
                    o = attention_sample(q_bf, k_bf, v_bf, cache_k_attn, cache_v_attn, j, lam, slopes2,
                                         attn_subln_g[j], 1.0 - lam_init)
                xs[si] = project_residual(stream, o, attn_w_o[j], xs[si], mods[si][2], "attn_out_proj")
                k_out[si].append(k32)
                v_out[si].append(v32)
        else:
            for si, stream in enumerate(streams):
                h = modulate(stream, xs[si], norm_mix_g[i], mods[si][0], mods[si][1])
                (z,) = matmul(h, sgu_w_in[j], bm=_row_block(stream, h.shape[0], 1024),
                              bn=_pick(sgu_w_in.shape[2], 1024),
                              out_dtypes=(BF16,), epilogue=_gelu_epilogue, name="sgu_in_proj")
                chunk_len = SGU_CHUNK if si == 0 else ss
                res = sgu_gate(z, sgu_ln_g[j], sgu_ln_b[j], sgu_w_s[j], sgu_b_s[j], chunk_len, write_v=(si == 1))
                if si == 1:
                    sgu_out.append(res[1])
                xs[si] = project_residual(stream, res[0], sgu_w_o[j], xs[si], mods[si][2], "sgu_out_proj")

        counts = jnp.zeros((moe_w_router.shape[2], 1), F32)
        per_token = ()
        for si, (stream, row0) in enumerate(zip(streams, (0, mp))):
            *per_token, counts = modulate(stream, xs[si], norm_ffn_g[i], mods[si][3], mods[si][4],
                                          router=(moe_w_router[i], moe_b_router[i], counts, mp + ms, row0, per_token))
        h_all, idx, wts, rank = per_token
        shared = shared_expert(h_all, *[w[i].astype(BF16) for w in (shared_w_gate, shared_w_up, shared_w_down)])
        out, dest = moe_routed(h_all, idx, rank, counts, i, moe_w_gate, moe_w_up, moe_w_down)
        for si, (stream, row0, m_rows) in enumerate(zip(streams, (0, mp), (mp, ms))):
            piece = _pick(m_rows, COMBINE_TOKENS)
            for tok0 in range(0, m_rows, piece):
                rows = dest[:, row0 + tok0:row0 + tok0 + piece].reshape(-1)
                picked = out.at[rows].get(mode="promise_in_bounds", unique_indices=True).reshape(TOP_K, piece, d)
                xs[si] = moe_combine_residual(stream, shared, picked, wts.T, row0, xs[si], mods[si][5], tok0)

    n_attn = len(k_out[0])
    k_prompt = jnp.stack(k_out[0]).reshape(n_attn, bp, sp, heads, 2, ATTN_HEAD_DIM)
    v_prompt = jnp.stack(v_out[0]).reshape(n_attn, bp, sp, heads, dv)
    k_sample = jnp.stack(k_out[1]).reshape(n_attn, bs, ss, heads, 2, ATTN_HEAD_DIM)
    v_sample = jnp.stack(v_out[1]).reshape(n_attn, bs, ss, heads, dv)
    sgu_v = jnp.stack(sgu_out).reshape(len(sgu_out), bs, ss, -1)
    return (xs[0].reshape(bp, sp, d), xs[1].reshape(bs, ss, d), k_prompt, v_prompt, k_sample, v_sample, sgu_v)
```

```python
import functools
import math

import jax
import jax.numpy as jnp
from jax import lax
from jax.experimental import pallas as pl
from jax.experimental.pallas import tpu as pltpu

F32 = jnp.float32
BF16 = jnp.bfloat16

EPS = 1e-6
CHUNK = 64
N_MIXERS = 2
ATTN_HEAD_DIM = 128
SGU_GROUPS = 8
SGU_CHUNK = 128
TOP_K = 8
N_EXPERT_GROUPS = 8
TOPK_GROUPS = 4
ROUTED_SCALE = 2.5
LOG2E = 1.4426950408889634
NEG_BIG = -1e30

VMEM_LIMIT = 56 * 1024 * 1024
MOE_BLOCK = 512
MOE_CHUNKS = 8
SGU_MIX_ROWS = 256
COMBINE_TOKENS = 2048
ATTN_ROW_CHUNK = 64
LANES = 128


def _params(n_axes):
    return pltpu.CompilerParams(dimension_semantics=("arbitrary",) * n_axes, vmem_limit_bytes=VMEM_LIMIT)


def _pick(n, pref):
    if n <= pref:
        return n
    b = pref
    while n % b:
        b //= 2
    return b


def _ada_kernel(c_ref, w_ref, b_ref, o_ref):
    c = c_ref[...]
    x = (c * jax.nn.sigmoid(c)).astype(BF16)
    o_ref[...] = jnp.dot(x, w_ref[...].astype(BF16), preferred_element_type=F32) + b_ref[...]


def ada_modulation(c, ada_w, ada_b):
    n_layers, d, n = ada_w.shape
    r = c.shape[0]
    bn = _pick(n, 1024)
    return pl.pallas_call(
        _ada_kernel,
        grid=(n_layers, n // bn),
        in_specs=[
            pl.BlockSpec((r, d), lambda l, j: (0, 0)),
            pl.BlockSpec((None, d, bn), lambda l, j: (l, 0, j)),
            pl.BlockSpec((None, 1, bn), lambda l, j: (l, 0, j)),
        ],
        out_specs=pl.BlockSpec((None, r, bn), lambda l, j: (l, 0, j)),
        out_shape=jax.ShapeDtypeStruct((n_layers, r, n), F32),
        compiler_params=_params(2),
        name="ada_modulation",
    )(c, ada_w, ada_b.reshape(n_layers, 1, n))


def _modulated(x, g, shift, scale):
    y = x * lax.rsqrt(jnp.mean(x * x, axis=-1, keepdims=True) + EPS)
    return y * g * (1.0 + scale) + shift


def _modulate_kernel(x_ref, g_ref, sh_ref, sc_ref, o_ref):
    o_ref[...] = _modulated(x_ref[...], g_ref[...], sh_ref[...], sc_ref[...]).astype(o_ref.dtype)


def _split3(x):
    hi = x.astype(BF16)
    lo = (x - hi.astype(F32)).astype(BF16)
    return hi, lo


def _first_max(x, ids, n_ids, axes):
    mx = x
    for ax in axes:
        mx = jnp.max(mx, axis=ax, keepdims=True)
    arg = jnp.where(x == mx, ids, n_ids)
    for ax in axes:
        arg = jnp.min(arg, axis=ax, keepdims=True)
    return mx, arg


def _sum_axes(x, axes):
    for ax in axes:
        x = jnp.sum(x, axis=ax, keepdims=True)
    return x


def _modulate_router_kernel(x_ref, g_ref, sh_ref, sc_ref, wrt_ref, br_ref, cin_ref, *refs, n_prev):
    o_ref, idx_ref, wts_ref, rank_ref, cnt_ref, carry = refs[n_prev:]
    h = _modulated(x_ref[...], g_ref[...], sh_ref[...], sc_ref[...])
    o_ref[...] = h.astype(o_ref.dtype)
    bm = h.shape[0]
    n_exp = wrt_ref.shape[0]
    per = n_exp // N_EXPERT_GROUPS
    grp_shape = (N_EXPERT_GROUPS, per, bm)

    h_hi, h_lo = _split3(h)
    w_hi, w_lo = _split3(wrt_ref[...])
    nt = (((1,), (1,)), ((), ()))
    logits = (lax.dot_general(w_hi, h_hi, nt, preferred_element_type=F32)
              + lax.dot_general(w_lo, h_hi, nt, preferred_element_type=F32)
              + lax.dot_general(w_hi, h_lo, nt, preferred_element_type=F32))
    s = jax.nn.sigmoid(logits)
    s3 = s.reshape(grp_shape)
    sb3 = (s + br_ref[...]).reshape(grp_shape)

    sub = lax.broadcasted_iota(jnp.int32, grp_shape, 1)
    gid = lax.broadcasted_iota(jnp.int32, (N_EXPERT_GROUPS, 1, bm), 0)
    eid = lax.broadcasted_iota(jnp.int32, grp_shape, 0) * per + sub

    m1, i1 = _first_max(sb3, sub, per, (1,))
    m2 = jnp.max(jnp.where(sub == i1, -jnp.inf, sb3), axis=1, keepdims=True)
    work = m1 + m2
    chosen = jnp.zeros_like(work)
    for _ in range(TOPK_GROUPS):
        _, gi = _first_max(work, gid, N_EXPERT_GROUPS, (0,))
        chosen = jnp.where(gid == gi, 1.0, chosen)
        work = jnp.where(gid == gi, -jnp.inf, work)
    sel = jnp.where(chosen > 0.0, sb3, -jnp.inf)

    hits, ids, raw = [], [], []
    for _ in range(TOP_K):
        _, ei = _first_max(sel, eid, n_exp, (1, 0))
        hit = eid == ei
        raw.append(_sum_axes(jnp.where(hit, s3, 0.0), (1, 0)))
        sel = jnp.where(hit, -jnp.inf, sel)
        hits.append(hit)
        ids.append(ei)
    total = sum(raw)

    member = sum(jnp.where(hit, 1.0, 0.0) for hit in hits).reshape(n_exp, bm)
    r_i = lax.broadcasted_iota(jnp.int32, (bm, bm), 0)
    c_i = lax.broadcasted_iota(jnp.int32, (bm, bm), 1)
    before = jnp.where(r_i < c_i, 1.0, 0.0).astype(BF16)

    @pl.when(pl.program_id(0) == 0)
    def _():
        carry[...] = cin_ref[...]

    rank_all = (jnp.dot(member.astype(BF16), before, preferred_element_type=F32) + carry[...]).reshape(grp_shape)
    carry[...] += jnp.sum(member, axis=1, keepdims=True)
    cnt_ref[...] = carry[...]
    for k in range(TOP_K):
        idx_ref[k:k + 1, :] = ids[k].reshape(1, bm)
        wts_ref[k:k + 1, :] = (raw[k] / total * ROUTED_SCALE).reshape(1, bm)
        rank_ref[k:k + 1, :] = _sum_axes(jnp.where(hits[k], rank_all, 0.0), (1, 0)).reshape(1, bm).astype(jnp.int32)


def _row_block(stream, m, pref):
    return _pick(m if stream["per_row"] else stream["rows_per_batch"], pref)


def _mod_spec(stream, bm, bn, ij):
    if stream["per_row"]:
        return pl.BlockSpec((None, bm, bn), lambda *g: (0,) + tuple(ij(*g)))
    bpg = stream["rows_per_batch"] // bm
    return pl.BlockSpec((None, 1, bn), lambda *g: (ij(*g)[0] // bpg, 0, ij(*g)[1]))


def modulate(stream, x, g, shift, scale, router=None):
    m, d = x.shape
    bm = _row_block(stream, m, 512)
    row_spec = pl.BlockSpec((bm, d), lambda i: (i, 0))
    mod_spec = _mod_spec(stream, bm, d, lambda i: (i, 0))
    g_spec = pl.BlockSpec((1, d), lambda i: (0, 0))
    if router is None:
        return pl.pallas_call(
            _modulate_kernel, grid=(m // bm,),
            in_specs=[row_spec, g_spec, mod_spec, mod_spec], out_specs=row_spec,
            out_shape=jax.ShapeDtypeStruct((m, d), BF16),
            compiler_params=_params(1), name="modulate",
        )(x, g.reshape(1, d), shift, scale)
    w_router, b_router, counts_in, t_all, row0, prev = router
    assert row0 % bm == 0
    off = row0 // bm
    e = w_router.shape[1]
    out_rows = pl.BlockSpec((bm, d), lambda i: (i + off, 0))
    tok_spec = pl.BlockSpec((TOP_K, bm), lambda i: (0, i + off))
    cnt_spec = pl.BlockSpec((e, 1), lambda i: (0, 0))
    n_in = 7
    return pl.pallas_call(
        functools.partial(_modulate_router_kernel, n_prev=len(prev)), grid=(m // bm,),
        in_specs=[row_spec, g_spec, mod_spec, mod_spec, pl.BlockSpec((e, d), lambda i: (0, 0)), cnt_spec, cnt_spec]
        + [pl.BlockSpec(memory_space=pl.ANY)] * len(prev),
        out_specs=[out_rows, tok_spec, tok_spec, tok_spec, cnt_spec],
        out_shape=[jax.ShapeDtypeStruct((t_all, d), BF16), jax.ShapeDtypeStruct((TOP_K, t_all), jnp.int32),
                   jax.ShapeDtypeStruct((TOP_K, t_all), F32), jax.ShapeDtypeStruct((TOP_K, t_all), jnp.int32),
                   jax.ShapeDtypeStruct((e, 1), F32)],
        input_output_aliases={n_in + k: k for k in range(len(prev))},
        scratch_shapes=[pltpu.VMEM((e, 1), F32)],
        compiler_params=_params(1), name="modulate_router",
    )(x, g.reshape(1, d), shift, scale, w_router.T, b_router.astype(F32).reshape(e, 1), counts_in, *prev)


def _mm_kernel(x_ref, w_ref, *refs, n_extra, epilogue):
    extra, outs, w_bf = refs[:n_extra], refs[n_extra:-1], refs[-1]

    @pl.when(pl.program_id(1) == 0)
    def _():
        w_bf[...] = w_ref[...].astype(BF16)

    acc = jnp.dot(x_ref[...], w_bf[...], preferred_element_type=F32)
    epilogue(acc, extra, outs)


def matmul(x, w, *, bm, bn, col_block_off=0, n_cols=None, extra=(), extra_specs=(), out_dtypes, epilogue, name):
    m, k = x.shape
    n = w.shape[1] if n_cols is None else n_cols
    grid = (n // bn, m // bm)
    out_spec = pl.BlockSpec((bm, bn), lambda j, i: (i, j))
    return pl.pallas_call(
        functools.partial(_mm_kernel, n_extra=len(extra), epilogue=epilogue),
        grid=grid,
        in_specs=[pl.BlockSpec((bm, k), lambda j, i: (i, 0)),
                  pl.BlockSpec((k, bn), lambda j, i: (0, j + col_block_off))] + list(extra_specs),
        out_specs=[out_spec] * len(out_dtypes),
        out_shape=[jax.ShapeDtypeStruct((m, n), dt) for dt in out_dtypes],
        scratch_shapes=[pltpu.VMEM((k, bn), BF16)],
        compiler_params=_params(2), name=name,
    )(x, w, *extra)


def _head_rms(acc, gain, post_scale):
    pieces = []
    for c in range(acc.shape[1] // ATTN_HEAD_DIM):
        seg = acc[:, c * ATTN_HEAD_DIM:(c + 1) * ATTN_HEAD_DIM]
        y = seg * lax.rsqrt(jnp.mean(seg * seg, axis=-1, keepdims=True) + EPS)
        pieces.append(y * (gain * post_scale))
    return jnp.concatenate(pieces, axis=-1)


def _q_epilogue(acc, extra, outs):
    outs[0][...] = _head_rms(acc, extra[0][...], ATTN_HEAD_DIM ** -0.5 * LOG2E).astype(BF16)


def _v_epilogue(acc, extra, outs):
    outs[0][...] = acc
    outs[1][...] = acc.astype(BF16)


def _gelu_epilogue(acc, extra, outs):
    outs[0][...] = (0.5 * acc * (1.0 + lax.erf(acc * (2.0 ** -0.5)))).astype(BF16)


def _residual_epilogue(acc, extra, outs):
    x_ref, gate_ref = extra
    outs[0][...] = x_ref[...] + gate_ref[...] * acc


def _k_proj_kernel(h_ref, w_ref, g_ref, k32_ref, kbf_ref):
    acc = jnp.dot(h_ref[...], w_ref[...], preferred_element_type=F32)
    bm = acc.shape[0]
    groups = acc.shape[1] // ATTN_HEAD_DIM
    gain = g_ref[...]
    for g in range(groups):
        cols = slice(g * ATTN_HEAD_DIM, (g + 1) * ATTN_HEAD_DIM)
        seg = acc[:, cols]
        kn = seg * lax.rsqrt(jnp.mean(seg * seg, axis=-1, keepdims=True) + EPS) * gain
        k32_ref[pl.ds(g, bm, stride=groups), :] = kn
        kbf_ref[:, cols] = kn.astype(BF16)


def k_project(stream, h, w_k, k_norm):
    m, d = h.shape
    n = w_k.shape[1]
    groups = n // ATTN_HEAD_DIM
    bm = _row_block(stream, m, 512)
    return pl.pallas_call(
        _k_proj_kernel, grid=(m // bm,),
        in_specs=[pl.BlockSpec((bm, d), lambda i: (i, 0)), pl.BlockSpec((d, n), lambda i: (0, 0)),
                  pl.BlockSpec((1, ATTN_HEAD_DIM), lambda i: (0, 0))],
        out_specs=[pl.BlockSpec((bm * groups, ATTN_HEAD_DIM), lambda i: (i, 0)),
                   pl.BlockSpec((bm, n), lambda i: (i, 0))],
        out_shape=[jax.ShapeDtypeStruct((m * groups, ATTN_HEAD_DIM), F32), jax.ShapeDtypeStruct((m, n), BF16)],
        compiler_params=_params(1), name="k_proj",
    )(h, w_k, k_norm.reshape(1, -1))


def qkv_project(stream, h, w_qkv, q_norm, k_norm):
    m, d = h.shape
    bm = _row_block(stream, m, 1024)
    bn = _pick(d, 1024)
    nb = d // bn
    gain_spec = pl.BlockSpec((1, ATTN_HEAD_DIM), lambda j, i: (0, 0))
    common = dict(bm=bm, bn=bn, n_cols=d)
    (q_bf,) = matmul(h, w_qkv, col_block_off=0, extra=(q_norm.reshape(1, -1),), extra_specs=(gain_spec,),
                     out_dtypes=(BF16,), epilogue=_q_epilogue, name="q_proj", **common)
    k32, k_bf = k_project(stream, h, w_qkv[:, d:2 * d].astype(BF16), k_norm)
    v32, v_bf = matmul(h, w_qkv, col_block_off=2 * nb, out_dtypes=(F32, BF16), epilogue=_v_epilogue,
                       name="v_proj", **common)
    return q_bf, k32, k_bf, v32, v_bf


def project_residual(stream, a, w, x, gate, name):
    k, n = w.shape
    deep = k > 4096
    bm = _row_block(stream, a.shape[0], 512 if deep else 1024)
    bn = _pick(n, 512 if deep else 1024)
    (out,) = matmul(a, w, bm=bm, bn=bn, extra=(x, gate),
                    extra_specs=(pl.BlockSpec((bm, bn), lambda j, i: (i, j)),
                                 _mod_spec(stream, bm, bn, lambda j, i: (i, j))),
                    out_dtypes=(F32,), epilogue=_residual_epilogue, name=name)
    return out


def _finish_heads(acc1, l1, acc2, l2, lam, g, post_scale):
    o = acc1 / l1 - lam * (acc2 / l2)
    o = o * lax.rsqrt(jnp.mean(o * o, axis=-1, keepdims=True) + EPS)
    return o * (g * post_scale)


def _attn_prompt_kernel(lam_ref, slope_ref, q_ref, k_ref, v_ref, g_ref, o_ref,
                        s_scr, p_scr, qa_scr, acc_scr, m_scr, l_scr, a_scr, *, bq, post_scale):
    hd = ATTN_HEAD_DIM
    rc = ATTN_ROW_CHUNK
    qi = pl.program_id(2)
    hh = pl.program_id(1)
    sl2 = slope_ref[hh]

    m_scr[...] = jnp.full(m_scr.shape, NEG_BIG, F32)
    l_scr[...] = jnp.zeros(l_scr.shape, F32)
    acc_scr[...] = jnp.zeros(acc_scr.shape, F32)

    lane = lax.broadcasted_iota(jnp.int32, (bq, LANES), 1)
    key_c = lax.broadcasted_iota(jnp.int32, (bq, LANES), 0)
    k_aug = jnp.where(lane < 3, key_c % 256, jnp.where(lane < 6, key_c // 256, 0)).astype(F32).astype(BF16)
    rest = jnp.full((bq, LANES), sl2, F32)
    piece = jnp.zeros((bq, LANES), F32)
    for i in range(3):
        part = rest.astype(BF16).astype(F32)
        piece = jnp.where(lane % 3 == i, part, piece)
        rest = rest - part
    q_aug = jnp.where(lane < 3, piece, jnp.where(lane < 6, 256.0 * piece, 0.0)).astype(BF16)
    for mi in range(2):
        qa_scr[mi, :, :hd] = q_ref[:, mi * hd:(mi + 1) * hd]
        qa_scr[mi, :, hd:] = q_aug

    def scores(kb, slot, maps=(0, 1)):
        k0 = pl.multiple_of(kb * bq, bq)
        for mi in maps:
            k_blk = jnp.concatenate([k_ref[pl.ds(k0, bq), mi * hd:(mi + 1) * hd], k_aug], axis=1)
            s_scr[2 * slot + mi] = lax.dot_general(qa_scr[mi], k_blk, (((1,), (1,)), ((), ())),
                                                   preferred_element_type=F32)

    def softmax(kb, slot, diag, maps=(0, 1)):
        shift_blk = sl2 * ((kb - qi) * bq + jnp.zeros((1, LANES), jnp.int32)).astype(F32)
        for mi in maps:
            s_map = s_scr.at[2 * slot + mi]
            p_map = p_scr.at[2 * slot + mi]
            for c in range(bq // rc):
                rows = slice(c * rc, (c + 1) * rc)
                ncol = min(bq, -(-((c + 1) * rc) // CHUNK) * CHUNK) if diag else bq
                lane_tiles = [slice(t * LANES, (t + 1) * LANES) for t in range(-(-ncol // LANES))]

                def tile(lanes, t):
                    s_t = s_map[rows, lanes]
                    if not diag:
                        return s_t
                    r_i = lax.broadcasted_iota(jnp.int32, (rc, LANES), 0) + c * rc
                    c_i = lax.broadcasted_iota(jnp.int32, (rc, LANES), 1) + t * LANES
                    s_t = s_t + (2.0 * sl2) * jnp.minimum(r_i - c_i, 0).astype(F32)
                    return jnp.where((c_i // CHUNK) <= (r_i // CHUNK), s_t, NEG_BIG)

                tiles = [tile(lanes, t) for t, lanes in enumerate(lane_tiles)]
                m_old = m_scr[mi, rows, :]
                blk_max = jnp.max(functools.reduce(jnp.maximum, tiles), axis=-1, keepdims=True)
                m_new = jnp.maximum(m_old, blk_max + shift_blk)
                alpha = jnp.exp2(m_old - m_new)
                m_scr[mi, rows, :] = m_new
                a_scr[2 * slot + mi, rows, :] = alpha
                if not diag:
                    tiles = [tile(lanes, t) for t, lanes in enumerate(lane_tiles)]
                off = m_new - shift_blk
                ps = [jnp.exp2(s_t - off) for s_t in tiles]
                l_scr[mi, rows, :] = alpha * l_scr[mi, rows, :] + functools.reduce(jnp.add, ps)
                for t, p_t in enumerate(ps):
                    p_map[rows, t * LANES:(t + 1) * LANES] = p_t.astype(BF16)
                if len(ps) * LANES < bq:
                    p_map[rows, len(ps) * LANES:] = jnp.zeros((rc, bq - len(ps) * LANES), BF16)

    def weighted_values(kb, slot, maps=(0, 1)):
        k0 = pl.multiple_of(jnp.maximum(kb, 0) * bq, bq)
        for mi in maps:
            pv = jnp.dot(p_scr[2 * slot + mi], v_ref[pl.ds(k0, bq), :], preferred_element_type=F32)
            for t in range(pv.shape[1] // LANES):
                lanes = slice(t * LANES, (t + 1) * LANES)
                acc_scr[mi, :, lanes] = a_scr[2 * slot + mi] * acc_scr[mi, :, lanes] + pv[:, lanes]

    p_scr[2] = jnp.zeros((bq, bq), BF16)
    p_scr[3] = jnp.zeros((bq, bq), BF16)
    a_scr[2] = jnp.ones((bq, LANES), F32)
    a_scr[3] = jnp.ones((bq, LANES), F32)

    def step(blk, slot, diag, with_next):
        weighted_values(blk - 1, 1 - slot, (0,))
        if with_next:
            scores(blk + 1, 1 - slot)
        softmax(blk, slot, diag, (0,))
        weighted_values(blk - 1, 1 - slot, (1,))
        softmax(blk, slot, diag, (1,))

    def pair(j2, carry):
        step(2 * j2, 0, False, True)
        step(2 * j2 + 1, 1, False, True)
        return carry

    scores(0, 0)
    lax.fori_loop(0, qi // 2, pair, 0)

    @pl.when(qi % 2 == 1)
    def _():
        step(qi - 1, 0, False, True)
        step(qi, 1, True, False)
        weighted_values(qi, 1)

    @pl.when(qi % 2 == 0)
    def _():
        step(qi, 0, True, False)
        weighted_values(qi, 0)
    l1 = jnp.sum(l_scr[0], axis=-1, keepdims=True)
    l2 = jnp.sum(l_scr[1], axis=-1, keepdims=True)
    o_ref[...] = _finish_heads(acc_scr[0], l1, acc_scr[1], l2, lam_ref[0], g_ref[...], post_scale).astype(o_ref.dtype)


def attention_prompt(q_bf, k_bf, v_bf, batch, lam, slopes2, subln_g, post_scale):
    m, width = q_bf.shape
    s = m // batch
    dv = 2 * ATTN_HEAD_DIM
    heads = width // dv
    bq = _pick(s, 512)
    assert bq % CHUNK == 0
    nq = s // bq
    kv_spec = pl.BlockSpec((s, dv), lambda b, h, i: (b, h))
    q_spec = pl.BlockSpec((bq, dv), lambda b, h, i: (b * nq + i, h))
    smem = pl.BlockSpec(memory_space=pltpu.SMEM)
    return pl.pallas_call(
        functools.partial(_attn_prompt_kernel, bq=bq, post_scale=post_scale),
        grid=(batch, heads, nq),
        in_specs=[smem, smem, q_spec, kv_spec, kv_spec, pl.BlockSpec((1, dv), lambda b, h, i: (0, 0))],
        out_specs=q_spec,
        out_shape=jax.ShapeDtypeStruct((m, width), BF16),
        scratch_shapes=[pltpu.VMEM((4, bq, bq), F32), pltpu.VMEM((4, bq, bq), BF16), pltpu.VMEM((2, bq, dv), BF16),
                        pltpu.VMEM((2, bq, dv), F32), pltpu.VMEM((2, bq, LANES), F32),
                        pltpu.VMEM((2, bq, LANES), F32), pltpu.VMEM((4, bq, LANES), F32)],
        compiler_params=_params(3), name="attention_prompt",
    )(lam, slopes2, q_bf, k_bf, v_bf, subln_g.reshape(1, dv))


def _attn_sample_kernel(lam_ref, slope_ref, q_ref, kn_ref, vn_ref, ck_ref, cv_ref, g_ref, o_ref, *, post_scale):
    hd = ATTN_HEAD_DIM
    dv = 2 * hd
    lam = lam_ref[0]
    t = q_ref.shape[0]
    heads = q_ref.shape[1] // dv
    past = ck_ref.shape[0] // (2 * heads)

    def dist_mask(n_keys, key0):
        q_pos = past + lax.broadcasted_iota(jnp.int32, (t, n_keys), 0)
        k_pos = key0 + lax.broadcasted_iota(jnp.int32, (t, n_keys), 1)
        return jnp.abs(q_pos - k_pos).astype(F32), (k_pos // CHUNK) <= (q_pos // CHUNK)

    dist_p, vis_p = dist_mask(past, 0)
    dist_n, vis_n = dist_mask(t, past)
    dn = (((1,), (1,)), ((), ()))
    for h in range(heads):
        cols = slice(h * dv, (h + 1) * dv)
        sl2 = slope_ref[h]
        v_past = jnp.concatenate(
            [cv_ref[pl.ds(c * heads + h, past, stride=heads * (dv // LANES)), :] for c in range(dv // LANES)],
            axis=1).astype(BF16)
        res = []
        for mi in range(2):
            qm = q_ref[:, h * dv + mi * hd:h * dv + (mi + 1) * hd]
            k_past = ck_ref[pl.ds(2 * h + mi, past, stride=2 * heads), :].astype(BF16)
            s_p = lax.dot_general(qm, k_past, dn, preferred_element_type=F32)
            s_n = lax.dot_general(qm, kn_ref[:, h * dv + mi * hd:h * dv + (mi + 1) * hd], dn,
                                  preferred_element_type=F32)
            s_p = jnp.where(vis_p, s_p - sl2 * dist_p, NEG_BIG)
            s_n = jnp.where(vis_n, s_n - sl2 * dist_n, NEG_BIG)
            m = jnp.maximum(jnp.max(s_p, axis=-1, keepdims=True), jnp.max(s_n, axis=-1, keepdims=True))
            p_p = jnp.exp2(s_p - m)
            p_n = jnp.exp2(s_n - m)
            l = jnp.sum(p_p, axis=-1, keepdims=True) + jnp.sum(p_n, axis=-1, keepdims=True)
            acc = (jnp.dot(p_p.astype(BF16), v_past, preferred_element_type=F32)
                   + jnp.dot(p_n.astype(BF16), vn_ref[:, cols], preferred_element_type=F32))
            res += [acc, l]
        o_ref[:, cols] = _finish_heads(res[0], res[1], res[2], res[3], lam, g_ref[...],
                                       post_scale).astype(o_ref.dtype)


def attention_sample(q_bf, k_bf, v_bf, cache_k, cache_v, layer, lam, slopes2, subln_g, post_scale):
    _, batch, past, heads, _, hd = cache_k.shape
    m, width = q_bf.shape
    t = m // batch
    dv = 2 * hd
    new_spec = pl.BlockSpec((t, width), lambda b: (b, 0))
    n_layers = cache_k.shape[0]
    cache_k = cache_k.reshape(n_layers, batch, past * heads * 2, hd)
    cache_v = cache_v.reshape(n_layers, batch, past, heads, dv // LANES, LANES)
    cache_v = cache_v.transpose(0, 1, 2, 4, 3, 5).reshape(n_layers, batch, past * (dv // LANES) * heads, LANES)
    ck_spec = pl.BlockSpec((None, None, past * heads * 2, hd), lambda b: (layer, b, 0, 0))
    cv_spec = pl.BlockSpec((None, None, past * (dv // LANES) * heads, LANES), lambda b: (layer, b, 0, 0))
    smem = pl.BlockSpec(memory_space=pltpu.SMEM)
    return pl.pallas_call(
        functools.partial(_attn_sample_kernel, post_scale=post_scale),
        grid=(batch,),
        in_specs=[smem, smem, new_spec, new_spec, new_spec, ck_spec, cv_spec,
                  pl.BlockSpec((1, dv), lambda b: (0, 0))],
        out_specs=new_spec,
        out_shape=jax.ShapeDtypeStruct((m, width), BF16),
        compiler_params=_params(1), name="attention_sample",
    )(lam, slopes2, q_bf, k_bf, v_bf, cache_k, cache_v, subln_g.reshape(1, dv))


def _sgu_gate_kernel(u_ref, v_ref, g_ref, b_ref, w_ref, bs_ref, *out_refs, chunk_len, write_v):
    bt, width = u_ref.shape
    gd = width // SGU_GROUPS
    v = v_ref[...].astype(F32)
    mu = jnp.mean(v, axis=-1, keepdims=True)
    vc = v - mu
    vn = vc * lax.rsqrt(jnp.mean(vc * vc, axis=-1, keepdims=True) + EPS) * g_ref[...] + b_ref[...]
    if write_v:
        out_refs[1][...] = vn
    vn_bf = vn.astype(BF16)
    mr = w_ref.shape[1]
    row = lax.broadcasted_iota(jnp.int32, (mr, mr), 0)
    col = lax.broadcasted_iota(jnp.int32, (mr, mr), 1)
    keep = ((row // chunk_len) == (col // chunk_len)) & ((col % chunk_len) <= (row % chunk_len))
    for gi in range(SGU_GROUPS):
        w = jnp.where(keep, w_ref[gi], 0.0).astype(BF16)
        bias = bs_ref[gi]
        for c in range(bt // mr):
            rows = slice(c * mr, (c + 1) * mr)
            cols = slice(gi * gd, (gi + 1) * gd)
            mixed = jnp.dot(w, vn_bf[rows, cols], preferred_element_type=F32) + bias
            out_refs[0][rows, cols] = (u_ref[rows, cols].astype(F32) * mixed).astype(BF16)


def sgu_gate(z, ln_g, ln_b, w_s, b_s, chunk_len, write_v):
    m, two_w = z.shape
    width = two_w // 2
    bt = _pick(m, SGU_MIX_ROWS)
    reps = bt // chunk_len
    w_t = jnp.tile(w_s[:, :chunk_len, :chunk_len], (1, reps, reps))
    b_t = jnp.tile(b_s[:, :chunk_len], (1, reps))[:, :, None]
    row_spec = lambda c: pl.BlockSpec((bt, width), lambda i, c=c: (i, c))
    vec_spec = pl.BlockSpec((1, width), lambda i: (0, 0))
    out_specs = [row_spec(0)]
    out_shape = [jax.ShapeDtypeStruct((m, width), BF16)]
    if write_v:
        out_specs.append(row_spec(0))
        out_shape.append(jax.ShapeDtypeStruct((m, width), F32))
    return pl.pallas_call(
        functools.partial(_sgu_gate_kernel, chunk_len=chunk_len, write_v=write_v),
        grid=(m // bt,),
        in_specs=[row_spec(0), row_spec(1), vec_spec, vec_spec,
                  pl.BlockSpec((SGU_GROUPS, bt, bt), lambda i: (0, 0, 0)),
                  pl.BlockSpec((SGU_GROUPS, bt, 1), lambda i: (0, 0, 0))],
        out_specs=out_specs, out_shape=out_shape,
        compiler_params=_params(1), name="sgu_gate",
    )(z, z, ln_g.reshape(1, width), ln_b.reshape(1, width), w_t, b_t)


def _swiglu(x, wg, wu, wd):
    g = jnp.dot(x, wg, preferred_element_type=F32)
    u = jnp.dot(x, wu, preferred_element_type=F32)
    h = (g * jax.nn.sigmoid(g) * u).astype(BF16)
    return jnp.dot(h, wd, preferred_element_type=F32)


def _moe_kernel(blk_e_ref, n_used_ref, x_ref, wg_ref, wu_ref, wd_ref, *refs, blk0):
    o_ref, wg_bf, wu_bf, wd_bf = refs[-4:]
    b = pl.program_id(0)
    e = blk_e_ref[blk0 + b]
    e_prev = blk_e_ref[blk0 + jnp.maximum(b - 1, 0)]

    @pl.when((b == 0) | (e != e_prev))
    def _():
        wg_bf[...] = wg_ref[...].astype(BF16)
        wu_bf[...] = wu_ref[...].astype(BF16)
        wd_bf[...] = wd_ref[...].astype(BF16)

    @pl.when(blk0 + b < n_used_ref[0])
    def _():
        o_ref[...] = _swiglu(x_ref[...], wg_bf[...], wu_bf[...], wd_bf[...]).astype(o_ref.dtype)

    @pl.when(blk0 + b >= n_used_ref[0])
    def _():
        o_ref[...] = jnp.zeros_like(o_ref)


def moe_experts(xg, blk_expert, n_used, layer, w_gate, w_up, w_down, out, blk0):
    rows, d = xg.shape
    ed = w_gate.shape[3]
    x_spec = pl.BlockSpec((MOE_BLOCK, d), lambda b, be, nu: (b, 0))
    w_idx = lambda b, be, nu: (layer, be[blk0 + b], 0, 0)
    in_specs = [x_spec, pl.BlockSpec((None, None, d, ed), w_idx), pl.BlockSpec((None, None, d, ed), w_idx),
                pl.BlockSpec((None, None, ed, d), w_idx)]
    args = [blk_expert, n_used, xg, w_gate, w_up, w_down]
    aliases = {}
    if out is not None:
        in_specs.append(pl.BlockSpec(memory_space=pl.ANY))
        args.append(out)
        aliases = {len(args) - 1: 0}
    return pl.pallas_call(
        functools.partial(_moe_kernel, blk0=blk0),
        grid_spec=pltpu.PrefetchScalarGridSpec(
            num_scalar_prefetch=2, grid=(rows // MOE_BLOCK,),
            in_specs=in_specs,
            out_specs=pl.BlockSpec((MOE_BLOCK, d), lambda b, be, nu: (blk0 + b, 0)),
            scratch_shapes=[pltpu.VMEM((d, ed), BF16), pltpu.VMEM((d, ed), BF16), pltpu.VMEM((ed, d), BF16)]),
        out_shape=jax.ShapeDtypeStruct((blk_expert.shape[0] * MOE_BLOCK, d), BF16),
        input_output_aliases=aliases,
        compiler_params=_params(1), name="moe_experts",
    )(*args)


def _shared_kernel(h_ref, wg_ref, wu_ref, wd_ref, o_ref):
    o_ref[...] = _swiglu(h_ref[...], wg_ref[...], wu_ref[...], wd_ref[...]).astype(o_ref.dtype)


def shared_expert(h, sw_gate, sw_up, sw_down):
    m, d = h.shape
    ed = sw_gate.shape[1]
    bm = _pick(m, 512)
    row_spec = pl.BlockSpec((bm, d), lambda i: (i, 0))
    full = lambda a, b: pl.BlockSpec((a, b), lambda i: (0, 0))
    return pl.pallas_call(
        _shared_kernel, grid=(m // bm,),
        in_specs=[row_spec, full(d, ed), full(d, ed), full(ed, d)],
        out_specs=row_spec,
        out_shape=jax.ShapeDtypeStruct((m, d), BF16),
        compiler_params=_params(1), name="shared_expert",
    )(h, sw_gate, sw_up, sw_down)


def _combine_kernel(s_ref, p_ref, w_ref, x_ref, gate_ref, o_ref):
    ffn = s_ref[...].astype(F32)
    w = w_ref[...]
    for k in range(TOP_K):
        ffn = ffn + w[:, k:k + 1] * p_ref[k].astype(F32)
    o_ref[...] = x_ref[...] + gate_ref[...] * ffn


def moe_combine_residual(stream, shared, picked, wts, row0, x, gate, tok0):
    m, d = x.shape
    n = picked.shape[1]
    bm = _row_block(stream, n, 256)
    assert row0 % bm == 0 and tok0 % bm == 0 and n % bm == 0
    off, off_all = tok0 // bm, (row0 + tok0) // bm
    row_spec = pl.BlockSpec((bm, d), lambda i: (i + off, 0))
    all_rows = lambda width: pl.BlockSpec((bm, width), lambda i: (i + off_all, 0))
    return pl.pallas_call(
        _combine_kernel, grid=(n // bm,),
        in_specs=[all_rows(d), pl.BlockSpec((TOP_K, bm, d), lambda i: (0, i, 0)), all_rows(TOP_K), row_spec,
                  _mod_spec(stream, bm, d, lambda i: (i + off, 0))],
        out_specs=row_spec,
        out_shape=jax.ShapeDtypeStruct((m, d), F32),
        input_output_aliases={3: 0},
        compiler_params=_params(1), name="moe_combine",
    )(shared, picked, wts, x, gate)


def _dest_kernel(start_ref, idx_ref, rank_ref, dest_ref):
    idx = idx_ref[...]

    def body(e, acc):
        return jnp.where(idx == e, start_ref[e], acc)

    dest_ref[...] = rank_ref[...] + lax.fori_loop(0, start_ref.shape[0], body, jnp.zeros_like(idx))


def assignment_rows(pad_start, idx, rank):
    return pl.pallas_call(
        _dest_kernel,
        in_specs=[pl.BlockSpec(memory_space=pltpu.SMEM), pl.BlockSpec(memory_space=pltpu.VMEM),
                  pl.BlockSpec(memory_space=pltpu.VMEM)],
        out_specs=pl.BlockSpec(memory_space=pltpu.VMEM),
        out_shape=jax.ShapeDtypeStruct(idx.shape, jnp.int32),
        name="assignment_rows",
    )(pad_start, idx, rank)


def moe_routed(h_all, idx, rank, counts, layer, w_gate, w_up, w_down):
    t, d = h_all.shape
    n_exp = counts.shape[0]
    n_assign = t * TOP_K
    counts = counts.reshape(n_exp).astype(jnp.int32)
    padded = (counts + MOE_BLOCK - 1) // MOE_BLOCK * MOE_BLOCK
    pad_end = jnp.cumsum(padded)
    pad_start = pad_end - padded
    dest = assignment_rows(pad_start, idx, rank).reshape(-1)
    n_blocks = -(-(n_assign + n_exp * (MOE_BLOCK - 1)) // MOE_BLOCK)
    n_blocks = -(-n_blocks // MOE_CHUNKS) * MOE_CHUNKS
    n_rows = n_blocks * MOE_BLOCK
    tok = jnp.tile(jnp.arange(t, dtype=jnp.int32), TOP_K)
    src_tok = jnp.arange(n_rows, dtype=jnp.int32) % t
    src_tok = src_tok.at[dest].add(tok - dest % t, unique_indices=True, mode="promise_in_bounds")
    blk_start = jnp.arange(n_blocks, dtype=jnp.int32) * MOE_BLOCK
    blk_expert = jnp.minimum(jnp.sum((pad_end[None, :] <= blk_start[:, None]).astype(jnp.int32), axis=1), n_exp - 1)
    n_used = pad_end[-1:] // MOE_BLOCK
    out = None
    chunk_rows = n_rows // MOE_CHUNKS
    for c in range(MOE_CHUNKS):
        xg = h_all.at[src_tok[c * chunk_rows:(c + 1) * chunk_rows]].get(mode="promise_in_bounds")
        out = moe_experts(xg, blk_expert, n_used, layer, w_gate, w_up, w_down, out, c * (n_blocks // MOE_CHUNKS))
    return out, dest.reshape(TOP_K, t)


def _stream(rows_per_batch):
    return dict(per_row=rows_per_batch < 1024, rows_per_batch=rows_per_batch)


def _mod_rows(stream, mod):
    b, d = mod.shape
    if not stream["per_row"]:
        return mod.reshape(b, 1, d)
    return jnp.repeat(mod, stream["rows_per_batch"], axis=0).reshape(1, -1, d)


def kernel(x_prompt, x_sample, cache_k_attn, cache_v_attn, c_prompt, c_sample, norm_mix_g, norm_ffn_g, ada_w, ada_b, attn_w_qkv, attn_w_o, attn_q_norm, attn_k_norm, attn_lambda_q1, attn_lambda_k1, attn_lambda_q2, attn_lambda_k2, attn_subln_g, sgu_w_in, sgu_ln_g, sgu_ln_b, sgu_w_s, sgu_b_s, sgu_w_o, moe_w_router, moe_b_router, moe_w_gate, moe_w_up, moe_w_down, shared_w_gate, shared_w_up, shared_w_down):
    bp, sp, d = x_prompt.shape
    bs, ss, _ = x_sample.shape
    depth = ada_w.shape[0]
    dv = 2 * ATTN_HEAD_DIM
    heads = d // dv
    mp, ms = bp * sp, bs * ss
    streams = (_stream(sp), _stream(ss))
    xs = [x_prompt.reshape(mp, d), x_sample.reshape(ms, d)]

    mod = ada_modulation(jnp.concatenate([c_prompt, c_sample], axis=0), ada_w, ada_b)
    slopes2 = (2.0 ** (-8.0 * jnp.arange(1, heads + 1, dtype=F32) / heads)) * LOG2E

    k_out, v_out, sgu_out = [[], []], [[], []], []
    for i in range(depth):
        j = i // N_MIXERS
        mods = []
        for si, (stream, rows) in enumerate(zip(streams, (slice(0, bp), slice(bp, bp + bs)))):
            mods.append([_mod_rows(stream, mod[i, rows, c * d:(c + 1) * d]) for c in range(6)])

        if i % N_MIXERS == 0:
            lam_init = 0.8 - 0.6 * math.exp(-0.3 * i)
            f = lambda a: a[j].astype(F32)
            lam = (jnp.exp(jnp.sum(f(attn_lambda_q1) * f(attn_lambda_k1)))
                   - jnp.exp(jnp.sum(f(attn_lambda_q2) * f(attn_lambda_k2))) + lam_init).reshape(1)
            for si, stream in enumerate(streams):
                h = modulate(stream, xs[si], norm_mix_g[i], mods[si][0], mods[si][1])
                q_bf, k32, k_bf, v32, v_bf = qkv_project(stream, h, attn_w_qkv[j], attn_q_norm[j], attn_k_norm[j])
                if si == 0:
                    o = attention_prompt(q_bf, k_bf, v_bf, bp, lam, slopes2, attn_subln_g[j], 1.0 - lam_init)
                else:
                    o = attention_sample(q_bf, k_bf, v_bf, cache_k_attn, cache_v_attn, j, lam, slopes2,
                                         attn_subln_g[j], 1.0 - lam_init)
                xs[si] = project_residual(stream, o, attn_w_o[j], xs[si], mods[si][2], "attn_out_proj")
                k_out[si].append(k32)
                v_out[si].append(v32)
        else:
            for si, stream in enumerate(streams):
                h = modulate(stream, xs[si], norm_mix_g[i], mods[si][0], mods[si][1])
                (z,) = matmul(h, sgu_w_in[j], bm=_row_block(stream, h.shape[0], 1024),
                              bn=_pick(sgu_w_in.shape[2], 1024),
                              out_dtypes=(BF16,), epilogue=_gelu_epilogue, name="sgu_in_proj")
                chunk_len = SGU_CHUNK if si == 0 else ss
                res = sgu_gate(z, sgu_ln_g[j], sgu_ln_b[j], sgu_w_s[j], sgu_b_s[j], chunk_len, write_v=(si == 1))
                if si == 1:
                    sgu_out.append(res[1])
                xs[si] = project_residual(stream, res[0], sgu_w_o[j], xs[si], mods[si][2], "sgu_out_proj")

        counts = jnp.zeros((moe_w_router.shape[2], 1), F32)
        per_token = ()
        for si, (stream, row0) in enumerate(zip(streams, (0, mp))):
            *per_token, counts = modulate(stream, xs[si], norm_ffn_g[i], mods[si][3], mods[si][4],
                                          router=(moe_w_router[i], moe_b_router[i], counts, mp + ms, row0, per_token))
        h_all, idx, wts, rank = per_token
        shared = shared_expert(h_all, *[w[i].astype(BF16) for w in (shared_w_gate, shared_w_up, shared_w_down)])
        out, dest = moe_routed(h_all, idx, rank, counts, i, moe_w_gate, moe_w_up, moe_w_down)
        for si, (stream, row0, m_rows) in enumerate(zip(streams, (0, mp), (mp, ms))):
            piece = _pick(m_rows, COMBINE_TOKENS)
            for tok0 in range(0, m_rows, piece):
                rows = dest[:, row0 + tok0:row0 + tok0 + piece].reshape(-1)
                picked = out.at[rows].get(mode="promise_in_bounds", unique_indices=True).reshape(TOP_K, piece, d)
                xs[si] = moe_combine_residual(stream, shared, picked, wts.T, row0, xs[si], mods[si][5], tok0)

    n_attn = len(k_out[0])
    k_prompt = jnp.stack(k_out[0]).reshape(n_attn, bp, sp, heads, 2, ATTN_HEAD_DIM)
    v_prompt = jnp.stack(v_out[0]).reshape(n_attn, bp, sp, heads, dv)
    k_sample = jnp.stack(k_out[1]).reshape(n_attn, bs, ss, heads, 2, ATTN_HEAD_DIM)
    v_sample = jnp.stack(v_out[1]).reshape(n_attn, bs, ss, heads, dv)
    sgu_v = jnp.stack(sgu_out).reshape(len(sgu_out), bs, ss, -1)
    return (xs[0].reshape(bp, sp, d), xs[1].reshape(bs, ss, d), k_prompt, v_prompt, k_sample, v_sample, sgu_v)
```

```python
import functools
import math

import jax
import jax.numpy as jnp
from jax import lax
from jax.experimental import pallas as pl
from jax.experimental.pallas import tpu as pltpu

F32 = jnp.float32
BF16 = jnp.bfloat16

EPS = 1e-6
CHUNK = 64
N_MIXERS = 2
ATTN_HEAD_DIM = 128
SGU_GROUPS = 8
SGU_CHUNK = 128
TOP_K = 8
N_EXPERT_GROUPS = 8
TOPK_GROUPS = 4
ROUTED_SCALE = 2.5
LOG2E = 1.4426950408889634
NEG_BIG = -1e30

VMEM_LIMIT = 56 * 1024 * 1024
MOE_BLOCK = 256
MOE_CHUNKS = 8
SGU_MIX_ROWS = 256
COMBINE_TOKENS = 2048
ATTN_ROW_CHUNK = 64
LANES = 128


def _params(n_axes):
    return pltpu.CompilerParams(dimension_semantics=("arbitrary",) * n_axes, vmem_limit_bytes=VMEM_LIMIT)


def _pick(n, pref):
    if n <= pref:
        return n
    b = pref
    while n % b:
        b //= 2
    return b


def _ada_kernel(c_ref, w_ref, b_ref, o_ref):
    c = c_ref[...]
    x = (c * jax.nn.sigmoid(c)).astype(BF16)
    o_ref[...] = jnp.dot(x, w_ref[...].astype(BF16), preferred_element_type=F32) + b_ref[...]


def ada_modulation(c, ada_w, ada_b):
    n_layers, d, n = ada_w.shape
    r = c.shape[0]
    bn = _pick(n, 1024)
    return pl.pallas_call(
        _ada_kernel,
        grid=(n_layers, n // bn),
        in_specs=[
            pl.BlockSpec((r, d), lambda l, j: (0, 0)),
            pl.BlockSpec((None, d, bn), lambda l, j: (l, 0, j)),
            pl.BlockSpec((None, 1, bn), lambda l, j: (l, 0, j)),
        ],
        out_specs=pl.BlockSpec((None, r, bn), lambda l, j: (l, 0, j)),
        out_shape=jax.ShapeDtypeStruct((n_layers, r, n), F32),
        compiler_params=_params(2),
        name="ada_modulation",
    )(c, ada_w, ada_b.reshape(n_layers, 1, n))


def _modulated(x, g, shift, scale):
    y = x * lax.rsqrt(jnp.mean(x * x, axis=-1, keepdims=True) + EPS)
    return y * g * (1.0 + scale) + shift


def _modulate_kernel(x_ref, g_ref, sh_ref, sc_ref, o_ref):
    o_ref[...] = _modulated(x_ref[...], g_ref[...], sh_ref[...], sc_ref[...]).astype(o_ref.dtype)


def _split3(x):
    hi = x.astype(BF16)
    lo = (x - hi.astype(F32)).astype(BF16)
    return hi, lo


def _first_max(x, ids, n_ids, axes):
    mx = x
    for ax in axes:
        mx = jnp.max(mx, axis=ax, keepdims=True)
    arg = jnp.where(x == mx, ids, n_ids)
    for ax in axes:
        arg = jnp.min(arg, axis=ax, keepdims=True)
    return mx, arg


def _sum_axes(x, axes):
    for ax in axes:
        x = jnp.sum(x, axis=ax, keepdims=True)
    return x


def _modulate_router_kernel(x_ref, g_ref, sh_ref, sc_ref, wrt_ref, br_ref, cin_ref, *refs, n_prev):
    o_ref, idx_ref, wts_ref, rank_ref, cnt_ref, carry = refs[n_prev:]
    h = _modulated(x_ref[...], g_ref[...], sh_ref[...], sc_ref[...])
    o_ref[...] = h.astype(o_ref.dtype)
    bm = h.shape[0]
    n_exp = wrt_ref.shape[0]
    per = n_exp // N_EXPERT_GROUPS
    grp_shape = (N_EXPERT_GROUPS, per, bm)

    h_hi, h_lo = _split3(h)
    w_hi, w_lo = _split3(wrt_ref[...])
    nt = (((1,), (1,)), ((), ()))
    logits = (lax.dot_general(w_hi, h_hi, nt, preferred_element_type=F32)
              + lax.dot_general(w_lo, h_hi, nt, preferred_element_type=F32)
              + lax.dot_general(w_hi, h_lo, nt, preferred_element_type=F32))
    s = jax.nn.sigmoid(logits)
    s3 = s.reshape(grp_shape)
    sb3 = (s + br_ref[...]).reshape(grp_shape)

    sub = lax.broadcasted_iota(jnp.int32, grp_shape, 1)
    gid = lax.broadcasted_iota(jnp.int32, (N_EXPERT_GROUPS, 1, bm), 0)
    eid = lax.broadcasted_iota(jnp.int32, grp_shape, 0) * per + sub

    m1, i1 = _first_max(sb3, sub, per, (1,))
    m2 = jnp.max(jnp.where(sub == i1, -jnp.inf, sb3), axis=1, keepdims=True)
    work = m1 + m2
    chosen = jnp.zeros_like(work)
    for _ in range(TOPK_GROUPS):
        _, gi = _first_max(work, gid, N_EXPERT_GROUPS, (0,))
        chosen = jnp.where(gid == gi, 1.0, chosen)
        work = jnp.where(gid == gi, -jnp.inf, work)
    sel = jnp.where(chosen > 0.0, sb3, -jnp.inf)

    hits, ids, raw = [], [], []
    for _ in range(TOP_K):
        _, ei = _first_max(sel, eid, n_exp, (1, 0))
        hit = eid == ei
        raw.append(_sum_axes(jnp.where(hit, s3, 0.0), (1, 0)))
        sel = jnp.where(hit, -jnp.inf, sel)
        hits.append(hit)
        ids.append(ei)
    total = sum(raw)

    member = sum(jnp.where(hit, 1.0, 0.0) for hit in hits).reshape(n_exp, bm)
    r_i = lax.broadcasted_iota(jnp.int32, (bm, bm), 0)
    c_i = lax.broadcasted_iota(jnp.int32, (bm, bm), 1)
    before = jnp.where(r_i < c_i, 1.0, 0.0).astype(BF16)

    @pl.when(pl.program_id(0) == 0)
    def _():
        carry[...] = cin_ref[...]

    rank_all = (jnp.dot(member.astype(BF16), before, preferred_element_type=F32) + carry[...]).reshape(grp_shape)
    carry[...] += jnp.sum(member, axis=1, keepdims=True)
    cnt_ref[...] = carry[...]
    for k in range(TOP_K):
        idx_ref[k:k + 1, :] = ids[k].reshape(1, bm)
        wts_ref[k:k + 1, :] = (raw[k] / total * ROUTED_SCALE).reshape(1, bm)
        rank_ref[k:k + 1, :] = _sum_axes(jnp.where(hits[k], rank_all, 0.0), (1, 0)).reshape(1, bm).astype(jnp.int32)


def _row_block(stream, m, pref):
    return _pick(m if stream["per_row"] else stream["rows_per_batch"], pref)


def _mod_spec(stream, bm, bn, ij):
    if stream["per_row"]:
        return pl.BlockSpec((None, bm, bn), lambda *g: (0,) + tuple(ij(*g)))
    bpg = stream["rows_per_batch"] // bm
    return pl.BlockSpec((None, 1, bn), lambda *g: (ij(*g)[0] // bpg, 0, ij(*g)[1]))


def modulate(stream, x, g, shift, scale, router=None):
    m, d = x.shape
    bm = _row_block(stream, m, 512)
    row_spec = pl.BlockSpec((bm, d), lambda i: (i, 0))
    mod_spec = _mod_spec(stream, bm, d, lambda i: (i, 0))
    g_spec = pl.BlockSpec((1, d), lambda i: (0, 0))
    if router is None:
        return pl.pallas_call(
            _modulate_kernel, grid=(m // bm,),
            in_specs=[row_spec, g_spec, mod_spec, mod_spec], out_specs=row_spec,
            out_shape=jax.ShapeDtypeStruct((m, d), BF16),
            compiler_params=_params(1), name="modulate",
        )(x, g.reshape(1, d), shift, scale)
    w_router, b_router, counts_in, t_all, row0, prev = router
    assert row0 % bm == 0
    off = row0 // bm
    e = w_router.shape[1]
    out_rows = pl.BlockSpec((bm, d), lambda i: (i + off, 0))
    tok_spec = pl.BlockSpec((TOP_K, bm), lambda i: (0, i + off))
    cnt_spec = pl.BlockSpec((e, 1), lambda i: (0, 0))
    n_in = 7
    return pl.pallas_call(
        functools.partial(_modulate_router_kernel, n_prev=len(prev)), grid=(m // bm,),
        in_specs=[row_spec, g_spec, mod_spec, mod_spec, pl.BlockSpec((e, d), lambda i: (0, 0)), cnt_spec, cnt_spec]
        + [pl.BlockSpec(memory_space=pl.ANY)] * len(prev),
        out_specs=[out_rows, tok_spec, tok_spec, tok_spec, cnt_spec],
        out_shape=[jax.ShapeDtypeStruct((t_all, d), BF16), jax.ShapeDtypeStruct((TOP_K, t_all), jnp.int32),
                   jax.ShapeDtypeStruct((TOP_K, t_all), F32), jax.ShapeDtypeStruct((TOP_K, t_all), jnp.int32),
                   jax.ShapeDtypeStruct((e, 1), F32)],
        input_output_aliases={n_in + k: k for k in range(len(prev))},
        scratch_shapes=[pltpu.VMEM((e, 1), F32)],
        compiler_params=_params(1), name="modulate_router",
    )(x, g.reshape(1, d), shift, scale, w_router.T, b_router.astype(F32).reshape(e, 1), counts_in, *prev)


def _mm_kernel(x_ref, w_ref, *refs, n_extra, epilogue):
    extra, outs, w_bf = refs[:n_extra], refs[n_extra:-1], refs[-1]

    @pl.when(pl.program_id(1) == 0)
    def _():
        w_bf[...] = w_ref[...].astype(BF16)

    acc = jnp.dot(x_ref[...], w_bf[...], preferred_element_type=F32)
    epilogue(acc, extra, outs)


def matmul(x, w, *, bm, bn, col_block_off=0, n_cols=None, extra=(), extra_specs=(), out_dtypes, epilogue, name):
    m, k = x.shape
    n = w.shape[1] if n_cols is None else n_cols
    grid = (n // bn, m // bm)
    out_spec = pl.BlockSpec((bm, bn), lambda j, i: (i, j))
    return pl.pallas_call(
        functools.partial(_mm_kernel, n_extra=len(extra), epilogue=epilogue),
        grid=grid,
        in_specs=[pl.BlockSpec((bm, k), lambda j, i: (i, 0)),
                  pl.BlockSpec((k, bn), lambda j, i: (0, j + col_block_off))] + list(extra_specs),
        out_specs=[out_spec] * len(out_dtypes),
        out_shape=[jax.ShapeDtypeStruct((m, n), dt) for dt in out_dtypes],
        scratch_shapes=[pltpu.VMEM((k, bn), BF16)],
        compiler_params=_params(2), name=name,
    )(x, w, *extra)


def _head_rms(acc, gain, post_scale):
    pieces = []
    for c in range(acc.shape[1] // ATTN_HEAD_DIM):
        seg = acc[:, c * ATTN_HEAD_DIM:(c + 1) * ATTN_HEAD_DIM]
        y = seg * lax.rsqrt(jnp.mean(seg * seg, axis=-1, keepdims=True) + EPS)
        pieces.append(y * (gain * post_scale))
    return jnp.concatenate(pieces, axis=-1)


def _q_epilogue(acc, extra, outs):
    outs[0][...] = _head_rms(acc, extra[0][...], ATTN_HEAD_DIM ** -0.5 * LOG2E).astype(BF16)


def _v_epilogue(acc, extra, outs):
    outs[0][...] = acc
    outs[1][...] = acc.astype(BF16)


def _gelu_epilogue(acc, extra, outs):
    outs[0][...] = (0.5 * acc * (1.0 + lax.erf(acc * (2.0 ** -0.5)))).astype(BF16)


def _residual_epilogue(acc, extra, outs):
    x_ref, gate_ref = extra
    outs[0][...] = x_ref[...] + gate_ref[...] * acc


def _k_proj_kernel(h_ref, w_ref, g_ref, k32_ref, kbf_ref):
    acc = jnp.dot(h_ref[...], w_ref[...], preferred_element_type=F32)
    bm = acc.shape[0]
    groups = acc.shape[1] // ATTN_HEAD_DIM
    gain = g_ref[...]
    for g in range(groups):
        cols = slice(g * ATTN_HEAD_DIM, (g + 1) * ATTN_HEAD_DIM)
        seg = acc[:, cols]
        kn = seg * lax.rsqrt(jnp.mean(seg * seg, axis=-1, keepdims=True) + EPS) * gain
        k32_ref[pl.ds(g, bm, stride=groups), :] = kn
        kbf_ref[:, cols] = kn.astype(BF16)


def k_project(stream, h, w_k, k_norm):
    m, d = h.shape
    n = w_k.shape[1]
    groups = n // ATTN_HEAD_DIM
    bm = _row_block(stream, m, 512)
    return pl.pallas_call(
        _k_proj_kernel, grid=(m // bm,),
        in_specs=[pl.BlockSpec((bm, d), lambda i: (i, 0)), pl.BlockSpec((d, n), lambda i: (0, 0)),
                  pl.BlockSpec((1, ATTN_HEAD_DIM), lambda i: (0, 0))],
        out_specs=[pl.BlockSpec((bm * groups, ATTN_HEAD_DIM), lambda i: (i, 0)),
                   pl.BlockSpec((bm, n), lambda i: (i, 0))],
        out_shape=[jax.ShapeDtypeStruct((m * groups, ATTN_HEAD_DIM), F32), jax.ShapeDtypeStruct((m, n), BF16)],
        compiler_params=_params(1), name="k_proj",
    )(h, w_k, k_norm.reshape(1, -1))


def qkv_project(stream, h, w_qkv, q_norm, k_norm):
    m, d = h.shape
    bm = _row_block(stream, m, 1024)
    bn = _pick(d, 1024)
    nb = d // bn
    gain_spec = pl.BlockSpec((1, ATTN_HEAD_DIM), lambda j, i: (0, 0))
    common = dict(bm=bm, bn=bn, n_cols=d)
    (q_bf,) = matmul(h, w_qkv, col_block_off=0, extra=(q_norm.reshape(1, -1),), extra_specs=(gain_spec,),
                     out_dtypes=(BF16,), epilogue=_q_epilogue, name="q_proj", **common)
    k32, k_bf = k_project(stream, h, w_qkv[:, d:2 * d].astype(BF16), k_norm)
    v32, v_bf = matmul(h, w_qkv, col_block_off=2 * nb, out_dtypes=(F32, BF16), epilogue=_v_epilogue,
                       name="v_proj", **common)
    return q_bf, k32, k_bf, v32, v_bf


def project_residual(stream, a, w, x, gate, name):
    k, n = w.shape
    deep = k > 4096
    bm = _row_block(stream, a.shape[0], 512 if deep else 1024)
    bn = _pick(n, 512 if deep else 1024)
    (out,) = matmul(a, w, bm=bm, bn=bn, extra=(x, gate),
                    extra_specs=(pl.BlockSpec((bm, bn), lambda j, i: (i, j)),
                                 _mod_spec(stream, bm, bn, lambda j, i: (i, j))),
                    out_dtypes=(F32,), epilogue=_residual_epilogue, name=name)
    return out


def _finish_heads(acc1, l1, acc2, l2, lam, g, post_scale):
    o = acc1 / l1 - lam * (acc2 / l2)
    o = o * lax.rsqrt(jnp.mean(o * o, axis=-1, keepdims=True) + EPS)
    return o * (g * post_scale)


def _attn_prompt_kernel(lam_ref, slope_ref, q_ref, k_ref, v_ref, g_ref, o_ref,
                        s_scr, p_scr, qa_scr, acc_scr, m_scr, l_scr, a_scr, *, bq, post_scale):
    hd = ATTN_HEAD_DIM
    rc = ATTN_ROW_CHUNK
    qi = pl.program_id(2)
    hh = pl.program_id(1)
    sl2 = slope_ref[hh]

    m_scr[...] = jnp.full(m_scr.shape, NEG_BIG, F32)
    l_scr[...] = jnp.zeros(l_scr.shape, F32)
    acc_scr[...] = jnp.zeros(acc_scr.shape, F32)

    lane = lax.broadcasted_iota(jnp.int32, (bq, LANES), 1)
    key_c = lax.broadcasted_iota(jnp.int32, (bq, LANES), 0)
    k_aug = jnp.where(lane < 3, key_c % 256, jnp.where(lane < 6, key_c // 256, 0)).astype(F32).astype(BF16)
    rest = jnp.full((bq, LANES), sl2, F32)
    piece = jnp.zeros((bq, LANES), F32)
    for i in range(3):
        part = rest.astype(BF16).astype(F32)
        piece = jnp.where(lane % 3 == i, part, piece)
        rest = rest - part
    q_aug = jnp.where(lane < 3, piece, jnp.where(lane < 6, 256.0 * piece, 0.0)).astype(BF16)
    for mi in range(2):
        qa_scr[mi, :, :hd] = q_ref[:, mi * hd:(mi + 1) * hd]
        qa_scr[mi, :, hd:] = q_aug

    def scores(kb, slot, maps=(0, 1)):
        k0 = pl.multiple_of(kb * bq, bq)
        for mi in maps:
            k_blk = jnp.concatenate([k_ref[pl.ds(k0, bq), mi * hd:(mi + 1) * hd], k_aug], axis=1)
            s_scr[2 * slot + mi] = lax.dot_general(qa_scr[mi], k_blk, (((1,), (1,)), ((), ())),
                                                   preferred_element_type=F32)

    def softmax(kb, slot, diag, maps=(0, 1)):
        shift_blk = sl2 * ((kb - qi) * bq + jnp.zeros((1, LANES), jnp.int32)).astype(F32)
        for mi in maps:
            s_map = s_scr.at[2 * slot + mi]
            p_map = p_scr.at[2 * slot + mi]
            for c in range(bq // rc):
                rows = slice(c * rc, (c + 1) * rc)
                ncol = min(bq, -(-((c + 1) * rc) // CHUNK) * CHUNK) if diag else bq
                lane_tiles = [slice(t * LANES, (t + 1) * LANES) for t in range(-(-ncol // LANES))]

                def tile(lanes, t):
                    s_t = s_map[rows, lanes]
                    if not diag:
                        return s_t
                    r_i = lax.broadcasted_iota(jnp.int32, (rc, LANES), 0) + c * rc
                    c_i = lax.broadcasted_iota(jnp.int32, (rc, LANES), 1) + t * LANES
                    s_t = s_t + (2.0 * sl2) * jnp.minimum(r_i - c_i, 0).astype(F32)
                    return jnp.where((c_i // CHUNK) <= (r_i // CHUNK), s_t, NEG_BIG)

                tiles = [tile(lanes, t) for t, lanes in enumerate(lane_tiles)]
                m_old = m_scr[mi, rows, :]
                blk_max = jnp.max(functools.reduce(jnp.maximum, tiles), axis=-1, keepdims=True)
                m_new = jnp.maximum(m_old, blk_max + shift_blk)
                alpha = jnp.exp2(m_old - m_new)
                m_scr[mi, rows, :] = m_new
                a_scr[2 * slot + mi, rows, :] = alpha
                if not diag:
                    tiles = [tile(lanes, t) for t, lanes in enumerate(lane_tiles)]
                off = m_new - shift_blk
                ps = [jnp.exp2(s_t - off) for s_t in tiles]
                l_scr[mi, rows, :] = alpha * l_scr[mi, rows, :] + functools.reduce(jnp.add, ps)
                for t, p_t in enumerate(ps):
                    p_map[rows, t * LANES:(t + 1) * LANES] = p_t.astype(BF16)
                if len(ps) * LANES < bq:
                    p_map[rows, len(ps) * LANES:] = jnp.zeros((rc, bq - len(ps) * LANES), BF16)

    def weighted_values(kb, slot, maps=(0, 1)):
        k0 = pl.multiple_of(jnp.maximum(kb, 0) * bq, bq)
        for mi in maps:
            pv = jnp.dot(p_scr[2 * slot + mi], v_ref[pl.ds(k0, bq), :], preferred_element_type=F32)
            for t in range(pv.shape[1] // LANES):
                lanes = slice(t * LANES, (t + 1) * LANES)
                acc_scr[mi, :, lanes] = a_scr[2 * slot + mi] * acc_scr[mi, :, lanes] + pv[:, lanes]

    p_scr[2] = jnp.zeros((bq, bq), BF16)
    p_scr[3] = jnp.zeros((bq, bq), BF16)
    a_scr[2] = jnp.ones((bq, LANES), F32)
    a_scr[3] = jnp.ones((bq, LANES), F32)

    def step(blk, slot, diag, with_next):
        weighted_values(blk - 1, 1 - slot, (0,))
        if with_next:
            scores(blk + 1, 1 - slot)
        softmax(blk, slot, diag, (0,))
        weighted_values(blk - 1, 1 - slot, (1,))
        softmax(blk, slot, diag, (1,))

    def pair(j2, carry):
        step(2 * j2, 0, False, True)
        step(2 * j2 + 1, 1, False, True)
        return carry

    scores(0, 0)
    lax.fori_loop(0, qi // 2, pair, 0)

    @pl.when(qi % 2 == 1)
    def _():
        step(qi - 1, 0, False, True)
        step(qi, 1, True, False)
        weighted_values(qi, 1)

    @pl.when(qi % 2 == 0)
    def _():
        step(qi, 0, True, False)
        weighted_values(qi, 0)
    l1 = jnp.sum(l_scr[0], axis=-1, keepdims=True)
    l2 = jnp.sum(l_scr[1], axis=-1, keepdims=True)
    o_ref[...] = _finish_heads(acc_scr[0], l1, acc_scr[1], l2, lam_ref[0], g_ref[...], post_scale).astype(o_ref.dtype)


def attention_prompt(q_bf, k_bf, v_bf, batch, lam, slopes2, subln_g, post_scale):
    m, width = q_bf.shape
    s = m // batch
    dv = 2 * ATTN_HEAD_DIM
    heads = width // dv
    bq = _pick(s, 512)
    assert bq % CHUNK == 0
    nq = s // bq
    kv_spec = pl.BlockSpec((s, dv), lambda b, h, i: (b, h))
    q_spec = pl.BlockSpec((bq, dv), lambda b, h, i: (b * nq + i, h))
    smem = pl.BlockSpec(memory_space=pltpu.SMEM)
    return pl.pallas_call(
        functools.partial(_attn_prompt_kernel, bq=bq, post_scale=post_scale),
        grid=(batch, heads, nq),
        in_specs=[smem, smem, q_spec, kv_spec, kv_spec, pl.BlockSpec((1, dv), lambda b, h, i: (0, 0))],
        out_specs=q_spec,
        out_shape=jax.ShapeDtypeStruct((m, width), BF16),
        scratch_shapes=[pltpu.VMEM((4, bq, bq), F32), pltpu.VMEM((4, bq, bq), BF16), pltpu.VMEM((2, bq, dv), BF16),
                        pltpu.VMEM((2, bq, dv), F32), pltpu.VMEM((2, bq, LANES), F32),
                        pltpu.VMEM((2, bq, LANES), F32), pltpu.VMEM((4, bq, LANES), F32)],
        compiler_params=_params(3), name="attention_prompt",
    )(lam, slopes2, q_bf, k_bf, v_bf, subln_g.reshape(1, dv))


def _attn_sample_kernel(lam_ref, slope_ref, q_ref, kn_ref, vn_ref, ck_ref, cv_ref, g_ref, o_ref, *, post_scale):
    hd = ATTN_HEAD_DIM
    dv = 2 * hd
    lam = lam_ref[0]
    t = q_ref.shape[0]
    heads = q_ref.shape[1] // dv
    past = ck_ref.shape[0] // (2 * heads)

    def dist_mask(n_keys, key0):
        q_pos = past + lax.broadcasted_iota(jnp.int32, (t, n_keys), 0)
        k_pos = key0 + lax.broadcasted_iota(jnp.int32, (t, n_keys), 1)
        return jnp.abs(q_pos - k_pos).astype(F32), (k_pos // CHUNK) <= (q_pos // CHUNK)

    dist_p, vis_p = dist_mask(past, 0)
    dist_n, vis_n = dist_mask(t, past)
    dn = (((1,), (1,)), ((), ()))
    for h in range(heads):
        cols = slice(h * dv, (h + 1) * dv)
        sl2 = slope_ref[h]
        v_past = jnp.concatenate(
            [cv_ref[pl.ds(c * heads + h, past, stride=heads * (dv // LANES)), :] for c in range(dv // LANES)],
            axis=1).astype(BF16)
        res = []
        for mi in range(2):
            qm = q_ref[:, h * dv + mi * hd:h * dv + (mi + 1) * hd]
            k_past = ck_ref[pl.ds(2 * h + mi, past, stride=2 * heads), :].astype(BF16)
            s_p = lax.dot_general(qm, k_past, dn, preferred_element_type=F32)
            s_n = lax.dot_general(qm, kn_ref[:, h * dv + mi * hd:h * dv + (mi + 1) * hd], dn,
                                  preferred_element_type=F32)
            s_p = jnp.where(vis_p, s_p - sl2 * dist_p, NEG_BIG)
            s_n = jnp.where(vis_n, s_n - sl2 * dist_n, NEG_BIG)
            m = jnp.maximum(jnp.max(s_p, axis=-1, keepdims=True), jnp.max(s_n, axis=-1, keepdims=True))
            p_p = jnp.exp2(s_p - m)
            p_n = jnp.exp2(s_n - m)
            l = jnp.sum(p_p, axis=-1, keepdims=True) + jnp.sum(p_n, axis=-1, keepdims=True)
            acc = (jnp.dot(p_p.astype(BF16), v_past, preferred_element_type=F32)
                   + jnp.dot(p_n.astype(BF16), vn_ref[:, cols], preferred_element_type=F32))
            res += [acc, l]
        o_ref[:, cols] = _finish_heads(res[0], res[1], res[2], res[3], lam, g_ref[...],
                                       post_scale).astype(o_ref.dtype)


def attention_sample(q_bf, k_bf, v_bf, cache_k, cache_v, layer, lam, slopes2, subln_g, post_scale):
    _, batch, past, heads, _, hd = cache_k.shape
    m, width = q_bf.shape
    t = m // batch
    dv = 2 * hd
    new_spec = pl.BlockSpec((t, width), lambda b: (b, 0))
    n_layers = cache_k.shape[0]
    cache_k = cache_k.reshape(n_layers, batch, past * heads * 2, hd)
    cache_v = cache_v.reshape(n_layers, batch, past, heads, dv // LANES, LANES)
    cache_v = cache_v.transpose(0, 1, 2, 4, 3, 5).reshape(n_layers, batch, past * (dv // LANES) * heads, LANES)
    ck_spec = pl.BlockSpec((None, None, past * heads * 2, hd), lambda b: (layer, b, 0, 0))
    cv_spec = pl.BlockSpec((None, None, past * (dv // LANES) * heads, LANES), lambda b: (layer, b, 0, 0))
    smem = pl.BlockSpec(memory_space=pltpu.SMEM)
    return pl.pallas_call(
        functools.partial(_attn_sample_kernel, post_scale=post_scale),
        grid=(batch,),
        in_specs=[smem, smem, new_spec, new_spec, new_spec, ck_spec, cv_spec,
                  pl.BlockSpec((1, dv), lambda b: (0, 0))],
        out_specs=new_spec,
        out_shape=jax.ShapeDtypeStruct((m, width), BF16),
        compiler_params=_params(1), name="attention_sample",
    )(lam, slopes2, q_bf, k_bf, v_bf, cache_k, cache_v, subln_g.reshape(1, dv))


def _sgu_gate_kernel(u_ref, v_ref, g_ref, b_ref, w_ref, bs_ref, *out_refs, chunk_len, write_v):
    bt, width = u_ref.shape
    gd = width // SGU_GROUPS
    v = v_ref[...].astype(F32)
    mu = jnp.mean(v, axis=-1, keepdims=True)
    vc = v - mu
    vn = vc * lax.rsqrt(jnp.mean(vc * vc, axis=-1, keepdims=True) + EPS) * g_ref[...] + b_ref[...]
    if write_v:
        out_refs[1][...] = vn
    vn_bf = vn.astype(BF16)
    mr = w_ref.shape[1]
    row = lax.broadcasted_iota(jnp.int32, (mr, mr), 0)
    col = lax.broadcasted_iota(jnp.int32, (mr, mr), 1)
    keep = ((row // chunk_len) == (col // chunk_len)) & ((col % chunk_len) <= (row % chunk_len))
    for gi in range(SGU_GROUPS):
        w = jnp.where(keep, w_ref[gi], 0.0).astype(BF16)
        bias = bs_ref[gi]
        for c in range(bt // mr):
            rows = slice(c * mr, (c + 1) * mr)
            cols = slice(gi * gd, (gi + 1) * gd)
            mixed = jnp.dot(w, vn_bf[rows, cols], preferred_element_type=F32) + bias
            out_refs[0][rows, cols] = (u_ref[rows, cols].astype(F32) * mixed).astype(BF16)


def sgu_gate(z, ln_g, ln_b, w_s, b_s, chunk_len, write_v):
    m, two_w = z.shape
    width = two_w // 2
    bt = _pick(m, SGU_MIX_ROWS)
    reps = bt // chunk_len
    w_t = jnp.tile(w_s[:, :chunk_len, :chunk_len], (1, reps, reps))
    b_t = jnp.tile(b_s[:, :chunk_len], (1, reps))[:, :, None]
    row_spec = lambda c: pl.BlockSpec((bt, width), lambda i, c=c: (i, c))
    vec_spec = pl.BlockSpec((1, width), lambda i: (0, 0))
    out_specs = [row_spec(0)]
    out_shape = [jax.ShapeDtypeStruct((m, width), BF16)]
    if write_v:
        out_specs.append(row_spec(0))
        out_shape.append(jax.ShapeDtypeStruct((m, width), F32))
    return pl.pallas_call(
        functools.partial(_sgu_gate_kernel, chunk_len=chunk_len, write_v=write_v),
        grid=(m // bt,),
        in_specs=[row_spec(0), row_spec(1), vec_spec, vec_spec,
                  pl.BlockSpec((SGU_GROUPS, bt, bt), lambda i: (0, 0, 0)),
                  pl.BlockSpec((SGU_GROUPS, bt, 1), lambda i: (0, 0, 0))],
        out_specs=out_specs, out_shape=out_shape,
        compiler_params=_params(1), name="sgu_gate",
    )(z, z, ln_g.reshape(1, width), ln_b.reshape(1, width), w_t, b_t)


def _swiglu(x, wg, wu, wd):
    g = jnp.dot(x, wg, preferred_element_type=F32)
    u = jnp.dot(x, wu, preferred_element_type=F32)
    h = (g * jax.nn.sigmoid(g) * u).astype(BF16)
    return jnp.dot(h, wd, preferred_element_type=F32)


def _moe_kernel(blk_e_ref, n_used_ref, x_ref, wg_ref, wu_ref, wd_ref, *refs, blk0):
    o_ref, wg_bf, wu_bf, wd_bf = refs[-4:]
    b = pl.program_id(0)
    e = blk_e_ref[blk0 + b]
    e_prev = blk_e_ref[blk0 + jnp.maximum(b - 1, 0)]

    @pl.when((b == 0) | (e != e_prev))
    def _():
        wg_bf[...] = wg_ref[...].astype(BF16)
        wu_bf[...] = wu_ref[...].astype(BF16)
        wd_bf[...] = wd_ref[...].astype(BF16)

    @pl.when(blk0 + b < n_used_ref[0])
    def _():
        o_ref[...] = _swiglu(x_ref[...], wg_bf[...], wu_bf[...], wd_bf[...]).astype(o_ref.dtype)

    @pl.when(blk0 + b >= n_used_ref[0])
    def _():
        o_ref[...] = jnp.zeros_like(o_ref)


def moe_experts(xg, blk_expert, n_used, layer, w_gate, w_up, w_down, out, blk0):
    rows, d = xg.shape
    ed = w_gate.shape[3]
    x_spec = pl.BlockSpec((MOE_BLOCK, d), lambda b, be, nu: (b, 0))
    w_idx = lambda b, be, nu: (layer, be[blk0 + b], 0, 0)
    in_specs = [x_spec, pl.BlockSpec((None, None, d, ed), w_idx), pl.BlockSpec((None, None, d, ed), w_idx),
                pl.BlockSpec((None, None, ed, d), w_idx)]
    args = [blk_expert, n_used, xg, w_gate, w_up, w_down]
    aliases = {}
    if out is not None:
        in_specs.append(pl.BlockSpec(memory_space=pl.ANY))
        args.append(out)
        aliases = {len(args) - 1: 0}
    return pl.pallas_call(
        functools.partial(_moe_kernel, blk0=blk0),
        grid_spec=pltpu.PrefetchScalarGridSpec(
            num_scalar_prefetch=2, grid=(rows // MOE_BLOCK,),
            in_specs=in_specs,
            out_specs=pl.BlockSpec((MOE_BLOCK, d), lambda b, be, nu: (blk0 + b, 0)),
            scratch_shapes=[pltpu.VMEM((d, ed), BF16), pltpu.VMEM((d, ed), BF16), pltpu.VMEM((ed, d), BF16)]),
        out_shape=jax.ShapeDtypeStruct((blk_expert.shape[0] * MOE_BLOCK, d), BF16),
        input_output_aliases=aliases,
        compiler_params=_params(1), name="moe_experts",
    )(*args)


def _shared_kernel(h_ref, wg_ref, wu_ref, wd_ref, o_ref):
    o_ref[...] = _swiglu(h_ref[...], wg_ref[...], wu_ref[...], wd_ref[...]).astype(o_ref.dtype)


def shared_expert(h, sw_gate, sw_up, sw_down):
    m, d = h.shape
    ed = sw_gate.shape[1]
    bm = _pick(m, 512)
    row_spec = pl.BlockSpec((bm, d), lambda i: (i, 0))
    full = lambda a, b: pl.BlockSpec((a, b), lambda i: (0, 0))
    return pl.pallas_call(
        _shared_kernel, grid=(m // bm,),
        in_specs=[row_spec, full(d, ed), full(d, ed), full(ed, d)],
        out_specs=row_spec,
        out_shape=jax.ShapeDtypeStruct((m, d), BF16),
        compiler_params=_params(1), name="shared_expert",
    )(h, sw_gate, sw_up, sw_down)


def _combine_kernel(s_ref, p_ref, w_ref, x_ref, gate_ref, o_ref):
    ffn = s_ref[...].astype(F32)
    w = w_ref[...]
    for k in range(TOP_K):
        ffn = ffn + w[:, k:k + 1] * p_ref[k].astype(F32)
    o_ref[...] = x_ref[...] + gate_ref[...] * ffn


def moe_combine_residual(stream, shared, picked, wts, row0, x, gate, tok0):
    m, d = x.shape
    n = picked.shape[1]
    bm = _row_block(stream, n, 256)
    assert row0 % bm == 0 and tok0 % bm == 0 and n % bm == 0
    off, off_all = tok0 // bm, (row0 + tok0) // bm
    row_spec = pl.BlockSpec((bm, d), lambda i: (i + off, 0))
    all_rows = lambda width: pl.BlockSpec((bm, width), lambda i: (i + off_all, 0))
    return pl.pallas_call(
        _combine_kernel, grid=(n // bm,),
        in_specs=[all_rows(d), pl.BlockSpec((TOP_K, bm, d), lambda i: (0, i, 0)), all_rows(TOP_K), row_spec,
                  _mod_spec(stream, bm, d, lambda i: (i + off, 0))],
        out_specs=row_spec,
        out_shape=jax.ShapeDtypeStruct((m, d), F32),
        input_output_aliases={3: 0},
        compiler_params=_params(1), name="moe_combine",
    )(shared, picked, wts, x, gate)


def _dest_kernel(start_ref, idx_ref, rank_ref, dest_ref):
    idx = idx_ref[...]

    def body(e, acc):
        return jnp.where(idx == e, start_ref[e], acc)

    dest_ref[...] = rank_ref[...] + lax.fori_loop(0, start_ref.shape[0], body, jnp.zeros_like(idx))


def assignment_rows(pad_start, idx, rank):
    return pl.pallas_call(
        _dest_kernel,
        in_specs=[pl.BlockSpec(memory_space=pltpu.SMEM), pl.BlockSpec(memory_space=pltpu.VMEM),
                  pl.BlockSpec(memory_space=pltpu.VMEM)],
        out_specs=pl.BlockSpec(memory_space=pltpu.VMEM),
        out_shape=jax.ShapeDtypeStruct(idx.shape, jnp.int32),
        name="assignment_rows",
    )(pad_start, idx, rank)


def moe_routed(h_all, idx, rank, counts, layer, w_gate, w_up, w_down):
    t, d = h_all.shape
    n_exp = counts.shape[0]
    n_assign = t * TOP_K
    counts = counts.reshape(n_exp).astype(jnp.int32)
    padded = (counts + MOE_BLOCK - 1) // MOE_BLOCK * MOE_BLOCK
    pad_end = jnp.cumsum(padded)
    pad_start = pad_end - padded
    dest = assignment_rows(pad_start, idx, rank).reshape(-1)
    n_blocks = -(-(n_assign + n_exp * (MOE_BLOCK - 1)) // MOE_BLOCK)
    n_blocks = -(-n_blocks // MOE_CHUNKS) * MOE_CHUNKS
    n_rows = n_blocks * MOE_BLOCK
    tok = jnp.tile(jnp.arange(t, dtype=jnp.int32), TOP_K)
    src_tok = jnp.arange(n_rows, dtype=jnp.int32) % t
    src_tok = src_tok.at[dest].add(tok - dest % t, unique_indices=True, mode="promise_in_bounds")
    blk_start = jnp.arange(n_blocks, dtype=jnp.int32) * MOE_BLOCK
    blk_expert = jnp.minimum(jnp.sum((pad_end[None, :] <= blk_start[:, None]).astype(jnp.int32), axis=1), n_exp - 1)
    n_used = pad_end[-1:] // MOE_BLOCK
    out = None
    chunk_rows = n_rows // MOE_CHUNKS
    for c in range(MOE_CHUNKS):
        xg = h_all.at[src_tok[c * chunk_rows:(c + 1) * chunk_rows]].get(mode="promise_in_bounds")
        out = moe_experts(xg, blk_expert, n_used, layer, w_gate, w_up, w_down, out, c * (n_blocks // MOE_CHUNKS))
    return out, dest.reshape(TOP_K, t)


def _stream(rows_per_batch):
    return dict(per_row=rows_per_batch < 1024, rows_per_batch=rows_per_batch)


def _mod_rows(stream, mod):
    b, d = mod.shape
    if not stream["per_row"]:
        return mod.reshape(b, 1, d)
    return jnp.repeat(mod, stream["rows_per_batch"], axis=0).reshape(1, -1, d)


def kernel(x_prompt, x_sample, cache_k_attn, cache_v_attn, c_prompt, c_sample, norm_mix_g, norm_ffn_g, ada_w, ada_b, attn_w_qkv, attn_w_o, attn_q_norm, attn_k_norm, attn_lambda_q1, attn_lambda_k1, attn_lambda_q2, attn_lambda_k2, attn_subln_g, sgu_w_in, sgu_ln_g, sgu_ln_b, sgu_w_s, sgu_b_s, sgu_w_o, moe_w_router, moe_b_router, moe_w_gate, moe_w_up, moe_w_down, shared_w_gate, shared_w_up, shared_w_down):
    bp, sp, d = x_prompt.shape
    bs, ss, _ = x_sample.shape
    depth = ada_w.shape[0]
    dv = 2 * ATTN_HEAD_DIM
    heads = d // dv
    mp, ms = bp * sp, bs * ss
    streams = (_stream(sp), _stream(ss))
    xs = [x_prompt.reshape(mp, d), x_sample.reshape(ms, d)]

    mod = ada_modulation(jnp.concatenate([c_prompt, c_sample], axis=0), ada_w, ada_b)
    slopes2 = (2.0 ** (-8.0 * jnp.arange(1, heads + 1, dtype=F32) / heads)) * LOG2E

    k_out, v_out, sgu_out = [[], []], [[], []], []
    for i in range(depth):
        j = i // N_MIXERS
        mods = []
        for si, (stream, rows) in enumerate(zip(streams, (slice(0, bp), slice(bp, bp + bs)))):
            mods.append([_mod_rows(stream, mod[i, rows, c * d:(c + 1) * d]) for c in range(6)])

        if i % N_MIXERS == 0:
            lam_init = 0.8 - 0.6 * math.exp(-0.3 * i)
            f = lambda a: a[j].astype(F32)
            lam = (jnp.exp(jnp.sum(f(attn_lambda_q1) * f(attn_lambda_k1)))
                   - jnp.exp(jnp.sum(f(attn_lambda_q2) * f(attn_lambda_k2))) + lam_init).reshape(1)
            for si, stream in enumerate(streams):
                h = modulate(stream, xs[si], norm_mix_g[i], mods[si][0], mods[si][1])
                q_bf, k32, k_bf, v32, v_bf = qkv_project(stream, h, attn_w_qkv[j], attn_q_norm[j], attn_k_norm[j])
                if si == 0:
                    o = attention_prompt(q_bf, k_bf, v_bf, bp, lam, slopes2, attn_subln_g[j], 1.0 - lam_init)
                else:
                    o = attention_sample(q_bf, k_bf, v_bf, cache_k_attn, cache_v_attn, j, lam, slopes2,
                                         attn_subln_g[j], 1.0 - lam_init)
                xs[si] = project_residual(stream, o, attn_w_o[j], xs[si], mods[si][2], "attn_out_proj")
                k_out[si].append(k32)
                v_out[si].append(v32)
        else:
            for si, stream in enumerate(streams):
                h = modulate(stream, xs[si], norm_mix_g[i], mods[si][0], mods[si][1])
                (z,) = matmul(h, sgu_w_in[j], bm=_row_block(stream, h.shape[0], 1024),
                              bn=_pick(sgu_w_in.shape[2], 1024),
                              out_dtypes=(BF16,), epilogue=_gelu_epilogue, name="sgu_in_proj")
                chunk_len = SGU_CHUNK if si == 0 else ss
                res = sgu_gate(z, sgu_ln_g[j], sgu_ln_b[j], sgu_w_s[j], sgu_b_s[j], chunk_len, write_v=(si == 1))
                if si == 1:
                    sgu_out.append(res[1])
                xs[si] = project_residual(stream, res[0], sgu_w_o[j], xs[si], mods[si][2], "sgu_out_proj")

        counts = jnp.zeros((moe_w_router.shape[2], 1), F32)
        per_token = ()
        for si, (stream, row0) in enumerate(zip(streams, (0, mp))):
            *per_token, counts = modulate(stream, xs[si], norm_ffn_g[i], mods[si][3], mods[si][4],
                                          router=(moe_w_router[i], moe_b_router[i], counts, mp + ms, row0, per_token))
        h_all, idx, wts, rank = per_token
        shared = shared_expert(h_all, *[w[i].astype(BF16) for w in (shared_w_gate, shared_w_up, shared_w_down)])
        out, dest = moe_routed(h_all, idx, rank, counts, i, moe_w_gate, moe_w_up, moe_w_down)
        for si, (stream, row0, m_rows) in enumerate(zip(streams, (0, mp), (mp, ms))):
            piece = _pick(m_rows, COMBINE_TOKENS)
            for tok0 in range(0, m_rows, piece):
                rows = dest[:, row0 + tok0:row0 + tok0 + piece].reshape(-1)
                picked = out.at[rows].get(mode="promise_in_bounds", unique_indices=True).reshape(TOP_K, piece, d)
                xs[si] = moe_combine_residual(stream, shared, picked, wts.T, row0, xs[si], mods[si][5], tok0)

    n_attn = len(k_out[0])
    k_prompt = jnp.stack(k_out[0]).reshape(n_attn, bp, sp, heads, 2, ATTN_HEAD_DIM)
    v_prompt = jnp.stack(v_out[0]).reshape(n_attn, bp, sp, heads, dv)
    k_sample = jnp.stack(k_out[1]).reshape(n_attn, bs, ss, heads, 2, ATTN_HEAD_DIM)
    v_sample = jnp.stack(v_out[1]).reshape(n_attn, bs, ss, heads, dv)
    sgu_v = jnp.stack(sgu_out).reshape(len(sgu_out), bs, ss, -1)
    return (xs[0].reshape(bp, sp, d), xs[1].reshape(bs, ss, d), k_prompt, v_prompt, k_sample, v_sample, sgu_v)
```

```python
import functools
import math

import jax
import jax.numpy as jnp
from jax import lax
from jax.experimental import pallas as pl
from jax.experimental.pallas import tpu as pltpu

F32 = jnp.float32
BF16 = jnp.bfloat16

EPS = 1e-6
CHUNK = 64
N_MIXERS = 2
ATTN_HEAD_DIM = 128
SGU_GROUPS = 8
SGU_CHUNK = 128
TOP_K = 8
N_EXPERT_GROUPS = 8
TOPK_GROUPS = 4
ROUTED_SCALE = 2.5
LOG2E = 1.4426950408889634
NEG_BIG = -1e30

VMEM_LIMIT = 56 * 1024 * 1024
MOE_BLOCK = 512
MOE_CHUNKS = 8
SGU_MIX_ROWS = 256
COMBINE_TOKENS = 2048
ATTN_ROW_CHUNK = 64
LANES = 128


def _params(n_axes):
    return pltpu.CompilerParams(dimension_semantics=("arbitrary",) * n_axes, vmem_limit_bytes=VMEM_LIMIT)


def _pick(n, pref):
    if n <= pref:
        return n
    b = pref
    while n % b:
        b //= 2
    return b


def _ada_kernel(c_ref, w_ref, b_ref, o_ref):
    c = c_ref[...]
    x = (c * jax.nn.sigmoid(c)).astype(BF16)
    o_ref[...] = jnp.dot(x, w_ref[...].astype(BF16), preferred_element_type=F32) + b_ref[...]


def ada_modulation(c, ada_w, ada_b):
    n_layers, d, n = ada_w.shape
    r = c.shape[0]
    bn = _pick(n, 1024)
    return pl.pallas_call(
        _ada_kernel,
        grid=(n_layers, n // bn),
        in_specs=[
            pl.BlockSpec((r, d), lambda l, j: (0, 0)),
            pl.BlockSpec((None, d, bn), lambda l, j: (l, 0, j)),
            pl.BlockSpec((None, 1, bn), lambda l, j: (l, 0, j)),
        ],
        out_specs=pl.BlockSpec((None, r, bn), lambda l, j: (l, 0, j)),
        out_shape=jax.ShapeDtypeStruct((n_layers, r, n), F32),
        compiler_params=_params(2),
        name="ada_modulation",
    )(c, ada_w, ada_b.reshape(n_layers, 1, n))


def _modulated(x, g, shift, scale):
    y = x * lax.rsqrt(jnp.mean(x * x, axis=-1, keepdims=True) + EPS)
    return y * g * (1.0 + scale) + shift


def _modulate_kernel(x_ref, g_ref, sh_ref, sc_ref, o_ref):
    o_ref[...] = _modulated(x_ref[...], g_ref[...], sh_ref[...], sc_ref[...]).astype(o_ref.dtype)


def _split3(x):
    hi = x.astype(BF16)
    lo = (x - hi.astype(F32)).astype(BF16)
    return hi, lo


def _first_max(x, ids, n_ids, axes):
    mx = x
    for ax in axes:
        mx = jnp.max(mx, axis=ax, keepdims=True)
    arg = jnp.where(x == mx, ids, n_ids)
    for ax in axes:
        arg = jnp.min(arg, axis=ax, keepdims=True)
    return mx, arg


def _sum_axes(x, axes):
    for ax in axes:
        x = jnp.sum(x, axis=ax, keepdims=True)
    return x


def _modulate_router_kernel(x_ref, g_ref, sh_ref, sc_ref, wrt_ref, br_ref, cin_ref, *refs, n_prev):
    o_ref, idx_ref, wts_ref, rank_ref, cnt_ref, carry = refs[n_prev:]
    h = _modulated(x_ref[...], g_ref[...], sh_ref[...], sc_ref[...])
    o_ref[...] = h.astype(o_ref.dtype)
    bm = h.shape[0]
    n_exp = wrt_ref.shape[0]
    per = n_exp // N_EXPERT_GROUPS
    grp_shape = (N_EXPERT_GROUPS, per, bm)

    h_hi, h_lo = _split3(h)
    w_hi, w_lo = _split3(wrt_ref[...])
    nt = (((1,), (1,)), ((), ()))
    logits = (lax.dot_general(w_hi, h_hi, nt, preferred_element_type=F32)
              + lax.dot_general(w_lo, h_hi, nt, preferred_element_type=F32)
              + lax.dot_general(w_hi, h_lo, nt, preferred_element_type=F32))
    s = jax.nn.sigmoid(logits)
    s3 = s.reshape(grp_shape)
    sb3 = (s + br_ref[...]).reshape(grp_shape)

    sub = lax.broadcasted_iota(jnp.int32, grp_shape, 1)
    gid = lax.broadcasted_iota(jnp.int32, (N_EXPERT_GROUPS, 1, bm), 0)
    eid = lax.broadcasted_iota(jnp.int32, grp_shape, 0) * per + sub

    m1, i1 = _first_max(sb3, sub, per, (1,))
    m2 = jnp.max(jnp.where(sub == i1, -jnp.inf, sb3), axis=1, keepdims=True)
    work = m1 + m2
    chosen = jnp.zeros_like(work)
    for _ in range(TOPK_GROUPS):
        _, gi = _first_max(work, gid, N_EXPERT_GROUPS, (0,))
        chosen = jnp.where(gid == gi, 1.0, chosen)
        work = jnp.where(gid == gi, -jnp.inf, work)
    sel = jnp.where(chosen > 0.0, sb3, -jnp.inf)

    hits, ids, raw = [], [], []
    for _ in range(TOP_K):
        _, ei = _first_max(sel, eid, n_exp, (1, 0))
        hit = eid == ei
        raw.append(_sum_axes(jnp.where(hit, s3, 0.0), (1, 0)))
        sel = jnp.where(hit, -jnp.inf, sel)
        hits.append(hit)
        ids.append(ei)
    total = sum(raw)

    member = sum(jnp.where(hit, 1.0, 0.0) for hit in hits).reshape(n_exp, bm)
    r_i = lax.broadcasted_iota(jnp.int32, (bm, bm), 0)
    c_i = lax.broadcasted_iota(jnp.int32, (bm, bm), 1)
    before = jnp.where(r_i < c_i, 1.0, 0.0).astype(BF16)

    @pl.when(pl.program_id(0) == 0)
    def _():
        carry[...] = cin_ref[...]

    rank_all = (jnp.dot(member.astype(BF16), before, preferred_element_type=F32) + carry[...]).reshape(grp_shape)
    carry[...] += jnp.sum(member, axis=1, keepdims=True)
    cnt_ref[...] = carry[...]
    for k in range(TOP_K):
        idx_ref[k:k + 1, :] = ids[k].reshape(1, bm)
        wts_ref[k:k + 1, :] = (raw[k] / total * ROUTED_SCALE).reshape(1, bm)
        rank_ref[k:k + 1, :] = _sum_axes(jnp.where(hits[k], rank_all, 0.0), (1, 0)).reshape(1, bm).astype(jnp.int32)


def _row_block(stream, m, pref):
    return _pick(m if stream["per_row"] else stream["rows_per_batch"], pref)


def _mod_spec(stream, bm, bn, ij):
    if stream["per_row"]:
        return pl.BlockSpec((None, bm, bn), lambda *g: (0,) + tuple(ij(*g)))
    bpg = stream["rows_per_batch"] // bm
    return pl.BlockSpec((None, 1, bn), lambda *g: (ij(*g)[0] // bpg, 0, ij(*g)[1]))


def modulate(stream, x, g, shift, scale, router=None):
    m, d = x.shape
    bm = _row_block(stream, m, 512)
    row_spec = pl.BlockSpec((bm, d), lambda i: (i, 0))
    mod_spec = _mod_spec(stream, bm, d, lambda i: (i, 0))
    g_spec = pl.BlockSpec((1, d), lambda i: (0, 0))
    if router is None:
        return pl.pallas_call(
            _modulate_kernel, grid=(m // bm,),
            in_specs=[row_spec, g_spec, mod_spec, mod_spec], out_specs=row_spec,
            out_shape=jax.ShapeDtypeStruct((m, d), BF16),
            compiler_params=_params(1), name="modulate",
        )(x, g.reshape(1, d), shift, scale)
    w_router, b_router, counts_in, t_all, row0, prev = router
    assert row0 % bm == 0
    off = row0 // bm
    e = w_router.shape[1]
    out_rows = pl.BlockSpec((bm, d), lambda i: (i + off, 0))
    tok_spec = pl.BlockSpec((TOP_K, bm), lambda i: (0, i + off))
    cnt_spec = pl.BlockSpec((e, 1), lambda i: (0, 0))
    n_in = 7
    return pl.pallas_call(
        functools.partial(_modulate_router_kernel, n_prev=len(prev)), grid=(m // bm,),
        in_specs=[row_spec, g_spec, mod_spec, mod_spec, pl.BlockSpec((e, d), lambda i: (0, 0)), cnt_spec, cnt_spec]
        + [pl.BlockSpec(memory_space=pl.ANY)] * len(prev),
        out_specs=[out_rows, tok_spec, tok_spec, tok_spec, cnt_spec],
        out_shape=[jax.ShapeDtypeStruct((t_all, d), BF16), jax.ShapeDtypeStruct((TOP_K, t_all), jnp.int32),
                   jax.ShapeDtypeStruct((TOP_K, t_all), F32), jax.ShapeDtypeStruct((TOP_K, t_all), jnp.int32),
                   jax.ShapeDtypeStruct((e, 1), F32)],
        input_output_aliases={n_in + k: k for k in range(len(prev))},
        scratch_shapes=[pltpu.VMEM((e, 1), F32)],
        compiler_params=_params(1), name="modulate_router",
    )(x, g.reshape(1, d), shift, scale, w_router.T, b_router.astype(F32).reshape(e, 1), counts_in, *prev)


def _mm_kernel(x_ref, w_ref, *refs, n_extra, epilogue):
    extra, outs, w_bf = refs[:n_extra], refs[n_extra:-1], refs[-1]

    @pl.when(pl.program_id(1) == 0)
    def _():
        w_bf[...] = w_ref[...].astype(BF16)

    acc = jnp.dot(x_ref[...], w_bf[...], preferred_element_type=F32)
    epilogue(acc, extra, outs)


def matmul(x, w, *, bm, bn, col_block_off=0, n_cols=None, extra=(), extra_specs=(), out_dtypes, epilogue, name):
    m, k = x.shape
    n = w.shape[1] if n_cols is None else n_cols
    grid = (n // bn, m // bm)
    out_spec = pl.BlockSpec((bm, bn), lambda j, i: (i, j))
    return pl.pallas_call(
        functools.partial(_mm_kernel, n_extra=len(extra), epilogue=epilogue),
        grid=grid,
        in_specs=[pl.BlockSpec((bm, k), lambda j, i: (i, 0)),
                  pl.BlockSpec((k, bn), lambda j, i: (0, j + col_block_off))] + list(extra_specs),
        out_specs=[out_spec] * len(out_dtypes),
        out_shape=[jax.ShapeDtypeStruct((m, n), dt) for dt in out_dtypes],
        scratch_shapes=[pltpu.VMEM((k, bn), BF16)],
        compiler_params=_params(2), name=name,
    )(x, w, *extra)


def _head_rms(acc, gain, post_scale):
    pieces = []
    for c in range(acc.shape[1] // ATTN_HEAD_DIM):
        seg = acc[:, c * ATTN_HEAD_DIM:(c + 1) * ATTN_HEAD_DIM]
        y = seg * lax.rsqrt(jnp.mean(seg * seg, axis=-1, keepdims=True) + EPS)
        pieces.append(y * (gain * post_scale))
    return jnp.concatenate(pieces, axis=-1)


def _q_epilogue(acc, extra, outs):
    outs[0][...] = _head_rms(acc, extra[0][...], ATTN_HEAD_DIM ** -0.5 * LOG2E).astype(BF16)


def _v_epilogue(acc, extra, outs):
    outs[0][...] = acc
    outs[1][...] = acc.astype(BF16)


def _gelu_epilogue(acc, extra, outs):
    outs[0][...] = (0.5 * acc * (1.0 + lax.erf(acc * (2.0 ** -0.5)))).astype(BF16)


def _residual_epilogue(acc, extra, outs):
    x_ref, gate_ref = extra
    outs[0][...] = x_ref[...] + gate_ref[...] * acc


def _k_proj_kernel(h_ref, w_ref, g_ref, k32_ref, kbf_ref):
    acc = jnp.dot(h_ref[...], w_ref[...], preferred_element_type=F32)
    bm = acc.shape[0]
    groups = acc.shape[1] // ATTN_HEAD_DIM
    gain = g_ref[...]
    for g in range(groups):
        cols = slice(g * ATTN_HEAD_DIM, (g + 1) * ATTN_HEAD_DIM)
        seg = acc[:, cols]
        kn = seg * lax.rsqrt(jnp.mean(seg * seg, axis=-1, keepdims=True) + EPS) * gain
        k32_ref[pl.ds(g, bm, stride=groups), :] = kn
        kbf_ref[:, cols] = kn.astype(BF16)


def k_project(stream, h, w_k, k_norm):
    m, d = h.shape
    n = w_k.shape[1]
    groups = n // ATTN_HEAD_DIM
    bm = _row_block(stream, m, 512)
    return pl.pallas_call(
        _k_proj_kernel, grid=(m // bm,),
        in_specs=[pl.BlockSpec((bm, d), lambda i: (i, 0)), pl.BlockSpec((d, n), lambda i: (0, 0)),
                  pl.BlockSpec((1, ATTN_HEAD_DIM), lambda i: (0, 0))],
        out_specs=[pl.BlockSpec((bm * groups, ATTN_HEAD_DIM), lambda i: (i, 0)),
                   pl.BlockSpec((bm, n), lambda i: (i, 0))],
        out_shape=[jax.ShapeDtypeStruct((m * groups, ATTN_HEAD_DIM), F32), jax.ShapeDtypeStruct((m, n), BF16)],
        compiler_params=_params(1), name="k_proj",
    )(h, w_k, k_norm.reshape(1, -1))


def qkv_project(stream, h, w_qkv, q_norm, k_norm):
    m, d = h.shape
    bm = _row_block(stream, m, 1024)
    bn = _pick(d, 1024)
    nb = d // bn
    gain_spec = pl.BlockSpec((1, ATTN_HEAD_DIM), lambda j, i: (0, 0))
    common = dict(bm=bm, bn=bn, n_cols=d)
    (q_bf,) = matmul(h, w_qkv, col_block_off=0, extra=(q_norm.reshape(1, -1),), extra_specs=(gain_spec,),
                     out_dtypes=(BF16,), epilogue=_q_epilogue, name="q_proj", **common)
    k32, k_bf = k_project(stream, h, w_qkv[:, d:2 * d].astype(BF16), k_norm)
    v32, v_bf = matmul(h, w_qkv, col_block_off=2 * nb, out_dtypes=(F32, BF16), epilogue=_v_epilogue,
                       name="v_proj", **common)
    return q_bf, k32, k_bf, v32, v_bf


def project_residual(stream, a, w, x, gate, name):
    k, n = w.shape
    deep = k > 4096
    bm = _row_block(stream, a.shape[0], 512 if deep else 1024)
    bn = _pick(n, 512 if deep else 1024)
    (out,) = matmul(a, w, bm=bm, bn=bn, extra=(x, gate),
                    extra_specs=(pl.BlockSpec((bm, bn), lambda j, i: (i, j)),
                                 _mod_spec(stream, bm, bn, lambda j, i: (i, j))),
                    out_dtypes=(F32,), epilogue=_residual_epilogue, name=name)
    return out


def _finish_heads(acc1, l1, acc2, l2, lam, g, post_scale):
    o = acc1 / l1 - lam * (acc2 / l2)
    o = o * lax.rsqrt(jnp.mean(o * o, axis=-1, keepdims=True) + EPS)
    return o * (g * post_scale)


def _attn_prompt_kernel(lam_ref, slope_ref, q_ref, k_ref, v_ref, g_ref, o_ref,
                        s_scr, p_scr, qa_scr, acc_scr, m_scr, l_scr, a_scr, *, bq, post_scale):
    hd = ATTN_HEAD_DIM
    rc = ATTN_ROW_CHUNK
    qi = pl.program_id(2)
    hh = pl.program_id(1)
    sl2 = slope_ref[hh]

    m_scr[...] = jnp.full(m_scr.shape, NEG_BIG, F32)
    l_scr[...] = jnp.zeros(l_scr.shape, F32)
    acc_scr[...] = jnp.zeros(acc_scr.shape, F32)

    lane = lax.broadcasted_iota(jnp.int32, (bq, LANES), 1)
    key_c = lax.broadcasted_iota(jnp.int32, (bq, LANES), 0)
    k_aug = jnp.where(lane < 3, key_c % 256, jnp.where(lane < 6, key_c // 256, 0)).astype(F32).astype(BF16)
    rest = jnp.full((bq, LANES), sl2, F32)
    piece = jnp.zeros((bq, LANES), F32)
    for i in range(3):
        part = rest.astype(BF16).astype(F32)
        piece = jnp.where(lane % 3 == i, part, piece)
        rest = rest - part
    q_aug = jnp.where(lane < 3, piece, jnp.where(lane < 6, 256.0 * piece, 0.0)).astype(BF16)
    for mi in range(2):
        qa_scr[mi, :, :hd] = q_ref[:, mi * hd:(mi + 1) * hd]
        qa_scr[mi, :, hd:] = q_aug

    def scores(kb, slot, maps=(0, 1)):
        k0 = pl.multiple_of(kb * bq, bq)
        for mi in maps:
            k_blk = jnp.concatenate([k_ref[pl.ds(k0, bq), mi * hd:(mi + 1) * hd], k_aug], axis=1)
            s_scr[2 * slot + mi] = lax.dot_general(qa_scr[mi], k_blk, (((1,), (1,)), ((), ())),
                                                   preferred_element_type=F32)

    def softmax(kb, slot, diag, maps=(0, 1)):
        shift_blk = sl2 * ((kb - qi) * bq + jnp.zeros((1, LANES), jnp.int32)).astype(F32)
        for mi in maps:
            s_map = s_scr.at[2 * slot + mi]
            p_map = p_scr.at[2 * slot + mi]
            for c in range(bq // rc):
                rows = slice(c * rc, (c + 1) * rc)
                ncol = min(bq, -(-((c + 1) * rc) // CHUNK) * CHUNK) if diag else bq
                lane_tiles = [slice(t * LANES, (t + 1) * LANES) for t in range(-(-ncol // LANES))]

                def tile(lanes, t):
                    s_t = s_map[rows, lanes]
                    if not diag:
                        return s_t
                    r_i = lax.broadcasted_iota(jnp.int32, (rc, LANES), 0) + c * rc
                    c_i = lax.broadcasted_iota(jnp.int32, (rc, LANES), 1) + t * LANES
                    s_t = s_t + (2.0 * sl2) * jnp.minimum(r_i - c_i, 0).astype(F32)
                    return jnp.where((c_i // CHUNK) <= (r_i // CHUNK), s_t, NEG_BIG)

                tiles = [tile(lanes, t) for t, lanes in enumerate(lane_tiles)]
                m_old = m_scr[mi, rows, :]
                blk_max = jnp.max(functools.reduce(jnp.maximum, tiles), axis=-1, keepdims=True)
                m_new = jnp.maximum(m_old, blk_max + shift_blk)
                alpha = jnp.exp2(m_old - m_new)
                m_scr[mi, rows, :] = m_new
                a_scr[2 * slot + mi, rows, :] = alpha
                if not diag:
                    tiles = [tile(lanes, t) for t, lanes in enumerate(lane_tiles)]
                off = m_new - shift_blk
                ps = [jnp.exp2(s_t - off) for s_t in tiles]
                l_scr[mi, rows, :] = alpha * l_scr[mi, rows, :] + functools.reduce(jnp.add, ps)
                for t, p_t in enumerate(ps):
                    p_map[rows, t * LANES:(t + 1) * LANES] = p_t.astype(BF16)
                if len(ps) * LANES < bq:
                    p_map[rows, len(ps) * LANES:] = jnp.zeros((rc, bq - len(ps) * LANES), BF16)

    def weighted_values(kb, slot, maps=(0, 1)):
        k0 = pl.multiple_of(jnp.maximum(kb, 0) * bq, bq)
        for mi in maps:
            pv = jnp.dot(p_scr[2 * slot + mi], v_ref[pl.ds(k0, bq), :], preferred_element_type=F32)
            for t in range(pv.shape[1] // LANES):
                lanes = slice(t * LANES, (t + 1) * LANES)
                acc_scr[mi, :, lanes] = a_scr[2 * slot + mi] * acc_scr[mi, :, lanes] + pv[:, lanes]

    p_scr[2] = jnp.zeros((bq, bq), BF16)
    p_scr[3] = jnp.zeros((bq, bq), BF16)
    a_scr[2] = jnp.ones((bq, LANES), F32)
    a_scr[3] = jnp.ones((bq, LANES), F32)

    def step(blk, slot, diag, with_next):
        weighted_values(blk - 1, 1 - slot, (0,))
        if with_next:
            scores(blk + 1, 1 - slot)
        softmax(blk, slot, diag, (0,))
        weighted_values(blk - 1, 1 - slot, (1,))
        softmax(blk, slot, diag, (1,))

    def pair(j2, carry):
        step(2 * j2, 0, False, True)
        step(2 * j2 + 1, 1, False, True)
        return carry

    scores(0, 0)
    lax.fori_loop(0, qi // 2, pair, 0)

    @pl.when(qi % 2 == 1)
    def _():
        step(qi - 1, 0, False, True)
        step(qi, 1, True, False)
        weighted_values(qi, 1)

    @pl.when(qi % 2 == 0)
    def _():
        step(qi, 0, True, False)
        weighted_values(qi, 0)
    l1 = jnp.sum(l_scr[0], axis=-1, keepdims=True)
    l2 = jnp.sum(l_scr[1], axis=-1, keepdims=True)
    o_ref[...] = _finish_heads(acc_scr[0], l1, acc_scr[1], l2, lam_ref[0], g_ref[...], post_scale).astype(o_ref.dtype)


def attention_prompt(q_bf, k_bf, v_bf, batch, lam, slopes2, subln_g, post_scale):
    m, width = q_bf.shape
    s = m // batch
    dv = 2 * ATTN_HEAD_DIM
    heads = width // dv
    bq = _pick(s, 512)
    assert bq % CHUNK == 0
    nq = s // bq
    kv_spec = pl.BlockSpec((s, dv), lambda b, h, i: (b, h))
    q_spec = pl.BlockSpec((bq, dv), lambda b, h, i: (b * nq + i, h))
    smem = pl.BlockSpec(memory_space=pltpu.SMEM)
    return pl.pallas_call(
        functools.partial(_attn_prompt_kernel, bq=bq, post_scale=post_scale),
        grid=(batch, heads, nq),
        in_specs=[smem, smem, q_spec, kv_spec, kv_spec, pl.BlockSpec((1, dv), lambda b, h, i: (0, 0))],
        out_specs=q_spec,
        out_shape=jax.ShapeDtypeStruct((m, width), BF16),
        scratch_shapes=[pltpu.VMEM((4, bq, bq), F32), pltpu.VMEM((4, bq, bq), BF16), pltpu.VMEM((2, bq, dv), BF16),
                        pltpu.VMEM((2, bq, dv), F32), pltpu.VMEM((2, bq, LANES), F32),
                        pltpu.VMEM((2, bq, LANES), F32), pltpu.VMEM((4, bq, LANES), F32)],
        compiler_params=_params(3), name="attention_prompt",
    )(lam, slopes2, q_bf, k_bf, v_bf, subln_g.reshape(1, dv))


def _attn_sample_kernel(lam_ref, slope_ref, q_ref, kn_ref, vn_ref, ck_ref, cv_ref, g_ref, o_ref, *, post_scale):
    hd = ATTN_HEAD_DIM
    dv = 2 * hd
    lam = lam_ref[0]
    t = q_ref.shape[0]
    heads = q_ref.shape[1] // dv
    past = ck_ref.shape[0] // (2 * heads)

    def dist_mask(n_keys, key0):
        q_pos = past + lax.broadcasted_iota(jnp.int32, (t, n_keys), 0)
        k_pos = key0 + lax.broadcasted_iota(jnp.int32, (t, n_keys), 1)
        return jnp.abs(q_pos - k_pos).astype(F32), (k_pos // CHUNK) <= (q_pos // CHUNK)

    dist_p, vis_p = dist_mask(past, 0)
    dist_n, vis_n = dist_mask(t, past)
    dn = (((1,), (1,)), ((), ()))
    for h in range(heads):
        cols = slice(h * dv, (h + 1) * dv)
        sl2 = slope_ref[h]
        v_past = jnp.concatenate(
            [cv_ref[pl.ds(c * heads + h, past, stride=heads * (dv // LANES)), :] for c in range(dv // LANES)],
            axis=1).astype(BF16)
        res = []
        for mi in range(2):
            qm = q_ref[:, h * dv + mi * hd:h * dv + (mi + 1) * hd]
            k_past = ck_ref[pl.ds(2 * h + mi, past, stride=2 * heads), :].astype(BF16)
            s_p = lax.dot_general(qm, k_past, dn, preferred_element_type=F32)
            s_n = lax.dot_general(qm, kn_ref[:, h * dv + mi * hd:h * dv + (mi + 1) * hd], dn,
                                  preferred_element_type=F32)
            s_p = jnp.where(vis_p, s_p - sl2 * dist_p, NEG_BIG)
            s_n = jnp.where(vis_n, s_n - sl2 * dist_n, NEG_BIG)
            m = jnp.maximum(jnp.max(s_p, axis=-1, keepdims=True), jnp.max(s_n, axis=-1, keepdims=True))
            p_p = jnp.exp2(s_p - m)
            p_n = jnp.exp2(s_n - m)
            l = jnp.sum(p_p, axis=-1, keepdims=True) + jnp.sum(p_n, axis=-1, keepdims=True)
            acc = (jnp.dot(p_p.astype(BF16), v_past, preferred_element_type=F32)
                   + jnp.dot(p_n.astype(BF16), vn_ref[:, cols], preferred_element_type=F32))
            res += [acc, l]
        o_ref[:, cols] = _finish_heads(res[0], res[1], res[2], res[3], lam, g_ref[...],
                                       post_scale).astype(o_ref.dtype)


def attention_sample(q_bf, k_bf, v_bf, cache_k, cache_v, layer, lam, slopes2, subln_g, post_scale):
    _, batch, past, heads, _, hd = cache_k.shape
    m, width = q_bf.shape
    t = m // batch
    dv = 2 * hd
    new_spec = pl.BlockSpec((t, width), lambda b: (b, 0))
    n_layers = cache_k.shape[0]
    cache_k = cache_k.reshape(n_layers, batch, past * heads * 2, hd)
    cache_v = cache_v.reshape(n_layers, batch, past, heads, dv // LANES, LANES)
    cache_v = cache_v.transpose(0, 1, 2, 4, 3, 5).reshape(n_layers, batch, past * (dv // LANES) * heads, LANES)
    ck_spec = pl.BlockSpec((None, None, past * heads * 2, hd), lambda b: (layer, b, 0, 0))
    cv_spec = pl.BlockSpec((None, None, past * (dv // LANES) * heads, LANES), lambda b: (layer, b, 0, 0))
    smem = pl.BlockSpec(memory_space=pltpu.SMEM)
    return pl.pallas_call(
        functools.partial(_attn_sample_kernel, post_scale=post_scale),
        grid=(batch,),
        in_specs=[smem, smem, new_spec, new_spec, new_spec, ck_spec, cv_spec,
                  pl.BlockSpec((1, dv), lambda b: (0, 0))],
        out_specs=new_spec,
        out_shape=jax.ShapeDtypeStruct((m, width), BF16),
        compiler_params=_params(1), name="attention_sample",
    )(lam, slopes2, q_bf, k_bf, v_bf, cache_k, cache_v, subln_g.reshape(1, dv))


def _sgu_gate_kernel(u_ref, v_ref, g_ref, b_ref, w_ref, bs_ref, *out_refs, chunk_len, write_v):
    bt, width = u_ref.shape
    gd = width // SGU_GROUPS
    v = v_ref[...].astype(F32)
    mu = jnp.mean(v, axis=-1, keepdims=True)
    vc = v - mu
    vn = vc * lax.rsqrt(jnp.mean(vc * vc, axis=-1, keepdims=True) + EPS) * g_ref[...] + b_ref[...]
    if write_v:
        out_refs[1][...] = vn
    vn_bf = vn.astype(BF16)
    mr = w_ref.shape[1]
    row = lax.broadcasted_iota(jnp.int32, (mr, mr), 0)
    col = lax.broadcasted_iota(jnp.int32, (mr, mr), 1)
    keep = ((row // chunk_len) == (col // chunk_len)) & ((col % chunk_len) <= (row % chunk_len))
    for gi in range(SGU_GROUPS):
        w = jnp.where(keep, w_ref[gi], 0.0).astype(BF16)
        bias = bs_ref[gi]
        for c in range(bt // mr):
            rows = slice(c * mr, (c + 1) * mr)
            cols = slice(gi * gd, (gi + 1) * gd)
            mixed = jnp.dot(w, vn_bf[rows, cols], preferred_element_type=F32) + bias
            out_refs[0][rows, cols] = (u_ref[rows, cols].astype(F32) * mixed).astype(BF16)


def sgu_gate(z, ln_g, ln_b, w_s, b_s, chunk_len, write_v):
    m, two_w = z.shape
    width = two_w // 2
    bt = _pick(m, SGU_MIX_ROWS)
    reps = bt // chunk_len
    w_t = jnp.tile(w_s[:, :chunk_len, :chunk_len], (1, reps, reps))
    b_t = jnp.tile(b_s[:, :chunk_len], (1, reps))[:, :, None]
    row_spec = lambda c: pl.BlockSpec((bt, width), lambda i, c=c: (i, c))
    vec_spec = pl.BlockSpec((1, width), lambda i: (0, 0))
    out_specs = [row_spec(0)]
    out_shape = [jax.ShapeDtypeStruct((m, width), BF16)]
    if write_v:
        out_specs.append(row_spec(0))
        out_shape.append(jax.ShapeDtypeStruct((m, width), F32))
    return pl.pallas_call(
        functools.partial(_sgu_gate_kernel, chunk_len=chunk_len, write_v=write_v),
        grid=(m // bt,),
        in_specs=[row_spec(0), row_spec(1), vec_spec, vec_spec,
                  pl.BlockSpec((SGU_GROUPS, bt, bt), lambda i: (0, 0, 0)),
                  pl.BlockSpec((SGU_GROUPS, bt, 1), lambda i: (0, 0, 0))],
        out_specs=out_specs, out_shape=out_shape,
        compiler_params=_params(1), name="sgu_gate",
    )(z, z, ln_g.reshape(1, width), ln_b.reshape(1, width), w_t, b_t)


def _swiglu(x, wg, wu, wd):
    g = jnp.dot(x, wg, preferred_element_type=F32)
    u = jnp.dot(x, wu, preferred_element_type=F32)
    h = (g * jax.nn.sigmoid(g) * u).astype(BF16)
    return jnp.dot(h, wd, preferred_element_type=F32)


def _moe_kernel(blk_e_ref, n_used_ref, x_ref, wg_ref, wu_ref, wd_ref, *refs, blk0):
    o_ref, wg_bf, wu_bf, wd_bf = refs[-4:]
    b = pl.program_id(0)
    e = blk_e_ref[blk0 + b]
    e_prev = blk_e_ref[blk0 + jnp.maximum(b - 1, 0)]

    @pl.when((b == 0) | (e != e_prev))
    def _():
        wg_bf[...] = wg_ref[...].astype(BF16)
        wu_bf[...] = wu_ref[...].astype(BF16)
        wd_bf[...] = wd_ref[...].astype(BF16)

    @pl.when(blk0 + b < n_used_ref[0])
    def _():
        o_ref[...] = _swiglu(x_ref[...], wg_bf[...], wu_bf[...], wd_bf[...]).astype(o_ref.dtype)

    @pl.when(blk0 + b >= n_used_ref[0])
    def _():
        o_ref[...] = jnp.zeros_like(o_ref)


def moe_experts(xg, blk_expert, n_used, layer, w_gate, w_up, w_down, out, blk0):
    rows, d = xg.shape
    ed = w_gate.shape[3]
    x_spec = pl.BlockSpec((MOE_BLOCK, d), lambda b, be, nu: (jnp.clip(nu[0] - 1 - blk0, 0, b), 0))
    w_idx = lambda b, be, nu: (layer, be[blk0 + b], 0, 0)
    in_specs = [x_spec, pl.BlockSpec((None, None, d, ed), w_idx), pl.BlockSpec((None, None, d, ed), w_idx),
                pl.BlockSpec((None, None, ed, d), w_idx)]
    args = [blk_expert, n_used, xg, w_gate, w_up, w_down]
    aliases = {}
    if out is not None:
        in_specs.append(pl.BlockSpec(memory_space=pl.ANY))
        args.append(out)
        aliases = {len(args) - 1: 0}
    return pl.pallas_call(
        functools.partial(_moe_kernel, blk0=blk0),
        grid_spec=pltpu.PrefetchScalarGridSpec(
            num_scalar_prefetch=2, grid=(rows // MOE_BLOCK,),
            in_specs=in_specs,
            out_specs=pl.BlockSpec((MOE_BLOCK, d), lambda b, be, nu: (blk0 + b, 0)),
            scratch_shapes=[pltpu.VMEM((d, ed), BF16), pltpu.VMEM((d, ed), BF16), pltpu.VMEM((ed, d), BF16)]),
        out_shape=jax.ShapeDtypeStruct((blk_expert.shape[0] * MOE_BLOCK, d), BF16),
        input_output_aliases=aliases,
        compiler_params=_params(1), name="moe_experts",
    )(*args)


def _shared_kernel(h_ref, wg_ref, wu_ref, wd_ref, o_ref):
    o_ref[...] = _swiglu(h_ref[...], wg_ref[...], wu_ref[...], wd_ref[...]).astype(o_ref.dtype)


def shared_expert(h, sw_gate, sw_up, sw_down):
    m, d = h.shape
    ed = sw_gate.shape[1]
    bm = _pick(m, 512)
    row_spec = pl.BlockSpec((bm, d), lambda i: (i, 0))
    full = lambda a, b: pl.BlockSpec((a, b), lambda i: (0, 0))
    return pl.pallas_call(
        _shared_kernel, grid=(m // bm,),
        in_specs=[row_spec, full(d, ed), full(d, ed), full(ed, d)],
        out_specs=row_spec,
        out_shape=jax.ShapeDtypeStruct((m, d), BF16),
        compiler_params=_params(1), name="shared_expert",
    )(h, sw_gate, sw_up, sw_down)


def _combine_kernel(s_ref, p_ref, w_ref, x_ref, gate_ref, o_ref):
    ffn = s_ref[...].astype(F32)
    w = w_ref[...]
    for k in range(TOP_K):
        ffn = ffn + w[:, k:k + 1] * p_ref[k].astype(F32)
    o_ref[...] = x_ref[...] + gate_ref[...] * ffn


def moe_combine_residual(stream, shared, picked, wts, row0, x, gate, tok0):
    m, d = x.shape
    n = picked.shape[1]
    bm = _row_block(stream, n, 256)
    assert row0 % bm == 0 and tok0 % bm == 0 and n % bm == 0
    off, off_all = tok0 // bm, (row0 + tok0) // bm
    row_spec = pl.BlockSpec((bm, d), lambda i: (i + off, 0))
    all_rows = lambda width: pl.BlockSpec((bm, width), lambda i: (i + off_all, 0))
    return pl.pallas_call(
        _combine_kernel, grid=(n // bm,),
        in_specs=[all_rows(d), pl.BlockSpec((TOP_K, bm, d), lambda i: (0, i, 0)), all_rows(TOP_K), row_spec,
                  _mod_spec(stream, bm, d, lambda i: (i + off, 0))],
        out_specs=row_spec,
        out_shape=jax.ShapeDtypeStruct((m, d), F32),
        input_output_aliases={3: 0},
        compiler_params=_params(1), name="moe_combine",
    )(shared, picked, wts, x, gate)


def _dest_kernel(start_ref, idx_ref, rank_ref, dest_ref):
    idx = idx_ref[...]

    def body(e, acc):
        return jnp.where(idx == e, start_ref[e], acc)

    dest_ref[...] = rank_ref[...] + lax.fori_loop(0, start_ref.shape[0], body, jnp.zeros_like(idx))


def assignment_rows(pad_start, idx, rank):
    return pl.pallas_call(
        _dest_kernel,
        in_specs=[pl.BlockSpec(memory_space=pltpu.SMEM), pl.BlockSpec(memory_space=pltpu.VMEM),
                  pl.BlockSpec(memory_space=pltpu.VMEM)],
        out_specs=pl.BlockSpec(memory_space=pltpu.VMEM),
        out_shape=jax.ShapeDtypeStruct(idx.shape, jnp.int32),
        name="assignment_rows",
    )(pad_start, idx, rank)


def moe_routed(h_all, idx, rank, counts, layer, w_gate, w_up, w_down):
    t, d = h_all.shape
    n_exp = counts.shape[0]
    n_assign = t * TOP_K
    counts = counts.reshape(n_exp).astype(jnp.int32)
    padded = (counts + MOE_BLOCK - 1) // MOE_BLOCK * MOE_BLOCK
    pad_end = jnp.cumsum(padded)
    pad_start = pad_end - padded
    dest = assignment_rows(pad_start, idx, rank).reshape(-1)
    n_blocks = -(-(n_assign + n_exp * (MOE_BLOCK - 1)) // MOE_BLOCK)
    n_blocks = -(-n_blocks // MOE_CHUNKS) * MOE_CHUNKS
    n_rows = n_blocks * MOE_BLOCK
    tok = jnp.tile(jnp.arange(t, dtype=jnp.int32), TOP_K)
    src_tok = jnp.arange(n_rows, dtype=jnp.int32) % t
    src_tok = src_tok.at[dest].add(tok - dest % t, unique_indices=True, mode="promise_in_bounds")
    blk_start = jnp.arange(n_blocks, dtype=jnp.int32) * MOE_BLOCK
    blk_expert = jnp.minimum(jnp.sum((pad_end[None, :] <= blk_start[:, None]).astype(jnp.int32), axis=1), n_exp - 1)
    n_used = pad_end[-1:] // MOE_BLOCK
    out = None
    chunk_rows = n_rows // MOE_CHUNKS
    for c in range(MOE_CHUNKS):
        xg = h_all.at[src_tok[c * chunk_rows:(c + 1) * chunk_rows]].get(mode="promise_in_bounds")
        out = moe_experts(xg, blk_expert, n_used, layer, w_gate, w_up, w_down, out, c * (n_blocks // MOE_CHUNKS))
    return out, dest.reshape(TOP_K, t)


def _stream(rows_per_batch):
    return dict(per_row=rows_per_batch < 1024, rows_per_batch=rows_per_batch)


def _mod_rows(stream, mod):
    b, d = mod.shape
    if not stream["per_row"]:
        return mod.reshape(b, 1, d)
    return jnp.repeat(mod, stream["rows_per_batch"], axis=0).reshape(1, -1, d)


def kernel(x_prompt, x_sample, cache_k_attn, cache_v_attn, c_prompt, c_sample, norm_mix_g, norm_ffn_g, ada_w, ada_b, attn_w_qkv, attn_w_o, attn_q_norm, attn_k_norm, attn_lambda_q1, attn_lambda_k1, attn_lambda_q2, attn_lambda_k2, attn_subln_g, sgu_w_in, sgu_ln_g, sgu_ln_b, sgu_w_s, sgu_b_s, sgu_w_o, moe_w_router, moe_b_router, moe_w_gate, moe_w_up, moe_w_down, shared_w_gate, shared_w_up, shared_w_down):
    bp, sp, d = x_prompt.shape
    bs, ss, _ = x_sample.shape
    depth = ada_w.shape[0]
    dv = 2 * ATTN_HEAD_DIM
    heads = d // dv
    mp, ms = bp * sp, bs * ss
    streams = (_stream(sp), _stream(ss))
    xs = [x_prompt.reshape(mp, d), x_sample.reshape(ms, d)]

    mod = ada_modulation(jnp.concatenate([c_prompt, c_sample], axis=0), ada_w, ada_b)
    slopes2 = (2.0 ** (-8.0 * jnp.arange(1, heads + 1, dtype=F32) / heads)) * LOG2E

    k_out, v_out, sgu_out = [[], []], [[], []], []
    for i in range(depth):
        j = i // N_MIXERS
        mods = []
        for si, (stream, rows) in enumerate(zip(streams, (slice(0, bp), slice(bp, bp + bs)))):
            mods.append([_mod_rows(stream, mod[i, rows, c * d:(c + 1) * d]) for c in range(6)])

        if i % N_MIXERS == 0:
            lam_init = 0.8 - 0.6 * math.exp(-0.3 * i)
            f = lambda a: a[j].astype(F32)
            lam = (jnp.exp(jnp.sum(f(attn_lambda_q1) * f(attn_lambda_k1)))
                   - jnp.exp(jnp.sum(f(attn_lambda_q2) * f(attn_lambda_k2))) + lam_init).reshape(1)
            for si, stream in enumerate(streams):
                h = modulate(stream, xs[si], norm_mix_g[i], mods[si][0], mods[si][1])
                q_bf, k32, k_bf, v32, v_bf = qkv_project(stream, h, attn_w_qkv[j], attn_q_norm[j], attn_k_norm[j])
                if si == 0:
                    o = attention_prompt(q_bf, k_bf, v_bf, bp, lam, slopes2, attn_subln_g[j], 1.0 - lam_init)
                else:
                    o = attention_sample(q_bf, k_bf, v_bf, cache_k_attn, cache_v_attn, j, lam, slopes2,
                                         attn_subln_g[j], 1.0 - lam_init)
                xs[si] = project_residual(stream, o, attn_w_o[j], xs[si], mods[si][2], "attn_out_proj")
                k_out[si].append(k32)
                v_out[si].append(v32)
        else:
            for si, stream in enumerate(streams):
                h = modulate(stream, xs[si], norm_mix_g[i], mods[si][0], mods[si][1])
                (z,) = matmul(h, sgu_w_in[j], bm=_row_block(stream, h.shape[0], 1024),
                              bn=_pick(sgu_w_in.shape[2], 1024),
                              out_dtypes=(BF16,), epilogue=_gelu_epilogue, name="sgu_in_proj")
                chunk_len = SGU_CHUNK if si == 0 else ss
                res = sgu_gate(z, sgu_ln_g[j], sgu_ln_b[j], sgu_w_s[j], sgu_b_s[j], chunk_len, write_v=(si == 1))
                if si == 1:
                    sgu_out.append(res[1])
                xs[si] = project_residual(stream, res[0], sgu_w_o[j], xs[si], mods[si][2], "sgu_out_proj")

        counts = jnp.zeros((moe_w_router.shape[2], 1), F32)
        per_token = ()
        for si, (stream, row0) in enumerate(zip(streams, (0, mp))):
            *per_token, counts = modulate(stream, xs[si], norm_ffn_g[i], mods[si][3], mods[si][4],
                                          router=(moe_w_router[i], moe_b_router[i], counts, mp + ms, row0, per_token))
        h_all, idx, wts, rank = per_token
        shared = shared_expert(h_all, *[w[i].astype(BF16) for w in (shared_w_gate, shared_w_up, shared_w_down)])
        out, dest = moe_routed(h_all, idx, rank, counts, i, moe_w_gate, moe_w_up, moe_w_down)
        for si, (stream, row0, m_rows) in enumerate(zip(streams, (0, mp), (mp, ms))):
            piece = _pick(m_rows, COMBINE_TOKENS)
            for tok0 in range(0, m_rows, piece):
                rows = dest[:, row0 + tok0:row0 + tok0 + piece].reshape(-1)
                picked = out.at[rows].get(mode="promise_in_bounds", unique_indices=True).reshape(TOP_K, piece, d)
                xs[si] = moe_combine_residual(stream, shared, picked, wts.T, row0, xs[si], mods[si][5], tok0)

    n_attn = len(k_out[0])
    k_prompt = jnp.stack(k_out[0]).reshape(n_attn, bp, sp, heads, 2, ATTN_HEAD_DIM)
    v_prompt = jnp.stack(v_out[0]).reshape(n_attn, bp, sp, heads, dv)
    k_sample = jnp.stack(k_out[1]).reshape(n_attn, bs, ss, heads, 2, ATTN_HEAD_DIM)
    v_sample = jnp.stack(v_out[1]).reshape(n_attn, bs, ss, heads, dv)
    sgu_v = jnp.stack(sgu_out).reshape(len(sgu_out), bs, ss, -1)
    return (xs[0].reshape(bp, sp, d), xs[1].reshape(bs, ss, d), k_prompt, v_prompt, k_sample, v_sample, sgu_v)
```

```python
import functools
import math

import jax
import jax.numpy as jnp
from jax import lax
from jax.experimental import pallas as pl
from jax.experimental.pallas import tpu as pltpu

F32 = jnp.float32
BF16 = jnp.bfloat16

EPS = 1e-6
CHUNK = 64
N_MIXERS = 2
ATTN_HEAD_DIM = 128
SGU_GROUPS = 8
SGU_CHUNK = 128
TOP_K = 8
N_EXPERT_GROUPS = 8
TOPK_GROUPS = 4
ROUTED_SCALE = 2.5
LOG2E = 1.4426950408889634
NEG_BIG = -1e30

VMEM_LIMIT = 56 * 1024 * 1024
MOE_BLOCK = 512
MOE_CHUNKS = 8
SGU_MIX_ROWS = 256
COMBINE_TOKENS = 2048
ATTN_ROW_CHUNK = 64
LANES = 128


def _params(n_axes):
    return pltpu.CompilerParams(dimension_semantics=("arbitrary",) * n_axes, vmem_limit_bytes=VMEM_LIMIT)


def _pick(n, pref):
    if n <= pref:
        return n
    b = pref
    while n % b:
        b //= 2
    return b


def _ada_kernel(c_ref, w_ref, b_ref, o_ref):
    c = c_ref[...]
    x = (c * jax.nn.sigmoid(c)).astype(BF16)
    o_ref[...] = jnp.dot(x, w_ref[...].astype(BF16), preferred_element_type=F32) + b_ref[...]


def ada_modulation(c, ada_w, ada_b):
    n_layers, d, n = ada_w.shape
    r = c.shape[0]
    bn = _pick(n, 1024)
    return pl.pallas_call(
        _ada_kernel,
        grid=(n_layers, n // bn),
        in_specs=[
            pl.BlockSpec((r, d), lambda l, j: (0, 0)),
            pl.BlockSpec((None, d, bn), lambda l, j: (l, 0, j)),
            pl.BlockSpec((None, 1, bn), lambda l, j: (l, 0, j)),
        ],
        out_specs=pl.BlockSpec((None, r, bn), lambda l, j: (l, 0, j)),
        out_shape=jax.ShapeDtypeStruct((n_layers, r, n), F32),
        compiler_params=_params(2),
        name="ada_modulation",
    )(c, ada_w, ada_b.reshape(n_layers, 1, n))


def _modulated(x, g, shift, scale):
    y = x * lax.rsqrt(jnp.mean(x * x, axis=-1, keepdims=True) + EPS)
    return y * g * (1.0 + scale) + shift


def _modulate_kernel(x_ref, g_ref, sh_ref, sc_ref, o_ref):
    o_ref[...] = _modulated(x_ref[...], g_ref[...], sh_ref[...], sc_ref[...]).astype(o_ref.dtype)


def _split3(x):
    hi = x.astype(BF16)
    lo = (x - hi.astype(F32)).astype(BF16)
    return hi, lo


def _first_max(x, ids, n_ids, axes):
    mx = x
    for ax in axes:
        mx = jnp.max(mx, axis=ax, keepdims=True)
    arg = jnp.where(x == mx, ids, n_ids)
    for ax in axes:
        arg = jnp.min(arg, axis=ax, keepdims=True)
    return mx, arg


def _sum_axes(x, axes):
    for ax in axes:
        x = jnp.sum(x, axis=ax, keepdims=True)
    return x


def _modulate_router_kernel(x_ref, g_ref, sh_ref, sc_ref, wrt_ref, br_ref, cin_ref, *refs, n_prev):
    o_ref, idx_ref, wts_ref, rank_ref, cnt_ref, carry = refs[n_prev:]
    h = _modulated(x_ref[...], g_ref[...], sh_ref[...], sc_ref[...])
    o_ref[...] = h.astype(o_ref.dtype)
    bm = h.shape[0]
    n_exp = wrt_ref.shape[0]
    per = n_exp // N_EXPERT_GROUPS
    grp_shape = (N_EXPERT_GROUPS, per, bm)

    h_hi, h_lo = _split3(h)
    w_hi, w_lo = _split3(wrt_ref[...])
    nt = (((1,), (1,)), ((), ()))
    logits = (lax.dot_general(w_hi, h_hi, nt, preferred_element_type=F32)
              + lax.dot_general(w_lo, h_hi, nt, preferred_element_type=F32)
              + lax.dot_general(w_hi, h_lo, nt, preferred_element_type=F32))
    s = jax.nn.sigmoid(logits)
    s3 = s.reshape(grp_shape)
    sb3 = (s + br_ref[...]).reshape(grp_shape)

    sub = lax.broadcasted_iota(jnp.int32, grp_shape, 1)
    gid = lax.broadcasted_iota(jnp.int32, (N_EXPERT_GROUPS, 1, bm), 0)
    eid = lax.broadcasted_iota(jnp.int32, grp_shape, 0) * per + sub

    m1, i1 = _first_max(sb3, sub, per, (1,))
    m2 = jnp.max(jnp.where(sub == i1, -jnp.inf, sb3), axis=1, keepdims=True)
    work = m1 + m2
    chosen = jnp.zeros_like(work)
    for _ in range(TOPK_GROUPS):
        _, gi = _first_max(work, gid, N_EXPERT_GROUPS, (0,))
        chosen = jnp.where(gid == gi, 1.0, chosen)
        work = jnp.where(gid == gi, -jnp.inf, work)
    sel = jnp.where(chosen > 0.0, sb3, -jnp.inf)

    hits, ids, raw = [], [], []
    for _ in range(TOP_K):
        _, ei = _first_max(sel, eid, n_exp, (1, 0))
        hit = eid == ei
        raw.append(_sum_axes(jnp.where(hit, s3, 0.0), (1, 0)))
        sel = jnp.where(hit, -jnp.inf, sel)
        hits.append(hit)
        ids.append(ei)
    total = sum(raw)

    member = sum(jnp.where(hit, 1.0, 0.0) for hit in hits).reshape(n_exp, bm)
    r_i = lax.broadcasted_iota(jnp.int32, (bm, bm), 0)
    c_i = lax.broadcasted_iota(jnp.int32, (bm, bm), 1)
    before = jnp.where(r_i < c_i, 1.0, 0.0).astype(BF16)

    @pl.when(pl.program_id(0) == 0)
    def _():
        carry[...] = cin_ref[...]

    rank_all = (jnp.dot(member.astype(BF16), before, preferred_element_type=F32) + carry[...]).reshape(grp_shape)
    carry[...] += jnp.sum(member, axis=1, keepdims=True)
    cnt_ref[...] = carry[...]
    for k in range(TOP_K):
        idx_ref[k:k + 1, :] = ids[k].reshape(1, bm)
        wts_ref[k:k + 1, :] = (raw[k] / total * ROUTED_SCALE).reshape(1, bm)
        rank_ref[k:k + 1, :] = _sum_axes(jnp.where(hits[k], rank_all, 0.0), (1, 0)).reshape(1, bm).astype(jnp.int32)


def _row_block(stream, m, pref):
    return _pick(m if stream["per_row"] else stream["rows_per_batch"], pref)


def _mod_spec(stream, bm, bn, ij):
    if stream["per_row"]:
        return pl.BlockSpec((None, bm, bn), lambda *g: (0,) + tuple(ij(*g)))
    bpg = stream["rows_per_batch"] // bm
    return pl.BlockSpec((None, 1, bn), lambda *g: (ij(*g)[0] // bpg, 0, ij(*g)[1]))


def modulate(stream, x, g, shift, scale, router=None):
    m, d = x.shape
    bm = _row_block(stream, m, 512)
    row_spec = pl.BlockSpec((bm, d), lambda i: (i, 0))
    mod_spec = _mod_spec(stream, bm, d, lambda i: (i, 0))
    g_spec = pl.BlockSpec((1, d), lambda i: (0, 0))
    if router is None:
        return pl.pallas_call(
            _modulate_kernel, grid=(m // bm,),
            in_specs=[row_spec, g_spec, mod_spec, mod_spec], out_specs=row_spec,
            out_shape=jax.ShapeDtypeStruct((m, d), BF16),
            compiler_params=_params(1), name="modulate",
        )(x, g.reshape(1, d), shift, scale)
    w_router, b_router, counts_in, t_all, row0, prev = router
    assert row0 % bm == 0
    off = row0 // bm
    e = w_router.shape[1]
    out_rows = pl.BlockSpec((bm, d), lambda i: (i + off, 0))
    tok_spec = pl.BlockSpec((TOP_K, bm), lambda i: (0, i + off))
    cnt_spec = pl.BlockSpec((e, 1), lambda i: (0, 0))
    n_in = 7
    return pl.pallas_call(
        functools.partial(_modulate_router_kernel, n_prev=len(prev)), grid=(m // bm,),
        in_specs=[row_spec, g_spec, mod_spec, mod_spec, pl.BlockSpec((e, d), lambda i: (0, 0)), cnt_spec, cnt_spec]
        + [pl.BlockSpec(memory_space=pl.ANY)] * len(prev),
        out_specs=[out_rows, tok_spec, tok_spec, tok_spec, cnt_spec],
        out_shape=[jax.ShapeDtypeStruct((t_all, d), BF16), jax.ShapeDtypeStruct((TOP_K, t_all), jnp.int32),
                   jax.ShapeDtypeStruct((TOP_K, t_all), F32), jax.ShapeDtypeStruct((TOP_K, t_all), jnp.int32),
                   jax.ShapeDtypeStruct((e, 1), F32)],
        input_output_aliases={n_in + k: k for k in range(len(prev))},
        scratch_shapes=[pltpu.VMEM((e, 1), F32)],
        compiler_params=_params(1), name="modulate_router",
    )(x, g.reshape(1, d), shift, scale, w_router.T, b_router.astype(F32).reshape(e, 1), counts_in, *prev)


def _mm_kernel(x_ref, w_ref, *refs, n_extra, epilogue):
    extra, outs, w_bf = refs[:n_extra], refs[n_extra:-1], refs[-1]

    @pl.when(pl.program_id(1) == 0)
    def _():
        w_bf[...] = w_ref[...].astype(BF16)

    acc = jnp.dot(x_ref[...], w_bf[...], preferred_element_type=F32)
    epilogue(acc, extra, outs)


def matmul(x, w, *, bm, bn, col_block_off=0, n_cols=None, extra=(), extra_specs=(), out_dtypes, epilogue, name):
    m, k = x.shape
    n = w.shape[1] if n_cols is None else n_cols
    grid = (n // bn, m // bm)
    out_spec = pl.BlockSpec((bm, bn), lambda j, i: (i, j))
    return pl.pallas_call(
        functools.partial(_mm_kernel, n_extra=len(extra), epilogue=epilogue),
        grid=grid,
        in_specs=[pl.BlockSpec((bm, k), lambda j, i: (i, 0)),
                  pl.BlockSpec((k, bn), lambda j, i: (0, j + col_block_off))] + list(extra_specs),
        out_specs=[out_spec] * len(out_dtypes),
        out_shape=[jax.ShapeDtypeStruct((m, n), dt) for dt in out_dtypes],
        scratch_shapes=[pltpu.VMEM((k, bn), BF16)],
        compiler_params=_params(2), name=name,
    )(x, w, *extra)


def _head_rms(acc, gain, post_scale):
    pieces = []
    for c in range(acc.shape[1] // ATTN_HEAD_DIM):
        seg = acc[:, c * ATTN_HEAD_DIM:(c + 1) * ATTN_HEAD_DIM]
        y = seg * lax.rsqrt(jnp.mean(seg * seg, axis=-1, keepdims=True) + EPS)
        pieces.append(y * (gain * post_scale))
    return jnp.concatenate(pieces, axis=-1)


def _q_epilogue(acc, extra, outs):
    outs[0][...] = _head_rms(acc, extra[0][...], ATTN_HEAD_DIM ** -0.5 * LOG2E).astype(BF16)


def _v_epilogue(acc, extra, outs):
    outs[0][...] = acc
    outs[1][...] = acc.astype(BF16)


def _gelu_epilogue(acc, extra, outs):
    outs[0][...] = (0.5 * acc * (1.0 + lax.erf(acc * (2.0 ** -0.5)))).astype(BF16)


def _residual_epilogue(acc, extra, outs):
    x_ref, gate_ref = extra
    outs[0][...] = x_ref[...] + gate_ref[...] * acc


def _k_proj_kernel(h_ref, w_ref, g_ref, k32_ref, kbf_ref):
    acc = jnp.dot(h_ref[...], w_ref[...], preferred_element_type=F32)
    bm = acc.shape[0]
    groups = acc.shape[1] // ATTN_HEAD_DIM
    gain = g_ref[...]
    for g in range(groups):
        cols = slice(g * ATTN_HEAD_DIM, (g + 1) * ATTN_HEAD_DIM)
        seg = acc[:, cols]
        kn = seg * lax.rsqrt(jnp.mean(seg * seg, axis=-1, keepdims=True) + EPS) * gain
        k32_ref[pl.ds(g, bm, stride=groups), :] = kn
        kbf_ref[:, cols] = kn.astype(BF16)


def k_project(stream, h, w_k, k_norm):
    m, d = h.shape
    n = w_k.shape[1]
    groups = n // ATTN_HEAD_DIM
    bm = _row_block(stream, m, 512)
    return pl.pallas_call(
        _k_proj_kernel, grid=(m // bm,),
        in_specs=[pl.BlockSpec((bm, d), lambda i: (i, 0)), pl.BlockSpec((d, n), lambda i: (0, 0)),
                  pl.BlockSpec((1, ATTN_HEAD_DIM), lambda i: (0, 0))],
        out_specs=[pl.BlockSpec((bm * groups, ATTN_HEAD_DIM), lambda i: (i, 0)),
                   pl.BlockSpec((bm, n), lambda i: (i, 0))],
        out_shape=[jax.ShapeDtypeStruct((m * groups, ATTN_HEAD_DIM), F32), jax.ShapeDtypeStruct((m, n), BF16)],
        compiler_params=_params(1), name="k_proj",
    )(h, w_k, k_norm.reshape(1, -1))


def qkv_project(stream, h, w_qkv, q_norm, k_norm):
    m, d = h.shape
    bm = _row_block(stream, m, 1024)
    bn = _pick(d, 1024)
    nb = d // bn
    gain_spec = pl.BlockSpec((1, ATTN_HEAD_DIM), lambda j, i: (0, 0))
    common = dict(bm=bm, bn=bn, n_cols=d)
    (q_bf,) = matmul(h, w_qkv, col_block_off=0, extra=(q_norm.reshape(1, -1),), extra_specs=(gain_spec,),
                     out_dtypes=(BF16,), epilogue=_q_epilogue, name="q_proj", **common)
    k32, k_bf = k_project(stream, h, w_qkv[:, d:2 * d].astype(BF16), k_norm)
    v32, v_bf = matmul(h, w_qkv, col_block_off=2 * nb, out_dtypes=(F32, BF16), epilogue=_v_epilogue,
                       name="v_proj", **common)
    return q_bf, k32, k_bf, v32, v_bf


def project_residual(stream, a, w, x, gate, name):
    k, n = w.shape
    deep = k > 4096
    bm = _row_block(stream, a.shape[0], 512 if deep else 1024)
    bn = _pick(n, 512 if deep else 1024)
    (out,) = matmul(a, w, bm=bm, bn=bn, extra=(x, gate),
                    extra_specs=(pl.BlockSpec((bm, bn), lambda j, i: (i, j)),
                                 _mod_spec(stream, bm, bn, lambda j, i: (i, j))),
                    out_dtypes=(F32,), epilogue=_residual_epilogue, name=name)
    return out


def _finish_heads(acc1, l1, acc2, l2, lam, g, post_scale):
    o = acc1 / l1 - lam * (acc2 / l2)
    o = o * lax.rsqrt(jnp.mean(o * o, axis=-1, keepdims=True) + EPS)
    return o * (g * post_scale)


def _attn_prompt_kernel(lam_ref, slope_ref, q_ref, k_ref, v_ref, g_ref, o_ref,
                        s_scr, p_scr, qa_scr, acc_scr, m_scr, l_scr, a_scr, *, bq, post_scale):
    hd = ATTN_HEAD_DIM
    rc = ATTN_ROW_CHUNK
    qi = pl.program_id(2)
    hh = pl.program_id(1)
    sl2 = slope_ref[hh]

    m_scr[...] = jnp.full(m_scr.shape, NEG_BIG, F32)
    l_scr[...] = jnp.zeros(l_scr.shape, F32)
    acc_scr[...] = jnp.zeros(acc_scr.shape, F32)

    lane = lax.broadcasted_iota(jnp.int32, (bq, LANES), 1)
    key_c = lax.broadcasted_iota(jnp.int32, (bq, LANES), 0)
    k_aug = jnp.where(lane < 3, key_c % 256, jnp.where(lane < 6, key_c // 256, 0)).astype(F32).astype(BF16)
    rest = jnp.full((bq, LANES), sl2, F32)
    piece = jnp.zeros((bq, LANES), F32)
    for i in range(3):
        part = rest.astype(BF16).astype(F32)
        piece = jnp.where(lane % 3 == i, part, piece)
        rest = rest - part
    q_aug = jnp.where(lane < 3, piece, jnp.where(lane < 6, 256.0 * piece, 0.0)).astype(BF16)
    for mi in range(2):
        qa_scr[mi, :, :hd] = q_ref[:, mi * hd:(mi + 1) * hd]
        qa_scr[mi, :, hd:] = q_aug

    def scores(kb, slot, maps=(0, 1)):
        k0 = pl.multiple_of(kb * bq, bq)
        for mi in maps:
            k_blk = jnp.concatenate([k_ref[pl.ds(k0, bq), mi * hd:(mi + 1) * hd], k_aug], axis=1)
            s_scr[2 * slot + mi] = lax.dot_general(qa_scr[mi], k_blk, (((1,), (1,)), ((), ())),
                                                   preferred_element_type=F32)

    def softmax(kb, slot, diag, maps=(0, 1)):
        shift_blk = sl2 * ((kb - qi) * bq + jnp.zeros((1, LANES), jnp.int32)).astype(F32)
        for mi in maps:
            s_map = s_scr.at[2 * slot + mi]
            p_map = p_scr.at[2 * slot + mi]
            for c in range(bq // rc):
                rows = slice(c * rc, (c + 1) * rc)
                ncol = min(bq, -(-((c + 1) * rc) // CHUNK) * CHUNK) if diag else bq
                lane_tiles = [slice(t * LANES, (t + 1) * LANES) for t in range(-(-ncol // LANES))]

                def tile(lanes, t):
                    s_t = s_map[rows, lanes]
                    if not diag:
                        return s_t
                    r_i = lax.broadcasted_iota(jnp.int32, (rc, LANES), 0) + c * rc
                    c_i = lax.broadcasted_iota(jnp.int32, (rc, LANES), 1) + t * LANES
                    s_t = s_t + (2.0 * sl2) * jnp.minimum(r_i - c_i, 0).astype(F32)
                    return jnp.where((c_i // CHUNK) <= (r_i // CHUNK), s_t, NEG_BIG)

                tiles = [tile(lanes, t) for t, lanes in enumerate(lane_tiles)]
                m_old = m_scr[mi, rows, :]
                blk_max = jnp.max(functools.reduce(jnp.maximum, tiles), axis=-1, keepdims=True)
                m_new = jnp.maximum(m_old, blk_max + shift_blk)
                alpha = jnp.exp2(m_old - m_new)
                m_scr[mi, rows, :] = m_new
                a_scr[2 * slot + mi, rows, :] = alpha
                if not diag:
                    tiles = [tile(lanes, t) for t, lanes in enumerate(lane_tiles)]
                off = m_new - shift_blk
                ps = [jnp.exp2(s_t - off) for s_t in tiles]
                l_scr[mi, rows, :] = alpha * l_scr[mi, rows, :] + functools.reduce(jnp.add, ps)
                for t, p_t in enumerate(ps):
                    p_map[rows, t * LANES:(t + 1) * LANES] = p_t.astype(BF16)
                if len(ps) * LANES < bq:
                    p_map[rows, len(ps) * LANES:] = jnp.zeros((rc, bq - len(ps) * LANES), BF16)

    def weighted_values(kb, slot, maps=(0, 1)):
        k0 = pl.multiple_of(jnp.maximum(kb, 0) * bq, bq)
        for mi in maps:
            pv = jnp.dot(p_scr[2 * slot + mi], v_ref[pl.ds(k0, bq), :], preferred_element_type=F32)
            for t in range(pv.shape[1] // LANES):
                lanes = slice(t * LANES, (t + 1) * LANES)
                acc_scr[mi, :, lanes] = a_scr[2 * slot + mi] * acc_scr[mi, :, lanes] + pv[:, lanes]

    p_scr[2] = jnp.zeros((bq, bq), BF16)
    p_scr[3] = jnp.zeros((bq, bq), BF16)
    a_scr[2] = jnp.ones((bq, LANES), F32)
    a_scr[3] = jnp.ones((bq, LANES), F32)

    def step(blk, slot, diag, with_next):
        weighted_values(blk - 1, 1 - slot, (0,))
        if with_next:
            scores(blk + 1, 1 - slot)
        softmax(blk, slot, diag, (0,))
        weighted_values(blk - 1, 1 - slot, (1,))
        softmax(blk, slot, diag, (1,))

    def pair(j2, carry):
        step(2 * j2, 0, False, True)
        step(2 * j2 + 1, 1, False, True)
        return carry

    scores(0, 0)
    lax.fori_loop(0, qi // 2, pair, 0)

    @pl.when(qi % 2 == 1)
    def _():
        step(qi - 1, 0, False, True)
        step(qi, 1, True, False)
        weighted_values(qi, 1)

    @pl.when(qi % 2 == 0)
    def _():
        step(qi, 0, True, False)
        weighted_values(qi, 0)
    l1 = jnp.sum(l_scr[0], axis=-1, keepdims=True)
    l2 = jnp.sum(l_scr[1], axis=-1, keepdims=True)
    o_ref[...] = _finish_heads(acc_scr[0], l1, acc_scr[1], l2, lam_ref[0], g_ref[...], post_scale).astype(o_ref.dtype)


def attention_prompt(q_bf, k_bf, v_bf, batch, lam, slopes2, subln_g, post_scale):
    m, width = q_bf.shape
    s = m // batch
    dv = 2 * ATTN_HEAD_DIM
    heads = width // dv
    bq = _pick(s, 512)
    assert bq % CHUNK == 0
    nq = s // bq
    kv_spec = pl.BlockSpec((s, dv), lambda b, h, i: (b, h))
    q_spec = pl.BlockSpec((bq, dv), lambda b, h, i: (b * nq + i, h))
    smem = pl.BlockSpec(memory_space=pltpu.SMEM)
    return pl.pallas_call(
        functools.partial(_attn_prompt_kernel, bq=bq, post_scale=post_scale),
        grid=(batch, heads, nq),
        in_specs=[smem, smem, q_spec, kv_spec, kv_spec, pl.BlockSpec((1, dv), lambda b, h, i: (0, 0))],
        out_specs=q_spec,
        out_shape=jax.ShapeDtypeStruct((m, width), BF16),
        scratch_shapes=[pltpu.VMEM((4, bq, bq), F32), pltpu.VMEM((4, bq, bq), BF16), pltpu.VMEM((2, bq, dv), BF16),
                        pltpu.VMEM((2, bq, dv), F32), pltpu.VMEM((2, bq, LANES), F32),
                        pltpu.VMEM((2, bq, LANES), F32), pltpu.VMEM((4, bq, LANES), F32)],
        compiler_params=_params(3), name="attention_prompt",
    )(lam, slopes2, q_bf, k_bf, v_bf, subln_g.reshape(1, dv))


def _attn_sample_kernel(lam_ref, slope_ref, q_ref, kn_ref, vn_ref, ck_ref, cv_ref, g_ref, o_ref, *, post_scale):
    hd = ATTN_HEAD_DIM
    dv = 2 * hd
    lam = lam_ref[0]
    t = q_ref.shape[0]
    heads = q_ref.shape[1] // dv
    past = ck_ref.shape[0] // (2 * heads)

    def dist_mask(n_keys, key0):
        q_pos = past + lax.broadcasted_iota(jnp.int32, (t, n_keys), 0)
        k_pos = key0 + lax.broadcasted_iota(jnp.int32, (t, n_keys), 1)
        return jnp.abs(q_pos - k_pos).astype(F32), (k_pos // CHUNK) <= (q_pos // CHUNK)

    dist_p, vis_p = dist_mask(past, 0)
    dist_n, vis_n = dist_mask(t, past)
    dn = (((1,), (1,)), ((), ()))
    for h in range(heads):
        cols = slice(h * dv, (h + 1) * dv)
        sl2 = slope_ref[h]
        v_past = jnp.concatenate(
            [cv_ref[pl.ds(c * heads + h, past, stride=heads * (dv // LANES)), :] for c in range(dv // LANES)],
            axis=1).astype(BF16)
        res = []
        for mi in range(2):
            qm = q_ref[:, h * dv + mi * hd:h * dv + (mi + 1) * hd]
            k_past = ck_ref[pl.ds(2 * h + mi, past, stride=2 * heads), :].astype(BF16)
            s_p = lax.dot_general(qm, k_past, dn, preferred_element_type=F32)
            s_n = lax.dot_general(qm, kn_ref[:, h * dv + mi * hd:h * dv + (mi + 1) * hd], dn,
                                  preferred_element_type=F32)
            s_p = jnp.where(vis_p, s_p - sl2 * dist_p, NEG_BIG)
            s_n = jnp.where(vis_n, s_n - sl2 * dist_n, NEG_BIG)
            m = jnp.maximum(jnp.max(s_p, axis=-1, keepdims=True), jnp.max(s_n, axis=-1, keepdims=True))
            p_p = jnp.exp2(s_p - m)
            p_n = jnp.exp2(s_n - m)
            l = jnp.sum(p_p, axis=-1, keepdims=True) + jnp.sum(p_n, axis=-1, keepdims=True)
            acc = (jnp.dot(p_p.astype(BF16), v_past, preferred_element_type=F32)
                   + jnp.dot(p_n.astype(BF16), vn_ref[:, cols], preferred_element_type=F32))
            res += [acc, l]
        o_ref[:, cols] = _finish_heads(res[0], res[1], res[2], res[3], lam, g_ref[...],
                                       post_scale).astype(o_ref.dtype)


def attention_sample(q_bf, k_bf, v_bf, cache_k, cache_v, layer, lam, slopes2, subln_g, post_scale):
    _, batch, past, heads, _, hd = cache_k.shape
    m, width = q_bf.shape
    t = m // batch
    dv = 2 * hd
    new_spec = pl.BlockSpec((t, width), lambda b: (b, 0))
    n_layers = cache_k.shape[0]
    cache_k = cache_k.reshape(n_layers, batch, past * heads * 2, hd)
    cache_v = cache_v.reshape(n_layers, batch, past, heads, dv // LANES, LANES)
    cache_v = cache_v.transpose(0, 1, 2, 4, 3, 5).reshape(n_layers, batch, past * (dv // LANES) * heads, LANES)
    ck_spec = pl.BlockSpec((None, None, past * heads * 2, hd), lambda b: (layer, b, 0, 0))
    cv_spec = pl.BlockSpec((None, None, past * (dv // LANES) * heads, LANES), lambda b: (layer, b, 0, 0))
    smem = pl.BlockSpec(memory_space=pltpu.SMEM)
    return pl.pallas_call(
        functools.partial(_attn_sample_kernel, post_scale=post_scale),
        grid=(batch,),
        in_specs=[smem, smem, new_spec, new_spec, new_spec, ck_spec, cv_spec,
                  pl.BlockSpec((1, dv), lambda b: (0, 0))],
        out_specs=new_spec,
        out_shape=jax.ShapeDtypeStruct((m, width), BF16),
        compiler_params=_params(1), name="attention_sample",
    )(lam, slopes2, q_bf, k_bf, v_bf, cache_k, cache_v, subln_g.reshape(1, dv))


def _sgu_gate_kernel(u_ref, v_ref, g_ref, b_ref, w_ref, bs_ref, *out_refs, chunk_len, write_v):
    bt, width = u_ref.shape
    gd = width // SGU_GROUPS
    v = v_ref[...].astype(F32)
    mu = jnp.mean(v, axis=-1, keepdims=True)
    vc = v - mu
    vn = vc * lax.rsqrt(jnp.mean(vc * vc, axis=-1, keepdims=True) + EPS) * g_ref[...] + b_ref[...]
    if write_v:
        out_refs[1][...] = vn
    vn_bf = vn.astype(BF16)
    mr = w_ref.shape[1]
    row = lax.broadcasted_iota(jnp.int32, (mr, mr), 0)
    col = lax.broadcasted_iota(jnp.int32, (mr, mr), 1)
    keep = ((row // chunk_len) == (col // chunk_len)) & ((col % chunk_len) <= (row % chunk_len))
    for gi in range(SGU_GROUPS):
        w = jnp.where(keep, w_ref[gi], 0.0).astype(BF16)
        bias = bs_ref[gi]
        for c in range(bt // mr):
            rows = slice(c * mr, (c + 1) * mr)
            cols = slice(gi * gd, (gi + 1) * gd)
            mixed = jnp.dot(w, vn_bf[rows, cols], preferred_element_type=F32) + bias
            out_refs[0][rows, cols] = (u_ref[rows, cols].astype(F32) * mixed).astype(BF16)


def sgu_gate(z, ln_g, ln_b, w_s, b_s, chunk_len, write_v):
    m, two_w = z.shape
    width = two_w // 2
    bt = _pick(m, SGU_MIX_ROWS)
    reps = bt // chunk_len
    w_t = jnp.tile(w_s[:, :chunk_len, :chunk_len], (1, reps, reps))
    b_t = jnp.tile(b_s[:, :chunk_len], (1, reps))[:, :, None]
    row_spec = lambda c: pl.BlockSpec((bt, width), lambda i, c=c: (i, c))
    vec_spec = pl.BlockSpec((1, width), lambda i: (0, 0))
    out_specs = [row_spec(0)]
    out_shape = [jax.ShapeDtypeStruct((m, width), BF16)]
    if write_v:
        out_specs.append(row_spec(0))
        out_shape.append(jax.ShapeDtypeStruct((m, width), F32))
    return pl.pallas_call(
        functools.partial(_sgu_gate_kernel, chunk_len=chunk_len, write_v=write_v),
        grid=(m // bt,),
        in_specs=[row_spec(0), row_spec(1), vec_spec, vec_spec,
                  pl.BlockSpec((SGU_GROUPS, bt, bt), lambda i: (0, 0, 0)),
                  pl.BlockSpec((SGU_GROUPS, bt, 1), lambda i: (0, 0, 0))],
        out_specs=out_specs, out_shape=out_shape,
        compiler_params=_params(1), name="sgu_gate",
    )(z, z, ln_g.reshape(1, width), ln_b.reshape(1, width), w_t, b_t)


def _swiglu(x, wg, wu, wd):
    g = jnp.dot(x, wg, preferred_element_type=F32)
    u = jnp.dot(x, wu, preferred_element_type=F32)
    h = (g * jax.nn.sigmoid(g) * u).astype(BF16)
    return jnp.dot(h, wd, preferred_element_type=F32)


def _moe_kernel(blk_e_ref, n_used_ref, x_ref, wg_ref, wu_ref, wd_ref, *refs, blk0):
    o_ref, wg_bf, wu_bf, wd_bf = refs[-4:]
    b = pl.program_id(0)
    e = blk_e_ref[blk0 + b]
    e_prev = blk_e_ref[blk0 + jnp.maximum(b - 1, 0)]

    @pl.when((b == 0) | (e != e_prev))
    def _():
        wg_bf[...] = wg_ref[...].astype(BF16)
        wu_bf[...] = wu_ref[...].astype(BF16)
        wd_bf[...] = wd_ref[...].astype(BF16)

    @pl.when(blk0 + b < n_used_ref[0])
    def _():
        o_ref[...] = _swiglu(x_ref[...], wg_bf[...], wu_bf[...], wd_bf[...]).astype(o_ref.dtype)

    @pl.when(blk0 + b >= n_used_ref[0])
    def _():
        o_ref[...] = jnp.zeros_like(o_ref)


def moe_experts(xg, blk_expert, n_used, layer, w_gate, w_up, w_down, out, blk0):
    rows, d = xg.shape
    ed = w_gate.shape[3]
    x_spec = pl.BlockSpec((MOE_BLOCK, d), lambda b, be, nu: (b, 0))
    w_idx = lambda b, be, nu: (layer, be[blk0 + b], 0, 0)
    in_specs = [x_spec, pl.BlockSpec((None, None, d, ed), w_idx), pl.BlockSpec((None, None, d, ed), w_idx),
                pl.BlockSpec((None, None, ed, d), w_idx)]
    args = [blk_expert, n_used, xg, w_gate, w_up, w_down]
    aliases = {}
    if out is not None:
        in_specs.append(pl.BlockSpec(memory_space=pl.ANY))
        args.append(out)
        aliases = {len(args) - 1: 0}
    return pl.pallas_call(
        functools.partial(_moe_kernel, blk0=blk0),
        grid_spec=pltpu.PrefetchScalarGridSpec(
            num_scalar_prefetch=2, grid=(rows // MOE_BLOCK,),
            in_specs=in_specs,
            out_specs=pl.BlockSpec((MOE_BLOCK, d), lambda b, be, nu: (blk0 + b, 0)),
            scratch_shapes=[pltpu.VMEM((d, ed), BF16), pltpu.VMEM((d, ed), BF16), pltpu.VMEM((ed, d), BF16)]),
        out_shape=jax.ShapeDtypeStruct((blk_expert.shape[0] * MOE_BLOCK, d), BF16),
        input_output_aliases=aliases,
        compiler_params=_params(1), name="moe_experts",
    )(*args)


def _shared_kernel(h_ref, wg_ref, wu_ref, wd_ref, o_ref):
    o_ref[...] = _swiglu(h_ref[...], wg_ref[...], wu_ref[...], wd_ref[...]).astype(o_ref.dtype)


def shared_expert(h, sw_gate, sw_up, sw_down):
    m, d = h.shape
    ed = sw_gate.shape[1]
    bm = _pick(m, 512)
    row_spec = pl.BlockSpec((bm, d), lambda i: (i, 0))
    full = lambda a, b: pl.BlockSpec((a, b), lambda i: (0, 0))
    return pl.pallas_call(
        _shared_kernel, grid=(m // bm,),
        in_specs=[row_spec, full(d, ed), full(d, ed), full(ed, d)],
        out_specs=row_spec,
        out_shape=jax.ShapeDtypeStruct((m, d), BF16),
        compiler_params=_params(1), name="shared_expert",
    )(h, sw_gate, sw_up, sw_down)


def _combine_kernel(h_ref, p_ref, w_ref, x_ref, gate_ref, wg_ref, wu_ref, wd_ref, o_ref):
    ffn = _swiglu(h_ref[...], wg_ref[...], wu_ref[...], wd_ref[...])
    w = w_ref[...]
    for k in range(TOP_K):
        ffn = ffn + w[:, k:k + 1] * p_ref[k].astype(F32)
    o_ref[...] = x_ref[...] + gate_ref[...] * ffn


def moe_combine_residual(stream, shared, picked, wts, row0, x, gate, tok0, sw_gate, sw_up, sw_down):
    m, d = x.shape
    ed = sw_gate.shape[1]
    full = lambda a, b: pl.BlockSpec((a, b), lambda i: (0, 0))
    n = picked.shape[1]
    bm = _row_block(stream, n, 256)
    assert row0 % bm == 0 and tok0 % bm == 0 and n % bm == 0
    off, off_all = tok0 // bm, (row0 + tok0) // bm
    row_spec = pl.BlockSpec((bm, d), lambda i: (i + off, 0))
    all_rows = lambda width: pl.BlockSpec((bm, width), lambda i: (i + off_all, 0))
    return pl.pallas_call(
        _combine_kernel, grid=(n // bm,),
        in_specs=[all_rows(d), pl.BlockSpec((TOP_K, bm, d), lambda i: (0, i, 0)), all_rows(TOP_K), row_spec,
                  _mod_spec(stream, bm, d, lambda i: (i + off, 0)), full(d, ed), full(d, ed), full(ed, d)],
        out_specs=row_spec,
        out_shape=jax.ShapeDtypeStruct((m, d), F32),
        input_output_aliases={3: 0},
        compiler_params=_params(1), name="moe_combine",
    )(shared, picked, wts, x, gate, sw_gate, sw_up, sw_down)


def _dest_kernel(start_ref, idx_ref, rank_ref, dest_ref):
    idx = idx_ref[...]

    def body(e, acc):
        return jnp.where(idx == e, start_ref[e], acc)

    dest_ref[...] = rank_ref[...] + lax.fori_loop(0, start_ref.shape[0], body, jnp.zeros_like(idx))


def assignment_rows(pad_start, idx, rank):
    return pl.pallas_call(
        _dest_kernel,
        in_specs=[pl.BlockSpec(memory_space=pltpu.SMEM), pl.BlockSpec(memory_space=pltpu.VMEM),
                  pl.BlockSpec(memory_space=pltpu.VMEM)],
        out_specs=pl.BlockSpec(memory_space=pltpu.VMEM),
        out_shape=jax.ShapeDtypeStruct(idx.shape, jnp.int32),
        name="assignment_rows",
    )(pad_start, idx, rank)


def moe_routed(h_all, idx, rank, counts, layer, w_gate, w_up, w_down):
    t, d = h_all.shape
    n_exp = counts.shape[0]
    n_assign = t * TOP_K
    counts = counts.reshape(n_exp).astype(jnp.int32)
    padded = (counts + MOE_BLOCK - 1) // MOE_BLOCK * MOE_BLOCK
    pad_end = jnp.cumsum(padded)
    pad_start = pad_end - padded
    dest = assignment_rows(pad_start, idx, rank).reshape(-1)
    n_blocks = -(-(n_assign + n_exp * (MOE_BLOCK - 1)) // MOE_BLOCK)
    n_blocks = -(-n_blocks // MOE_CHUNKS) * MOE_CHUNKS
    n_rows = n_blocks * MOE_BLOCK
    tok = jnp.tile(jnp.arange(t, dtype=jnp.int32), TOP_K)
    src_tok = jnp.arange(n_rows, dtype=jnp.int32) % t
    src_tok = src_tok.at[dest].add(tok - dest % t, unique_indices=True, mode="promise_in_bounds")
    blk_start = jnp.arange(n_blocks, dtype=jnp.int32) * MOE_BLOCK
    blk_expert = jnp.minimum(jnp.sum((pad_end[None, :] <= blk_start[:, None]).astype(jnp.int32), axis=1), n_exp - 1)
    n_used = pad_end[-1:] // MOE_BLOCK
    out = None
    chunk_rows = n_rows // MOE_CHUNKS
    for c in range(MOE_CHUNKS):
        xg = h_all.at[src_tok[c * chunk_rows:(c + 1) * chunk_rows]].get(mode="promise_in_bounds")
        out = moe_experts(xg, blk_expert, n_used, layer, w_gate, w_up, w_down, out, c * (n_blocks // MOE_CHUNKS))
    return out, dest.reshape(TOP_K, t)


def _stream(rows_per_batch):
    return dict(per_row=rows_per_batch < 1024, rows_per_batch=rows_per_batch)


def _mod_rows(stream, mod):
    b, d = mod.shape
    if not stream["per_row"]:
        return mod.reshape(b, 1, d)
    return jnp.repeat(mod, stream["rows_per_batch"], axis=0).reshape(1, -1, d)


def kernel(x_prompt, x_sample, cache_k_attn, cache_v_attn, c_prompt, c_sample, norm_mix_g, norm_ffn_g, ada_w, ada_b, attn_w_qkv, attn_w_o, attn_q_norm, attn_k_norm, attn_lambda_q1, attn_lambda_k1, attn_lambda_q2, attn_lambda_k2, attn_subln_g, sgu_w_in, sgu_ln_g, sgu_ln_b, sgu_w_s, sgu_b_s, sgu_w_o, moe_w_router, moe_b_router, moe_w_gate, moe_w_up, moe_w_down, shared_w_gate, shared_w_up, shared_w_down):
    bp, sp, d = x_prompt.shape
    bs, ss, _ = x_sample.shape
    depth = ada_w.shape[0]
    dv = 2 * ATTN_HEAD_DIM
    heads = d // dv
    mp, ms = bp * sp, bs * ss
    streams = (_stream(sp), _stream(ss))
    xs = [x_prompt.reshape(mp, d), x_sample.reshape(ms, d)]

    mod = ada_modulation(jnp.concatenate([c_prompt, c_sample], axis=0), ada_w, ada_b)
    slopes2 = (2.0 ** (-8.0 * jnp.arange(1, heads + 1, dtype=F32) / heads)) * LOG2E

    k_out, v_out, sgu_out = [[], []], [[], []], []
    for i in range(depth):
        j = i // N_MIXERS
        mods = []
        for si, (stream, rows) in enumerate(zip(streams, (slice(0, bp), slice(bp, bp + bs)))):
            mods.append([_mod_rows(stream, mod[i, rows, c * d:(c + 1) * d]) for c in range(6)])

        if i % N_MIXERS == 0:
            lam_init = 0.8 - 0.6 * math.exp(-0.3 * i)
            f = lambda a: a[j].astype(F32)
            lam = (jnp.exp(jnp.sum(f(attn_lambda_q1) * f(attn_lambda_k1)))
                   - jnp.exp(jnp.sum(f(attn_lambda_q2) * f(attn_lambda_k2))) + lam_init).reshape(1)
            for si, stream in enumerate(streams):
                h = modulate(stream, xs[si], norm_mix_g[i], mods[si][0], mods[si][1])
                q_bf, k32, k_bf, v32, v_bf = qkv_project(stream, h, attn_w_qkv[j], attn_q_norm[j], attn_k_norm[j])
                if si == 0:
                    o = attention_prompt(q_bf, k_bf, v_bf, bp, lam, slopes2, attn_subln_g[j], 1.0 - lam_init)
                else:
                    o = attention_sample(q_bf, k_bf, v_bf, cache_k_attn, cache_v_attn, j, lam, slopes2,
                                         attn_subln_g[j], 1.0 - lam_init)
                xs[si] = project_residual(stream, o, attn_w_o[j], xs[si], mods[si][2], "attn_out_proj")
                k_out[si].append(k32)
                v_out[si].append(v32)
        else:
            for si, stream in enumerate(streams):
                h = modulate(stream, xs[si], norm_mix_g[i], mods[si][0], mods[si][1])
                (z,) = matmul(h, sgu_w_in[j], bm=_row_block(stream, h.shape[0], 1024),
                              bn=_pick(sgu_w_in.shape[2], 1024),
                              out_dtypes=(BF16,), epilogue=_gelu_epilogue, name="sgu_in_proj")
                chunk_len = SGU_CHUNK if si == 0 else ss
                res = sgu_gate(z, sgu_ln_g[j], sgu_ln_b[j], sgu_w_s[j], sgu_b_s[j], chunk_len, write_v=(si == 1))
                if si == 1:
                    sgu_out.append(res[1])
                xs[si] = project_residual(stream, res[0], sgu_w_o[j], xs[si], mods[si][2], "sgu_out_proj")

        counts = jnp.zeros((moe_w_router.shape[2], 1), F32)
        per_token = ()
        for si, (stream, row0) in enumerate(zip(streams, (0, mp))):
            *per_token, counts = modulate(stream, xs[si], norm_ffn_g[i], mods[si][3], mods[si][4],
                                          router=(moe_w_router[i], moe_b_router[i], counts, mp + ms, row0, per_token))
        h_all, idx, wts, rank = per_token
        sw = [w[i].astype(BF16) for w in (shared_w_gate, shared_w_up, shared_w_down)]
        out, dest = moe_routed(h_all, idx, rank, counts, i, moe_w_gate, moe_w_up, moe_w_down)
        for si, (stream, row0, m_rows) in enumerate(zip(streams, (0, mp), (mp, ms))):
            piece = _pick(m_rows, COMBINE_TOKENS)
            for tok0 in range(0, m_rows, piece):
                rows = dest[:, row0 + tok0:row0 + tok0 + piece].reshape(-1)
                picked = out.at[rows].get(mode="promise_in_bounds", unique_indices=True).reshape(TOP_K, piece, d)
                xs[si] = moe_combine_residual(stream, h_all, picked, wts.T, row0, xs[si], mods[si][5], tok0, *sw)

    n_attn = len(k_out[0])
    k_prompt = jnp.stack(k_out[0]).reshape(n_attn, bp, sp, heads, 2, ATTN_HEAD_DIM)
    v_prompt = jnp.stack(v_out[0]).reshape(n_attn, bp, sp, heads, dv)
    k_sample = jnp.stack(k_out[1]).reshape(n_attn, bs, ss, heads, 2, ATTN_HEAD_DIM)
    v_sample = jnp.stack(v_out[1]).reshape(n_attn, bs, ss, heads, dv)
    sgu_v = jnp.stack(sgu_out).reshape(len(sgu_out), bs, ss, -1)
    return (xs[0].reshape(bp, sp, d), xs[1].reshape(bs, ss, d), k_prompt, v_prompt, k_sample, v_sample, sgu_v)
```
